```python
import math
import jax
import jax.numpy as jnp
from jax import lax
import numpy as np

D_MODEL = 1024
BATCH = 16
SEQ = 256
DEPTH = 4
DEC_BATCH = 4
DEC_SEQ = 1024
PAST_LEN = 512

F32 = jnp.float32
GRID_W = 64
N_MIXERS = 3
N_A = (DEPTH + 2) // N_MIXERS
N_B = (DEPTH + 1) // N_MIXERS
N_C = DEPTH // N_MIXERS
EPS = 1e-6

D_RNN = D_MODEL
RG_HEADS = 8
RG_BS = D_RNN // RG_HEADS
CONV_W = 4
CONV_PAD = (2, 1)
RG_C = 8.0

S5_GROUP = 16
S5_G = D_MODEL // S5_GROUP
S5_P = 64

HEAD_DIM = 64
N_HEADS = D_MODEL // HEAD_DIM
N_KV = 4
Q_PER_KV = N_HEADS // N_KV
WINDOW = 128
ATT_BLOCK = 128
ROPE_BASE = 10000.0
ATT_SCALE = HEAD_DIM ** -0.5
NEG_INF = -1e30

N_EXPERTS = 16
D_EXPERT = 1024
EC_FACTOR = 2

kernel_name = 'hybrid_rglru_s5_swa_ec_diffusion_step'


def rms_norm(x, g):
    xf = x.astype(F32)
    y = xf * lax.rsqrt(jnp.mean(xf * xf, axis=-1, keepdims=True) + EPS)
    return (y * g.astype(F32)).astype(x.dtype)


def ada_modulation(cond, w, b):
    m = jnp.dot(jax.nn.silu(cond), w) + b
    return jnp.split(m[:, None, :], 6, axis=-1)


def linear_scan(a, b, h0, reverse):
    if reverse:
        a, b = jnp.flip(a, 1), jnp.flip(b, 1)

    def combine(l, r):
        return (l[0] * r[0], r[0] * l[1] + r[1])

    a_cum, h = lax.associative_scan(combine, (a, b), axis=1)
    if h0 is not None:
        h = h + a_cum * h0[:, None]
    if reverse:
        h = jnp.flip(h, 1)
    return h


def rglru_gates(u, w_a, b_a, w_x, b_x, lam):
    bn, t, _ = u.shape
    ub = u.reshape(bn, t, RG_HEADS, RG_BS)
    r = jax.nn.sigmoid(jnp.einsum('btnk,nkj->btnj', ub, w_a.astype(F32)).reshape(bn, t, D_RNN) + b_a.astype(F32))
    i = jax.nn.sigmoid(jnp.einsum('btnk,nkj->btnj', ub, w_x.astype(F32)).reshape(bn, t, D_RNN) + b_x.astype(F32))
    log_a = -RG_C * r * jax.nn.softplus(-lam.astype(F32))
    a = jnp.exp(log_a)
    b = jnp.sqrt(-jnp.expm1(2.0 * log_a)) * (i * u)
    return a, b


def rglru_mixer(h, w_in, conv_w, conv_b, w_a, b_a, w_x, b_x, lam, w_out, h0_f, h0_b):
    gate, u = jnp.split(jnp.dot(h, w_in), 2, axis=-1)
    u = lax.conv_general_dilated(u, conv_w[:, None, :], (1,), [CONV_PAD],
                                 dimension_numbers=('NWC', 'WIO', 'NWC'),
                                 feature_group_count=D_RNN) + conv_b
    uf = u.astype(F32)
    a, b = rglru_gates(uf, w_a[0], b_a[0], w_x[0], b_x[0], lam[0])
    hf = linear_scan(a, b, h0_f, False)
    a, b = rglru_gates(uf, w_a[1], b_a[1], w_x[1], b_x[1], lam[1])
    hb = linear_scan(a, b, h0_b, True)
    y = ((hf + hb) * jax.nn.gelu(gate.astype(F32))).astype(h.dtype)
    return jnp.dot(y, w_out), hf[:, -1], hb[:, 0]


def s5_mixer(h, a_re, a_im, log_dt, b_re, b_im, c_re, c_im, d, w_glu, s0_f, s0_b):
    bn, t, _ = h.shape
    xf = h.astype(F32)
    xg = xf.reshape(bn, t, S5_G, S5_GROUP)
    y = d.astype(F32) * xf
    finals = []
    for k, s0, reverse in ((0, s0_f, False), (1, s0_b, True)):
        lam = lax.complex(a_re[k].astype(F32), a_im[k].astype(F32))
        a_bar = jnp.exp(lam * jnp.exp(log_dt[k].astype(F32))[:, None])
        b_bar = ((a_bar - 1.0) / lam)[..., None] * lax.complex(b_re[k].astype(F32), b_im[k].astype(F32))
        bu = lax.complex(jnp.einsum('gph,btgh->btgp', jnp.real(b_bar), xg),
                         jnp.einsum('gph,btgh->btgp', jnp.imag(b_bar), xg))
        s = linear_scan(jnp.broadcast_to(a_bar, bu.shape), bu, s0, reverse)
        yk = (jnp.einsum('ghp,btgp->btgh', c_re[k].astype(F32), jnp.real(s))
              - jnp.einsum('ghp,btgp->btgh', c_im[k].astype(F32), jnp.imag(s)))
        y = y + yk.reshape(bn, t, D_MODEL)
        finals.append(s[:, 0] if reverse else s[:, -1])
    fin = jnp.stack(finals, axis=1)
    state = jnp.stack([jnp.real(fin), jnp.imag(fin)], axis=2)
    u = jax.nn.gelu(y).astype(h.dtype)
    val, g = jnp.split(jnp.dot(u, w_glu), 2, axis=-1)
    return val * jax.nn.sigmoid(g), state


def split_qkv(h, w_qkv):
    bn, t, _ = h.shape
    q, k, v = jnp.split(jnp.dot(h, w_qkv), [N_HEADS * HEAD_DIM, (N_HEADS + N_KV) * HEAD_DIM], axis=-1)
    return (q.reshape(bn, t, N_KV, Q_PER_KV, HEAD_DIM),
            k.reshape(bn, t, N_KV, HEAD_DIM),
            v.reshape(bn, t, N_KV, HEAD_DIM))


def rope_2d(x, row, col):
    half = HEAD_DIM // 2
    quarter = half // 2
    freqs = ROPE_BASE ** (-jnp.arange(quarter, dtype=F32) / quarter)

    def rotate(seg, pos):
        ang = pos.astype(F32)[:, None] * freqs
        shape = (1, ang.shape[0]) + (1,) * (seg.ndim - 3) + (quarter,)
        cos, sin = jnp.cos(ang).reshape(shape), jnp.sin(ang).reshape(shape)
        s1, s2 = seg[..., :quarter], seg[..., quarter:]
        return jnp.concatenate([s1 * cos - s2 * sin, s2 * cos + s1 * sin], axis=-1)

    xf = x.astype(F32)
    return jnp.concatenate([rotate(xf[..., :half], row), rotate(xf[..., half:], col)], axis=-1).astype(x.dtype)


def sink_logits(sink, lead):
    return jnp.broadcast_to(sink.astype(F32).reshape((N_KV, Q_PER_KV) + (1,) * (len(lead) - 2)),
                            tuple(lead) + (1,))


def attn_context(h, w_qkv, w_o, sink):
    bn, l, _ = h.shape
    q, k, v = split_qkv(h, w_qkv)
    s = jnp.einsum('bqhgd,bkhd->bhgqk', q, k).astype(F32) * ATT_SCALE
    snk = sink_logits(sink, (bn, N_KV, Q_PER_KV, l))
    p = jax.nn.softmax(jnp.concatenate([s, snk], axis=-1), axis=-1)[..., :l]
    o = jnp.einsum('bhgqk,bkhd->bqhgd', p.astype(v.dtype), v)
    return jnp.dot(o.reshape(bn, l, D_MODEL), w_o), k, v


def band_blocks(x, nb):
    bn = x.shape[0]
    xp = jnp.pad(x, ((0, 0), (ATT_BLOCK, ATT_BLOCK), (0, 0), (0, 0))).reshape(bn, nb + 2, ATT_BLOCK, N_KV, HEAD_DIM)
    return jnp.concatenate([xp[:, :-2], xp[:, 1:-1], xp[:, 2:]], axis=2)


def attn_latent(h, w_qkv, w_o, sink, k_ctx, v_ctx):
    bn, t, _ = h.shape
    rows = t // GRID_W
    row = jnp.repeat(jnp.arange(rows), GRID_W)
    col = jnp.tile(jnp.arange(GRID_W), rows)
    q, k, v = split_qkv(h, w_qkv)
    q, k = rope_2d(q, row, col), rope_2d(k, row, col)
    nb = t // ATT_BLOCK
    nw = 3 * ATT_BLOCK
    lc = k_ctx.shape[1]
    qb = q.reshape(bn, nb, ATT_BLOCK, N_KV, Q_PER_KV, HEAD_DIM)
    kw, vw = band_blocks(k, nb), band_blocks(v, nb)
    s_loc = jnp.einsum('bnqhgd,bnkhd->bhgnqk', qb, kw).astype(F32) * ATT_SCALE
    qi = jnp.arange(ATT_BLOCK)[:, None]
    kj = jnp.arange(nw)[None, :]
    kpos = jnp.arange(nb)[:, None, None] * ATT_BLOCK - ATT_BLOCK + kj[None]
    mask = (jnp.abs(kj - ATT_BLOCK - qi) <= WINDOW)[None] & (kpos >= 0) & (kpos < t)
    s_loc = jnp.where(mask, s_loc, NEG_INF)
    s_ctx = jnp.einsum('bnqhgd,bkhd->bhgnqk', qb, k_ctx).astype(F32) * ATT_SCALE
    snk = sink_logits(sink, (bn, N_KV, Q_PER_KV, nb, ATT_BLOCK))
    p = jax.nn.softmax(jnp.concatenate([s_loc, s_ctx, snk], axis=-1), axis=-1).astype(v.dtype)
    o = (jnp.einsum('bhgnqk,bnkhd->bnqhgd', p[..., :nw], vw)
         + jnp.einsum('bhgnqk,bkhd->bnqhgd', p[..., nw:nw + lc], v_ctx))
    return jnp.dot(o.reshape(bn, t, D_MODEL), w_o)


def moe_expert_choice(h, router_w, w_gate, w_up, w_down):
    bn, t, d = h.shape
    n = bn * t
    cap = (EC_FACTOR * n) // N_EXPERTS
    xt = h.reshape(n, d)
    aff = jax.nn.softmax(jnp.dot(xt, router_w).astype(F32), axis=-1)
    gates, idx = lax.top_k(aff.T, cap)
    xe = xt[idx]
    he = jax.nn.silu(jnp.einsum('ecd,edf->ecf', xe, w_gate)) * jnp.einsum('ecd,edf->ecf', xe, w_up)
    ye = jnp.einsum('ecf,efd->ecd', he, w_down) * gates[..., None].astype(h.dtype)
    out = jnp.zeros_like(xt).at[idx.reshape(-1)].add(ye.reshape(-1, d))
    return out.reshape(bn, t, d)


def setup_inputs(seed: int = 0) -> dict:
    key = jax.random.key(seed)
    ks = iter(jax.random.split(key, 64))

    def nrm(shape, scale):
        return jax.random.normal(next(ks), shape, F32) * scale

    D, R, H = D_MODEL, D_RNN, S5_GROUP
    u = jax.random.uniform(next(ks), (N_A, 2, R), F32, 0.9, 0.999)
    a_root = u ** (1.0 / RG_C)
    rg_lambda = jnp.log(a_root) - jnp.log1p(-a_root)
    log_dt = jax.random.uniform(next(ks), (N_B, 2, S5_G), F32, math.log(1e-3), math.log(1e-1))
    return {
        'x_prompt': nrm((BATCH, SEQ, D), 1.0),
        'x_sample': nrm((DEC_BATCH, DEC_SEQ, D), 1.0),
        'state_rglru': nrm((DEC_BATCH, N_A, 2, R), 0.5),
        'state_s5': nrm((DEC_BATCH, N_B, 2, 2, S5_G, S5_P), 0.2),
        'cache_k': nrm((DEC_BATCH, N_C, PAST_LEN, N_KV, HEAD_DIM), 1.0),
        'cache_v': nrm((DEC_BATCH, N_C, PAST_LEN, N_KV, HEAD_DIM), 1.0),
        'c': nrm((DEC_BATCH, D), 1.0),
        'c_ctx': nrm((D,), 1.0),
        'ada_w': nrm((DEPTH, D, 6 * D), 0.5 * D ** -0.5),
        'ada_b': nrm((DEPTH, 6 * D), 0.02),
        'norm1_g': 1.0 + nrm((DEPTH, D), 0.02),
        'norm2_g': 1.0 + nrm((DEPTH, D), 0.02),
        'rg_w_in': nrm((N_A, D, 2 * R), D ** -0.5),
        'rg_conv_w': nrm((N_A, CONV_W, R), CONV_W ** -0.5),
        'rg_conv_b': nrm((N_A, R), 0.02),
        'rg_w_a': nrm((N_A, 2, RG_HEADS, RG_BS, RG_BS), RG_BS ** -0.5),
        'rg_b_a': nrm((N_A, 2, R), 0.02),
        'rg_w_x': nrm((N_A, 2, RG_HEADS, RG_BS, RG_BS), RG_BS ** -0.5),
        'rg_b_x': nrm((N_A, 2, R), 0.02),
        'rg_lambda': rg_lambda,
        'rg_w_out': nrm((N_A, R, D), R ** -0.5),
        's5_a_re': -0.5 + nrm((N_B, 2, S5_G, S5_P), 0.01),
        's5_a_im': jnp.pi * jnp.arange(S5_P, dtype=F32) + nrm((N_B, 2, S5_G, S5_P), 0.01),
        's5_log_dt': log_dt,
        's5_b_re': nrm((N_B, 2, S5_G, S5_P, H), (2 * H) ** -0.5),
        's5_b_im': nrm((N_B, 2, S5_G, S5_P, H), (2 * H) ** -0.5),
        's5_c_re': nrm((N_B, 2, S5_G, H, S5_P), (2 * S5_P) ** -0.5),
        's5_c_im': nrm((N_B, 2, S5_G, H, S5_P), (2 * S5_P) ** -0.5),
        's5_d': nrm((N_B, D), 1.0),
        's5_w_glu': nrm((N_B, D, 2 * D), D ** -0.5),
        'attn_w_qkv': nrm((N_C, D, (N_HEADS + 2 * N_KV) * HEAD_DIM), D ** -0.5),
        'attn_w_o': nrm((N_C, N_HEADS * HEAD_DIM, D), (N_HEADS * HEAD_DIM) ** -0.5),
        'attn_sink': nrm((N_C, N_HEADS), 0.5),
        'router_w': nrm((DEPTH, D, N_EXPERTS), D ** -0.5),
        'moe_w_gate': nrm((DEPTH, N_EXPERTS, D, D_EXPERT), D ** -0.5),
        'moe_w_up': nrm((DEPTH, N_EXPERTS, D, D_EXPERT), D ** -0.5),
        'moe_w_down': nrm((DEPTH, N_EXPERTS, D_EXPERT, D), D_EXPERT ** -0.5),
        'final_norm_g': 1.0 + nrm((D,), 0.02),
    }


def reference(x_prompt, x_sample, state_rglru, state_s5, cache_k, cache_v, c, c_ctx,
              ada_w, ada_b, norm1_g, norm2_g,
              rg_w_in, rg_conv_w, rg_conv_b, rg_w_a, rg_b_a, rg_w_x, rg_b_x, rg_lambda, rg_w_out,
              s5_a_re, s5_a_im, s5_log_dt, s5_b_re, s5_b_im, s5_c_re, s5_c_im, s5_d, s5_w_glu,
              attn_w_qkv, attn_w_o, attn_sink,
              router_w, moe_w_gate, moe_w_up, moe_w_down, final_norm_g):
    xp, xs = x_prompt, x_sample
    new_rg, new_s5, new_k, new_v = [], [], [], []
    for l in range(DEPTH):
        kind, j = l % N_MIXERS, l // N_MIXERS
        sh1p, sc1p, g1p, sh2p, sc2p, g2p = ada_modulation(c_ctx[None, :], ada_w[l], ada_b[l])
        sh1s, sc1s, g1s, sh2s, sc2s, g2s = ada_modulation(c, ada_w[l], ada_b[l])
        hp = rms_norm(xp, norm1_g[l]) * (1.0 + sc1p) + sh1p
        hs = rms_norm(xs, norm1_g[l]) * (1.0 + sc1s) + sh1s
        if kind == 0:
            rg = (rg_w_in[j], rg_conv_w[j], rg_conv_b[j], rg_w_a[j], rg_b_a[j],
                  rg_w_x[j], rg_b_x[j], rg_lambda[j], rg_w_out[j])
            mp, fin_f, fin_b = rglru_mixer(hp, *rg, None, None)
            new_rg.append(jnp.stack([fin_f, fin_b], axis=1))
            ms, _, _ = rglru_mixer(hs, *rg, state_rglru[:, j, 0], state_rglru[:, j, 1])
        elif kind == 1:
            s5 = (s5_a_re[j], s5_a_im[j], s5_log_dt[j], s5_b_re[j], s5_b_im[j],
                  s5_c_re[j], s5_c_im[j], s5_d[j], s5_w_glu[j])
            mp, st = s5_mixer(hp, *s5, None, None)
            new_s5.append(st)
            st_in = state_s5[:, j].astype(F32)
            ms, _ = s5_mixer(hs, *s5, lax.complex(st_in[:, 0, 0], st_in[:, 0, 1]),
                             lax.complex(st_in[:, 1, 0], st_in[:, 1, 1]))
        else:
            mp, kc, vc = attn_context(hp, attn_w_qkv[j], attn_w_o[j], attn_sink[j])
            new_k.append(kc)
            new_v.append(vc)
            ms = attn_latent(hs, attn_w_qkv[j], attn_w_o[j], attn_sink[j], cache_k[:, j], cache_v[:, j])
        xp = xp + g1p * mp
        xs = xs + g1s * ms
        xp = xp + g2p * moe_expert_choice(rms_norm(xp, norm2_g[l]) * (1.0 + sc2p) + sh2p,
                                          router_w[l], moe_w_gate[l], moe_w_up[l], moe_w_down[l])
        xs = xs + g2s * moe_expert_choice(rms_norm(xs, norm2_g[l]) * (1.0 + sc2s) + sh2s,
                                          router_w[l], moe_w_gate[l], moe_w_up[l], moe_w_down[l])
    y_prompt = rms_norm(xp, final_norm_g)
    y_sample = rms_norm(xs, final_norm_g)
    new_state_rglru = jnp.stack(new_rg, axis=1)
    new_state_s5 = jnp.stack(new_s5, axis=1)
    new_cache_k = jnp.stack(new_k, axis=1)
    new_cache_v = jnp.stack(new_v, axis=1)
    return (y_prompt, y_sample, new_state_rglru, new_state_s5, new_cache_k, new_cache_v)
```

```python
import functools
import math

import jax
import jax.numpy as jnp
from jax import lax
from jax.experimental import pallas as pl
from jax.experimental.pallas import tpu as pltpu

F32 = jnp.float32
BF16 = jnp.bfloat16
HIGHEST = lax.Precision.HIGHEST

EPS = 1e-6
RG_C = 8.0
RG_BS = 128
S5_H = 16
S5_L = 16
N_HEADS = 16
N_KV = 4
HEAD_DIM = 64
GRID_W = 64
WINDOW = 128
ATT_BLOCK = 128
ROPE_BASE = 10000.0
NEG_INF = -1e30
N_EXPERTS = 16
EC_FACTOR = 2
SUBLANES = 8
ROW_TILE = 512
MOE_ROW_SUB = 256
VMEM_LIMIT = 56 * 1024 * 1024


def _cparams(*sem):
    return pltpu.CompilerParams(dimension_semantics=sem, vmem_limit_bytes=VMEM_LIMIT)


def _gelu(x):
    return x * (0.5 * (1.0 + jnp.tanh(math.sqrt(2.0 / math.pi) * (x + 0.044715 * (x * x * x)))))


def _norm_mod(x, g, sc, sh):
    ms = jnp.mean(x * x, axis=-1, keepdims=True)
    return ((x * lax.rsqrt(ms + EPS)) * g) * (1.0 + sc) + sh


def _mod_kernel(c_ref, w_ref, b_ref, o_ref):
    c = c_ref[...]
    s = (c * jax.nn.sigmoid(c)).astype(BF16)
    o_ref[...] = jnp.dot(s, w_ref[...].astype(BF16), preferred_element_type=F32) + b_ref[...]


def ada_modulation_all(cond, ada_w, ada_b):
    n_layers, d, n = ada_w.shape
    tn = 1536
    return pl.pallas_call(
        _mod_kernel,
        grid=(n_layers, n // tn),
        in_specs=[pl.BlockSpec((SUBLANES, d), lambda l, j: (0, 0)),
                  pl.BlockSpec((None, d, tn), lambda l, j: (l, 0, j)),
                  pl.BlockSpec((None, 1, tn), lambda l, j: (l, 0, j))],
        out_specs=pl.BlockSpec((None, SUBLANES, tn), lambda l, j: (l, 0, j)),
        out_shape=jax.ShapeDtypeStruct((n_layers, SUBLANES, n), F32),
        compiler_params=_cparams("arbitrary", "arbitrary"),
        name="ada_mod",
    )(cond, ada_w, ada_b.reshape(n_layers, 1, n))


class _Rows:
    def __init__(self, n_prompt, t_sample):
        self.n_prompt = n_prompt
        self.t_sample = t_sample
        self.tm = min(ROW_TILE, n_prompt, t_sample)
        assert n_prompt % self.tm == 0 and t_sample % self.tm == 0

    def seg(self, i, tm):
        r = i * tm
        return jnp.where(r < self.n_prompt, 0, 1 + lax.div(r - self.n_prompt, self.t_sample))


def _mod_spec(rows, tm, d, chunk, m_axis):
    def imap(*ids):
        return (rows.seg(ids[m_axis], tm), 0, chunk)
    return pl.BlockSpec((None, 1, d), imap)


def _nm_kernel(x_ref, g_ref, sc_ref, sh_ref, w_ref, o_ref, wbf_ref):
    @pl.when(pl.program_id(1) == 0)
    def _():
        wbf_ref[...] = w_ref[...].astype(BF16)
    h = _norm_mod(x_ref[...], g_ref[...], sc_ref[...], sh_ref[...])
    o_ref[...] = jnp.dot(h.astype(BF16), wbf_ref[...], preferred_element_type=F32)


def norm_mod_matmul(x, g, mod, rows, w, *, tn):
    tm = rows.tm
    m, d = x.shape
    n = w.shape[1]
    return pl.pallas_call(
        _nm_kernel,
        grid=(n // tn, m // tm),
        in_specs=[pl.BlockSpec((tm, d), lambda j, i: (i, 0)),
                  pl.BlockSpec((1, d), lambda j, i: (0, 0)),
                  _mod_spec(rows, tm, d, 1, 1),
                  _mod_spec(rows, tm, d, 0, 1),
                  pl.BlockSpec((d, tn), lambda j, i: (0, j))],
        out_specs=pl.BlockSpec((tm, tn), lambda j, i: (i, j)),
        out_shape=jax.ShapeDtypeStruct((m, n), F32),
        scratch_shapes=[pltpu.VMEM((d, tn), BF16)],
        compiler_params=_cparams("arbitrary", "arbitrary"),
        name="norm_mod_matmul",
    )(x, g.reshape(1, d), mod, mod, w)


def _norm_only_kernel(x_ref, g_ref, sc_ref, sh_ref, o_ref):
    o_ref[...] = _norm_mod(x_ref[...], g_ref[...], sc_ref[...], sh_ref[...])


def norm_mod(x, g, mod, rows):
    tm = rows.tm
    m, d = x.shape
    return pl.pallas_call(
        _norm_only_kernel,
        grid=(m // tm,),
        in_specs=[pl.BlockSpec((tm, d), lambda i: (i, 0)),
                  pl.BlockSpec((1, d), lambda i: (0, 0)),
                  _mod_spec(rows, tm, d, 1, 0),
                  _mod_spec(rows, tm, d, 0, 0)],
        out_specs=pl.BlockSpec((tm, d), lambda i: (i, 0)),
        out_shape=jax.ShapeDtypeStruct((m, d), F32),
        compiler_params=_cparams("arbitrary"),
        name="norm_mod",
    )(x, g.reshape(1, d), mod, mod)


def _final_norm_kernel(x_ref, g_ref, o_ref):
    x = x_ref[...]
    ms = jnp.mean(x * x, axis=-1, keepdims=True)
    o_ref[...] = (x * lax.rsqrt(ms + EPS)) * g_ref[...]


def final_norm(x, g, tm):
    m, d = x.shape
    return pl.pallas_call(
        _final_norm_kernel,
        grid=(m // tm,),
        in_specs=[pl.BlockSpec((tm, d), lambda i: (i, 0)),
                  pl.BlockSpec((1, d), lambda i: (0, 0))],
        out_specs=pl.BlockSpec((tm, d), lambda i: (i, 0)),
        out_shape=jax.ShapeDtypeStruct((m, d), F32),
        compiler_params=_cparams("arbitrary"),
        name="final_norm",
    )(x, g.reshape(1, d))


def _mmres_kernel(a_ref, w_ref, r_ref, gt_ref, o_ref, wbf_ref):
    @pl.when(pl.program_id(1) == 0)
    def _():
        wbf_ref[...] = w_ref[...].astype(BF16)
    acc = jnp.dot(a_ref[...].astype(BF16), wbf_ref[...], preferred_element_type=F32)
    o_ref[...] = r_ref[...] + gt_ref[...] * acc


def matmul_gated_residual(a, w, resid, mod, rows, *, tn=512):
    tm = rows.tm
    m, k = a.shape
    d = w.shape[1]
    return pl.pallas_call(
        _mmres_kernel,
        grid=(d // tn, m // tm),
        in_specs=[pl.BlockSpec((tm, k), lambda j, i: (i, 0)),
                  pl.BlockSpec((k, tn), lambda j, i: (0, j)),
                  pl.BlockSpec((tm, tn), lambda j, i: (i, j)),
                  pl.BlockSpec((None, 1, tn), lambda j, i: (rows.seg(i, tm), 0, 2 * (d // tn) + j))],
        out_specs=pl.BlockSpec((tm, tn), lambda j, i: (i, j)),
        out_shape=jax.ShapeDtypeStruct((m, d), F32),
        scratch_shapes=[pltpu.VMEM((k, tn), BF16)],
        compiler_params=_cparams("arbitrary", "arbitrary"),
        name="matmul_gated_residual",
    )(a, w, resid, mod)


def _glures_kernel(a_ref, wv_ref, wg_ref, r_ref, gt_ref, o_ref, wv_bf, wg_bf):
    @pl.when(pl.program_id(1) == 0)
    def _():
        wv_bf[...] = wv_ref[...].astype(BF16)
        wg_bf[...] = wg_ref[...].astype(BF16)
    a = a_ref[...].astype(BF16)
    v = jnp.dot(a, wv_bf[...], preferred_element_type=F32)
    g = jnp.dot(a, wg_bf[...], preferred_element_type=F32)
    o_ref[...] = r_ref[...] + gt_ref[...] * (v * jax.nn.sigmoid(g))


def glu_gated_residual(a, w_glu, resid, mod, rows, *, tn=512):
    tm = rows.tm
    m, k = a.shape
    d = w_glu.shape[1] // 2
    nt = d // tn
    return pl.pallas_call(
        _glures_kernel,
        grid=(nt, m // tm),
        in_specs=[pl.BlockSpec((tm, k), lambda j, i: (i, 0)),
                  pl.BlockSpec((k, tn), lambda j, i: (0, j)),
                  pl.BlockSpec((k, tn), lambda j, i: (0, nt + j)),
                  pl.BlockSpec((tm, tn), lambda j, i: (i, j)),
                  pl.BlockSpec((None, 1, tn), lambda j, i: (rows.seg(i, tm), 0, 2 * nt + j))],
        out_specs=pl.BlockSpec((tm, tn), lambda j, i: (i, j)),
        out_shape=jax.ShapeDtypeStruct((m, d), F32),
        scratch_shapes=[pltpu.VMEM((k, tn), BF16), pltpu.VMEM((k, tn), BF16)],
        compiler_params=_cparams("arbitrary", "arbitrary"),
        name="glu_gated_residual",
    )(a, w_glu, w_glu, resid, mod)


def _rglru_kernel(gate_ref, u_ref, cw_ref, cb_ref, wa_ref, ba_ref, wx_ref, bx_ref, lam_ref, h0_ref,
                  y_ref, fin_ref, af_s, bf_s, ab_s, bb_s, hf_s, hb_s):
    t, cw = u_ref.shape
    u = u_ref[...]
    row = lax.broadcasted_iota(jnp.int32, (t, cw), 0)

    def shifted(x, k):
        if k > 0:
            return jnp.where(row >= k, pltpu.roll(x, k, axis=0), 0.0)
        return jnp.where(row < t + k, pltpu.roll(x, t + k, axis=0), 0.0)

    cwv = cw_ref[...]
    uc = (cwv[0:1] * shifted(u, 2) + cwv[1:2] * shifted(u, 1) + cwv[2:3] * u
          + cwv[3:4] * shifted(u, -1) + cb_ref[...])

    a_scr = (af_s, ab_s)
    b_scr = (bf_s, bb_s)
    for k in range(2):
        nl = -lam_ref[k:k + 1, :]
        sp = jnp.maximum(nl, 0.0) + jnp.log1p(jnp.exp(-jnp.abs(nl)))
        for hh in range(cw // RG_BS):
            sl = slice(hh * RG_BS, (hh + 1) * RG_BS)
            uh = uc[:, sl]
            ub = uh.astype(BF16)
            r = jax.nn.sigmoid(jnp.dot(ub, wa_ref[k, hh].astype(BF16), preferred_element_type=F32)
                               + ba_ref[k:k + 1, sl])
            i = jax.nn.sigmoid(jnp.dot(ub, wx_ref[k, hh].astype(BF16), preferred_element_type=F32)
                               + bx_ref[k:k + 1, sl])
            log_a = (-RG_C * r) * sp[:, sl]
            a = jnp.exp(log_a)
            a_scr[k][:, sl] = a
            b_scr[k][:, sl] = jnp.sqrt(jnp.tanh(-log_a) * (a * a + 1.0)) * (i * uh)

    nblk = t // SUBLANES
    srow = lax.broadcasted_iota(jnp.int32, (SUBLANES, cw), 0)

    def body(n, carry):
        cf, cb = carry
        rf = pl.multiple_of(n * SUBLANES, SUBLANES)
        rb = pl.multiple_of((nblk - 1 - n) * SUBLANES, SUBLANES)
        a = af_s[pl.ds(rf, SUBLANES), :]
        b = bf_s[pl.ds(rf, SUBLANES), :]
        a2 = ab_s[pl.ds(rb, SUBLANES), :]
        b2 = bb_s[pl.ds(rb, SUBLANES), :]
        for s in (1, 2, 4):
            m = srow >= s
            b = jnp.where(m, a * pltpu.roll(b, s, axis=0) + b, b)
            a = jnp.where(m, a * pltpu.roll(a, s, axis=0), a)
            m2 = srow < SUBLANES - s
            b2 = jnp.where(m2, a2 * pltpu.roll(b2, SUBLANES - s, axis=0) + b2, b2)
            a2 = jnp.where(m2, a2 * pltpu.roll(a2, SUBLANES - s, axis=0), a2)
        hf = a * cf + b
        hb = a2 * cb + b2
        hf_s[pl.ds(rf, SUBLANES), :] = hf
        hb_s[pl.ds(rb, SUBLANES), :] = hb
        return hf[SUBLANES - 1:SUBLANES, :], hb[0:1, :]

    cf, cb = lax.fori_loop(0, nblk, body, (h0_ref[0:1, :], h0_ref[1:2, :]))
    fin_ref[0:1, :] = cf
    fin_ref[1:2, :] = cb
    y_ref[...] = (hf_s[...] + hb_s[...]) * _gelu(gate_ref[...])


def rglru_scan(gu, row0, n_seq, t, conv_w, conv_b, w_a, b_a, w_x, b_x, lam, h0, *, cw=256):
    r = gu.shape[1] // 2
    nh = cw // RG_BS
    blk0 = row0 // t
    nc = r // cw
    scr = [pltpu.VMEM((t, cw), F32) for _ in range(6)]
    return pl.pallas_call(
        _rglru_kernel,
        grid=(n_seq, nc),
        in_specs=[pl.BlockSpec((t, cw), lambda b, c: (blk0 + b, c)),
                  pl.BlockSpec((t, cw), lambda b, c: (blk0 + b, nc + c)),
                  pl.BlockSpec((4, cw), lambda b, c: (0, c)),
                  pl.BlockSpec((1, cw), lambda b, c: (0, c)),
                  pl.BlockSpec((2, nh, RG_BS, RG_BS), lambda b, c: (0, c, 0, 0)),
                  pl.BlockSpec((2, cw), lambda b, c: (0, c)),
                  pl.BlockSpec((2, nh, RG_BS, RG_BS), lambda b, c: (0, c, 0, 0)),
                  pl.BlockSpec((2, cw), lambda b, c: (0, c)),
                  pl.BlockSpec((2, cw), lambda b, c: (0, c)),
                  pl.BlockSpec((None, 2, cw), lambda b, c: (b, 0, c))],
        out_specs=[pl.BlockSpec((t, cw), lambda b, c: (b, c)),
                   pl.BlockSpec((None, 2, cw), lambda b, c: (b, 0, c))],
        out_shape=[jax.ShapeDtypeStruct((n_seq * t, r), F32),
                   jax.ShapeDtypeStruct((n_seq, 2, r), F32)],
        scratch_shapes=scr,
        compiler_params=_cparams("arbitrary", "arbitrary"),
        name="rglru_scan",
    )(gu, gu, conv_w, conv_b.reshape(1, r), w_a, b_a, w_x, b_x, lam, h0)


def _s5_kernel(x_ref, tm_ref, win_ref, wre_ref, wim_ref, ar_ref, ai_ref, d_ref, s0re_ref, s0im_ref,
               y_ref, fin_ref, ure_s, uim_s, fre_s, fim_s, bre_s, bim_s, *, bp):
    m = x_ref.shape[0]
    nc = m // bp
    half = ure_s.shape[1] // 2
    x = x_ref[...]
    xb = x.astype(BF16)
    u = jnp.dot(xb, win_ref[...].astype(BF16), preferred_element_type=F32)
    ure_s[...] = u[:, :2 * half]
    uim_s[...] = u[:, 2 * half:]
    is_fwd = lax.broadcasted_iota(jnp.int32, (bp, 2 * half), 1) < half
    ar = ar_ref[...]
    ai = ai_ref[...]

    def body(k, carry):
        re, im = carry
        rf = pl.multiple_of(k * bp, bp)
        rb = pl.multiple_of((nc - 1 - k) * bp, bp)
        fre_s[pl.ds(rf, bp), :] = re
        fim_s[pl.ds(rf, bp), :] = im
        bre_s[pl.ds(rb, bp), :] = re
        bim_s[pl.ds(rb, bp), :] = im
        ure = jnp.where(is_fwd, ure_s[pl.ds(rf, bp), :], ure_s[pl.ds(rb, bp), :])
        uim = jnp.where(is_fwd, uim_s[pl.ds(rf, bp), :], uim_s[pl.ds(rb, bp), :])
        return ar * re - ai * im + ure, ar * im + ai * re + uim

    re, im = lax.fori_loop(0, nc, body, (s0re_ref[...], s0im_ref[...]))
    fin_ref[:, :2 * half] = re
    fin_ref[:, 2 * half:] = im
    fwd_all = lax.broadcasted_iota(jnp.int32, (m, 2 * half), 1) < half
    hre = jnp.where(fwd_all, fre_s[...], bre_s[...]).astype(BF16)
    him = jnp.where(fwd_all, fim_s[...], bim_s[...]).astype(BF16)
    y = (jnp.dot(xb, tm_ref[...].astype(BF16), preferred_element_type=F32)
         + jnp.dot(hre, wre_ref[...].astype(BF16), preferred_element_type=F32)
         + jnp.dot(him, wim_ref[...].astype(BF16), preferred_element_type=F32)
         + d_ref[...] * x)
    y_ref[...] = _gelu(y)


def s5_chunked(xg, tmat, win, wre, wim, ar, ai, dg, s0re, s0im, *, bp):
    g, m, w = xg.shape
    p2 = ar.shape[-1]
    kern = functools.partial(_s5_kernel, bp=bp)
    per_g = lambda shape: pl.BlockSpec((None,) + shape, lambda i: (i, 0, 0))
    return pl.pallas_call(
        kern,
        grid=(g,),
        in_specs=[per_g((m, w)), per_g((w, w)), per_g((w, 2 * p2)), per_g((p2, w)), per_g((p2, w)),
                  per_g((1, p2)), per_g((1, p2)), per_g((1, w)), per_g((bp, p2)), per_g((bp, p2))],
        out_specs=[per_g((m, w)), per_g((bp, 2 * p2))],
        out_shape=[jax.ShapeDtypeStruct((g, m, w), F32), jax.ShapeDtypeStruct((g, bp, 2 * p2), F32)],
        scratch_shapes=[pltpu.VMEM((m, p2), F32) for _ in range(6)],
        compiler_params=_cparams("arbitrary"),
        name="s5_chunked",
    )(xg, tmat, win, wre, wim, ar, ai, dg, s0re, s0im)


def _s5_matrices(a_re, a_im, log_dt, b_re, b_im, c_re, c_im, d):
    _, g, p = a_re.shape
    h = b_re.shape[-1]
    ell = S5_L
    dt = jnp.exp(log_dt)[:, :, None]
    lam = lax.complex(a_re, a_im)
    steps = jnp.arange(ell + 1, dtype=F32)[:, None, None, None]
    mag = jnp.exp(steps * (a_re * dt)[None])
    ang = steps * (a_im * dt)[None]
    am = lax.complex(mag * jnp.cos(ang), mag * jnp.sin(ang))
    bb = ((am[1] - 1.0) / lam)[..., None] * lax.complex(b_re, b_im)
    cc = lax.complex(c_re, c_im)
    win_f = jnp.einsum('lgp,gph->glhp', jnp.flip(am[:ell, 0], 0), bb[0])
    win_b = jnp.einsum('lgp,gph->glhp', am[:ell, 1], bb[1])
    win = jnp.concatenate([jnp.real(win_f), jnp.real(win_b), jnp.imag(win_f), jnp.imag(win_b)],
                          axis=-1).reshape(g, ell * h, 4 * p)
    wout_f = jnp.einsum('ghp,lgp->gplh', cc[0], am[1:, 0])
    wout_b = jnp.einsum('ghp,lgp->gplh', cc[1], jnp.flip(am[1:, 1], 0))
    wre = jnp.concatenate([jnp.real(wout_f), jnp.real(wout_b)], axis=1).reshape(g, 2 * p, ell * h)
    wim = jnp.concatenate([-jnp.imag(wout_f), -jnp.imag(wout_b)], axis=1).reshape(g, 2 * p, ell * h)
    kf = jnp.real(jnp.einsum('ghp,mgp,gpk->mghk', cc[0], am[:ell, 0], bb[0], precision=HIGHEST))
    kb = jnp.real(jnp.einsum('ghp,mgp,gpk->mghk', cc[1], am[:ell, 1], bb[1], precision=HIGHEST))
    li = jnp.arange(ell)
    lag = li[None, :] - li[:, None]
    tf = jnp.where((lag >= 0)[:, :, None, None, None], kf[jnp.clip(lag, 0, ell - 1)], 0.0)
    tb = jnp.where((lag <= 0)[:, :, None, None, None], kb[jnp.clip(-lag, 0, ell - 1)], 0.0)
    tmat = jnp.transpose(tf + tb, (2, 0, 4, 1, 3)).reshape(g, ell * h, ell * h)
    a_l = am[ell]
    ar = jnp.concatenate([jnp.real(a_l[0]), jnp.real(a_l[1])], -1)[:, None, :]
    ai = jnp.concatenate([jnp.imag(a_l[0]), jnp.imag(a_l[1])], -1)[:, None, :]
    dg = jnp.tile(d.reshape(g, 1, h), (1, ell, 1)).reshape(g, 1, ell * h)
    return tmat, win, wre, wim, ar, ai, dg


def s5_mixer_group(hn, n_seq, t, mats, s0):
    tmat, win, wre, wim, ar, ai, dg = mats
    g = tmat.shape[0]
    p = ar.shape[-1] // 2
    ell, h = S5_L, S5_H
    nc = t // ell
    bp = -(-n_seq // SUBLANES) * SUBLANES
    xg = hn.reshape(n_seq, nc, ell, g, h).transpose(3, 1, 0, 2, 4)
    xg = jnp.pad(xg, ((0, 0), (0, 0), (0, bp - n_seq), (0, 0), (0, 0))).reshape(g, nc * bp, ell * h)
    if s0 is None:
        s0re = jnp.zeros((g, bp, 2 * p), F32)
        s0im = s0re
    else:
        st = jnp.transpose(s0, (3, 0, 2, 1, 4)).reshape(g, n_seq, 2, 2 * p)
        st = jnp.pad(st, ((0, 0), (0, bp - n_seq), (0, 0), (0, 0)))
        s0re, s0im = st[:, :, 0], st[:, :, 1]
    yg, fin = s5_chunked(xg, tmat, win, wre, wim, ar, ai, dg, s0re, s0im, bp=bp)
    y = yg.reshape(g, nc, bp, ell, h)[:, :, :n_seq].transpose(2, 1, 3, 0, 4).reshape(n_seq * t, g * h)
    fin = fin.reshape(g, bp, 2, 2, p)[:, :n_seq]
    return y, jnp.transpose(fin, (1, 3, 2, 0, 4))


def _rope_2d(x, row, col):
    half = HEAD_DIM // 2
    quarter = half // 2
    freqs = ROPE_BASE ** (-jnp.arange(quarter, dtype=F32) / quarter)

    def rotate(seg, pos):
        ang = pos.astype(F32)[:, None] * freqs
        shape = (1, ang.shape[0]) + (1,) * (seg.ndim - 3) + (quarter,)
        cos, sin = jnp.cos(ang).reshape(shape), jnp.sin(ang).reshape(shape)
        s1, s2 = seg[..., :quarter], seg[..., quarter:]
        return jnp.concatenate([s1 * cos - s2 * sin, s2 * cos + s1 * sin], axis=-1)

    return jnp.concatenate([rotate(x[..., :half], row), rotate(x[..., half:], col)], axis=-1)


def _split_qkv(qkv, bn, t):
    q, k, v = jnp.split(qkv, [N_HEADS * HEAD_DIM, (N_HEADS + N_KV) * HEAD_DIM], axis=-1)
    return (q.reshape(bn, t, N_KV, N_HEADS // N_KV, HEAD_DIM),
            k.reshape(bn, t, N_KV, HEAD_DIM), v.reshape(bn, t, N_KV, HEAD_DIM))


def _sink_logits(sink, lead):
    return jnp.broadcast_to(sink.reshape((N_KV, N_HEADS // N_KV) + (1,) * (len(lead) - 2)), tuple(lead) + (1,))


def _attn_context(qkv, bn, l, sink):
    q, k, v = _split_qkv(qkv, bn, l)
    s = jnp.einsum('bqhgd,bkhd->bhgqk', q, k) * (HEAD_DIM ** -0.5)
    snk = _sink_logits(sink, (bn, N_KV, N_HEADS // N_KV, l))
    p = jax.nn.softmax(jnp.concatenate([s, snk], axis=-1), axis=-1)[..., :l]
    o = jnp.einsum('bhgqk,bkhd->bqhgd', p, v)
    return o.reshape(bn * l, N_HEADS * HEAD_DIM), k, v


def _band_blocks(x, nb):
    bn = x.shape[0]
    xp = jnp.pad(x, ((0, 0), (ATT_BLOCK, ATT_BLOCK), (0, 0), (0, 0))).reshape(bn, nb + 2, ATT_BLOCK, N_KV, HEAD_DIM)
    return jnp.concatenate([xp[:, :-2], xp[:, 1:-1], xp[:, 2:]], axis=2)


def _attn_latent(qkv, bn, t, sink, k_ctx, v_ctx):
    rows = t // GRID_W
    row = jnp.repeat(jnp.arange(rows), GRID_W)
    col = jnp.tile(jnp.arange(GRID_W), rows)
    q, k, v = _split_qkv(qkv, bn, t)
    q, k = _rope_2d(q, row, col), _rope_2d(k, row, col)
    nb = t // ATT_BLOCK
    nw = 3 * ATT_BLOCK
    lc = k_ctx.shape[1]
    qpk = N_HEADS // N_KV
    qb = q.reshape(bn, nb, ATT_BLOCK, N_KV, qpk, HEAD_DIM)
    kw, vw = _band_blocks(k, nb), _band_blocks(v, nb)
    scale = HEAD_DIM ** -0.5
    s_loc = jnp.einsum('bnqhgd,bnkhd->bhgnqk', qb, kw) * scale
    qi = jnp.arange(ATT_BLOCK)[:, None]
    kj = jnp.arange(nw)[None, :]
    kpos = jnp.arange(nb)[:, None, None] * ATT_BLOCK - ATT_BLOCK + kj[None]
    mask = (jnp.abs(kj - ATT_BLOCK - qi) <= WINDOW)[None] & (kpos >= 0) & (kpos < t)
    s_loc = jnp.where(mask, s_loc, NEG_INF)
    s_ctx = jnp.einsum('bnqhgd,bkhd->bhgnqk', qb, k_ctx) * scale
    snk = _sink_logits(sink, (bn, N_KV, qpk, nb, ATT_BLOCK))
    p = jax.nn.softmax(jnp.concatenate([s_loc, s_ctx, snk], axis=-1), axis=-1)
    o = (jnp.einsum('bhgnqk,bnkhd->bnqhgd', p[..., :nw], vw)
         + jnp.einsum('bhgnqk,bkhd->bnqhgd', p[..., nw:nw + lc], v_ctx))
    return o.reshape(bn * t, N_HEADS * HEAD_DIM)


def _router_kernel(x_ref, g_ref, sc_ref, sh_ref, rw_ref, h_ref, aff_ref):
    h = _norm_mod(x_ref[...], g_ref[...], sc_ref[...], sh_ref[...])
    h_ref[...] = h.astype(BF16)
    logits = jnp.dot(h, rw_ref[...], precision=HIGHEST, preferred_element_type=F32)
    e = jnp.exp(logits - jnp.max(logits, axis=-1, keepdims=True))
    aff_ref[...] = e / jnp.sum(e, axis=-1, keepdims=True)


def norm_router(x, g, mod, rows, router_w):
    tm = rows.tm
    m, d = x.shape
    ne = router_w.shape[1]
    return pl.pallas_call(
        _router_kernel,
        grid=(m // tm,),
        in_specs=[pl.BlockSpec((tm, d), lambda i: (i, 0)),
                  pl.BlockSpec((1, d), lambda i: (0, 0)),
                  _mod_spec(rows, tm, d, 4, 0),
                  _mod_spec(rows, tm, d, 3, 0),
                  pl.BlockSpec((d, ne), lambda i: (0, 0))],
        out_specs=[pl.BlockSpec((tm, d), lambda i: (i, 0)),
                   pl.BlockSpec((tm, ne), lambda i: (i, 0))],
        out_shape=[jax.ShapeDtypeStruct((m, d), BF16), jax.ShapeDtypeStruct((m, ne), F32)],
        compiler_params=_cparams("arbitrary"),
        name="norm_router",
    )(x, g.reshape(1, d), mod, mod, router_w)


def _moe_kernel(x_ref, wg_ref, wu_ref, wd_ref, gt_ref, o_ref, acc_ref, wg_bf, wu_bf, wd_bf, *, rsub):
    f = pl.program_id(1)
    wg_bf[...] = wg_ref[...].astype(BF16)
    wu_bf[...] = wu_ref[...].astype(BF16)
    wd_bf[...] = wd_ref[...].astype(BF16)
    for r in range(x_ref.shape[0] // rsub):
        rs = slice(r * rsub, (r + 1) * rsub)
        x = x_ref[rs, :]
        hg = jnp.dot(x, wg_bf[...], preferred_element_type=F32)
        hu = jnp.dot(x, wu_bf[...], preferred_element_type=F32)
        he = ((hg * jax.nn.sigmoid(hg)) * hu).astype(BF16)
        part = jnp.dot(he, wd_bf[...], preferred_element_type=F32)

        @pl.when(f == 0)
        def _():
            acc_ref[rs, :] = part

        @pl.when(f > 0)
        def _():
            acc_ref[rs, :] += part

    @pl.when(f == pl.num_programs(1) - 1)
    def _():
        o_ref[...] = acc_ref[...] * gt_ref[...]


def moe_experts(xe, w_gate, w_up, w_down, gates, *, tf=512):
    ne, r, d = xe.shape
    dff = w_gate.shape[2]
    kern = functools.partial(_moe_kernel, rsub=min(MOE_ROW_SUB, r))
    return pl.pallas_call(
        kern,
        grid=(ne, dff // tf),
        in_specs=[pl.BlockSpec((None, r, d), lambda e, f: (e, 0, 0)),
                  pl.BlockSpec((None, d, tf), lambda e, f: (e, 0, f)),
                  pl.BlockSpec((None, d, tf), lambda e, f: (e, 0, f)),
                  pl.BlockSpec((None, tf, d), lambda e, f: (e, f, 0)),
                  pl.BlockSpec((None, r, 1), lambda e, f: (e, 0, 0))],
        out_specs=pl.BlockSpec((None, r, d), lambda e, f: (e, 0, 0)),
        out_shape=jax.ShapeDtypeStruct((ne, r, d), F32),
        scratch_shapes=[pltpu.VMEM((r, d), F32), pltpu.VMEM((d, tf), BF16),
                        pltpu.VMEM((d, tf), BF16), pltpu.VMEM((tf, d), BF16)],
        compiler_params=_cparams("arbitrary", "arbitrary"),
        name="moe_experts",
    )(xe, w_gate, w_up, w_down, gates)


def moe_layer(x, g, mod, rows, router_w, w_gate, w_up, w_down, group_sizes):
    m, d = x.shape
    h2, aff = norm_router(x, g, mod, rows, router_w)
    idx_l, gate_l = [], []
    off = 0
    for n in group_sizes:
        cap = (EC_FACTOR * n) // N_EXPERTS
        gt, ix = lax.top_k(aff[off:off + n].T, cap)
        idx_l.append(ix + off)
        gate_l.append(gt)
        off += n
    idx = jnp.concatenate(idx_l, axis=1)
    gates = jnp.concatenate(gate_l, axis=1)
    xe = h2[idx]
    ye = moe_experts(xe, w_gate, w_up, w_down, gates[..., None])
    out = jnp.zeros((m, d), F32).at[idx.reshape(-1)].add(ye.reshape(-1, d))
    g2 = mod[:, 0, 5 * d:6 * d]
    seg = jnp.concatenate([jnp.zeros((rows.n_prompt,), jnp.int32),
                           1 + jnp.arange(m - rows.n_prompt, dtype=jnp.int32) // rows.t_sample])
    return x + g2[seg] * out


def kernel(x_prompt, x_sample, state_rglru, state_s5, cache_k, cache_v, c, c_ctx, ada_w, ada_b, norm1_g, norm2_g, rg_w_in, rg_conv_w, rg_conv_b, rg_w_a, rg_b_a, rg_w_x, rg_b_x, rg_lambda, rg_w_out, s5_a_re, s5_a_im, s5_log_dt, s5_b_re, s5_b_im, s5_c_re, s5_c_im, s5_d, s5_w_glu, attn_w_qkv, attn_w_o, attn_sink, router_w, moe_w_gate, moe_w_up, moe_w_down, final_norm_g):
    bp_, tp, d = x_prompt.shape
    bs, ts, _ = x_sample.shape
    n_p, n_s = bp_ * tp, bs * ts
    depth = ada_w.shape[0]
    rows = _Rows(n_p, ts)
    assert bs + 1 <= SUBLANES and n_p % ts == 0

    x = jnp.concatenate([x_prompt.reshape(n_p, d), x_sample.reshape(n_s, d)], axis=0)
    cond = jnp.concatenate([c_ctx[None, :], c, jnp.zeros((SUBLANES - 1 - bs, d), F32)], axis=0)
    mod_all = ada_modulation_all(cond, ada_w, ada_b)

    new_rg, new_s5, new_k, new_v = [], [], [], []
    for l in range(depth):
        kind, j = l % 3, l // 3
        mod = mod_all[l].reshape(SUBLANES, 1, 6 * d)
        if kind == 0:
            gu = norm_mod_matmul(x, norm1_g[l], mod, rows, rg_w_in[j], tn=1024)
            args = (rg_conv_w[j], rg_conv_b[j], rg_w_a[j], rg_b_a[j], rg_w_x[j], rg_b_x[j], rg_lambda[j])
            r = gu.shape[1] // 2
            yp, fin = rglru_scan(gu, 0, bp_, tp, *args, jnp.zeros((bp_, 2, r), F32))
            ys, _ = rglru_scan(gu, n_p, bs, ts, *args, state_rglru[:, j])
            new_rg.append(fin)
            x = matmul_gated_residual(jnp.concatenate([yp, ys], axis=0), rg_w_out[j], x, mod, rows)
        elif kind == 1:
            hn = norm_mod(x, norm1_g[l], mod, rows)
            mats = _s5_matrices(s5_a_re[j], s5_a_im[j], s5_log_dt[j], s5_b_re[j], s5_b_im[j],
                                s5_c_re[j], s5_c_im[j], s5_d[j])
            up, st = s5_mixer_group(hn[:n_p], bp_, tp, mats, None)
            us, _ = s5_mixer_group(hn[n_p:], bs, ts, mats, state_s5[:, j])
            new_s5.append(st)
            x = glu_gated_residual(jnp.concatenate([up, us], axis=0), s5_w_glu[j], x, mod, rows)
        else:
            qkv = norm_mod_matmul(x, norm1_g[l], mod, rows, attn_w_qkv[j], tn=512)
            op, kc, vc = _attn_context(qkv[:n_p], bp_, tp, attn_sink[j])
            new_k.append(kc)
            new_v.append(vc)
            os_ = _attn_latent(qkv[n_p:], bs, ts, attn_sink[j], cache_k[:, j], cache_v[:, j])
            x = matmul_gated_residual(jnp.concatenate([op, os_], axis=0), attn_w_o[j], x, mod, rows)
        x = moe_layer(x, norm2_g[l], mod, rows, router_w[l], moe_w_gate[l], moe_w_up[l], moe_w_down[l],
                      (n_p, n_s))

    y = final_norm(x, final_norm_g, rows.tm)
    return (y[:n_p].reshape(bp_, tp, d), y[n_p:].reshape(bs, ts, d),
            jnp.stack(new_rg, axis=1), jnp.stack(new_s5, axis=1),
            jnp.stack(new_k, axis=1), jnp.stack(new_v, axis=1))
```

```python
import functools
import math

import jax
import jax.numpy as jnp
from jax import lax
from jax.experimental import pallas as pl
from jax.experimental.pallas import tpu as pltpu

F32 = jnp.float32
BF16 = jnp.bfloat16
HIGHEST = lax.Precision.HIGHEST

EPS = 1e-6
RG_C = 8.0
RG_BS = 128
S5_H = 16
S5_L = 16
N_HEADS = 16
N_KV = 4
Q_PER_KV = N_HEADS // N_KV
HEAD_DIM = 64
KV_W = N_KV * HEAD_DIM
GRID_W = 64
WINDOW = 128
ATT_BLOCK = 128
ROPE_BASE = 10000.0
ATT_SCALE = HEAD_DIM ** -0.5
NEG_INF = -1e30
N_EXPERTS = 16
EC_FACTOR = 2
SUBLANES = 8
ROW_TILE = 512
MOE_ROW_SUB = 256
VMEM_LIMIT = 56 * 1024 * 1024
NT_DIMS = (((1,), (1,)), ((), ()))


def _cparams(*sem):
    return pltpu.CompilerParams(dimension_semantics=sem, vmem_limit_bytes=VMEM_LIMIT)


def _gelu(x):
    return x * (0.5 * (1.0 + jnp.tanh(math.sqrt(2.0 / math.pi) * (x + 0.044715 * (x * x * x)))))


def _norm_mod(x, g, sc, sh):
    ms = jnp.mean(x * x, axis=-1, keepdims=True)
    return ((x * lax.rsqrt(ms + EPS)) * g) * (1.0 + sc) + sh


def _mod_kernel(c_ref, w_ref, b_ref, o_ref):
    c = c_ref[...]
    s = (c * jax.nn.sigmoid(c)).astype(BF16)
    o_ref[...] = jnp.dot(s, w_ref[...].astype(BF16), preferred_element_type=F32) + b_ref[...]


def ada_modulation_all(cond, ada_w, ada_b):
    n_layers, d, n = ada_w.shape
    tn = 1536
    return pl.pallas_call(
        _mod_kernel,
        grid=(n_layers, n // tn),
        in_specs=[pl.BlockSpec((SUBLANES, d), lambda l, j: (0, 0)),
                  pl.BlockSpec((None, d, tn), lambda l, j: (l, 0, j)),
                  pl.BlockSpec((None, 1, tn), lambda l, j: (l, 0, j))],
        out_specs=pl.BlockSpec((None, SUBLANES, tn), lambda l, j: (l, 0, j)),
        out_shape=jax.ShapeDtypeStruct((n_layers, SUBLANES, n), F32),
        compiler_params=_cparams("arbitrary", "arbitrary"),
        name="ada_mod",
    )(cond, ada_w, ada_b.reshape(n_layers, 1, n))


class _Rows:
    def __init__(self, n_prompt, n_sample, t_sample):
        self.n_prompt = n_prompt
        self.n_sample = n_sample
        self.t_sample = t_sample
        self.tm = min(ROW_TILE, n_prompt, t_sample)
        assert n_prompt % self.tm == 0 and t_sample % self.tm == 0
        self.prompt_blocks = n_prompt // self.tm
        self.sample_blocks = n_sample // self.tm

    def seg(self, i):
        r = i * self.tm
        return jnp.where(r < self.n_prompt, 0, 1 + lax.div(r - self.n_prompt, self.t_sample))


def _mod_spec(rows, layer, width, chunk, m_axis, chunk_axis=None):
    def imap(*ids):
        c = chunk if chunk_axis is None else chunk + ids[chunk_axis]
        return (layer * SUBLANES + rows.seg(ids[m_axis]), 0, c)
    return pl.BlockSpec((None, 1, width), imap)


def _gain_spec(layer, d):
    return pl.BlockSpec((None, 1, d), lambda *ids: (layer, 0, 0))


def _nm_kernel(x_ref, g_ref, sc_ref, sh_ref, w_ref, o_ref, wbf_ref):
    @pl.when(pl.program_id(1) == 0)
    def _():
        wbf_ref[...] = w_ref[...].astype(BF16)
    h = _norm_mod(x_ref[...], g_ref[...], sc_ref[...], sh_ref[...])
    o_ref[...] = jnp.dot(h.astype(BF16), wbf_ref[...], preferred_element_type=F32)


def norm_mod_matmul(x, gains, mod, rows, layer, w, wl, *, tn):
    m, d = x.shape
    n = w.shape[2]
    tm = rows.tm
    return pl.pallas_call(
        _nm_kernel,
        grid=(n // tn, m // tm),
        in_specs=[pl.BlockSpec((tm, d), lambda j, i: (i, 0)),
                  _gain_spec(layer, d),
                  _mod_spec(rows, layer, d, 1, 1),
                  _mod_spec(rows, layer, d, 0, 1),
                  pl.BlockSpec((None, d, tn), lambda j, i: (wl, 0, j))],
        out_specs=pl.BlockSpec((tm, tn), lambda j, i: (i, j)),
        out_shape=jax.ShapeDtypeStruct((m, n), F32),
        scratch_shapes=[pltpu.VMEM((d, tn), BF16)],
        compiler_params=_cparams("arbitrary", "arbitrary"),
        name="norm_mod_matmul",
    )(x, gains, mod, mod, w)


def _norm_only_kernel(x_ref, g_ref, sc_ref, sh_ref, o_ref):
    o_ref[...] = _norm_mod(x_ref[...], g_ref[...], sc_ref[...], sh_ref[...])


def norm_mod(x, gains, mod, rows, layer):
    m, d = x.shape
    tm = rows.tm
    return pl.pallas_call(
        _norm_only_kernel,
        grid=(m // tm,),
        in_specs=[pl.BlockSpec((tm, d), lambda i: (i, 0)),
                  _gain_spec(layer, d),
                  _mod_spec(rows, layer, d, 1, 0),
                  _mod_spec(rows, layer, d, 0, 0)],
        out_specs=pl.BlockSpec((tm, d), lambda i: (i, 0)),
        out_shape=jax.ShapeDtypeStruct((m, d), F32),
        compiler_params=_cparams("arbitrary"),
        name="norm_mod",
    )(x, gains, mod, mod)


def _final_norm_kernel(x_ref, g_ref, o_ref):
    x = x_ref[...]
    ms = jnp.mean(x * x, axis=-1, keepdims=True)
    o_ref[...] = (x * lax.rsqrt(ms + EPS)) * g_ref[...]


def final_norm(x, g, tm):
    m, d = x.shape
    return pl.pallas_call(
        _final_norm_kernel,
        grid=(m // tm,),
        in_specs=[pl.BlockSpec((tm, d), lambda i: (i, 0)),
                  pl.BlockSpec((1, d), lambda i: (0, 0))],
        out_specs=pl.BlockSpec((tm, d), lambda i: (i, 0)),
        out_shape=jax.ShapeDtypeStruct((m, d), F32),
        compiler_params=_cparams("arbitrary"),
        name="final_norm",
    )(x, g.reshape(1, d))


def _two_group_specs(rows, k):
    npb, nsb, tm = rows.prompt_blocks, rows.sample_blocks, rows.tm
    return [pl.BlockSpec((tm, k), lambda j, i: (jnp.minimum(i, npb - 1), 0)),
            pl.BlockSpec((tm, k), lambda j, i: (jnp.clip(i - npb, 0, nsb - 1), 0))]


def _mmres_kernel(ap_ref, as_ref, w_ref, r_ref, gt_ref, o_ref, wbf_ref, *, npb):
    i = pl.program_id(1)

    @pl.when(i == 0)
    def _():
        wbf_ref[...] = w_ref[...].astype(BF16)

    def emit(a_ref):
        acc = jnp.dot(a_ref[...].astype(BF16), wbf_ref[...], preferred_element_type=F32)
        o_ref[...] = r_ref[...] + gt_ref[...] * acc

    pl.when(i < npb)(lambda: emit(ap_ref))
    pl.when(i >= npb)(lambda: emit(as_ref))


def matmul_gated_residual(a_p, a_s, w, wl, resid, mod, rows, layer, *, tn=512):
    k = a_p.shape[1]
    m, d = resid.shape
    tm = rows.tm
    nt = d // tn
    return pl.pallas_call(
        functools.partial(_mmres_kernel, npb=rows.prompt_blocks),
        grid=(nt, m // tm),
        in_specs=_two_group_specs(rows, k) + [
            pl.BlockSpec((None, k, tn), lambda j, i: (wl, 0, j)),
            pl.BlockSpec((tm, tn), lambda j, i: (i, j)),
            _mod_spec(rows, layer, tn, 2 * nt, 1, chunk_axis=0)],
        out_specs=pl.BlockSpec((tm, tn), lambda j, i: (i, j)),
        out_shape=jax.ShapeDtypeStruct((m, d), F32),
        scratch_shapes=[pltpu.VMEM((k, tn), BF16)],
        compiler_params=_cparams("arbitrary", "arbitrary"),
        name="matmul_gated_residual",
    )(a_p, a_s, w, resid, mod)


def _glures_kernel(ap_ref, as_ref, wv_ref, wg_ref, r_ref, gt_ref, o_ref, wv_bf, wg_bf, *, npb):
    i = pl.program_id(1)

    @pl.when(i == 0)
    def _():
        wv_bf[...] = wv_ref[...].astype(BF16)
        wg_bf[...] = wg_ref[...].astype(BF16)

    def emit(a_ref):
        a = a_ref[...].astype(BF16)
        v = jnp.dot(a, wv_bf[...], preferred_element_type=F32)
        g = jnp.dot(a, wg_bf[...], preferred_element_type=F32)
        o_ref[...] = r_ref[...] + gt_ref[...] * (v * jax.nn.sigmoid(g))

    pl.when(i < npb)(lambda: emit(ap_ref))
    pl.when(i >= npb)(lambda: emit(as_ref))


def glu_gated_residual(a_p, a_s, w_glu, wl, resid, mod, rows, layer, *, tn=512):
    k = a_p.shape[1]
    m, d = resid.shape
    tm = rows.tm
    nt = d // tn
    return pl.pallas_call(
        functools.partial(_glures_kernel, npb=rows.prompt_blocks),
        grid=(nt, m // tm),
        in_specs=_two_group_specs(rows, k) + [
            pl.BlockSpec((None, k, tn), lambda j, i: (wl, 0, j)),
            pl.BlockSpec((None, k, tn), lambda j, i: (wl, 0, nt + j)),
            pl.BlockSpec((tm, tn), lambda j, i: (i, j)),
            _mod_spec(rows, layer, tn, 2 * nt, 1, chunk_axis=0)],
        out_specs=pl.BlockSpec((tm, tn), lambda j, i: (i, j)),
        out_shape=jax.ShapeDtypeStruct((m, d), F32),
        scratch_shapes=[pltpu.VMEM((k, tn), BF16), pltpu.VMEM((k, tn), BF16)],
        compiler_params=_cparams("arbitrary", "arbitrary"),
        name="glu_gated_residual",
    )(a_p, a_s, w_glu, w_glu, resid, mod)


def _rglru_kernel(gate_ref, u_ref, cw_ref, cb_ref, wa_ref, ba_ref, wx_ref, bx_ref, lam_ref, h0_ref,
                  y_ref, fin_ref, af_s, bf_s, ab_s, bb_s, hf_s, hb_s):
    t, cw = u_ref.shape
    u = u_ref[...]
    row = lax.broadcasted_iota(jnp.int32, (t, cw), 0)

    def shifted(x, k):
        if k > 0:
            return jnp.where(row >= k, pltpu.roll(x, k, axis=0), 0.0)
        return jnp.where(row < t + k, pltpu.roll(x, t + k, axis=0), 0.0)

    cwv = cw_ref[...]
    uc = (cwv[0:1] * shifted(u, 2) + cwv[1:2] * shifted(u, 1) + cwv[2:3] * u
          + cwv[3:4] * shifted(u, -1) + cb_ref[...])

    a_scr = (af_s, ab_s)
    b_scr = (bf_s, bb_s)
    for k in range(2):
        nl = -lam_ref[k:k + 1, :]
        sp = jnp.maximum(nl, 0.0) + jnp.log1p(jnp.exp(-jnp.abs(nl)))
        for hh in range(cw // RG_BS):
            sl = slice(hh * RG_BS, (hh + 1) * RG_BS)
            uh = uc[:, sl]
            ub = uh.astype(BF16)
            r = jax.nn.sigmoid(jnp.dot(ub, wa_ref[k, hh].astype(BF16), preferred_element_type=F32)
                               + ba_ref[k:k + 1, sl])
            i = jax.nn.sigmoid(jnp.dot(ub, wx_ref[k, hh].astype(BF16), preferred_element_type=F32)
                               + bx_ref[k:k + 1, sl])
            log_a = (-RG_C * r) * sp[:, sl]
            a = jnp.exp(log_a)
            a_scr[k][:, sl] = a
            b_scr[k][:, sl] = jnp.sqrt(jnp.tanh(-log_a) * (a * a + 1.0)) * (i * uh)

    nblk = t // SUBLANES
    srow = lax.broadcasted_iota(jnp.int32, (SUBLANES, cw), 0)

    def body(n, carry):
        cf, cb = carry
        rf = pl.multiple_of(n * SUBLANES, SUBLANES)
        rb = pl.multiple_of((nblk - 1 - n) * SUBLANES, SUBLANES)
        a = af_s[pl.ds(rf, SUBLANES), :]
        b = bf_s[pl.ds(rf, SUBLANES), :]
        a2 = ab_s[pl.ds(rb, SUBLANES), :]
        b2 = bb_s[pl.ds(rb, SUBLANES), :]
        for s in (1, 2, 4):
            m = srow >= s
            b = jnp.where(m, a * pltpu.roll(b, s, axis=0) + b, b)
            a = jnp.where(m, a * pltpu.roll(a, s, axis=0), a)
            m2 = srow < SUBLANES - s
            b2 = jnp.where(m2, a2 * pltpu.roll(b2, SUBLANES - s, axis=0) + b2, b2)
            a2 = jnp.where(m2, a2 * pltpu.roll(a2, SUBLANES - s, axis=0), a2)
        hf = a * cf + b
        hb = a2 * cb + b2
        hf_s[pl.ds(rf, SUBLANES), :] = hf
        hb_s[pl.ds(rb, SUBLANES), :] = hb
        return hf[SUBLANES - 1:SUBLANES, :], hb[0:1, :]

    cf, cb = lax.fori_loop(0, nblk, body, (h0_ref[0:1, :], h0_ref[1:2, :]))
    fin_ref[0:1, :] = cf
    fin_ref[1:2, :] = cb
    y_ref[...] = (hf_s[...] + hb_s[...]) * _gelu(gate_ref[...])


def rglru_scan(gu, row0, n_seq, t, j, conv_w, conv_b, w_a, b_a, w_x, b_x, lam, h0, h0_j, *, cw=256):
    r = gu.shape[1] // 2
    nh = cw // RG_BS
    blk0 = row0 // t
    nc = r // cw
    scr = [pltpu.VMEM((t, cw), F32) for _ in range(6)]
    vec2 = pl.BlockSpec((None, 2, cw), lambda b, c: (j, 0, c))
    gatew = pl.BlockSpec((None, 2, nh, RG_BS, RG_BS), lambda b, c: (j, 0, c, 0, 0))
    return pl.pallas_call(
        _rglru_kernel,
        grid=(n_seq, nc),
        in_specs=[pl.BlockSpec((t, cw), lambda b, c: (blk0 + b, c)),
                  pl.BlockSpec((t, cw), lambda b, c: (blk0 + b, nc + c)),
                  pl.BlockSpec((None, 4, cw), lambda b, c: (j, 0, c)),
                  pl.BlockSpec((None, 1, cw), lambda b, c: (j, 0, c)),
                  gatew, vec2, gatew, vec2, vec2,
                  pl.BlockSpec((None, None, 2, cw), lambda b, c: (b, h0_j, 0, c))],
        out_specs=[pl.BlockSpec((t, cw), lambda b, c: (b, c)),
                   pl.BlockSpec((None, 2, cw), lambda b, c: (b, 0, c))],
        out_shape=[jax.ShapeDtypeStruct((n_seq * t, r), F32),
                   jax.ShapeDtypeStruct((n_seq, 2, r), F32)],
        scratch_shapes=scr,
        compiler_params=_cparams("arbitrary", "arbitrary"),
        name="rglru_scan",
    )(gu, gu, conv_w, conv_b.reshape(conv_b.shape[0], 1, r), w_a, b_a, w_x, b_x, lam, h0)


def _s5_kernel(x_ref, tm_ref, win_ref, wre_ref, wim_ref, ar_ref, ai_ref, d_ref, s0re_ref, s0im_ref,
               y_ref, fin_ref, ure_s, uim_s, fre_s, fim_s, bre_s, bim_s, *, bp):
    m = x_ref.shape[0]
    nc = m // bp
    half = ure_s.shape[1] // 2
    x = x_ref[...]
    xb = x.astype(BF16)
    u = jnp.dot(xb, win_ref[...].astype(BF16), preferred_element_type=F32)
    ure_s[...] = u[:, :2 * half]
    uim_s[...] = u[:, 2 * half:]
    is_fwd = lax.broadcasted_iota(jnp.int32, (bp, 2 * half), 1) < half
    ar = ar_ref[...]
    ai = ai_ref[...]

    def body(k, carry):
        re, im = carry
        rf = pl.multiple_of(k * bp, bp)
        rb = pl.multiple_of((nc - 1 - k) * bp, bp)
        fre_s[pl.ds(rf, bp), :] = re
        fim_s[pl.ds(rf, bp), :] = im
        bre_s[pl.ds(rb, bp), :] = re
        bim_s[pl.ds(rb, bp), :] = im
        ure = jnp.where(is_fwd, ure_s[pl.ds(rf, bp), :], ure_s[pl.ds(rb, bp), :])
        uim = jnp.where(is_fwd, uim_s[pl.ds(rf, bp), :], uim_s[pl.ds(rb, bp), :])
        return ar * re - ai * im + ure, ar * im + ai * re + uim

    re, im = lax.fori_loop(0, nc, body, (s0re_ref[...], s0im_ref[...]))
    fin_ref[:, :2 * half] = re
    fin_ref[:, 2 * half:] = im
    fwd_all = lax.broadcasted_iota(jnp.int32, (m, 2 * half), 1) < half
    hre = jnp.where(fwd_all, fre_s[...], bre_s[...]).astype(BF16)
    him = jnp.where(fwd_all, fim_s[...], bim_s[...]).astype(BF16)
    y = (jnp.dot(xb, tm_ref[...].astype(BF16), preferred_element_type=F32)
         + jnp.dot(hre, wre_ref[...].astype(BF16), preferred_element_type=F32)
         + jnp.dot(him, wim_ref[...].astype(BF16), preferred_element_type=F32)
         + d_ref[...] * x)
    y_ref[...] = _gelu(y)


def s5_chunked(xg, tmat, win, wre, wim, ar, ai, dg, s0re, s0im, *, bp):
    g, m, w = xg.shape
    p2 = ar.shape[-1]
    kern = functools.partial(_s5_kernel, bp=bp)
    per_g = lambda shape: pl.BlockSpec((None,) + shape, lambda i: (i, 0, 0))
    return pl.pallas_call(
        kern,
        grid=(g,),
        in_specs=[per_g((m, w)), per_g((w, w)), per_g((w, 2 * p2)), per_g((p2, w)), per_g((p2, w)),
                  per_g((1, p2)), per_g((1, p2)), per_g((1, w)), per_g((bp, p2)), per_g((bp, p2))],
        out_specs=[per_g((m, w)), per_g((bp, 2 * p2))],
        out_shape=[jax.ShapeDtypeStruct((g, m, w), F32), jax.ShapeDtypeStruct((g, bp, 2 * p2), F32)],
        scratch_shapes=[pltpu.VMEM((m, p2), F32) for _ in range(6)],
        compiler_params=_cparams("arbitrary"),
        name="s5_chunked",
    )(xg, tmat, win, wre, wim, ar, ai, dg, s0re, s0im)


def _cmul(ar, ai, br, bi):
    return ar * br - ai * bi, ar * bi + ai * br


def _s5_matrices(a_re, a_im, log_dt, b_re, b_im, c_re, c_im, d):
    _, g, p = a_re.shape
    h = b_re.shape[-1]
    ell = S5_L
    dt = jnp.exp(log_dt)[:, :, None]
    steps = jnp.arange(ell + 1, dtype=F32)[:, None, None, None]
    mag = jnp.exp(steps * (a_re * dt)[None])
    ang = steps * (a_im * dt)[None]
    amr, ami = mag * jnp.cos(ang), mag * jnp.sin(ang)
    nr, ni = amr[1] - 1.0, ami[1]
    den = a_re * a_re + a_im * a_im
    qr, qi = (nr * a_re + ni * a_im) / den, (ni * a_re - nr * a_im) / den
    bbr, bbi = _cmul(qr[..., None], qi[..., None], b_re, b_im)

    def win_dir(k, pr, pi):
        wr, wi = _cmul(pr[..., None], pi[..., None], bbr[k][None], bbi[k][None])
        return jnp.transpose(wr, (1, 0, 3, 2)), jnp.transpose(wi, (1, 0, 3, 2))

    wfr, wfi = win_dir(0, jnp.flip(amr[:ell, 0], 0), jnp.flip(ami[:ell, 0], 0))
    wbr, wbi = win_dir(1, amr[:ell, 1], ami[:ell, 1])
    win = jnp.concatenate([wfr, wbr, wfi, wbi], axis=-1).reshape(g, ell * h, 4 * p)

    def c_pow(k):
        return _cmul(c_re[k][None], c_im[k][None], amr[:, k][:, :, None, :], ami[:, k][:, :, None, :])

    cfr, cfi = c_pow(0)
    cbr, cbi = c_pow(1)
    to_gplh = lambda z: jnp.transpose(z, (1, 3, 0, 2))
    wre = jnp.concatenate([to_gplh(cfr[1:]), to_gplh(jnp.flip(cbr[1:], 0))], axis=1).reshape(g, 2 * p, ell * h)
    wim = jnp.concatenate([-to_gplh(cfi[1:]), -to_gplh(jnp.flip(cbi[1:], 0))], axis=1).reshape(g, 2 * p, ell * h)

    def lag_kernel(zr, zi, k):
        return (jnp.einsum('mghp,gpk->mghk', zr[:ell], bbr[k], precision=HIGHEST)
                - jnp.einsum('mghp,gpk->mghk', zi[:ell], bbi[k], precision=HIGHEST))

    kf = lag_kernel(cfr, cfi, 0)
    kb = lag_kernel(cbr, cbi, 1)
    li = jnp.arange(ell)
    lag = li[None, :] - li[:, None]
    tf = jnp.where((lag >= 0)[:, :, None, None, None], kf[jnp.clip(lag, 0, ell - 1)], 0.0)
    tb = jnp.where((lag <= 0)[:, :, None, None, None], kb[jnp.clip(-lag, 0, ell - 1)], 0.0)
    tmat = jnp.transpose(tf + tb, (2, 0, 4, 1, 3)).reshape(g, ell * h, ell * h)
    ar = jnp.concatenate([amr[ell, 0], amr[ell, 1]], axis=-1)[:, None, :]
    ai = jnp.concatenate([ami[ell, 0], ami[ell, 1]], axis=-1)[:, None, :]
    dg = jnp.tile(d.reshape(g, 1, h), (1, ell, 1)).reshape(g, 1, ell * h)
    return tmat, win, wre, wim, ar, ai, dg


def s5_mixer_group(hn, n_seq, t, mats, s0):
    tmat, win, wre, wim, ar, ai, dg = mats
    g = tmat.shape[0]
    p = ar.shape[-1] // 2
    ell, h = S5_L, S5_H
    nc = t // ell
    bp = -(-n_seq // SUBLANES) * SUBLANES
    xg = hn.reshape(n_seq, nc, ell, g, h).transpose(3, 1, 0, 2, 4)
    xg = jnp.pad(xg, ((0, 0), (0, 0), (0, bp - n_seq), (0, 0), (0, 0))).reshape(g, nc * bp, ell * h)
    if s0 is None:
        s0re = jnp.zeros((g, bp, 2 * p), F32)
        s0im = s0re
    else:
        st = jnp.transpose(s0, (3, 0, 2, 1, 4)).reshape(g, n_seq, 2, 2 * p)
        st = jnp.pad(st, ((0, 0), (0, bp - n_seq), (0, 0), (0, 0)))
        s0re, s0im = st[:, :, 0], st[:, :, 1]
    yg, fin = s5_chunked(xg, tmat, win, wre, wim, ar, ai, dg, s0re, s0im, bp=bp)
    y = yg.reshape(g, nc, bp, ell, h)[:, :, :n_seq].transpose(2, 1, 3, 0, 4).reshape(n_seq * t, g * h)
    fin = fin.reshape(g, bp, 2, 2, p)[:, :n_seq]
    return y, jnp.transpose(fin, (1, 3, 2, 0, 4))


def _softmax_pv(scores, values, sink):
    m = sink
    for s in scores:
        m = jnp.maximum(m, jnp.max(s, axis=-1, keepdims=True))
    den = jnp.exp(sink - m)
    acc = None
    for s, v in zip(scores, values):
        p = jnp.exp(s - m)
        den = den + jnp.sum(p, axis=-1, keepdims=True)
        pv = jnp.dot(p.astype(BF16), v, preferred_element_type=F32)
        acc = pv if acc is None else acc + pv
    return acc / den


def _attn_prompt_kernel(sink_ref, q_ref, k_ref, v_ref, o_ref):
    k = k_ref[...].astype(BF16)
    v = v_ref[...].astype(BF16)
    for h in range(N_KV):
        hs = slice(h * HEAD_DIM, (h + 1) * HEAD_DIM)
        kh, vh = k[:, hs], v[:, hs]
        for g in range(Q_PER_KV):
            c0 = (h * Q_PER_KV + g) * HEAD_DIM
            qg = q_ref[:, c0:c0 + HEAD_DIM].astype(BF16)
            s = lax.dot_general(qg, kh, NT_DIMS, preferred_element_type=F32) * ATT_SCALE
            o_ref[:, c0:c0 + HEAD_DIM] = _softmax_pv([s], [vh], sink_ref[h * Q_PER_KV + g])


def attn_prompt(qkv, sink, n_seq, t):
    dq = N_HEADS * HEAD_DIM
    kcol = dq // KV_W
    return pl.pallas_call(
        _attn_prompt_kernel,
        grid=(n_seq,),
        in_specs=[pl.BlockSpec(memory_space=pltpu.SMEM),
                  pl.BlockSpec((t, dq), lambda b: (b, 0)),
                  pl.BlockSpec((t, KV_W), lambda b: (b, kcol)),
                  pl.BlockSpec((t, KV_W), lambda b: (b, kcol + 1))],
        out_specs=pl.BlockSpec((t, dq), lambda b: (b, 0)),
        out_shape=jax.ShapeDtypeStruct((n_seq * t, dq), F32),
        compiler_params=_cparams("arbitrary"),
        name="attn_prompt",
    )(sink, qkv, qkv, qkv)


def _rope(x, cos, sin):
    w = x.shape[1]
    low = (lax.broadcasted_iota(jnp.int32, x.shape, 1) & (HEAD_DIM // 4)) == 0
    partner = jnp.where(low, pltpu.roll(x, w - HEAD_DIM // 4, axis=1), pltpu.roll(x, HEAD_DIM // 4, axis=1))
    return x * cos + partner * sin


def _attn_sample_kernel(sink_ref, q_ref, k_ref, v_ref, kc_ref, vc_ref, cos_ref, sin_ref, o_ref,
                        kw_s, vw_s, kc_s, vc_s, *, t):
    n = pl.program_id(1)
    blk = ATT_BLOCK

    @pl.when(n == 0)
    def _():
        zeros = jnp.zeros((blk, KV_W), BF16)
        kw_s[0:blk, :] = zeros
        vw_s[0:blk, :] = zeros
        kw_s[blk + t:2 * blk + t, :] = zeros
        vw_s[blk + t:2 * blk + t, :] = zeros
        kw_s[blk:blk + t, :] = _rope(k_ref[...], cos_ref[...], sin_ref[...]).astype(BF16)
        vw_s[blk:blk + t, :] = v_ref[...].astype(BF16)
        kc_s[...] = kc_ref[...].astype(BF16)
        vc_s[...] = vc_ref[...].astype(BF16)

    r0 = pl.multiple_of(n * blk, blk)
    cq = cos_ref[pl.ds(r0, blk), :]
    sq = sin_ref[pl.ds(r0, blk), :]
    kw = kw_s[pl.ds(r0, 3 * blk), :]
    vw = vw_s[pl.ds(r0, 3 * blk), :]
    qi = lax.broadcasted_iota(jnp.int32, (blk, 3 * blk), 0)
    kj = lax.broadcasted_iota(jnp.int32, (blk, 3 * blk), 1)
    kpos = n * blk - blk + kj
    valid = (jnp.abs(kj - blk - qi) <= WINDOW) & (kpos >= 0) & (kpos < t)
    for h in range(N_KV):
        hs = slice(h * HEAD_DIM, (h + 1) * HEAD_DIM)
        qh = _rope(q_ref[:, h * KV_W:(h + 1) * KV_W], cq, sq).astype(BF16)
        kh, vh, kch, vch = kw[:, hs], vw[:, hs], kc_s[:, hs], vc_s[:, hs]
        for g in range(Q_PER_KV):
            qg = qh[:, g * HEAD_DIM:(g + 1) * HEAD_DIM]
            s_loc = lax.dot_general(qg, kh, NT_DIMS, preferred_element_type=F32) * ATT_SCALE
            s_loc = jnp.where(valid, s_loc, NEG_INF)
            s_ctx = lax.dot_general(qg, kch, NT_DIMS, preferred_element_type=F32) * ATT_SCALE
            c0 = (h * Q_PER_KV + g) * HEAD_DIM
            o_ref[:, c0:c0 + HEAD_DIM] = _softmax_pv([s_loc, s_ctx], [vh, vch], sink_ref[h * Q_PER_KV + g])


def _rope_tables(t):
    quarter = HEAD_DIM // 4
    freqs = ROPE_BASE ** (-jnp.arange(quarter, dtype=F32) / quarter)
    pos = jnp.arange(t)
    ang_r = (pos // GRID_W).astype(F32)[:, None] * freqs
    ang_c = (pos % GRID_W).astype(F32)[:, None] * freqs
    cos = jnp.concatenate([jnp.cos(ang_r), jnp.cos(ang_r), jnp.cos(ang_c), jnp.cos(ang_c)], axis=-1)
    sin = jnp.concatenate([-jnp.sin(ang_r), jnp.sin(ang_r), -jnp.sin(ang_c), jnp.sin(ang_c)], axis=-1)
    return jnp.tile(cos, (1, N_KV)), jnp.tile(sin, (1, N_KV))


def attn_sample(qkv, row0, sink, n_seq, t, k_ctx, v_ctx):
    dq = N_HEADS * HEAD_DIM
    kcol = dq // KV_W
    nb = t // ATT_BLOCK
    lc = k_ctx.shape[1]
    cos, sin = _rope_tables(t)
    qblk0, sblk0 = row0 // ATT_BLOCK, row0 // t
    return pl.pallas_call(
        functools.partial(_attn_sample_kernel, t=t),
        grid=(n_seq, nb),
        in_specs=[pl.BlockSpec(memory_space=pltpu.SMEM),
                  pl.BlockSpec((ATT_BLOCK, dq), lambda b, n: (qblk0 + b * nb + n, 0)),
                  pl.BlockSpec((t, KV_W), lambda b, n: (sblk0 + b, kcol)),
                  pl.BlockSpec((t, KV_W), lambda b, n: (sblk0 + b, kcol + 1)),
                  pl.BlockSpec((None, lc, KV_W), lambda b, n: (b, 0, 0)),
                  pl.BlockSpec((None, lc, KV_W), lambda b, n: (b, 0, 0)),
                  pl.BlockSpec((t, KV_W), lambda b, n: (0, 0)),
                  pl.BlockSpec((t, KV_W), lambda b, n: (0, 0))],
        out_specs=pl.BlockSpec((ATT_BLOCK, dq), lambda b, n: (b * nb + n, 0)),
        out_shape=jax.ShapeDtypeStruct((n_seq * t, dq), F32),
        scratch_shapes=[pltpu.VMEM((t + 2 * ATT_BLOCK, KV_W), BF16), pltpu.VMEM((t + 2 * ATT_BLOCK, KV_W), BF16),
                        pltpu.VMEM((lc, KV_W), BF16), pltpu.VMEM((lc, KV_W), BF16)],
        compiler_params=_cparams("arbitrary", "arbitrary"),
        name="attn_sample",
    )(sink, qkv, qkv, qkv, k_ctx, v_ctx, cos, sin)


def _router_kernel(x_ref, g_ref, sc_ref, sh_ref, rw_ref, h_ref, aff_ref):
    h = _norm_mod(x_ref[...], g_ref[...], sc_ref[...], sh_ref[...])
    h_ref[...] = h.astype(BF16)
    logits = jnp.dot(h, rw_ref[...], precision=HIGHEST, preferred_element_type=F32)
    e = jnp.exp(logits - jnp.max(logits, axis=-1, keepdims=True))
    aff_ref[...] = e / jnp.sum(e, axis=-1, keepdims=True)


def norm_router(x, gains, mod, rows, layer, router_w):
    m, d = x.shape
    ne = router_w.shape[2]
    tm = rows.tm
    return pl.pallas_call(
        _router_kernel,
        grid=(m // tm,),
        in_specs=[pl.BlockSpec((tm, d), lambda i: (i, 0)),
                  _gain_spec(layer, d),
                  _mod_spec(rows, layer, d, 4, 0),
                  _mod_spec(rows, layer, d, 3, 0),
                  pl.BlockSpec((None, d, ne), lambda i: (layer, 0, 0))],
        out_specs=[pl.BlockSpec((tm, d), lambda i: (i, 0)),
                   pl.BlockSpec((tm, ne), lambda i: (i, 0))],
        out_shape=[jax.ShapeDtypeStruct((m, d), BF16), jax.ShapeDtypeStruct((m, ne), F32)],
        compiler_params=_cparams("arbitrary"),
        name="norm_router",
    )(x, gains, mod, mod, router_w)


def _moe_kernel(x_ref, wg_ref, wu_ref, wd_ref, gt_ref, o_ref, acc_ref, wg_bf, wu_bf, wd_bf, *, rsub):
    f = pl.program_id(1)
    wg_bf[...] = wg_ref[...].astype(BF16)
    wu_bf[...] = wu_ref[...].astype(BF16)
    wd_bf[...] = wd_ref[...].astype(BF16)
    for r in range(x_ref.shape[0] // rsub):
        rs = slice(r * rsub, (r + 1) * rsub)
        x = x_ref[rs, :]
        hg = jnp.dot(x, wg_bf[...], preferred_element_type=F32)
        hu = jnp.dot(x, wu_bf[...], preferred_element_type=F32)
        he = ((hg * jax.nn.sigmoid(hg)) * hu).astype(BF16)
        part = jnp.dot(he, wd_bf[...], preferred_element_type=F32)

        @pl.when(f == 0)
        def _():
            acc_ref[rs, :] = part

        @pl.when(f > 0)
        def _():
            acc_ref[rs, :] += part

    @pl.when(f == pl.num_programs(1) - 1)
    def _():
        o_ref[...] = acc_ref[...] * gt_ref[...]


def moe_experts(xe, w_gate, w_up, w_down, layer, gates, *, tf=512):
    ne, r, d = xe.shape
    dff = w_gate.shape[3]
    kern = functools.partial(_moe_kernel, rsub=min(MOE_ROW_SUB, r))
    return pl.pallas_call(
        kern,
        grid=(ne, dff // tf),
        in_specs=[pl.BlockSpec((None, r, d), lambda e, f: (e, 0, 0)),
                  pl.BlockSpec((None, None, d, tf), lambda e, f: (layer, e, 0, f)),
                  pl.BlockSpec((None, None, d, tf), lambda e, f: (layer, e, 0, f)),
                  pl.BlockSpec((None, None, tf, d), lambda e, f: (layer, e, f, 0)),
                  pl.BlockSpec((None, r, 1), lambda e, f: (e, 0, 0))],
        out_specs=pl.BlockSpec((None, r, d), lambda e, f: (e, 0, 0)),
        out_shape=jax.ShapeDtypeStruct((ne, r, d), F32),
        scratch_shapes=[pltpu.VMEM((r, d), F32), pltpu.VMEM((d, tf), BF16),
                        pltpu.VMEM((d, tf), BF16), pltpu.VMEM((tf, d), BF16)],
        compiler_params=_cparams("arbitrary", "arbitrary"),
        name="moe_experts",
    )(xe, w_gate, w_up, w_down, gates)


def moe_layer(x, gains, mod, mod_all, rows, layer, router_w, w_gate, w_up, w_down):
    m, d = x.shape
    h2, aff = norm_router(x, gains, mod, rows, layer, router_w)
    idx_l, gate_l = [], []
    off = 0
    for n in (rows.n_prompt, rows.n_sample):
        cap = (EC_FACTOR * n) // N_EXPERTS
        gt, ix = lax.top_k(aff[off:off + n].T, cap)
        idx_l.append(ix + off)
        gate_l.append(gt)
        off += n
    idx = jnp.concatenate(idx_l, axis=1)
    gates = jnp.concatenate(gate_l, axis=1)
    xe = h2[idx]
    ye = moe_experts(xe, w_gate, w_up, w_down, layer, gates[..., None])
    out = jnp.zeros((m, d), F32).at[idx.reshape(-1)].add(ye.reshape(-1, d))
    g2 = mod_all[layer, :, 5 * d:6 * d]
    seg = jnp.concatenate([jnp.zeros((rows.n_prompt,), jnp.int32),
                           1 + jnp.arange(rows.n_sample, dtype=jnp.int32) // rows.t_sample])
    return x + g2[seg] * out


def kernel(x_prompt, x_sample, state_rglru, state_s5, cache_k, cache_v, c, c_ctx, ada_w, ada_b, norm1_g, norm2_g, rg_w_in, rg_conv_w, rg_conv_b, rg_w_a, rg_b_a, rg_w_x, rg_b_x, rg_lambda, rg_w_out, s5_a_re, s5_a_im, s5_log_dt, s5_b_re, s5_b_im, s5_c_re, s5_c_im, s5_d, s5_w_glu, attn_w_qkv, attn_w_o, attn_sink, router_w, moe_w_gate, moe_w_up, moe_w_down, final_norm_g):
    bp_, tp, d = x_prompt.shape
    bs, ts, _ = x_sample.shape
    n_p, n_s = bp_ * tp, bs * ts
    depth = ada_w.shape[0]
    rows = _Rows(n_p, n_s, ts)
    assert bs + 1 <= SUBLANES and n_p % ts == 0

    x = jnp.concatenate([x_prompt.reshape(n_p, d), x_sample.reshape(n_s, d)], axis=0)
    cond = jnp.concatenate([c_ctx[None, :], c, jnp.zeros((SUBLANES - 1 - bs, d), F32)], axis=0)
    mod_all = ada_modulation_all(cond, ada_w, ada_b)
    mod = mod_all.reshape(depth * SUBLANES, 1, 6 * d)
    g1 = norm1_g.reshape(depth, 1, d)
    g2 = norm2_g.reshape(depth, 1, d)

    new_rg, new_s5, new_k, new_v = [], [], [], []
    for l in range(depth):
        kind, j = l % 3, l // 3
        if kind == 0:
            gu = norm_mod_matmul(x, g1, mod, rows, l, rg_w_in, j, tn=1024)
            args = (j, rg_conv_w, rg_conv_b, rg_w_a, rg_b_a, rg_w_x, rg_b_x, rg_lambda)
            r = gu.shape[1] // 2
            yp, fin = rglru_scan(gu, 0, bp_, tp, *args, jnp.zeros((bp_, 1, 2, r), F32), 0)
            ys, _ = rglru_scan(gu, n_p, bs, ts, *args, state_rglru, j)
            new_rg.append(fin)
            x = matmul_gated_residual(yp, ys, rg_w_out, j, x, mod, rows, l)
        elif kind == 1:
            hn = norm_mod(x, g1, mod, rows, l)
            mats = _s5_matrices(s5_a_re[j], s5_a_im[j], s5_log_dt[j], s5_b_re[j], s5_b_im[j],
                                s5_c_re[j], s5_c_im[j], s5_d[j])
            up, st = s5_mixer_group(hn[:n_p], bp_, tp, mats, None)
            us, _ = s5_mixer_group(hn[n_p:], bs, ts, mats, state_s5[:, j])
            new_s5.append(st)
            x = glu_gated_residual(up, us, s5_w_glu, j, x, mod, rows, l)
        else:
            qkv = norm_mod_matmul(x, g1, mod, rows, l, attn_w_qkv, j, tn=512)
            dq = N_HEADS * HEAD_DIM
            new_k.append(qkv[:n_p, dq:dq + KV_W].reshape(bp_, tp, N_KV, HEAD_DIM))
            new_v.append(qkv[:n_p, dq + KV_W:].reshape(bp_, tp, N_KV, HEAD_DIM))
            op = attn_prompt(qkv, attn_sink[j], bp_, tp)
            lc = cache_k.shape[2]
            os_ = attn_sample(qkv, n_p, attn_sink[j], bs, ts,
                              cache_k[:, j].reshape(bs, lc, KV_W), cache_v[:, j].reshape(bs, lc, KV_W))
            x = matmul_gated_residual(op, os_, attn_w_o, j, x, mod, rows, l)
        x = moe_layer(x, g2, mod, mod_all, rows, l, router_w, moe_w_gate, moe_w_up, moe_w_down)

    y = final_norm(x, final_norm_g, rows.tm)
    return (y[:n_p].reshape(bp_, tp, d), y[n_p:].reshape(bs, ts, d),
            jnp.stack(new_rg, axis=1), jnp.stack(new_s5, axis=1),
            jnp.stack(new_k, axis=1), jnp.stack(new_v, axis=1))
```

```python
import functools
import math

import jax
import jax.numpy as jnp
from jax import lax
from jax.experimental import pallas as pl
from jax.experimental.pallas import tpu as pltpu

F32 = jnp.float32
BF16 = jnp.bfloat16
HIGHEST = lax.Precision.HIGHEST

EPS = 1e-6
RG_C = 8.0
RG_BS = 128
S5_H = 16
S5_L = 16
N_HEADS = 16
N_KV = 4
Q_PER_KV = N_HEADS // N_KV
HEAD_DIM = 64
KV_W = N_KV * HEAD_DIM
GRID_W = 64
WINDOW = 128
ATT_BLOCK = 128
ROPE_BASE = 10000.0
ATT_SCALE = HEAD_DIM ** -0.5
NEG_INF = -1e30
N_EXPERTS = 16
EC_FACTOR = 2
SUBLANES = 8
ROW_TILE = 512
MOE_ROW_SUB = 256
VMEM_LIMIT = 56 * 1024 * 1024
NT_DIMS = (((1,), (1,)), ((), ()))


def _cparams(*sem):
    return pltpu.CompilerParams(dimension_semantics=sem, vmem_limit_bytes=VMEM_LIMIT)


def _gelu(x):
    return x * (0.5 * (1.0 + jnp.tanh(math.sqrt(2.0 / math.pi) * (x + 0.044715 * (x * x * x)))))


def _norm_mod(x, g, sc, sh):
    ms = jnp.mean(x * x, axis=-1, keepdims=True)
    return ((x * lax.rsqrt(ms + EPS)) * g) * (1.0 + sc) + sh


def _mod_kernel(c_ref, w_ref, b_ref, o_ref):
    c = c_ref[...]
    s = (c * jax.nn.sigmoid(c)).astype(BF16)
    o_ref[...] = jnp.dot(s, w_ref[...].astype(BF16), preferred_element_type=F32) + b_ref[...]


def ada_modulation_all(cond, ada_w, ada_b):
    n_layers, d, n = ada_w.shape
    tn = 1536
    return pl.pallas_call(
        _mod_kernel,
        grid=(n_layers, n // tn),
        in_specs=[pl.BlockSpec((SUBLANES, d), lambda l, j: (0, 0)),
                  pl.BlockSpec((None, d, tn), lambda l, j: (l, 0, j)),
                  pl.BlockSpec((None, 1, tn), lambda l, j: (l, 0, j))],
        out_specs=pl.BlockSpec((None, SUBLANES, tn), lambda l, j: (l, 0, j)),
        out_shape=jax.ShapeDtypeStruct((n_layers, SUBLANES, n), F32),
        compiler_params=_cparams("arbitrary", "arbitrary"),
        name="ada_mod",
    )(cond, ada_w, ada_b.reshape(n_layers, 1, n))


class _Rows:
    def __init__(self, n_prompt, n_sample, t_sample):
        self.n_prompt = n_prompt
        self.n_sample = n_sample
        self.t_sample = t_sample
        self.tm = min(ROW_TILE, n_prompt, t_sample)
        assert n_prompt % self.tm == 0 and t_sample % self.tm == 0
        self.prompt_blocks = n_prompt // self.tm
        self.sample_blocks = n_sample // self.tm

    def seg(self, i):
        r = i * self.tm
        return jnp.where(r < self.n_prompt, 0, 1 + lax.div(r - self.n_prompt, self.t_sample))


def _mod_spec(rows, layer, width, chunk, m_axis, chunk_axis=None):
    def imap(*ids):
        c = chunk if chunk_axis is None else chunk + ids[chunk_axis]
        return (layer * SUBLANES + rows.seg(ids[m_axis]), 0, c)
    return pl.BlockSpec((None, 1, width), imap)


def _gain_spec(layer, d):
    return pl.BlockSpec((None, 1, d), lambda *ids: (layer, 0, 0))


def _nm_kernel(x_ref, g_ref, sc_ref, sh_ref, w_ref, o_ref, wbf_ref):
    @pl.when(pl.program_id(1) == 0)
    def _():
        wbf_ref[...] = w_ref[...].astype(BF16)
    h = _norm_mod(x_ref[...], g_ref[...], sc_ref[...], sh_ref[...])
    o_ref[...] = jnp.dot(h.astype(BF16), wbf_ref[...], preferred_element_type=F32)


def norm_mod_matmul(x, gains, mod, rows, layer, w, wl, *, tn):
    m, d = x.shape
    n = w.shape[2]
    tm = rows.tm
    return pl.pallas_call(
        _nm_kernel,
        grid=(n // tn, m // tm),
        in_specs=[pl.BlockSpec((tm, d), lambda j, i: (i, 0)),
                  _gain_spec(layer, d),
                  _mod_spec(rows, layer, d, 1, 1),
                  _mod_spec(rows, layer, d, 0, 1),
                  pl.BlockSpec((None, d, tn), lambda j, i: (wl, 0, j))],
        out_specs=pl.BlockSpec((tm, tn), lambda j, i: (i, j)),
        out_shape=jax.ShapeDtypeStruct((m, n), F32),
        scratch_shapes=[pltpu.VMEM((d, tn), BF16)],
        compiler_params=_cparams("arbitrary", "arbitrary"),
        name="norm_mod_matmul",
    )(x, gains, mod, mod, w)


def _norm_only_kernel(x_ref, g_ref, sc_ref, sh_ref, o_ref):
    o_ref[...] = _norm_mod(x_ref[...], g_ref[...], sc_ref[...], sh_ref[...])


def norm_mod(x, gains, mod, rows, layer):
    m, d = x.shape
    tm = rows.tm
    return pl.pallas_call(
        _norm_only_kernel,
        grid=(m // tm,),
        in_specs=[pl.BlockSpec((tm, d), lambda i: (i, 0)),
                  _gain_spec(layer, d),
                  _mod_spec(rows, layer, d, 1, 0),
                  _mod_spec(rows, layer, d, 0, 0)],
        out_specs=pl.BlockSpec((tm, d), lambda i: (i, 0)),
        out_shape=jax.ShapeDtypeStruct((m, d), F32),
        compiler_params=_cparams("arbitrary"),
        name="norm_mod",
    )(x, gains, mod, mod)


def _final_norm_kernel(x_ref, g_ref, o_ref):
    x = x_ref[...]
    ms = jnp.mean(x * x, axis=-1, keepdims=True)
    o_ref[...] = (x * lax.rsqrt(ms + EPS)) * g_ref[...]


def final_norm(x, g, tm):
    m, d = x.shape
    return pl.pallas_call(
        _final_norm_kernel,
        grid=(m // tm,),
        in_specs=[pl.BlockSpec((tm, d), lambda i: (i, 0)),
                  pl.BlockSpec((1, d), lambda i: (0, 0))],
        out_specs=pl.BlockSpec((tm, d), lambda i: (i, 0)),
        out_shape=jax.ShapeDtypeStruct((m, d), F32),
        compiler_params=_cparams("arbitrary"),
        name="final_norm",
    )(x, g.reshape(1, d))


def _two_group_specs(rows, k):
    npb, nsb, tm = rows.prompt_blocks, rows.sample_blocks, rows.tm
    return [pl.BlockSpec((tm, k), lambda j, i: (jnp.minimum(i, npb - 1), 0)),
            pl.BlockSpec((tm, k), lambda j, i: (jnp.clip(i - npb, 0, nsb - 1), 0))]


def _mmres_kernel(ap_ref, as_ref, w_ref, r_ref, gt_ref, o_ref, wbf_ref, *, npb):
    i = pl.program_id(1)

    @pl.when(i == 0)
    def _():
        wbf_ref[...] = w_ref[...].astype(BF16)

    def emit(a_ref):
        acc = jnp.dot(a_ref[...].astype(BF16), wbf_ref[...], preferred_element_type=F32)
        o_ref[...] = r_ref[...] + gt_ref[...] * acc

    pl.when(i < npb)(lambda: emit(ap_ref))
    pl.when(i >= npb)(lambda: emit(as_ref))


def matmul_gated_residual(a_p, a_s, w, wl, resid, mod, rows, layer, *, tn=512):
    k = a_p.shape[1]
    m, d = resid.shape
    tm = rows.tm
    nt = d // tn
    return pl.pallas_call(
        functools.partial(_mmres_kernel, npb=rows.prompt_blocks),
        grid=(nt, m // tm),
        in_specs=_two_group_specs(rows, k) + [
            pl.BlockSpec((None, k, tn), lambda j, i: (wl, 0, j)),
            pl.BlockSpec((tm, tn), lambda j, i: (i, j)),
            _mod_spec(rows, layer, tn, 2 * nt, 1, chunk_axis=0)],
        out_specs=pl.BlockSpec((tm, tn), lambda j, i: (i, j)),
        out_shape=jax.ShapeDtypeStruct((m, d), F32),
        scratch_shapes=[pltpu.VMEM((k, tn), BF16)],
        compiler_params=_cparams("arbitrary", "arbitrary"),
        name="matmul_gated_residual",
    )(a_p, a_s, w, resid, mod)


def _glures_kernel(ap_ref, as_ref, wv_ref, wg_ref, r_ref, gt_ref, o_ref, wv_bf, wg_bf, *, npb):
    i = pl.program_id(1)

    @pl.when(i == 0)
    def _():
        wv_bf[...] = wv_ref[...].astype(BF16)
        wg_bf[...] = wg_ref[...].astype(BF16)

    def emit(a_ref):
        a = a_ref[...].astype(BF16)
        v = jnp.dot(a, wv_bf[...], preferred_element_type=F32)
        g = jnp.dot(a, wg_bf[...], preferred_element_type=F32)
        o_ref[...] = r_ref[...] + gt_ref[...] * (v * jax.nn.sigmoid(g))

    pl.when(i < npb)(lambda: emit(ap_ref))
    pl.when(i >= npb)(lambda: emit(as_ref))


def glu_gated_residual(a_p, a_s, w_glu, wl, resid, mod, rows, layer, *, tn=512):
    k = a_p.shape[1]
    m, d = resid.shape
    tm = rows.tm
    nt = d // tn
    return pl.pallas_call(
        functools.partial(_glures_kernel, npb=rows.prompt_blocks),
        grid=(nt, m // tm),
        in_specs=_two_group_specs(rows, k) + [
            pl.BlockSpec((None, k, tn), lambda j, i: (wl, 0, j)),
            pl.BlockSpec((None, k, tn), lambda j, i: (wl, 0, nt + j)),
            pl.BlockSpec((tm, tn), lambda j, i: (i, j)),
            _mod_spec(rows, layer, tn, 2 * nt, 1, chunk_axis=0)],
        out_specs=pl.BlockSpec((tm, tn), lambda j, i: (i, j)),
        out_shape=jax.ShapeDtypeStruct((m, d), F32),
        scratch_shapes=[pltpu.VMEM((k, tn), BF16), pltpu.VMEM((k, tn), BF16)],
        compiler_params=_cparams("arbitrary", "arbitrary"),
        name="glu_gated_residual",
    )(a_p, a_s, w_glu, w_glu, resid, mod)


def _rglru_kernel(gate_ref, u_ref, cw_ref, cb_ref, wa_ref, ba_ref, wx_ref, bx_ref, lam_ref, h0_ref,
                  y_ref, fin_ref, af_s, bf_s, ab_s, bb_s, hf_s, hb_s):
    t, cw = u_ref.shape
    u = u_ref[...]
    row = lax.broadcasted_iota(jnp.int32, (t, cw), 0)

    def shifted(x, k):
        if k > 0:
            return jnp.where(row >= k, pltpu.roll(x, k, axis=0), 0.0)
        return jnp.where(row < t + k, pltpu.roll(x, t + k, axis=0), 0.0)

    cwv = cw_ref[...]
    uc = (cwv[0:1] * shifted(u, 2) + cwv[1:2] * shifted(u, 1) + cwv[2:3] * u
          + cwv[3:4] * shifted(u, -1) + cb_ref[...])

    a_scr = (af_s, ab_s)
    b_scr = (bf_s, bb_s)
    for k in range(2):
        nl = -lam_ref[k:k + 1, :]
        sp = jnp.maximum(nl, 0.0) + jnp.log1p(jnp.exp(-jnp.abs(nl)))
        for hh in range(cw // RG_BS):
            sl = slice(hh * RG_BS, (hh + 1) * RG_BS)
            uh = uc[:, sl]
            ub = uh.astype(BF16)
            r = jax.nn.sigmoid(jnp.dot(ub, wa_ref[k, hh].astype(BF16), preferred_element_type=F32)
                               + ba_ref[k:k + 1, sl])
            i = jax.nn.sigmoid(jnp.dot(ub, wx_ref[k, hh].astype(BF16), preferred_element_type=F32)
                               + bx_ref[k:k + 1, sl])
            log_a = (-RG_C * r) * sp[:, sl]
            a = jnp.exp(log_a)
            a_scr[k][:, sl] = a
            b_scr[k][:, sl] = jnp.sqrt(jnp.tanh(-log_a) * (a * a + 1.0)) * (i * uh)

    nblk = t // SUBLANES
    srow = lax.broadcasted_iota(jnp.int32, (SUBLANES, cw), 0)

    def body(n, carry):
        cf, cb = carry
        rf = pl.multiple_of(n * SUBLANES, SUBLANES)
        rb = pl.multiple_of((nblk - 1 - n) * SUBLANES, SUBLANES)
        a = af_s[pl.ds(rf, SUBLANES), :]
        b = bf_s[pl.ds(rf, SUBLANES), :]
        a2 = ab_s[pl.ds(rb, SUBLANES), :]
        b2 = bb_s[pl.ds(rb, SUBLANES), :]
        for s in (1, 2, 4):
            m = srow >= s
            b = jnp.where(m, a * pltpu.roll(b, s, axis=0) + b, b)
            a = jnp.where(m, a * pltpu.roll(a, s, axis=0), a)
            m2 = srow < SUBLANES - s
            b2 = jnp.where(m2, a2 * pltpu.roll(b2, SUBLANES - s, axis=0) + b2, b2)
            a2 = jnp.where(m2, a2 * pltpu.roll(a2, SUBLANES - s, axis=0), a2)
        hf = a * cf + b
        hb = a2 * cb + b2
        hf_s[pl.ds(rf, SUBLANES), :] = hf
        hb_s[pl.ds(rb, SUBLANES), :] = hb
        return hf[SUBLANES - 1:SUBLANES, :], hb[0:1, :]

    cf, cb = lax.fori_loop(0, nblk, body, (h0_ref[0:1, :], h0_ref[1:2, :]))
    fin_ref[0:1, :] = cf
    fin_ref[1:2, :] = cb
    y_ref[...] = (hf_s[...] + hb_s[...]) * _gelu(gate_ref[...])


def rglru_scan(gu, row0, n_seq, t, j, conv_w, conv_b, w_a, b_a, w_x, b_x, lam, h0, h0_j, *, cw=256):
    r = gu.shape[1] // 2
    nh = cw // RG_BS
    blk0 = row0 // t
    nc = r // cw
    scr = [pltpu.VMEM((t, cw), F32) for _ in range(6)]
    vec2 = pl.BlockSpec((None, 2, cw), lambda b, c: (j, 0, c))
    gatew = pl.BlockSpec((None, 2, nh, RG_BS, RG_BS), lambda b, c: (j, 0, c, 0, 0))
    return pl.pallas_call(
        _rglru_kernel,
        grid=(n_seq, nc),
        in_specs=[pl.BlockSpec((t, cw), lambda b, c: (blk0 + b, c)),
                  pl.BlockSpec((t, cw), lambda b, c: (blk0 + b, nc + c)),
                  pl.BlockSpec((None, 4, cw), lambda b, c: (j, 0, c)),
                  pl.BlockSpec((None, 1, cw), lambda b, c: (j, 0, c)),
                  gatew, vec2, gatew, vec2, vec2,
                  pl.BlockSpec((None, None, 2, cw), lambda b, c: (b, h0_j, 0, c))],
        out_specs=[pl.BlockSpec((t, cw), lambda b, c: (b, c)),
                   pl.BlockSpec((None, 2, cw), lambda b, c: (b, 0, c))],
        out_shape=[jax.ShapeDtypeStruct((n_seq * t, r), F32),
                   jax.ShapeDtypeStruct((n_seq, 2, r), F32)],
        scratch_shapes=scr,
        compiler_params=_cparams("arbitrary", "arbitrary"),
        name="rglru_scan",
    )(gu, gu, conv_w, conv_b.reshape(conv_b.shape[0], 1, r), w_a, b_a, w_x, b_x, lam, h0)


def _s5_kernel(*refs, bp, n_seq, nc, gpb):
    ell, gw = S5_L, S5_H
    x_refs = refs[:ell]
    tm_ref, win_ref, wre_ref, wim_ref, ar_ref, ai_ref, d_ref, s0re_ref, s0im_ref = refs[ell:ell + 9]
    o_refs = refs[ell + 9:2 * ell + 9]
    fin_ref = refs[2 * ell + 9]
    xg_s, ure_s, uim_s, fre_s, fim_s, bre_s, bim_s = refs[2 * ell + 10:]
    m, mp = n_seq * nc, bp * nc
    lanes = x_refs[0].shape[1]
    per_tile = lanes // gw
    half = ure_s.shape[2] // 2
    lane_grp = lax.shift_right_logical(lax.broadcasted_iota(jnp.int32, (m, lanes), 1), gw.bit_length() - 1)

    def perm(shape, chunk_major_axis):
        i = lax.broadcasted_iota(jnp.int32, shape, chunk_major_axis)
        j = lax.broadcasted_iota(jnp.int32, shape, 1 - chunk_major_axis)
        b = i & (bp - 1)
        c = lax.shift_right_logical(i, bp.bit_length() - 1)
        return jnp.where((j == b * nc + c) & (b < n_seq), 1.0, 0.0).astype(BF16)

    to_chunk_major = perm((mp, m), 0)
    to_batch_major = perm((m, mp), 1)

    def block_transpose(v):
        k = per_tile // 2
        while k >= 1:
            low = (lane_grp & k) == 0
            nxt = list(v)
            for i in range(per_tile):
                if i & k == 0:
                    a, b = v[i], v[i + k]
                    nxt[i] = jnp.where(low, a, pltpu.roll(b, k * gw, axis=1))
                    nxt[i + k] = jnp.where(low, pltpu.roll(a, lanes - k * gw, axis=1), b)
            v = nxt
            k //= 2
        return v

    for tile in range(ell // per_tile):
        by_group = block_transpose([x_refs[tile * per_tile + j][...] for j in range(per_tile)])
        for g in range(gpb):
            xg_s[g, :, tile * lanes:(tile + 1) * lanes] = by_group[g]

    for g in range(gpb):
        xg = xg_s[g]
        xp = jnp.dot(to_chunk_major, xg.astype(BF16), preferred_element_type=F32).astype(BF16)
        u = jnp.dot(xp, win_ref[g].astype(BF16), preferred_element_type=F32)
        ure_s[g] = u[:, :2 * half]
        uim_s[g] = u[:, 2 * half:]

    is_fwd = lax.broadcasted_iota(jnp.int32, (bp, 2 * half), 1) < half
    ar = [ar_ref[g] for g in range(gpb)]
    ai = [ai_ref[g] for g in range(gpb)]

    def body(k, carry):
        rf = pl.multiple_of(k * bp, bp)
        rb = pl.multiple_of((nc - 1 - k) * bp, bp)
        out = []
        for g in range(gpb):
            re, im = carry[2 * g], carry[2 * g + 1]
            fre_s[g, pl.ds(rf, bp), :] = re
            fim_s[g, pl.ds(rf, bp), :] = im
            bre_s[g, pl.ds(rb, bp), :] = re
            bim_s[g, pl.ds(rb, bp), :] = im
            ure = jnp.where(is_fwd, ure_s[g, pl.ds(rf, bp), :], ure_s[g, pl.ds(rb, bp), :])
            uim = jnp.where(is_fwd, uim_s[g, pl.ds(rf, bp), :], uim_s[g, pl.ds(rb, bp), :])
            out += [ar[g] * re - ai[g] * im + ure, ar[g] * im + ai[g] * re + uim]
        return tuple(out)

    init = tuple(r[g] for g in range(gpb) for r in (s0re_ref, s0im_ref))
    fin = lax.fori_loop(0, nc, body, init)
    fwd_all = lax.broadcasted_iota(jnp.int32, (mp, 2 * half), 1) < half
    for g in range(gpb):
        fin_ref[g, :, :2 * half] = fin[2 * g]
        fin_ref[g, :, 2 * half:] = fin[2 * g + 1]
        hre = jnp.where(fwd_all, fre_s[g], bre_s[g]).astype(BF16)
        him = jnp.where(fwd_all, fim_s[g], bim_s[g]).astype(BF16)
        hre = jnp.dot(to_batch_major, hre, preferred_element_type=F32).astype(BF16)
        him = jnp.dot(to_batch_major, him, preferred_element_type=F32).astype(BF16)
        xg = xg_s[g]
        y = (jnp.dot(xg.astype(BF16), tm_ref[g].astype(BF16), preferred_element_type=F32)
             + jnp.dot(hre, wre_ref[g].astype(BF16), preferred_element_type=F32)
             + jnp.dot(him, wim_ref[g].astype(BF16), preferred_element_type=F32)
             + d_ref[g] * xg)
        xg_s[g] = _gelu(y)

    for tile in range(ell // per_tile):
        by_token = block_transpose([xg_s[g, :, tile * lanes:(tile + 1) * lanes] for g in range(gpb)])
        for j in range(per_tile):
            o_refs[tile * per_tile + j][...] = by_token[j]


def s5_chunked(hn2, row_blk, n_seq, nc, mats, s0re, s0im, *, bp):
    tmat, win, wre, wim, ar, ai, dg = mats
    g, w, _ = tmat.shape
    p2 = ar.shape[-1]
    ell = S5_L
    lanes = 128
    gpb = lanes // S5_H
    d = hn2.shape[1] // ell
    m, mp = n_seq * nc, bp * nc
    assert bp & (bp - 1) == 0 and S5_H & (S5_H - 1) == 0 and g % gpb == 0
    col_tiles = d // lanes
    blk = lambda shape: pl.BlockSpec((gpb,) + shape, lambda i: (i, 0, 0))
    x_specs = [pl.BlockSpec((m, lanes), functools.partial(lambda i, l: (row_blk, l * col_tiles + i), l=l))
               for l in range(ell)]
    o_specs = [pl.BlockSpec((m, lanes), lambda i: (0, i)) for _ in range(ell)]
    outs = pl.pallas_call(
        functools.partial(_s5_kernel, bp=bp, n_seq=n_seq, nc=nc, gpb=gpb),
        grid=(g // gpb,),
        in_specs=x_specs + [blk((w, w)), blk((w, 2 * p2)), blk((p2, w)), blk((p2, w)),
                            blk((1, p2)), blk((1, p2)), blk((1, w)), blk((bp, p2)), blk((bp, p2))],
        out_specs=o_specs + [blk((bp, 2 * p2))],
        out_shape=[jax.ShapeDtypeStruct((m, d), F32)] * ell + [jax.ShapeDtypeStruct((g, bp, 2 * p2), F32)],
        scratch_shapes=[pltpu.VMEM((gpb, m, w), F32)] + [pltpu.VMEM((gpb, mp, p2), F32) for _ in range(6)],
        compiler_params=_cparams("arbitrary"),
        name="s5_chunked",
    )(*([hn2] * ell), tmat, win, wre, wim, ar, ai, dg, s0re, s0im)
    return outs[:ell], outs[ell]


def _cmul(ar, ai, br, bi):
    return ar * br - ai * bi, ar * bi + ai * br


def _s5_matrices(a_re, a_im, log_dt, b_re, b_im, c_re, c_im, d):
    _, g, p = a_re.shape
    h = b_re.shape[-1]
    ell = S5_L
    dt = jnp.exp(log_dt)[:, :, None]
    steps = jnp.arange(ell + 1, dtype=F32)[:, None, None, None]
    mag = jnp.exp(steps * (a_re * dt)[None])
    ang = steps * (a_im * dt)[None]
    amr, ami = mag * jnp.cos(ang), mag * jnp.sin(ang)
    nr, ni = amr[1] - 1.0, ami[1]
    den = a_re * a_re + a_im * a_im
    qr, qi = (nr * a_re + ni * a_im) / den, (ni * a_re - nr * a_im) / den
    bbr, bbi = _cmul(qr[..., None], qi[..., None], b_re, b_im)

    def win_dir(k, pr, pi):
        wr, wi = _cmul(pr[..., None], pi[..., None], bbr[k][None], bbi[k][None])
        return jnp.transpose(wr, (1, 0, 3, 2)), jnp.transpose(wi, (1, 0, 3, 2))

    wfr, wfi = win_dir(0, jnp.flip(amr[:ell, 0], 0), jnp.flip(ami[:ell, 0], 0))
    wbr, wbi = win_dir(1, amr[:ell, 1], ami[:ell, 1])
    win = jnp.concatenate([wfr, wbr, wfi, wbi], axis=-1).reshape(g, ell * h, 4 * p)

    def c_pow(k):
        return _cmul(c_re[k][None], c_im[k][None], amr[:, k][:, :, None, :], ami[:, k][:, :, None, :])

    cfr, cfi = c_pow(0)
    cbr, cbi = c_pow(1)
    to_gplh = lambda z: jnp.transpose(z, (1, 3, 0, 2))
    wre = jnp.concatenate([to_gplh(cfr[1:]), to_gplh(jnp.flip(cbr[1:], 0))], axis=1).reshape(g, 2 * p, ell * h)
    wim = jnp.concatenate([-to_gplh(cfi[1:]), -to_gplh(jnp.flip(cbi[1:], 0))], axis=1).reshape(g, 2 * p, ell * h)

    def lag_kernel(zr, zi, k):
        return (jnp.einsum('mghp,gpk->mghk', zr[:ell], bbr[k], precision=HIGHEST)
                - jnp.einsum('mghp,gpk->mghk', zi[:ell], bbi[k], precision=HIGHEST))

    kf = lag_kernel(cfr, cfi, 0)
    kb = lag_kernel(cbr, cbi, 1)
    li = jnp.arange(ell)
    lag = li[None, :] - li[:, None]
    tf = jnp.where((lag >= 0)[:, :, None, None, None], kf[jnp.clip(lag, 0, ell - 1)], 0.0)
    tb = jnp.where((lag <= 0)[:, :, None, None, None], kb[jnp.clip(-lag, 0, ell - 1)], 0.0)
    tmat = jnp.transpose(tf + tb, (2, 0, 4, 1, 3)).reshape(g, ell * h, ell * h)
    ar = jnp.concatenate([amr[ell, 0], amr[ell, 1]], axis=-1)[:, None, :]
    ai = jnp.concatenate([ami[ell, 0], ami[ell, 1]], axis=-1)[:, None, :]
    dg = jnp.tile(d.reshape(g, 1, h), (1, ell, 1)).reshape(g, 1, ell * h)
    return tmat, win, wre, wim, ar, ai, dg


def s5_mixer_group(hn2, row0, n_seq, t, mats, s0):
    ar = mats[4]
    g = ar.shape[0]
    p = ar.shape[-1] // 2
    nc = t // S5_L
    bp = -(-n_seq // SUBLANES) * SUBLANES
    m = n_seq * nc
    assert (row0 // S5_L) % m == 0
    if s0 is None:
        s0re = jnp.zeros((g, bp, 2 * p), F32)
        s0im = s0re
    else:
        st = jnp.transpose(s0, (3, 0, 2, 1, 4)).reshape(g, n_seq, 2, 2 * p)
        st = jnp.pad(st, ((0, 0), (0, bp - n_seq), (0, 0), (0, 0)))
        s0re, s0im = st[:, :, 0], st[:, :, 1]
    u2, fin = s5_chunked(hn2, (row0 // S5_L) // m, n_seq, nc, mats, s0re, s0im, bp=bp)
    u = jnp.stack(u2, axis=1).reshape(m * S5_L, -1)
    fin = fin.reshape(g, bp, 2, 2, p)[:, :n_seq]
    return u, jnp.transpose(fin, (1, 3, 2, 0, 4))


def _softmax_pv(scores, values, sink):
    m = sink
    for s in scores:
        m = jnp.maximum(m, jnp.max(s, axis=-1, keepdims=True))
    den = jnp.exp(sink - m)
    acc = None
    for s, v in zip(scores, values):
        p = jnp.exp(s - m)
        den = den + jnp.sum(p, axis=-1, keepdims=True)
        pv = jnp.dot(p.astype(BF16), v, preferred_element_type=F32)
        acc = pv if acc is None else acc + pv
    return acc / den


def _stack_heads(qh):
    return jnp.concatenate([qh[:, g * HEAD_DIM:(g + 1) * HEAD_DIM] for g in range(Q_PER_KV)], axis=0)


def _stack_sinks(sink_ref, h, rows):
    return jnp.concatenate([jnp.full((rows, 1), sink_ref[h * Q_PER_KV + g], F32) for g in range(Q_PER_KV)], axis=0)


def _store_heads(o_ref, h, o4):
    rows = o_ref.shape[0]
    for g in range(Q_PER_KV):
        c0 = (h * Q_PER_KV + g) * HEAD_DIM
        o_ref[:, c0:c0 + HEAD_DIM] = o4[g * rows:(g + 1) * rows, :]


def _attn_prompt_kernel(sink_ref, q_ref, k_ref, v_ref, o_ref):
    t = q_ref.shape[0]
    k = k_ref[...].astype(BF16)
    v = v_ref[...].astype(BF16)
    for h in range(N_KV):
        hs = slice(h * HEAD_DIM, (h + 1) * HEAD_DIM)
        q4 = _stack_heads(q_ref[:, h * KV_W:(h + 1) * KV_W].astype(BF16))
        s = lax.dot_general(q4, k[:, hs], NT_DIMS, preferred_element_type=F32) * ATT_SCALE
        _store_heads(o_ref, h, _softmax_pv([s], [v[:, hs]], _stack_sinks(sink_ref, h, t)))


def attn_prompt(qkv, sink, n_seq, t):
    dq = N_HEADS * HEAD_DIM
    kcol = dq // KV_W
    return pl.pallas_call(
        _attn_prompt_kernel,
        grid=(n_seq,),
        in_specs=[pl.BlockSpec(memory_space=pltpu.SMEM),
                  pl.BlockSpec((t, dq), lambda b: (b, 0)),
                  pl.BlockSpec((t, KV_W), lambda b: (b, kcol)),
                  pl.BlockSpec((t, KV_W), lambda b: (b, kcol + 1))],
        out_specs=pl.BlockSpec((t, dq), lambda b: (b, 0)),
        out_shape=jax.ShapeDtypeStruct((n_seq * t, dq), F32),
        compiler_params=_cparams("arbitrary"),
        name="attn_prompt",
    )(sink, qkv, qkv, qkv)


def _rope(x, cos, sin):
    w = x.shape[1]
    low = (lax.broadcasted_iota(jnp.int32, x.shape, 1) & (HEAD_DIM // 4)) == 0
    partner = jnp.where(low, pltpu.roll(x, w - HEAD_DIM // 4, axis=1), pltpu.roll(x, HEAD_DIM // 4, axis=1))
    return x * cos + partner * sin


def _attn_sample_kernel(sink_ref, q_ref, k_ref, v_ref, kc_ref, vc_ref, cos_ref, sin_ref, o_ref,
                        kw_s, vw_s, kc_s, vc_s, *, t):
    n = pl.program_id(1)
    blk = ATT_BLOCK

    @pl.when(n == 0)
    def _():
        zeros = jnp.zeros((blk, KV_W), BF16)
        kw_s[0:blk, :] = zeros
        vw_s[0:blk, :] = zeros
        kw_s[blk + t:2 * blk + t, :] = zeros
        vw_s[blk + t:2 * blk + t, :] = zeros
        kw_s[blk:blk + t, :] = _rope(k_ref[...], cos_ref[...], sin_ref[...]).astype(BF16)
        vw_s[blk:blk + t, :] = v_ref[...].astype(BF16)
        kc_s[...] = kc_ref[...].astype(BF16)
        vc_s[...] = vc_ref[...].astype(BF16)

    r0 = pl.multiple_of(n * blk, blk)
    cq = cos_ref[pl.ds(r0, blk), :]
    sq = sin_ref[pl.ds(r0, blk), :]
    kw = kw_s[pl.ds(r0, 3 * blk), :]
    vw = vw_s[pl.ds(r0, 3 * blk), :]
    qi = lax.broadcasted_iota(jnp.int32, (blk, 3 * blk), 0)
    kj = lax.broadcasted_iota(jnp.int32, (blk, 3 * blk), 1)
    kpos = n * blk - blk + kj
    valid = (jnp.abs(kj - blk - qi) <= WINDOW) & (kpos >= 0) & (kpos < t)
    valid4 = jnp.concatenate([valid] * Q_PER_KV, axis=0)
    for h in range(N_KV):
        hs = slice(h * HEAD_DIM, (h + 1) * HEAD_DIM)
        q4 = _stack_heads(_rope(q_ref[:, h * KV_W:(h + 1) * KV_W], cq, sq).astype(BF16))
        s_loc = lax.dot_general(q4, kw[:, hs], NT_DIMS, preferred_element_type=F32) * ATT_SCALE
        s_loc = jnp.where(valid4, s_loc, NEG_INF)
        s_ctx = lax.dot_general(q4, kc_s[:, hs], NT_DIMS, preferred_element_type=F32) * ATT_SCALE
        o4 = _softmax_pv([s_loc, s_ctx], [vw[:, hs], vc_s[:, hs]], _stack_sinks(sink_ref, h, blk))
        _store_heads(o_ref, h, o4)


def _rope_tables(t):
    quarter = HEAD_DIM // 4
    freqs = ROPE_BASE ** (-jnp.arange(quarter, dtype=F32) / quarter)
    pos = jnp.arange(t)
    ang_r = (pos // GRID_W).astype(F32)[:, None] * freqs
    ang_c = (pos % GRID_W).astype(F32)[:, None] * freqs
    cos = jnp.concatenate([jnp.cos(ang_r), jnp.cos(ang_r), jnp.cos(ang_c), jnp.cos(ang_c)], axis=-1)
    sin = jnp.concatenate([-jnp.sin(ang_r), jnp.sin(ang_r), -jnp.sin(ang_c), jnp.sin(ang_c)], axis=-1)
    return jnp.tile(cos, (1, N_KV)), jnp.tile(sin, (1, N_KV))


def attn_sample(qkv, row0, sink, n_seq, t, k_ctx, v_ctx):
    dq = N_HEADS * HEAD_DIM
    kcol = dq // KV_W
    nb = t // ATT_BLOCK
    lc = k_ctx.shape[1]
    cos, sin = _rope_tables(t)
    qblk0, sblk0 = row0 // ATT_BLOCK, row0 // t
    return pl.pallas_call(
        functools.partial(_attn_sample_kernel, t=t),
        grid=(n_seq, nb),
        in_specs=[pl.BlockSpec(memory_space=pltpu.SMEM),
                  pl.BlockSpec((ATT_BLOCK, dq), lambda b, n: (qblk0 + b * nb + n, 0)),
                  pl.BlockSpec((t, KV_W), lambda b, n: (sblk0 + b, kcol)),
                  pl.BlockSpec((t, KV_W), lambda b, n: (sblk0 + b, kcol + 1)),
                  pl.BlockSpec((None, lc, KV_W), lambda b, n: (b, 0, 0)),
                  pl.BlockSpec((None, lc, KV_W), lambda b, n: (b, 0, 0)),
                  pl.BlockSpec((t, KV_W), lambda b, n: (0, 0)),
                  pl.BlockSpec((t, KV_W), lambda b, n: (0, 0))],
        out_specs=pl.BlockSpec((ATT_BLOCK, dq), lambda b, n: (b * nb + n, 0)),
        out_shape=jax.ShapeDtypeStruct((n_seq * t, dq), F32),
        scratch_shapes=[pltpu.VMEM((t + 2 * ATT_BLOCK, KV_W), BF16), pltpu.VMEM((t + 2 * ATT_BLOCK, KV_W), BF16),
                        pltpu.VMEM((lc, KV_W), BF16), pltpu.VMEM((lc, KV_W), BF16)],
        compiler_params=_cparams("arbitrary", "arbitrary"),
        name="attn_sample",
    )(sink, qkv, qkv, qkv, k_ctx, v_ctx, cos, sin)


def _router_kernel(x_ref, g_ref, sc_ref, sh_ref, rw_ref, h_ref, aff_ref):
    h = _norm_mod(x_ref[...], g_ref[...], sc_ref[...], sh_ref[...])
    h_hi = h.astype(BF16)
    h_ref[...] = h_hi
    h_lo = (h - h_hi.astype(F32)).astype(BF16)
    rw = rw_ref[...]
    rw_hi = rw.astype(BF16)
    rw_lo = (rw - rw_hi.astype(F32)).astype(BF16)
    logits = (jnp.dot(h_hi, rw_hi, preferred_element_type=F32)
              + (jnp.dot(h_hi, rw_lo, preferred_element_type=F32)
                 + jnp.dot(h_lo, rw_hi, preferred_element_type=F32)))
    e = jnp.exp(logits - jnp.max(logits, axis=-1, keepdims=True))
    aff_ref[...] = e / jnp.sum(e, axis=-1, keepdims=True)


def norm_router(x, gains, mod, rows, layer, router_w):
    m, d = x.shape
    ne = router_w.shape[2]
    tm = rows.tm
    return pl.pallas_call(
        _router_kernel,
        grid=(m // tm,),
        in_specs=[pl.BlockSpec((tm, d), lambda i: (i, 0)),
                  _gain_spec(layer, d),
                  _mod_spec(rows, layer, d, 4, 0),
                  _mod_spec(rows, layer, d, 3, 0),
                  pl.BlockSpec((None, d, ne), lambda i: (layer, 0, 0))],
        out_specs=[pl.BlockSpec((tm, d), lambda i: (i, 0)),
                   pl.BlockSpec((tm, ne), lambda i: (i, 0))],
        out_shape=[jax.ShapeDtypeStruct((m, d), BF16), jax.ShapeDtypeStruct((m, ne), F32)],
        compiler_params=_cparams("arbitrary"),
        name="norm_router",
    )(x, gains, mod, mod, router_w)


def _moe_kernel(x_ref, wg_ref, wu_ref, wd_ref, gt_ref, seg_ref, g2_ref, o_ref, wg_bf, wu_bf, wd_bf, *, n_seg):
    f = pl.program_id(1)
    wg_bf[...] = wg_ref[...].astype(BF16)
    wu_bf[...] = wu_ref[...].astype(BF16)
    wd_bf[...] = wd_ref[...].astype(BF16)
    rsub = min(MOE_ROW_SUB, x_ref.shape[0])
    for r in range(x_ref.shape[0] // rsub):
        rs = slice(r * rsub, (r + 1) * rsub)
        x = x_ref[rs, :]
        hg = jnp.dot(x, wg_bf[...], preferred_element_type=F32)
        hu = jnp.dot(x, wu_bf[...], preferred_element_type=F32)
        he = ((hg * jax.nn.sigmoid(hg)) * hu).astype(BF16)
        part = jnp.dot(he, wd_bf[...], preferred_element_type=F32)

        @pl.when(f == 0)
        def _():
            o_ref[rs, :] = part

        @pl.when(f > 0)
        def _():
            o_ref[rs, :] += part

    @pl.when(f == pl.num_programs(1) - 1)
    def _():
        seg = seg_ref[...]
        g2 = jnp.zeros(o_ref.shape, F32)
        for s in range(n_seg):
            g2 = jnp.where(seg == s, g2_ref[s:s + 1, :], g2)
        o_ref[...] = o_ref[...] * (gt_ref[...] * g2)


def moe_experts(xe, w_gate, w_up, w_down, layer, gates, seg, mod, n_seg, *, tf=512):
    ne, r, d = xe.shape
    dff = w_gate.shape[3]
    return pl.pallas_call(
        functools.partial(_moe_kernel, n_seg=n_seg),
        grid=(ne, dff // tf),
        in_specs=[pl.BlockSpec((None, r, d), lambda e, f: (e, 0, 0)),
                  pl.BlockSpec((None, None, d, tf), lambda e, f: (layer, e, 0, f)),
                  pl.BlockSpec((None, None, d, tf), lambda e, f: (layer, e, 0, f)),
                  pl.BlockSpec((None, None, tf, d), lambda e, f: (layer, e, f, 0)),
                  pl.BlockSpec((None, r, 1), lambda e, f: (e, 0, 0)),
                  pl.BlockSpec((None, r, 1), lambda e, f: (e, 0, 0)),
                  pl.BlockSpec((SUBLANES, None, d), lambda e, f: (layer, 0, 5))],
        out_specs=pl.BlockSpec((None, r, d), lambda e, f: (e, 0, 0)),
        out_shape=jax.ShapeDtypeStruct((ne, r, d), F32),
        scratch_shapes=[pltpu.VMEM((d, tf), BF16), pltpu.VMEM((d, tf), BF16), pltpu.VMEM((tf, d), BF16)],
        compiler_params=_cparams("arbitrary", "arbitrary"),
        name="moe_experts",
    )(xe, w_gate, w_up, w_down, gates, seg, mod)


def moe_layer(x, gains, mod, rows, layer, router_w, w_gate, w_up, w_down, n_seg):
    m, d = x.shape
    h2, aff = norm_router(x, gains, mod, rows, layer, router_w)
    idx_l, gate_l = [], []
    off = 0
    for n in (rows.n_prompt, rows.n_sample):
        cap = (EC_FACTOR * n) // N_EXPERTS
        gt, ix = lax.top_k(aff[off:off + n].T, cap)
        idx_l.append(ix + off)
        gate_l.append(gt)
        off += n
    idx = jnp.concatenate(idx_l, axis=1)
    gates = jnp.concatenate(gate_l, axis=1)
    seg = jnp.where(idx < rows.n_prompt, 0, 1 + (idx - rows.n_prompt) // rows.t_sample)
    xe = h2[idx]
    ye = moe_experts(xe, w_gate, w_up, w_down, layer, gates[..., None], seg[..., None], mod, n_seg)
    return x.at[idx.reshape(-1)].add(ye.reshape(-1, d))


def kernel(x_prompt, x_sample, state_rglru, state_s5, cache_k, cache_v, c, c_ctx, ada_w, ada_b, norm1_g, norm2_g, rg_w_in, rg_conv_w, rg_conv_b, rg_w_a, rg_b_a, rg_w_x, rg_b_x, rg_lambda, rg_w_out, s5_a_re, s5_a_im, s5_log_dt, s5_b_re, s5_b_im, s5_c_re, s5_c_im, s5_d, s5_w_glu, attn_w_qkv, attn_w_o, attn_sink, router_w, moe_w_gate, moe_w_up, moe_w_down, final_norm_g):
    bp_, tp, d = x_prompt.shape
    bs, ts, _ = x_sample.shape
    n_p, n_s = bp_ * tp, bs * ts
    depth = ada_w.shape[0]
    rows = _Rows(n_p, n_s, ts)
    assert bs + 1 <= SUBLANES and n_p % ts == 0

    x = jnp.concatenate([x_prompt.reshape(n_p, d), x_sample.reshape(n_s, d)], axis=0)
    cond = jnp.concatenate([c_ctx[None, :], c, jnp.zeros((SUBLANES - 1 - bs, d), F32)], axis=0)
    mod_all = ada_modulation_all(cond, ada_w, ada_b)
    mod = mod_all.reshape(depth * SUBLANES, 1, 6 * d)
    g1 = norm1_g.reshape(depth, 1, d)
    g2 = norm2_g.reshape(depth, 1, d)

    new_rg, new_s5, new_k, new_v = [], [], [], []
    for l in range(depth):
        kind, j = l % 3, l // 3
        if kind == 0:
            gu = norm_mod_matmul(x, g1, mod, rows, l, rg_w_in, j, tn=1024)
            args = (j, rg_conv_w, rg_conv_b, rg_w_a, rg_b_a, rg_w_x, rg_b_x, rg_lambda)
            r = gu.shape[1] // 2
            yp, fin = rglru_scan(gu, 0, bp_, tp, *args, jnp.zeros((bp_, 1, 2, r), F32), 0)
            ys, _ = rglru_scan(gu, n_p, bs, ts, *args, state_rglru, j)
            new_rg.append(fin)
            x = matmul_gated_residual(yp, ys, rg_w_out, j, x, mod, rows, l)
        elif kind == 1:
            hn = norm_mod(x, g1, mod, rows, l)
            mats = _s5_matrices(s5_a_re[j], s5_a_im[j], s5_log_dt[j], s5_b_re[j], s5_b_im[j],
                                s5_c_re[j], s5_c_im[j], s5_d[j])
            hn2 = hn.reshape((n_p + n_s) // S5_L, S5_L * d)
            up, st = s5_mixer_group(hn2, 0, bp_, tp, mats, None)
            us, _ = s5_mixer_group(hn2, n_p, bs, ts, mats, state_s5[:, j])
            new_s5.append(st)
            x = glu_gated_residual(up, us, s5_w_glu, j, x, mod, rows, l)
        else:
            qkv = norm_mod_matmul(x, g1, mod, rows, l, attn_w_qkv, j, tn=512)
            dq = N_HEADS * HEAD_DIM
            new_k.append(qkv[:n_p, dq:dq + KV_W].reshape(bp_, tp, N_KV, HEAD_DIM))
            new_v.append(qkv[:n_p, dq + KV_W:].reshape(bp_, tp, N_KV, HEAD_DIM))
            op = attn_prompt(qkv, attn_sink[j], bp_, tp)
            lc = cache_k.shape[2]
            os_ = attn_sample(qkv, n_p, attn_sink[j], bs, ts,
                              cache_k[:, j].reshape(bs, lc, KV_W), cache_v[:, j].reshape(bs, lc, KV_W))
            x = matmul_gated_residual(op, os_, attn_w_o, j, x, mod, rows, l)
        x = moe_layer(x, g2, mod, rows, l, router_w, moe_w_gate, moe_w_up, moe_w_down, bs + 1)

    y = final_norm(x, final_norm_g, rows.tm)
    return (y[:n_p].reshape(bp_, tp, d), y[n_p:].reshape(bs, ts, d),
            jnp.stack(new_rg, axis=1), jnp.stack(new_s5, axis=1),
            jnp.stack(new_k, axis=1), jnp.stack(new_v, axis=1))
```

```python
import functools
import math

import jax
import jax.numpy as jnp
from jax import lax
from jax.experimental import pallas as pl
from jax.experimental.pallas import tpu as pltpu

F32 = jnp.float32
BF16 = jnp.bfloat16
HIGHEST = lax.Precision.HIGHEST

EPS = 1e-6
RG_C = 8.0
RG_BS = 128
S5_H = 16
S5_L = 16
N_HEADS = 16
N_KV = 4
Q_PER_KV = N_HEADS // N_KV
HEAD_DIM = 64
KV_W = N_KV * HEAD_DIM
GRID_W = 64
WINDOW = 128
ATT_BLOCK = 128
ROPE_BASE = 10000.0
ATT_SCALE = HEAD_DIM ** -0.5
NEG_INF = -1e30
N_EXPERTS = 16
EC_FACTOR = 2
SUBLANES = 8
ROW_TILE = 512
MOE_ROW_SUB = 256
VMEM_LIMIT = 56 * 1024 * 1024
NT_DIMS = (((1,), (1,)), ((), ()))


def _cparams(*sem):
    return pltpu.CompilerParams(dimension_semantics=sem, vmem_limit_bytes=VMEM_LIMIT)


def _gelu(x):
    return x * (0.5 * (1.0 + jnp.tanh(math.sqrt(2.0 / math.pi) * (x + 0.044715 * (x * x * x)))))


def _norm_mod(x, g, sc, sh):
    ms = jnp.mean(x * x, axis=-1, keepdims=True)
    return ((x * lax.rsqrt(ms + EPS)) * g) * (1.0 + sc) + sh


def _mod_kernel(c_ref, w_ref, b_ref, o_ref):
    c = c_ref[...]
    s = (c * jax.nn.sigmoid(c)).astype(BF16)
    o_ref[...] = jnp.dot(s, w_ref[...].astype(BF16), preferred_element_type=F32) + b_ref[...]


def ada_modulation_all(cond, ada_w, ada_b):
    n_layers, d, n = ada_w.shape
    tn = 1536
    return pl.pallas_call(
        _mod_kernel,
        grid=(n_layers, n // tn),
        in_specs=[pl.BlockSpec((SUBLANES, d), lambda l, j: (0, 0)),
                  pl.BlockSpec((None, d, tn), lambda l, j: (l, 0, j)),
                  pl.BlockSpec((None, 1, tn), lambda l, j: (l, 0, j))],
        out_specs=pl.BlockSpec((None, SUBLANES, tn), lambda l, j: (l, 0, j)),
        out_shape=jax.ShapeDtypeStruct((n_layers, SUBLANES, n), F32),
        compiler_params=_cparams("arbitrary", "arbitrary"),
        name="ada_mod",
    )(cond, ada_w, ada_b.reshape(n_layers, 1, n))


class _Rows:
    def __init__(self, n_prompt, n_sample, t_sample):
        self.n_prompt = n_prompt
        self.n_sample = n_sample
        self.t_sample = t_sample
        self.tm = min(ROW_TILE, n_prompt, t_sample)
        assert n_prompt % self.tm == 0 and t_sample % self.tm == 0
        self.prompt_blocks = n_prompt // self.tm
        self.sample_blocks = n_sample // self.tm

    def seg(self, i):
        r = i * self.tm
        return jnp.where(r < self.n_prompt, 0, 1 + lax.div(r - self.n_prompt, self.t_sample))


def _mod_spec(rows, layer, width, chunk, m_axis, chunk_axis=None):
    def imap(*ids):
        c = chunk if chunk_axis is None else chunk + ids[chunk_axis]
        return (layer * SUBLANES + rows.seg(ids[m_axis]), 0, c)
    return pl.BlockSpec((None, 1, width), imap)


def _gain_spec(layer, d):
    return pl.BlockSpec((None, 1, d), lambda *ids: (layer, 0, 0))


def _nm_kernel(x_ref, g_ref, sc_ref, sh_ref, w_ref, o_ref, wbf_ref):
    @pl.when(pl.program_id(1) == 0)
    def _():
        wbf_ref[...] = w_ref[...].astype(BF16)
    h = _norm_mod(x_ref[...], g_ref[...], sc_ref[...], sh_ref[...])
    o_ref[...] = jnp.dot(h.astype(BF16), wbf_ref[...], preferred_element_type=F32)


def norm_mod_matmul(x, gains, mod, rows, layer, w, wl, *, tn):
    m, d = x.shape
    n = w.shape[2]
    tm = rows.tm
    return pl.pallas_call(
        _nm_kernel,
        grid=(n // tn, m // tm),
        in_specs=[pl.BlockSpec((tm, d), lambda j, i: (i, 0)),
                  _gain_spec(layer, d),
                  _mod_spec(rows, layer, d, 1, 1),
                  _mod_spec(rows, layer, d, 0, 1),
                  pl.BlockSpec((None, d, tn), lambda j, i: (wl, 0, j))],
        out_specs=pl.BlockSpec((tm, tn), lambda j, i: (i, j)),
        out_shape=jax.ShapeDtypeStruct((m, n), F32),
        scratch_shapes=[pltpu.VMEM((d, tn), BF16)],
        compiler_params=_cparams("arbitrary", "arbitrary"),
        name="norm_mod_matmul",
    )(x, gains, mod, mod, w)


def _norm_only_kernel(x_ref, g_ref, sc_ref, sh_ref, o_ref):
    o_ref[...] = _norm_mod(x_ref[...], g_ref[...], sc_ref[...], sh_ref[...])


def norm_mod(x, gains, mod, rows, layer):
    m, d = x.shape
    tm = rows.tm
    return pl.pallas_call(
        _norm_only_kernel,
        grid=(m // tm,),
        in_specs=[pl.BlockSpec((tm, d), lambda i: (i, 0)),
                  _gain_spec(layer, d),
                  _mod_spec(rows, layer, d, 1, 0),
                  _mod_spec(rows, layer, d, 0, 0)],
        out_specs=pl.BlockSpec((tm, d), lambda i: (i, 0)),
        out_shape=jax.ShapeDtypeStruct((m, d), F32),
        compiler_params=_cparams("arbitrary"),
        name="norm_mod",
    )(x, gains, mod, mod)


def _final_norm_kernel(x_ref, g_ref, o_ref):
    x = x_ref[...]
    ms = jnp.mean(x * x, axis=-1, keepdims=True)
    o_ref[...] = (x * lax.rsqrt(ms + EPS)) * g_ref[...]


def final_norm(x, g, tm):
    m, d = x.shape
    return pl.pallas_call(
        _final_norm_kernel,
        grid=(m // tm,),
        in_specs=[pl.BlockSpec((tm, d), lambda i: (i, 0)),
                  pl.BlockSpec((1, d), lambda i: (0, 0))],
        out_specs=pl.BlockSpec((tm, d), lambda i: (i, 0)),
        out_shape=jax.ShapeDtypeStruct((m, d), F32),
        compiler_params=_cparams("arbitrary"),
        name="final_norm",
    )(x, g.reshape(1, d))


def _two_group_specs(rows, k):
    npb, nsb, tm = rows.prompt_blocks, rows.sample_blocks, rows.tm
    return [pl.BlockSpec((tm, k), lambda j, i: (jnp.minimum(i, npb - 1), 0)),
            pl.BlockSpec((tm, k), lambda j, i: (jnp.clip(i - npb, 0, nsb - 1), 0))]


def _mmres_kernel(ap_ref, as_ref, w_ref, r_ref, gt_ref, o_ref, wbf_ref, *, npb):
    i = pl.program_id(1)

    @pl.when(i == 0)
    def _():
        wbf_ref[...] = w_ref[...].astype(BF16)

    def emit(a_ref):
        acc = jnp.dot(a_ref[...].astype(BF16), wbf_ref[...], preferred_element_type=F32)
        o_ref[...] = r_ref[...] + gt_ref[...] * acc

    pl.when(i < npb)(lambda: emit(ap_ref))
    pl.when(i >= npb)(lambda: emit(as_ref))


def matmul_gated_residual(a_p, a_s, w, wl, resid, mod, rows, layer, *, tn=512):
    k = a_p.shape[1]
    m, d = resid.shape
    tm = rows.tm
    nt = d // tn
    return pl.pallas_call(
        functools.partial(_mmres_kernel, npb=rows.prompt_blocks),
        grid=(nt, m // tm),
        in_specs=_two_group_specs(rows, k) + [
            pl.BlockSpec((None, k, tn), lambda j, i: (wl, 0, j)),
            pl.BlockSpec((tm, tn), lambda j, i: (i, j)),
            _mod_spec(rows, layer, tn, 2 * nt, 1, chunk_axis=0)],
        out_specs=pl.BlockSpec((tm, tn), lambda j, i: (i, j)),
        out_shape=jax.ShapeDtypeStruct((m, d), F32),
        scratch_shapes=[pltpu.VMEM((k, tn), BF16)],
        compiler_params=_cparams("arbitrary", "arbitrary"),
        name="matmul_gated_residual",
    )(a_p, a_s, w, resid, mod)


def _glures_kernel(ap_ref, as_ref, wv_ref, wg_ref, r_ref, gt_ref, o_ref, wv_bf, wg_bf, *, npb):
    i = pl.program_id(1)

    @pl.when(i == 0)
    def _():
        wv_bf[...] = wv_ref[...].astype(BF16)
        wg_bf[...] = wg_ref[...].astype(BF16)

    def emit(a_ref):
        a = a_ref[...].astype(BF16)
        v = jnp.dot(a, wv_bf[...], preferred_element_type=F32)
        g = jnp.dot(a, wg_bf[...], preferred_element_type=F32)
        o_ref[...] = r_ref[...] + gt_ref[...] * (v * jax.nn.sigmoid(g))

    pl.when(i < npb)(lambda: emit(ap_ref))
    pl.when(i >= npb)(lambda: emit(as_ref))


def glu_gated_residual(a_p, a_s, w_glu, wl, resid, mod, rows, layer, *, tn=512):
    k = a_p.shape[1]
    m, d = resid.shape
    tm = rows.tm
    nt = d // tn
    return pl.pallas_call(
        functools.partial(_glures_kernel, npb=rows.prompt_blocks),
        grid=(nt, m // tm),
        in_specs=_two_group_specs(rows, k) + [
            pl.BlockSpec((None, k, tn), lambda j, i: (wl, 0, j)),
            pl.BlockSpec((None, k, tn), lambda j, i: (wl, 0, nt + j)),
            pl.BlockSpec((tm, tn), lambda j, i: (i, j)),
            _mod_spec(rows, layer, tn, 2 * nt, 1, chunk_axis=0)],
        out_specs=pl.BlockSpec((tm, tn), lambda j, i: (i, j)),
        out_shape=jax.ShapeDtypeStruct((m, d), F32),
        scratch_shapes=[pltpu.VMEM((k, tn), BF16), pltpu.VMEM((k, tn), BF16)],
        compiler_params=_cparams("arbitrary", "arbitrary"),
        name="glu_gated_residual",
    )(a_p, a_s, w_glu, w_glu, resid, mod)


def _rglru_kernel(gate_ref, u_ref, cw_ref, cb_ref, wa_ref, ba_ref, wx_ref, bx_ref, lam_ref, h0_ref,
                  y_ref, fin_ref, af_s, bf_s, ab_s, bb_s, hf_s, hb_s):
    t, cw = u_ref.shape
    u = u_ref[...]
    row = lax.broadcasted_iota(jnp.int32, (t, cw), 0)

    def shifted(x, k):
        if k > 0:
            return jnp.where(row >= k, pltpu.roll(x, k, axis=0), 0.0)
        return jnp.where(row < t + k, pltpu.roll(x, t + k, axis=0), 0.0)

    cwv = cw_ref[...]
    uc = (cwv[0:1] * shifted(u, 2) + cwv[1:2] * shifted(u, 1) + cwv[2:3] * u
          + cwv[3:4] * shifted(u, -1) + cb_ref[...])

    a_scr = (af_s, ab_s)
    b_scr = (bf_s, bb_s)
    for k in range(2):
        nl = -lam_ref[k:k + 1, :]
        sp = jnp.maximum(nl, 0.0) + jnp.log1p(jnp.exp(-jnp.abs(nl)))
        for hh in range(cw // RG_BS):
            sl = slice(hh * RG_BS, (hh + 1) * RG_BS)
            uh = uc[:, sl]
            ub = uh.astype(BF16)
            r = jax.nn.sigmoid(jnp.dot(ub, wa_ref[k, hh].astype(BF16), preferred_element_type=F32)
                               + ba_ref[k:k + 1, sl])
            i = jax.nn.sigmoid(jnp.dot(ub, wx_ref[k, hh].astype(BF16), preferred_element_type=F32)
                               + bx_ref[k:k + 1, sl])
            log_a = (-RG_C * r) * sp[:, sl]
            a = jnp.exp(log_a)
            a_scr[k][:, sl] = a
            b_scr[k][:, sl] = jnp.sqrt(jnp.tanh(-log_a) * (a * a + 1.0)) * (i * uh)

    nblk = t // SUBLANES
    srow = lax.broadcasted_iota(jnp.int32, (SUBLANES, cw), 0)

    def body(n, carry):
        cf, cb = carry
        rf = pl.multiple_of(n * SUBLANES, SUBLANES)
        rb = pl.multiple_of((nblk - 1 - n) * SUBLANES, SUBLANES)
        a = af_s[pl.ds(rf, SUBLANES), :]
        b = bf_s[pl.ds(rf, SUBLANES), :]
        a2 = ab_s[pl.ds(rb, SUBLANES), :]
        b2 = bb_s[pl.ds(rb, SUBLANES), :]
        for s in (1, 2, 4):
            m = srow >= s
            b = jnp.where(m, a * pltpu.roll(b, s, axis=0) + b, b)
            a = jnp.where(m, a * pltpu.roll(a, s, axis=0), a)
            m2 = srow < SUBLANES - s
            b2 = jnp.where(m2, a2 * pltpu.roll(b2, SUBLANES - s, axis=0) + b2, b2)
            a2 = jnp.where(m2, a2 * pltpu.roll(a2, SUBLANES - s, axis=0), a2)
        hf = a * cf + b
        hb = a2 * cb + b2
        hf_s[pl.ds(rf, SUBLANES), :] = hf
        hb_s[pl.ds(rb, SUBLANES), :] = hb
        return hf[SUBLANES - 1:SUBLANES, :], hb[0:1, :]

    cf, cb = lax.fori_loop(0, nblk, body, (h0_ref[0:1, :], h0_ref[1:2, :]))
    fin_ref[0:1, :] = cf
    fin_ref[1:2, :] = cb
    y_ref[...] = (hf_s[...] + hb_s[...]) * _gelu(gate_ref[...])


def rglru_scan(gu, row0, n_seq, t, j, conv_w, conv_b, w_a, b_a, w_x, b_x, lam, h0, h0_j, *, cw=256):
    r = gu.shape[1] // 2
    nh = cw // RG_BS
    blk0 = row0 // t
    nc = r // cw
    scr = [pltpu.VMEM((t, cw), F32) for _ in range(6)]
    vec2 = pl.BlockSpec((None, 2, cw), lambda b, c: (j, 0, c))
    gatew = pl.BlockSpec((None, 2, nh, RG_BS, RG_BS), lambda b, c: (j, 0, c, 0, 0))
    return pl.pallas_call(
        _rglru_kernel,
        grid=(n_seq, nc),
        in_specs=[pl.BlockSpec((t, cw), lambda b, c: (blk0 + b, c)),
                  pl.BlockSpec((t, cw), lambda b, c: (blk0 + b, nc + c)),
                  pl.BlockSpec((None, 4, cw), lambda b, c: (j, 0, c)),
                  pl.BlockSpec((None, 1, cw), lambda b, c: (j, 0, c)),
                  gatew, vec2, gatew, vec2, vec2,
                  pl.BlockSpec((None, None, 2, cw), lambda b, c: (b, h0_j, 0, c))],
        out_specs=[pl.BlockSpec((t, cw), lambda b, c: (b, c)),
                   pl.BlockSpec((None, 2, cw), lambda b, c: (b, 0, c))],
        out_shape=[jax.ShapeDtypeStruct((n_seq * t, r), F32),
                   jax.ShapeDtypeStruct((n_seq, 2, r), F32)],
        scratch_shapes=scr,
        compiler_params=_cparams("arbitrary", "arbitrary"),
        name="rglru_scan",
    )(gu, gu, conv_w, conv_b.reshape(conv_b.shape[0], 1, r), w_a, b_a, w_x, b_x, lam, h0)


def _s5_kernel(*refs, bp, n_seq, nc, gpb):
    ell, gw = S5_L, S5_H
    x_refs = refs[:ell]
    tm_ref, win_ref, wre_ref, wim_ref, ar_ref, ai_ref, d_ref, s0re_ref, s0im_ref = refs[ell:ell + 9]
    o_refs = refs[ell + 9:2 * ell + 9]
    fin_ref = refs[2 * ell + 9]
    xg_s, ure_s, uim_s, fre_s, fim_s, bre_s, bim_s = refs[2 * ell + 10:]
    m, mp = n_seq * nc, bp * nc
    lanes = x_refs[0].shape[1]
    per_tile = lanes // gw
    half = ure_s.shape[2] // 2
    lane_grp = lax.shift_right_logical(lax.broadcasted_iota(jnp.int32, (m, lanes), 1), gw.bit_length() - 1)

    def perm(shape, chunk_major_axis):
        i = lax.broadcasted_iota(jnp.int32, shape, chunk_major_axis)
        j = lax.broadcasted_iota(jnp.int32, shape, 1 - chunk_major_axis)
        b = i & (bp - 1)
        c = lax.shift_right_logical(i, bp.bit_length() - 1)
        return jnp.where((j == b * nc + c) & (b < n_seq), 1.0, 0.0).astype(BF16)

    to_chunk_major = perm((mp, m), 0)
    to_batch_major = perm((m, mp), 1)

    def block_transpose(v):
        k = per_tile // 2
        while k >= 1:
            low = (lane_grp & k) == 0
            nxt = list(v)
            for i in range(per_tile):
                if i & k == 0:
                    a, b = v[i], v[i + k]
                    nxt[i] = jnp.where(low, a, pltpu.roll(b, k * gw, axis=1))
                    nxt[i + k] = jnp.where(low, pltpu.roll(a, lanes - k * gw, axis=1), b)
            v = nxt
            k //= 2
        return v

    for tile in range(ell // per_tile):
        by_group = block_transpose([x_refs[tile * per_tile + j][...] for j in range(per_tile)])
        for g in range(gpb):
            xg_s[g, :, tile * lanes:(tile + 1) * lanes] = by_group[g]

    for g in range(gpb):
        xg = xg_s[g]
        xp = jnp.dot(to_chunk_major, xg.astype(BF16), preferred_element_type=F32).astype(BF16)
        u = jnp.dot(xp, win_ref[g].astype(BF16), preferred_element_type=F32)
        ure_s[g] = u[:, :2 * half]
        uim_s[g] = u[:, 2 * half:]

    is_fwd = lax.broadcasted_iota(jnp.int32, (bp, 2 * half), 1) < half
    ar = [ar_ref[g] for g in range(gpb)]
    ai = [ai_ref[g] for g in range(gpb)]

    def body(k, carry):
        rf = pl.multiple_of(k * bp, bp)
        rb = pl.multiple_of((nc - 1 - k) * bp, bp)
        out = []
        for g in range(gpb):
            re, im = carry[2 * g], carry[2 * g + 1]
            fre_s[g, pl.ds(rf, bp), :] = re
            fim_s[g, pl.ds(rf, bp), :] = im
            bre_s[g, pl.ds(rb, bp), :] = re
            bim_s[g, pl.ds(rb, bp), :] = im
            ure = jnp.where(is_fwd, ure_s[g, pl.ds(rf, bp), :], ure_s[g, pl.ds(rb, bp), :])
            uim = jnp.where(is_fwd, uim_s[g, pl.ds(rf, bp), :], uim_s[g, pl.ds(rb, bp), :])
            out += [ar[g] * re - ai[g] * im + ure, ar[g] * im + ai[g] * re + uim]
        return tuple(out)

    init = tuple(r[g] for g in range(gpb) for r in (s0re_ref, s0im_ref))
    fin = lax.fori_loop(0, nc, body, init)
    fwd_all = lax.broadcasted_iota(jnp.int32, (mp, 2 * half), 1) < half
    for g in range(gpb):
        fin_ref[g, :, :2 * half] = fin[2 * g]
        fin_ref[g, :, 2 * half:] = fin[2 * g + 1]
        hre = jnp.where(fwd_all, fre_s[g], bre_s[g]).astype(BF16)
        him = jnp.where(fwd_all, fim_s[g], bim_s[g]).astype(BF16)
        hre = jnp.dot(to_batch_major, hre, preferred_element_type=F32).astype(BF16)
        him = jnp.dot(to_batch_major, him, preferred_element_type=F32).astype(BF16)
        xg = xg_s[g]
        y = (jnp.dot(xg.astype(BF16), tm_ref[g].astype(BF16), preferred_element_type=F32)
             + jnp.dot(hre, wre_ref[g].astype(BF16), preferred_element_type=F32)
             + jnp.dot(him, wim_ref[g].astype(BF16), preferred_element_type=F32)
             + d_ref[g] * xg)
        xg_s[g] = _gelu(y)

    for tile in range(ell // per_tile):
        by_token = block_transpose([xg_s[g, :, tile * lanes:(tile + 1) * lanes] for g in range(gpb)])
        for j in range(per_tile):
            o_refs[tile * per_tile + j][...] = by_token[j]


def s5_chunked(hn2, row_blk, n_seq, nc, mats, s0re, s0im, *, bp):
    tmat, win, wre, wim, ar, ai, dg = mats
    g, w, _ = tmat.shape
    p2 = ar.shape[-1]
    ell = S5_L
    lanes = 128
    gpb = lanes // S5_H
    d = hn2.shape[1] // ell
    m, mp = n_seq * nc, bp * nc
    assert bp & (bp - 1) == 0 and S5_H & (S5_H - 1) == 0 and g % gpb == 0
    col_tiles = d // lanes
    blk = lambda shape: pl.BlockSpec((gpb,) + shape, lambda i: (i, 0, 0))
    x_specs = [pl.BlockSpec((m, lanes), functools.partial(lambda i, l: (row_blk, l * col_tiles + i), l=l))
               for l in range(ell)]
    o_specs = [pl.BlockSpec((m, lanes), lambda i: (0, i)) for _ in range(ell)]
    outs = pl.pallas_call(
        functools.partial(_s5_kernel, bp=bp, n_seq=n_seq, nc=nc, gpb=gpb),
        grid=(g // gpb,),
        in_specs=x_specs + [blk((w, w)), blk((w, 2 * p2)), blk((p2, w)), blk((p2, w)),
                            blk((1, p2)), blk((1, p2)), blk((1, w)), blk((bp, p2)), blk((bp, p2))],
        out_specs=o_specs + [blk((bp, 2 * p2))],
        out_shape=[jax.ShapeDtypeStruct((m, d), F32)] * ell + [jax.ShapeDtypeStruct((g, bp, 2 * p2), F32)],
        scratch_shapes=[pltpu.VMEM((gpb, m, w), F32)] + [pltpu.VMEM((gpb, mp, p2), F32) for _ in range(6)],
        compiler_params=_cparams("arbitrary"),
        name="s5_chunked",
    )(*([hn2] * ell), tmat, win, wre, wim, ar, ai, dg, s0re, s0im)
    return outs[:ell], outs[ell]


def _cmul(ar, ai, br, bi):
    return ar * br - ai * bi, ar * bi + ai * br


def _s5_matrices(a_re, a_im, log_dt, b_re, b_im, c_re, c_im, d):
    _, g, p = a_re.shape
    h = b_re.shape[-1]
    ell = S5_L
    dt = jnp.exp(log_dt)[:, :, None]
    steps = jnp.arange(ell + 1, dtype=F32)[:, None, None, None]
    mag = jnp.exp(steps * (a_re * dt)[None])
    ang = steps * (a_im * dt)[None]
    amr, ami = mag * jnp.cos(ang), mag * jnp.sin(ang)
    nr, ni = amr[1] - 1.0, ami[1]
    den = a_re * a_re + a_im * a_im
    qr, qi = (nr * a_re + ni * a_im) / den, (ni * a_re - nr * a_im) / den
    bbr, bbi = _cmul(qr[..., None], qi[..., None], b_re, b_im)

    def win_dir(k, pr, pi):
        wr, wi = _cmul(pr[..., None], pi[..., None], bbr[k][None], bbi[k][None])
        return jnp.transpose(wr, (1, 0, 3, 2)), jnp.transpose(wi, (1, 0, 3, 2))

    wfr, wfi = win_dir(0, jnp.flip(amr[:ell, 0], 0), jnp.flip(ami[:ell, 0], 0))
    wbr, wbi = win_dir(1, amr[:ell, 1], ami[:ell, 1])
    win = jnp.concatenate([wfr, wbr, wfi, wbi], axis=-1).reshape(g, ell * h, 4 * p)

    def c_pow(k):
        return _cmul(c_re[k][None], c_im[k][None], amr[:, k][:, :, None, :], ami[:, k][:, :, None, :])

    cfr, cfi = c_pow(0)
    cbr, cbi = c_pow(1)
    to_gplh = lambda z: jnp.transpose(z, (1, 3, 0, 2))
    wre = jnp.concatenate([to_gplh(cfr[1:]), to_gplh(jnp.flip(cbr[1:], 0))], axis=1).reshape(g, 2 * p, ell * h)
    wim = jnp.concatenate([-to_gplh(cfi[1:]), -to_gplh(jnp.flip(cbi[1:], 0))], axis=1).reshape(g, 2 * p, ell * h)

    def lag_kernel(zr, zi, k):
        br = jnp.swapaxes(bbr[k], 1, 2)[None, :, None]
        bi = jnp.swapaxes(bbi[k], 1, 2)[None, :, None]
        return jnp.sum(zr[:ell, :, :, None, :] * br - zi[:ell, :, :, None, :] * bi, axis=-1)

    kf = lag_kernel(cfr, cfi, 0)
    kb = lag_kernel(cbr, cbi, 1)
    li = jnp.arange(ell)
    lag = li[None, :] - li[:, None]
    tf = jnp.where((lag >= 0)[:, :, None, None, None], kf[jnp.clip(lag, 0, ell - 1)], 0.0)
    tb = jnp.where((lag <= 0)[:, :, None, None, None], kb[jnp.clip(-lag, 0, ell - 1)], 0.0)
    tmat = jnp.transpose(tf + tb, (2, 0, 4, 1, 3)).reshape(g, ell * h, ell * h)
    ar = jnp.concatenate([amr[ell, 0], amr[ell, 1]], axis=-1)[:, None, :]
    ai = jnp.concatenate([ami[ell, 0], ami[ell, 1]], axis=-1)[:, None, :]
    dg = jnp.tile(d.reshape(g, 1, h), (1, ell, 1)).reshape(g, 1, ell * h)
    return tmat, win, wre, wim, ar, ai, dg


def s5_mixer_group(hn2, row0, n_seq, t, mats, s0):
    ar = mats[4]
    g = ar.shape[0]
    p = ar.shape[-1] // 2
    nc = t // S5_L
    bp = -(-n_seq // SUBLANES) * SUBLANES
    m = n_seq * nc
    assert (row0 // S5_L) % m == 0
    if s0 is None:
        s0re = jnp.zeros((g, bp, 2 * p), F32)
        s0im = s0re
    else:
        st = jnp.transpose(s0, (3, 0, 2, 1, 4)).reshape(g, n_seq, 2, 2 * p)
        st = jnp.pad(st, ((0, 0), (0, bp - n_seq), (0, 0), (0, 0)))
        s0re, s0im = st[:, :, 0], st[:, :, 1]
    u2, fin = s5_chunked(hn2, (row0 // S5_L) // m, n_seq, nc, mats, s0re, s0im, bp=bp)
    u = jnp.stack(u2, axis=1).reshape(m * S5_L, -1)
    fin = fin.reshape(g, bp, 2, 2, p)[:, :n_seq]
    return u, jnp.transpose(fin, (1, 3, 2, 0, 4))


def _softmax_pv(scores, values, sink):
    m = sink
    for s in scores:
        m = jnp.maximum(m, jnp.max(s, axis=-1, keepdims=True))
    den = jnp.exp(sink - m)
    acc = None
    for s, v in zip(scores, values):
        p = jnp.exp(s - m)
        den = den + jnp.sum(p, axis=-1, keepdims=True)
        pv = jnp.dot(p.astype(BF16), v, preferred_element_type=F32)
        acc = pv if acc is None else acc + pv
    return acc / den


def _attn_prompt_kernel(sink_ref, q_ref, k_ref, v_ref, o_ref):
    k = k_ref[...].astype(BF16)
    v = v_ref[...].astype(BF16)
    for h in range(N_KV):
        hs = slice(h * HEAD_DIM, (h + 1) * HEAD_DIM)
        kh, vh = k[:, hs], v[:, hs]
        for g in range(Q_PER_KV):
            c0 = (h * Q_PER_KV + g) * HEAD_DIM
            qg = q_ref[:, c0:c0 + HEAD_DIM].astype(BF16)
            s = lax.dot_general(qg, kh, NT_DIMS, preferred_element_type=F32) * ATT_SCALE
            o_ref[:, c0:c0 + HEAD_DIM] = _softmax_pv([s], [vh], sink_ref[h * Q_PER_KV + g])


def attn_prompt(qkv, sink, n_seq, t):
    dq = N_HEADS * HEAD_DIM
    kcol = dq // KV_W
    return pl.pallas_call(
        _attn_prompt_kernel,
        grid=(n_seq,),
        in_specs=[pl.BlockSpec(memory_space=pltpu.SMEM),
                  pl.BlockSpec((t, dq), lambda b: (b, 0)),
                  pl.BlockSpec((t, KV_W), lambda b: (b, kcol)),
                  pl.BlockSpec((t, KV_W), lambda b: (b, kcol + 1))],
        out_specs=pl.BlockSpec((t, dq), lambda b: (b, 0)),
        out_shape=jax.ShapeDtypeStruct((n_seq * t, dq), F32),
        compiler_params=_cparams("arbitrary"),
        name="attn_prompt",
    )(sink, qkv, qkv, qkv)


def _rope(x, cos, sin):
    w = x.shape[1]
    low = (lax.broadcasted_iota(jnp.int32, x.shape, 1) & (HEAD_DIM // 4)) == 0
    partner = jnp.where(low, pltpu.roll(x, w - HEAD_DIM // 4, axis=1), pltpu.roll(x, HEAD_DIM // 4, axis=1))
    return x * cos + partner * sin


def _attn_sample_kernel(sink_ref, q_ref, k_ref, v_ref, kc_ref, vc_ref, cos_ref, sin_ref, o_ref,
                        kw_s, vw_s, kc_s, vc_s, *, t):
    n = pl.program_id(1)
    blk = ATT_BLOCK

    @pl.when(n == 0)
    def _():
        zeros = jnp.zeros((blk, KV_W), BF16)
        kw_s[0:blk, :] = zeros
        vw_s[0:blk, :] = zeros
        kw_s[blk + t:2 * blk + t, :] = zeros
        vw_s[blk + t:2 * blk + t, :] = zeros
        kw_s[blk:blk + t, :] = _rope(k_ref[...], cos_ref[...], sin_ref[...]).astype(BF16)
        vw_s[blk:blk + t, :] = v_ref[...].astype(BF16)
        kc_s[...] = kc_ref[...].astype(BF16)
        vc_s[...] = vc_ref[...].astype(BF16)

    r0 = pl.multiple_of(n * blk, blk)
    cq = cos_ref[pl.ds(r0, blk), :]
    sq = sin_ref[pl.ds(r0, blk), :]
    kw = kw_s[pl.ds(r0, 3 * blk), :]
    vw = vw_s[pl.ds(r0, 3 * blk), :]
    qi = lax.broadcasted_iota(jnp.int32, (blk, 3 * blk), 0)
    kj = lax.broadcasted_iota(jnp.int32, (blk, 3 * blk), 1)
    kpos = n * blk - blk + kj
    valid = (jnp.abs(kj - blk - qi) <= WINDOW) & (kpos >= 0) & (kpos < t)
    for h in range(N_KV):
        hs = slice(h * HEAD_DIM, (h + 1) * HEAD_DIM)
        qh = _rope(q_ref[:, h * KV_W:(h + 1) * KV_W], cq, sq).astype(BF16)
        kh, vh, kch, vch = kw[:, hs], vw[:, hs], kc_s[:, hs], vc_s[:, hs]
        for g in range(Q_PER_KV):
            qg = qh[:, g * HEAD_DIM:(g + 1) * HEAD_DIM]
            s_loc = lax.dot_general(qg, kh, NT_DIMS, preferred_element_type=F32) * ATT_SCALE
            s_loc = jnp.where(valid, s_loc, NEG_INF)
            s_ctx = lax.dot_general(qg, kch, NT_DIMS, preferred_element_type=F32) * ATT_SCALE
            c0 = (h * Q_PER_KV + g) * HEAD_DIM
            o_ref[:, c0:c0 + HEAD_DIM] = _softmax_pv([s_loc, s_ctx], [vh, vch], sink_ref[h * Q_PER_KV + g])


def _rope_tables(t):
    quarter = HEAD_DIM // 4
    freqs = ROPE_BASE ** (-jnp.arange(quarter, dtype=F32) / quarter)
    pos = jnp.arange(t)
    ang_r = (pos // GRID_W).astype(F32)[:, None] * freqs
    ang_c = (pos % GRID_W).astype(F32)[:, None] * freqs
    cos = jnp.concatenate([jnp.cos(ang_r), jnp.cos(ang_r), jnp.cos(ang_c), jnp.cos(ang_c)], axis=-1)
    sin = jnp.concatenate([-jnp.sin(ang_r), jnp.sin(ang_r), -jnp.sin(ang_c), jnp.sin(ang_c)], axis=-1)
    return jnp.tile(cos, (1, N_KV)), jnp.tile(sin, (1, N_KV))


def attn_sample(qkv, row0, sink, n_seq, t, k_ctx, v_ctx):
    dq = N_HEADS * HEAD_DIM
    kcol = dq // KV_W
    nb = t // ATT_BLOCK
    lc = k_ctx.shape[1]
    cos, sin = _rope_tables(t)
    qblk0, sblk0 = row0 // ATT_BLOCK, row0 // t
    return pl.pallas_call(
        functools.partial(_attn_sample_kernel, t=t),
        grid=(n_seq, nb),
        in_specs=[pl.BlockSpec(memory_space=pltpu.SMEM),
                  pl.BlockSpec((ATT_BLOCK, dq), lambda b, n: (qblk0 + b * nb + n, 0)),
                  pl.BlockSpec((t, KV_W), lambda b, n: (sblk0 + b, kcol)),
                  pl.BlockSpec((t, KV_W), lambda b, n: (sblk0 + b, kcol + 1)),
                  pl.BlockSpec((None, lc, KV_W), lambda b, n: (b, 0, 0)),
                  pl.BlockSpec((None, lc, KV_W), lambda b, n: (b, 0, 0)),
                  pl.BlockSpec((t, KV_W), lambda b, n: (0, 0)),
                  pl.BlockSpec((t, KV_W), lambda b, n: (0, 0))],
        out_specs=pl.BlockSpec((ATT_BLOCK, dq), lambda b, n: (b * nb + n, 0)),
        out_shape=jax.ShapeDtypeStruct((n_seq * t, dq), F32),
        scratch_shapes=[pltpu.VMEM((t + 2 * ATT_BLOCK, KV_W), BF16), pltpu.VMEM((t + 2 * ATT_BLOCK, KV_W), BF16),
                        pltpu.VMEM((lc, KV_W), BF16), pltpu.VMEM((lc, KV_W), BF16)],
        compiler_params=_cparams("arbitrary", "arbitrary"),
        name="attn_sample",
    )(sink, qkv, qkv, qkv, k_ctx, v_ctx, cos, sin)


def _router_kernel(x_ref, g_ref, sc_ref, sh_ref, rw_ref, h_ref, aff_ref):
    h = _norm_mod(x_ref[...], g_ref[...], sc_ref[...], sh_ref[...])
    h_hi = h.astype(BF16)
    h_ref[...] = h_hi
    h_lo = (h - h_hi.astype(F32)).astype(BF16)
    rw = rw_ref[...]
    rw_hi = rw.astype(BF16)
    rw_lo = (rw - rw_hi.astype(F32)).astype(BF16)
    logits = (jnp.dot(h_hi, rw_hi, preferred_element_type=F32)
              + (jnp.dot(h_hi, rw_lo, preferred_element_type=F32)
                 + jnp.dot(h_lo, rw_hi, preferred_element_type=F32)))
    e = jnp.exp(logits - jnp.max(logits, axis=-1, keepdims=True))
    aff_ref[...] = e / jnp.sum(e, axis=-1, keepdims=True)


def norm_router(x, gains, mod, rows, layer, router_w):
    m, d = x.shape
    ne = router_w.shape[2]
    tm = rows.tm
    return pl.pallas_call(
        _router_kernel,
        grid=(m // tm,),
        in_specs=[pl.BlockSpec((tm, d), lambda i: (i, 0)),
                  _gain_spec(layer, d),
                  _mod_spec(rows, layer, d, 4, 0),
                  _mod_spec(rows, layer, d, 3, 0),
                  pl.BlockSpec((None, d, ne), lambda i: (layer, 0, 0))],
        out_specs=[pl.BlockSpec((tm, d), lambda i: (i, 0)),
                   pl.BlockSpec((tm, ne), lambda i: (i, 0))],
        out_shape=[jax.ShapeDtypeStruct((m, d), BF16), jax.ShapeDtypeStruct((m, ne), F32)],
        compiler_params=_cparams("arbitrary"),
        name="norm_router",
    )(x, gains, mod, mod, router_w)


def _moe_kernel(x_ref, wg_ref, wu_ref, wd_ref, gt_ref, seg_ref, g2_ref, o_ref, wg_bf, wu_bf, wd_bf, *, n_seg):
    wg_bf[...] = wg_ref[...].astype(BF16)
    wu_bf[...] = wu_ref[...].astype(BF16)
    wd_bf[...] = wd_ref[...].astype(BF16)
    rsub = min(MOE_ROW_SUB, x_ref.shape[0])
    for r in range(x_ref.shape[0] // rsub):
        rs = slice(r * rsub, (r + 1) * rsub)
        x = x_ref[rs, :]
        hg = jnp.dot(x, wg_bf[...], preferred_element_type=F32)
        hu = jnp.dot(x, wu_bf[...], preferred_element_type=F32)
        he = ((hg * jax.nn.sigmoid(hg)) * hu).astype(BF16)
        y = jnp.dot(he, wd_bf[...], preferred_element_type=F32)
        seg = seg_ref[rs, :]
        g2 = jnp.zeros(y.shape, F32)
        for s in range(n_seg):
            g2 = jnp.where(seg == s, g2_ref[s:s + 1, :], g2)
        o_ref[rs, :] = y * (gt_ref[rs, :] * g2)


def moe_experts(xe, w_gate, w_up, w_down, layer, gates, seg, mod, n_seg):
    ne, r, d = xe.shape
    dff = w_gate.shape[3]
    return pl.pallas_call(
        functools.partial(_moe_kernel, n_seg=n_seg),
        grid=(ne,),
        in_specs=[pl.BlockSpec((None, r, d), lambda e: (e, 0, 0)),
                  pl.BlockSpec((None, None, d, dff), lambda e: (layer, e, 0, 0)),
                  pl.BlockSpec((None, None, d, dff), lambda e: (layer, e, 0, 0)),
                  pl.BlockSpec((None, None, dff, d), lambda e: (layer, e, 0, 0)),
                  pl.BlockSpec((None, r, 1), lambda e: (e, 0, 0)),
                  pl.BlockSpec((None, r, 1), lambda e: (e, 0, 0)),
                  pl.BlockSpec((SUBLANES, None, d), lambda e: (layer, 0, 5))],
        out_specs=pl.BlockSpec((None, r, d), lambda e: (e, 0, 0)),
        out_shape=jax.ShapeDtypeStruct((ne, r, d), F32),
        scratch_shapes=[pltpu.VMEM((d, dff), BF16), pltpu.VMEM((d, dff), BF16), pltpu.VMEM((dff, d), BF16)],
        compiler_params=_cparams("arbitrary"),
        name="moe_experts",
    )(xe, w_gate, w_up, w_down, gates, seg, mod)


def moe_layer(x, gains, mod, rows, layer, router_w, w_gate, w_up, w_down, n_seg):
    m, d = x.shape
    h2, aff = norm_router(x, gains, mod, rows, layer, router_w)
    idx_l, gate_l = [], []
    off = 0
    for n in (rows.n_prompt, rows.n_sample):
        cap = (EC_FACTOR * n) // N_EXPERTS
        gt, ix = lax.top_k(aff[off:off + n].T, cap)
        idx_l.append(ix + off)
        gate_l.append(gt)
        off += n
    idx = jnp.concatenate(idx_l, axis=1)
    gates = jnp.concatenate(gate_l, axis=1)
    seg = jnp.where(idx < rows.n_prompt, 0, 1 + (idx - rows.n_prompt) // rows.t_sample)
    xe = h2[idx]
    ye = moe_experts(xe, w_gate, w_up, w_down, layer, gates[..., None], seg[..., None], mod, n_seg)
    return x.at[idx.reshape(-1)].add(ye.reshape(-1, d))


def kernel(x_prompt, x_sample, state_rglru, state_s5, cache_k, cache_v, c, c_ctx, ada_w, ada_b, norm1_g, norm2_g, rg_w_in, rg_conv_w, rg_conv_b, rg_w_a, rg_b_a, rg_w_x, rg_b_x, rg_lambda, rg_w_out, s5_a_re, s5_a_im, s5_log_dt, s5_b_re, s5_b_im, s5_c_re, s5_c_im, s5_d, s5_w_glu, attn_w_qkv, attn_w_o, attn_sink, router_w, moe_w_gate, moe_w_up, moe_w_down, final_norm_g):
    bp_, tp, d = x_prompt.shape
    bs, ts, _ = x_sample.shape
    n_p, n_s = bp_ * tp, bs * ts
    depth = ada_w.shape[0]
    rows = _Rows(n_p, n_s, ts)
    assert bs + 1 <= SUBLANES and n_p % ts == 0

    x = jnp.concatenate([x_prompt.reshape(n_p, d), x_sample.reshape(n_s, d)], axis=0)
    cond = jnp.concatenate([c_ctx[None, :], c, jnp.zeros((SUBLANES - 1 - bs, d), F32)], axis=0)
    mod_all = ada_modulation_all(cond, ada_w, ada_b)
    mod = mod_all.reshape(depth * SUBLANES, 1, 6 * d)
    g1 = norm1_g.reshape(depth, 1, d)
    g2 = norm2_g.reshape(depth, 1, d)

    new_rg, new_s5, new_k, new_v = [], [], [], []
    for l in range(depth):
        kind, j = l % 3, l // 3
        if kind == 0:
            gu = norm_mod_matmul(x, g1, mod, rows, l, rg_w_in, j, tn=1024)
            args = (j, rg_conv_w, rg_conv_b, rg_w_a, rg_b_a, rg_w_x, rg_b_x, rg_lambda)
            r = gu.shape[1] // 2
            yp, fin = rglru_scan(gu, 0, bp_, tp, *args, jnp.zeros((bp_, 1, 2, r), F32), 0)
            ys, _ = rglru_scan(gu, n_p, bs, ts, *args, state_rglru, j)
            new_rg.append(fin)
            x = matmul_gated_residual(yp, ys, rg_w_out, j, x, mod, rows, l)
        elif kind == 1:
            hn = norm_mod(x, g1, mod, rows, l)
            mats = _s5_matrices(s5_a_re[j], s5_a_im[j], s5_log_dt[j], s5_b_re[j], s5_b_im[j],
                                s5_c_re[j], s5_c_im[j], s5_d[j])
            hn2 = hn.reshape((n_p + n_s) // S5_L, S5_L * d)
            up, st = s5_mixer_group(hn2, 0, bp_, tp, mats, None)
            us, _ = s5_mixer_group(hn2, n_p, bs, ts, mats, state_s5[:, j])
            new_s5.append(st)
            x = glu_gated_residual(up, us, s5_w_glu, j, x, mod, rows, l)
        else:
            qkv = norm_mod_matmul(x, g1, mod, rows, l, attn_w_qkv, j, tn=512)
            dq = N_HEADS * HEAD_DIM
            new_k.append(qkv[:n_p, dq:dq + KV_W].reshape(bp_, tp, N_KV, HEAD_DIM))
            new_v.append(qkv[:n_p, dq + KV_W:].reshape(bp_, tp, N_KV, HEAD_DIM))
            op = attn_prompt(qkv, attn_sink[j], bp_, tp)
            lc = cache_k.shape[2]
            os_ = attn_sample(qkv, n_p, attn_sink[j], bs, ts,
                              cache_k[:, j].reshape(bs, lc, KV_W), cache_v[:, j].reshape(bs, lc, KV_W))
            x = matmul_gated_residual(op, os_, attn_w_o, j, x, mod, rows, l)
        x = moe_layer(x, g2, mod, rows, l, router_w, moe_w_gate, moe_w_up, moe_w_down, bs + 1)

    y = final_norm(x, final_norm_g, rows.tm)
    return (y[:n_p].reshape(bp_, tp, d), y[n_p:].reshape(bs, ts, d),
            jnp.stack(new_rg, axis=1), jnp.stack(new_s5, axis=1),
            jnp.stack(new_k, axis=1), jnp.stack(new_v, axis=1))
```

```python
import functools
import math

import jax
import jax.numpy as jnp
from jax import lax
from jax.experimental import pallas as pl
from jax.experimental.pallas import tpu as pltpu

F32 = jnp.float32
BF16 = jnp.bfloat16
HIGHEST = lax.Precision.HIGHEST

EPS = 1e-6
RG_C = 8.0
RG_BS = 128
S5_H = 16
S5_L = 16
N_HEADS = 16
N_KV = 4
Q_PER_KV = N_HEADS // N_KV
HEAD_DIM = 64
KV_W = N_KV * HEAD_DIM
GRID_W = 64
WINDOW = 128
ATT_BLOCK = 128
ROPE_BASE = 10000.0
ATT_SCALE = HEAD_DIM ** -0.5
assert math.frexp(ATT_SCALE)[0] == 0.5
NEG_INF = -1e30
N_EXPERTS = 16
EC_FACTOR = 2
SUBLANES = 8
ROW_TILE = 512
MOE_ROW_SUB = 256
VMEM_LIMIT = 56 * 1024 * 1024
NT_DIMS = (((1,), (1,)), ((), ()))


def _cparams(*sem):
    return pltpu.CompilerParams(dimension_semantics=sem, vmem_limit_bytes=VMEM_LIMIT)


def _gelu(x):
    return x * (0.5 * (1.0 + jnp.tanh(math.sqrt(2.0 / math.pi) * (x + 0.044715 * (x * x * x)))))


def _sigmoid(x):
    return 0.5 * jnp.tanh(0.5 * x) + 0.5


def _norm_mod(x, g, sc, sh):
    ms = jnp.mean(x * x, axis=-1, keepdims=True)
    return ((x * lax.rsqrt(ms + EPS)) * g) * (1.0 + sc) + sh


def _mod_kernel(c_ref, w_ref, b_ref, o_ref):
    c = c_ref[...]
    s = (c * jax.nn.sigmoid(c)).astype(BF16)
    o_ref[...] = jnp.dot(s, w_ref[...].astype(BF16), preferred_element_type=F32) + b_ref[...]


def ada_modulation_all(cond, ada_w, ada_b):
    n_layers, d, n = ada_w.shape
    tn = 1536
    return pl.pallas_call(
        _mod_kernel,
        grid=(n_layers, n // tn),
        in_specs=[pl.BlockSpec((SUBLANES, d), lambda l, j: (0, 0)),
                  pl.BlockSpec((None, d, tn), lambda l, j: (l, 0, j)),
                  pl.BlockSpec((None, 1, tn), lambda l, j: (l, 0, j))],
        out_specs=pl.BlockSpec((None, SUBLANES, tn), lambda l, j: (l, 0, j)),
        out_shape=jax.ShapeDtypeStruct((n_layers, SUBLANES, n), F32),
        compiler_params=_cparams("arbitrary", "arbitrary"),
        name="ada_mod",
    )(cond, ada_w, ada_b.reshape(n_layers, 1, n))


class _Rows:
    def __init__(self, n_prompt, n_sample, t_sample):
        self.n_prompt = n_prompt
        self.n_sample = n_sample
        self.t_sample = t_sample
        self.tm = min(ROW_TILE, n_prompt, t_sample)
        assert n_prompt % self.tm == 0 and t_sample % self.tm == 0
        self.prompt_blocks = n_prompt // self.tm
        self.sample_blocks = n_sample // self.tm

    def seg(self, i):
        r = i * self.tm
        return jnp.where(r < self.n_prompt, 0, 1 + lax.div(r - self.n_prompt, self.t_sample))


def _mod_spec(rows, layer, width, chunk, m_axis, chunk_axis=None):
    def imap(*ids):
        c = chunk if chunk_axis is None else chunk + ids[chunk_axis]
        return (layer * SUBLANES + rows.seg(ids[m_axis]), 0, c)
    return pl.BlockSpec((None, 1, width), imap)


def _gain_spec(layer, d):
    return pl.BlockSpec((None, 1, d), lambda *ids: (layer, 0, 0))


def _nm_kernel(x_ref, g_ref, sc_ref, sh_ref, w_ref, o_ref, wbf_ref):
    @pl.when(pl.program_id(1) == 0)
    def _():
        wbf_ref[...] = w_ref[...].astype(BF16)
    h = _norm_mod(x_ref[...], g_ref[...], sc_ref[...], sh_ref[...])
    o_ref[...] = jnp.dot(h.astype(BF16), wbf_ref[...], preferred_element_type=F32)


def norm_mod_matmul(x, gains, mod, rows, layer, w, wl):
    m, d = x.shape
    n = w.shape[2]
    tm = rows.tm
    tn = n
    return pl.pallas_call(
        _nm_kernel,
        grid=(n // tn, m // tm),
        in_specs=[pl.BlockSpec((tm, d), lambda j, i: (i, 0)),
                  _gain_spec(layer, d),
                  _mod_spec(rows, layer, d, 1, 1),
                  _mod_spec(rows, layer, d, 0, 1),
                  pl.BlockSpec((None, d, tn), lambda j, i: (wl, 0, j))],
        out_specs=pl.BlockSpec((tm, tn), lambda j, i: (i, j)),
        out_shape=jax.ShapeDtypeStruct((m, n), F32),
        scratch_shapes=[pltpu.VMEM((d, tn), BF16)],
        compiler_params=_cparams("arbitrary", "arbitrary"),
        name="norm_mod_matmul",
    )(x, gains, mod, mod, w)


def _norm_only_kernel(x_ref, g_ref, sc_ref, sh_ref, o_ref):
    o_ref[...] = _norm_mod(x_ref[...], g_ref[...], sc_ref[...], sh_ref[...])


def norm_mod(x, gains, mod, rows, layer):
    m, d = x.shape
    tm = rows.tm
    return pl.pallas_call(
        _norm_only_kernel,
        grid=(m // tm,),
        in_specs=[pl.BlockSpec((tm, d), lambda i: (i, 0)),
                  _gain_spec(layer, d),
                  _mod_spec(rows, layer, d, 1, 0),
                  _mod_spec(rows, layer, d, 0, 0)],
        out_specs=pl.BlockSpec((tm, d), lambda i: (i, 0)),
        out_shape=jax.ShapeDtypeStruct((m, d), F32),
        compiler_params=_cparams("arbitrary"),
        name="norm_mod",
    )(x, gains, mod, mod)


def _final_norm_kernel(x_ref, g_ref, o_ref):
    x = x_ref[...]
    ms = jnp.mean(x * x, axis=-1, keepdims=True)
    o_ref[...] = (x * lax.rsqrt(ms + EPS)) * g_ref[...]


def final_norm(x, g, tm):
    m, d = x.shape
    return pl.pallas_call(
        _final_norm_kernel,
        grid=(m // tm,),
        in_specs=[pl.BlockSpec((tm, d), lambda i: (i, 0)),
                  pl.BlockSpec((1, d), lambda i: (0, 0))],
        out_specs=pl.BlockSpec((tm, d), lambda i: (i, 0)),
        out_shape=jax.ShapeDtypeStruct((m, d), F32),
        compiler_params=_cparams("arbitrary"),
        name="final_norm",
    )(x, g.reshape(1, d))


def _two_group_specs(rows, k):
    npb, nsb, tm = rows.prompt_blocks, rows.sample_blocks, rows.tm
    return [pl.BlockSpec((tm, k), lambda j, i: (jnp.minimum(i, npb - 1), 0)),
            pl.BlockSpec((tm, k), lambda j, i: (jnp.clip(i - npb, 0, nsb - 1), 0))]


def _mmres_kernel(ap_ref, as_ref, w_ref, r_ref, gt_ref, o_ref, wbf_ref, *, npb):
    i = pl.program_id(1)

    @pl.when(i == 0)
    def _():
        wbf_ref[...] = w_ref[...].astype(BF16)

    def emit(a_ref):
        acc = jnp.dot(a_ref[...].astype(BF16), wbf_ref[...], preferred_element_type=F32)
        o_ref[...] = r_ref[...] + gt_ref[...] * acc

    pl.when(i < npb)(lambda: emit(ap_ref))
    pl.when(i >= npb)(lambda: emit(as_ref))


def matmul_gated_residual(a_p, a_s, w, wl, resid, mod, rows, layer):
    k = a_p.shape[1]
    m, d = resid.shape
    tm = rows.tm
    tn = d
    nt = d // tn
    return pl.pallas_call(
        functools.partial(_mmres_kernel, npb=rows.prompt_blocks),
        grid=(nt, m // tm),
        in_specs=_two_group_specs(rows, k) + [
            pl.BlockSpec((None, k, tn), lambda j, i: (wl, 0, j)),
            pl.BlockSpec((tm, tn), lambda j, i: (i, j)),
            _mod_spec(rows, layer, tn, 2 * nt, 1, chunk_axis=0)],
        out_specs=pl.BlockSpec((tm, tn), lambda j, i: (i, j)),
        out_shape=jax.ShapeDtypeStruct((m, d), F32),
        scratch_shapes=[pltpu.VMEM((k, tn), BF16)],
        compiler_params=_cparams("arbitrary", "arbitrary"),
        name="matmul_gated_residual",
    )(a_p, a_s, w, resid, mod)


def _glures_kernel(ap_ref, as_ref, wv_ref, wg_ref, r_ref, gt_ref, o_ref, wv_bf, wg_bf, *, npb):
    i = pl.program_id(1)

    @pl.when(i == 0)
    def _():
        wv_bf[...] = wv_ref[...].astype(BF16)
        wg_bf[...] = wg_ref[...].astype(BF16)

    def emit(a_ref):
        a = a_ref[...].astype(BF16)
        v = jnp.dot(a, wv_bf[...], preferred_element_type=F32)
        g = jnp.dot(a, wg_bf[...], preferred_element_type=F32)
        o_ref[...] = r_ref[...] + gt_ref[...] * (v * jax.nn.sigmoid(g))

    pl.when(i < npb)(lambda: emit(ap_ref))
    pl.when(i >= npb)(lambda: emit(as_ref))


def glu_gated_residual(a_p, a_s, w_glu, wl, resid, mod, rows, layer):
    k = a_p.shape[1]
    m, d = resid.shape
    tm = rows.tm
    tn = d
    nt = d // tn
    return pl.pallas_call(
        functools.partial(_glures_kernel, npb=rows.prompt_blocks),
        grid=(nt, m // tm),
        in_specs=_two_group_specs(rows, k) + [
            pl.BlockSpec((None, k, tn), lambda j, i: (wl, 0, j)),
            pl.BlockSpec((None, k, tn), lambda j, i: (wl, 0, nt + j)),
            pl.BlockSpec((tm, tn), lambda j, i: (i, j)),
            _mod_spec(rows, layer, tn, 2 * nt, 1, chunk_axis=0)],
        out_specs=pl.BlockSpec((tm, tn), lambda j, i: (i, j)),
        out_shape=jax.ShapeDtypeStruct((m, d), F32),
        scratch_shapes=[pltpu.VMEM((k, tn), BF16), pltpu.VMEM((k, tn), BF16)],
        compiler_params=_cparams("arbitrary", "arbitrary"),
        name="glu_gated_residual",
    )(a_p, a_s, w_glu, w_glu, resid, mod)


def _rglru_kernel(gate_ref, u_ref, cw_ref, cb_ref, wa_ref, ba_ref, wx_ref, bx_ref, lam_ref, h0_ref,
                  y_ref, fin_ref, af_s, bf_s, ab_s, bb_s, hf_s, hb_s):
    t, cw = u_ref.shape
    u = u_ref[...]
    row = lax.broadcasted_iota(jnp.int32, (t, cw), 0)

    def shifted(x, k):
        if k > 0:
            return jnp.where(row >= k, pltpu.roll(x, k, axis=0), 0.0)
        return jnp.where(row < t + k, pltpu.roll(x, t + k, axis=0), 0.0)

    cwv = cw_ref[...]
    uc = (cwv[0:1] * shifted(u, 2) + cwv[1:2] * shifted(u, 1) + cwv[2:3] * u
          + cwv[3:4] * shifted(u, -1) + cb_ref[...])

    a_scr = (af_s, ab_s)
    b_scr = (bf_s, bb_s)
    for k in range(2):
        nl = -lam_ref[k:k + 1, :]
        sp = jnp.maximum(nl, 0.0) + jnp.log1p(jnp.exp(-jnp.abs(nl)))
        for hh in range(cw // RG_BS):
            sl = slice(hh * RG_BS, (hh + 1) * RG_BS)
            uh = uc[:, sl]
            ub = uh.astype(BF16)
            r = _sigmoid(jnp.dot(ub, wa_ref[k, hh].astype(BF16), preferred_element_type=F32) + ba_ref[k:k + 1, sl])
            i = _sigmoid(jnp.dot(ub, wx_ref[k, hh].astype(BF16), preferred_element_type=F32) + bx_ref[k:k + 1, sl])
            log_a = (-RG_C * r) * sp[:, sl]
            a = jnp.exp(log_a)
            a_scr[k][:, sl] = a
            b_scr[k][:, sl] = jnp.sqrt(jnp.tanh(-log_a) * (a * a + 1.0)) * (i * uh)

    nblk = t // SUBLANES
    srow = lax.broadcasted_iota(jnp.int32, (SUBLANES, cw), 0)

    def body(n, carry):
        cf, cb = carry
        rf = pl.multiple_of(n * SUBLANES, SUBLANES)
        rb = pl.multiple_of((nblk - 1 - n) * SUBLANES, SUBLANES)
        a = af_s[pl.ds(rf, SUBLANES), :]
        b = bf_s[pl.ds(rf, SUBLANES), :]
        a2 = ab_s[pl.ds(rb, SUBLANES), :]
        b2 = bb_s[pl.ds(rb, SUBLANES), :]
        for s in (1, 2, 4):
            m = srow >= s
            b = jnp.where(m, a * pltpu.roll(b, s, axis=0) + b, b)
            a = jnp.where(m, a * pltpu.roll(a, s, axis=0), a)
            m2 = srow < SUBLANES - s
            b2 = jnp.where(m2, a2 * pltpu.roll(b2, SUBLANES - s, axis=0) + b2, b2)
            a2 = jnp.where(m2, a2 * pltpu.roll(a2, SUBLANES - s, axis=0), a2)
        hf = a * cf + b
        hb = a2 * cb + b2
        hf_s[pl.ds(rf, SUBLANES), :] = hf
        hb_s[pl.ds(rb, SUBLANES), :] = hb
        return hf[SUBLANES - 1:SUBLANES, :], hb[0:1, :]

    cf, cb = lax.fori_loop(0, nblk, body, (h0_ref[0:1, :], h0_ref[1:2, :]))
    fin_ref[0:1, :] = cf
    fin_ref[1:2, :] = cb
    y_ref[...] = (hf_s[...] + hb_s[...]) * _gelu(gate_ref[...])


def rglru_scan(gu, row0, n_seq, t, j, conv_w, conv_b, w_a, b_a, w_x, b_x, lam, h0, h0_j, *, cw=256):
    r = gu.shape[1] // 2
    nh = cw // RG_BS
    blk0 = row0 // t
    nc = r // cw
    scr = [pltpu.VMEM((t, cw), F32) for _ in range(6)]
    vec2 = pl.BlockSpec((None, 2, cw), lambda b, c: (j, 0, c))
    gatew = pl.BlockSpec((None, 2, nh, RG_BS, RG_BS), lambda b, c: (j, 0, c, 0, 0))
    return pl.pallas_call(
        _rglru_kernel,
        grid=(n_seq, nc),
        in_specs=[pl.BlockSpec((t, cw), lambda b, c: (blk0 + b, c)),
                  pl.BlockSpec((t, cw), lambda b, c: (blk0 + b, nc + c)),
                  pl.BlockSpec((None, 4, cw), lambda b, c: (j, 0, c)),
                  pl.BlockSpec((None, 1, cw), lambda b, c: (j, 0, c)),
                  gatew, vec2, gatew, vec2, vec2,
                  pl.BlockSpec((None, None, 2, cw), lambda b, c: (b, h0_j, 0, c))],
        out_specs=[pl.BlockSpec((t, cw), lambda b, c: (b, c)),
                   pl.BlockSpec((None, 2, cw), lambda b, c: (b, 0, c))],
        out_shape=[jax.ShapeDtypeStruct((n_seq * t, r), F32),
                   jax.ShapeDtypeStruct((n_seq, 2, r), F32)],
        scratch_shapes=scr,
        compiler_params=_cparams("arbitrary", "arbitrary"),
        name="rglru_scan",
    )(gu, gu, conv_w, conv_b.reshape(conv_b.shape[0], 1, r), w_a, b_a, w_x, b_x, lam, h0)


def _s5_kernel(*refs, bp, n_seq, nc, gpb):
    ell, gw = S5_L, S5_H
    x_refs = refs[:ell]
    tm_ref, win_ref, wre_ref, wim_ref, ar_ref, ai_ref, d_ref, s0re_ref, s0im_ref = refs[ell:ell + 9]
    o_refs = refs[ell + 9:2 * ell + 9]
    fin_ref = refs[2 * ell + 9]
    xg_s, ure_s, uim_s, fre_s, fim_s, bre_s, bim_s = refs[2 * ell + 10:]
    m, mp = n_seq * nc, bp * nc
    lanes = x_refs[0].shape[1]
    per_tile = lanes // gw
    half = ure_s.shape[2] // 2
    lane_grp = lax.shift_right_logical(lax.broadcasted_iota(jnp.int32, (m, lanes), 1), gw.bit_length() - 1)

    def perm(shape, chunk_major_axis):
        i = lax.broadcasted_iota(jnp.int32, shape, chunk_major_axis)
        j = lax.broadcasted_iota(jnp.int32, shape, 1 - chunk_major_axis)
        b = i & (bp - 1)
        c = lax.shift_right_logical(i, bp.bit_length() - 1)
        return jnp.where((j == b * nc + c) & (b < n_seq), 1.0, 0.0).astype(BF16)

    to_chunk_major = perm((mp, m), 0)
    to_batch_major = perm((m, mp), 1)

    def block_transpose(v):
        k = per_tile // 2
        while k >= 1:
            low = (lane_grp & k) == 0
            nxt = list(v)
            for i in range(per_tile):
                if i & k == 0:
                    a, b = v[i], v[i + k]
                    nxt[i] = jnp.where(low, a, pltpu.roll(b, k * gw, axis=1))
                    nxt[i + k] = jnp.where(low, pltpu.roll(a, lanes - k * gw, axis=1), b)
            v = nxt
            k //= 2
        return v

    for tile in range(ell // per_tile):
        by_group = block_transpose([x_refs[tile * per_tile + j][...] for j in range(per_tile)])
        for g in range(gpb):
            xg_s[g, :, tile * lanes:(tile + 1) * lanes] = by_group[g]

    for g in range(gpb):
        xg = xg_s[g]
        xp = jnp.dot(to_chunk_major, xg.astype(BF16), preferred_element_type=F32).astype(BF16)
        u = jnp.dot(xp, win_ref[g].astype(BF16), preferred_element_type=F32)
        ure_s[g] = u[:, :2 * half]
        uim_s[g] = u[:, 2 * half:]

    is_fwd = lax.broadcasted_iota(jnp.int32, (bp, 2 * half), 1) < half
    ar = [ar_ref[g] for g in range(gpb)]
    ai = [ai_ref[g] for g in range(gpb)]

    def body(k, carry):
        rf = pl.multiple_of(k * bp, bp)
        rb = pl.multiple_of((nc - 1 - k) * bp, bp)
        out = []
        for g in range(gpb):
            re, im = carry[2 * g], carry[2 * g + 1]
            fre_s[g, pl.ds(rf, bp), :] = re
            fim_s[g, pl.ds(rf, bp), :] = im
            bre_s[g, pl.ds(rb, bp), :] = re
            bim_s[g, pl.ds(rb, bp), :] = im
            ure = jnp.where(is_fwd, ure_s[g, pl.ds(rf, bp), :], ure_s[g, pl.ds(rb, bp), :])
            uim = jnp.where(is_fwd, uim_s[g, pl.ds(rf, bp), :], uim_s[g, pl.ds(rb, bp), :])
            out += [ar[g] * re - ai[g] * im + ure, ar[g] * im + ai[g] * re + uim]
        return tuple(out)

    init = tuple(r[g] for g in range(gpb) for r in (s0re_ref, s0im_ref))
    fin = lax.fori_loop(0, nc, body, init)
    fwd_all = lax.broadcasted_iota(jnp.int32, (mp, 2 * half), 1) < half
    for g in range(gpb):
        fin_ref[g, :, :2 * half] = fin[2 * g]
        fin_ref[g, :, 2 * half:] = fin[2 * g + 1]
        hre = jnp.where(fwd_all, fre_s[g], bre_s[g]).astype(BF16)
        him = jnp.where(fwd_all, fim_s[g], bim_s[g]).astype(BF16)
        hre = jnp.dot(to_batch_major, hre, preferred_element_type=F32).astype(BF16)
        him = jnp.dot(to_batch_major, him, preferred_element_type=F32).astype(BF16)
        xg = xg_s[g]
        y = (jnp.dot(xg.astype(BF16), tm_ref[g].astype(BF16), preferred_element_type=F32)
             + jnp.dot(hre, wre_ref[g].astype(BF16), preferred_element_type=F32)
             + jnp.dot(him, wim_ref[g].astype(BF16), preferred_element_type=F32)
             + d_ref[g] * xg)
        xg_s[g] = _gelu(y)

    for tile in range(ell // per_tile):
        by_token = block_transpose([xg_s[g, :, tile * lanes:(tile + 1) * lanes] for g in range(gpb)])
        for j in range(per_tile):
            o_refs[tile * per_tile + j][...] = by_token[j]


def s5_chunked(hn2, row_blk, n_seq, nc, mats, s0re, s0im, *, bp):
    tmat, win, wre, wim, ar, ai, dg = mats
    g, w, _ = tmat.shape
    p2 = ar.shape[-1]
    ell = S5_L
    lanes = 128
    gpb = lanes // S5_H
    d = hn2.shape[1] // ell
    m, mp = n_seq * nc, bp * nc
    assert bp & (bp - 1) == 0 and S5_H & (S5_H - 1) == 0 and g % gpb == 0
    col_tiles = d // lanes
    blk = lambda shape: pl.BlockSpec((gpb,) + shape, lambda i: (i, 0, 0))
    x_specs = [pl.BlockSpec((m, lanes), functools.partial(lambda i, l: (row_blk, l * col_tiles + i), l=l))
               for l in range(ell)]
    o_specs = [pl.BlockSpec((m, lanes), lambda i: (0, i)) for _ in range(ell)]
    outs = pl.pallas_call(
        functools.partial(_s5_kernel, bp=bp, n_seq=n_seq, nc=nc, gpb=gpb),
        grid=(g // gpb,),
        in_specs=x_specs + [blk((w, w)), blk((w, 2 * p2)), blk((p2, w)), blk((p2, w)),
                            blk((1, p2)), blk((1, p2)), blk((1, w)), blk((bp, p2)), blk((bp, p2))],
        out_specs=o_specs + [blk((bp, 2 * p2))],
        out_shape=[jax.ShapeDtypeStruct((m, d), F32)] * ell + [jax.ShapeDtypeStruct((g, bp, 2 * p2), F32)],
        scratch_shapes=[pltpu.VMEM((gpb, m, w), F32)] + [pltpu.VMEM((gpb, mp, p2), F32) for _ in range(6)],
        compiler_params=_cparams("arbitrary"),
        name="s5_chunked",
    )(*([hn2] * ell), tmat, win, wre, wim, ar, ai, dg, s0re, s0im)
    return outs[:ell], outs[ell]


def _cmul(ar, ai, br, bi):
    return ar * br - ai * bi, ar * bi + ai * br


def _s5_prep_kernel(are_ref, aim_ref, ldt_ref, btr_ref, bti_ref, cr_ref, ci_ref,
                    tm_ref, win_ref, wre_ref, wim_ref, ar_ref, ai_ref):
    ell = S5_L
    h, p2 = cr_ref.shape
    w = ell * h
    a_re, a_im = are_ref[...], aim_ref[...]
    dt = jnp.exp(ldt_ref[...])
    steps = lax.broadcasted_iota(jnp.int32, (3 * SUBLANES, p2), 0).astype(F32)
    mag = jnp.exp(steps * (a_re * dt))
    ang = steps * (a_im * dt)
    pw_r, pw_i = mag * jnp.cos(ang), mag * jnp.sin(ang)
    nr, ni = pw_r[1:2] - 1.0, pw_i[1:2]
    den = a_re * a_re + a_im * a_im
    qr, qi = (nr * a_re + ni * a_im) / den, (ni * a_re - nr * a_im) / den
    bb_r, bb_i = _cmul(qr, qi, btr_ref[...], bti_ref[...])
    c_r, c_i = cr_ref[...], ci_ref[...]
    fwd = lax.broadcasted_iota(jnp.int32, (1, p2), 1) < p2 // 2

    def power_rows(m_fwd, m_bwd):
        return (jnp.where(fwd, pw_r[m_fwd:m_fwd + 1], pw_r[m_bwd:m_bwd + 1]),
                jnp.where(fwd, pw_i[m_fwd:m_fwd + 1], pw_i[m_bwd:m_bwd + 1]))

    def stack(x_r, x_i, powers):
        parts = [_cmul(x_r, x_i, *power_rows(*powers(l))) for l in range(ell)]
        return (jnp.concatenate([q[0] for q in parts], axis=0), jnp.concatenate([q[1] for q in parts], axis=0))

    win_r, win_i = stack(bb_r, bb_i, lambda l: (ell - 1 - l, l))
    win_ref[...] = jnp.concatenate([win_r, win_i], axis=1).astype(win_ref.dtype)
    z_r, z_i = stack(c_r, c_i, lambda l: (l + 1, ell - l))
    wre_ref[...] = z_r.T.astype(wre_ref.dtype)
    wim_ref[...] = (-z_i).T.astype(wim_ref.dtype)
    k_r, k_i = stack(c_r, c_i, lambda m: (m, ell - 1 - m))
    mask_f = jnp.where(fwd, 1.0, 0.0)

    def lag_rows(mask):
        return (lax.dot_general(bb_r * mask, k_r, NT_DIMS, precision=HIGHEST, preferred_element_type=F32)
                - lax.dot_general(bb_i * mask, k_i, NT_DIMS, precision=HIGHEST, preferred_element_type=F32))

    kf = lag_rows(mask_f)
    kb = lag_rows(1.0 - mask_f)
    lane = lax.broadcasted_iota(jnp.int32, (h, w), 1)
    blocks = []
    for li in range(ell):
        f_part = kf if li == 0 else pltpu.roll(kf, li * h, axis=1)
        s_b = (w - (ell - 1 - li) * h) % w
        b_part = kb if s_b == 0 else pltpu.roll(kb, s_b, axis=1)
        blocks.append(jnp.where(lane >= li * h, f_part, 0.0) + jnp.where(lane < (li + 1) * h, b_part, 0.0))
    tm_ref[...] = jnp.concatenate(blocks, axis=0).astype(tm_ref.dtype)
    ar_ref[...] = pw_r[ell:ell + 1]
    ai_ref[...] = pw_i[ell:ell + 1]


def s5_chunk_operators(a_re, a_im, log_dt, b_re, b_im, c_re, c_im, d):
    _, g, p = a_re.shape
    h = b_re.shape[-1]
    w = S5_L * h
    two_dir = lambda x: jnp.transpose(x, (1, 0, 2)).reshape(g, 1, 2 * p)
    ldt = two_dir(jnp.broadcast_to(log_dt[:, :, None], (2, g, p)))
    bt = lambda x: jnp.transpose(x, (1, 3, 0, 2)).reshape(g, h, 2 * p)
    ct = lambda x: jnp.transpose(x, (1, 2, 0, 3)).reshape(g, h, 2 * p)
    per_g = lambda shape: pl.BlockSpec((None,) + shape, lambda i: (i, 0, 0))
    tmat, win, wre, wim, ar, ai = pl.pallas_call(
        _s5_prep_kernel,
        grid=(g,),
        in_specs=[per_g((1, 2 * p))] * 3 + [per_g((h, 2 * p))] * 4,
        out_specs=[per_g((w, w)), per_g((w, 4 * p)), per_g((2 * p, w)), per_g((2 * p, w)),
                   per_g((1, 2 * p)), per_g((1, 2 * p))],
        out_shape=[jax.ShapeDtypeStruct((g, w, w), BF16), jax.ShapeDtypeStruct((g, w, 4 * p), BF16),
                   jax.ShapeDtypeStruct((g, 2 * p, w), BF16), jax.ShapeDtypeStruct((g, 2 * p, w), BF16),
                   jax.ShapeDtypeStruct((g, 1, 2 * p), F32), jax.ShapeDtypeStruct((g, 1, 2 * p), F32)],
        compiler_params=_cparams("arbitrary"),
        name="s5_chunk_operators",
    )(two_dir(a_re), two_dir(a_im), ldt, bt(b_re), bt(b_im), ct(c_re), ct(c_im))
    dg = jnp.tile(d.reshape(g, 1, h), (1, S5_L, 1)).reshape(g, 1, w)
    return tmat, win, wre, wim, ar, ai, dg


def s5_mixer_group(hn2, row0, n_seq, t, mats, s0):
    ar = mats[4]
    g = ar.shape[0]
    p = ar.shape[-1] // 2
    nc = t // S5_L
    bp = -(-n_seq // SUBLANES) * SUBLANES
    m = n_seq * nc
    assert (row0 // S5_L) % m == 0
    if s0 is None:
        s0re = jnp.zeros((g, bp, 2 * p), F32)
        s0im = s0re
    else:
        st = jnp.transpose(s0, (3, 0, 2, 1, 4)).reshape(g, n_seq, 2, 2 * p)
        st = jnp.pad(st, ((0, 0), (0, bp - n_seq), (0, 0), (0, 0)))
        s0re, s0im = st[:, :, 0], st[:, :, 1]
    u2, fin = s5_chunked(hn2, (row0 // S5_L) // m, n_seq, nc, mats, s0re, s0im, bp=bp)
    u = jnp.stack(u2, axis=1).reshape(m * S5_L, -1)
    fin = fin.reshape(g, bp, 2, 2, p)[:, :n_seq]
    return u, jnp.transpose(fin, (1, 3, 2, 0, 4))


def _softmax_pv(scores, values, sink):
    m = sink
    for s in scores:
        m = jnp.maximum(m, jnp.max(s, axis=-1, keepdims=True))
    den = jnp.exp(sink - m)
    acc = None
    for s, v in zip(scores, values):
        p = jnp.exp(s - m)
        den = den + jnp.sum(p, axis=-1, keepdims=True)
        pv = jnp.dot(p.astype(BF16), v, preferred_element_type=F32)
        acc = pv if acc is None else acc + pv
    return acc / den


def _attn_prompt_kernel(sink_ref, q_ref, k_ref, v_ref, o_ref):
    k = k_ref[...].astype(BF16)
    v = v_ref[...].astype(BF16)
    for h in range(N_KV):
        hs = slice(h * HEAD_DIM, (h + 1) * HEAD_DIM)
        kh, vh = k[:, hs], v[:, hs]
        for g in range(Q_PER_KV):
            c0 = (h * Q_PER_KV + g) * HEAD_DIM
            qg = (q_ref[:, c0:c0 + HEAD_DIM] * ATT_SCALE).astype(BF16)
            s = lax.dot_general(qg, kh, NT_DIMS, preferred_element_type=F32)
            o_ref[:, c0:c0 + HEAD_DIM] = _softmax_pv([s], [vh], sink_ref[h * Q_PER_KV + g])


def attn_prompt(qkv, sink, n_seq, t):
    dq = N_HEADS * HEAD_DIM
    kcol = dq // KV_W
    return pl.pallas_call(
        _attn_prompt_kernel,
        grid=(n_seq,),
        in_specs=[pl.BlockSpec(memory_space=pltpu.SMEM),
                  pl.BlockSpec((t, dq), lambda b: (b, 0)),
                  pl.BlockSpec((t, KV_W), lambda b: (b, kcol)),
                  pl.BlockSpec((t, KV_W), lambda b: (b, kcol + 1))],
        out_specs=pl.BlockSpec((t, dq), lambda b: (b, 0)),
        out_shape=jax.ShapeDtypeStruct((n_seq * t, dq), F32),
        compiler_params=_cparams("arbitrary"),
        name="attn_prompt",
    )(sink, qkv, qkv, qkv)


def _rope(x, cos, sin):
    w = x.shape[1]
    low = (lax.broadcasted_iota(jnp.int32, x.shape, 1) & (HEAD_DIM // 4)) == 0
    partner = jnp.where(low, pltpu.roll(x, w - HEAD_DIM // 4, axis=1), pltpu.roll(x, HEAD_DIM // 4, axis=1))
    return x * cos + partner * sin


def _attn_sample_kernel(sink_ref, q_ref, k_ref, v_ref, kc_ref, vc_ref, cos_ref, sin_ref, o_ref,
                        kw_s, vw_s, kc_s, vc_s, *, t):
    n = pl.program_id(1)
    blk = ATT_BLOCK

    @pl.when(n == 0)
    def _():
        zeros = jnp.zeros((blk, KV_W), BF16)
        kw_s[0:blk, :] = zeros
        vw_s[0:blk, :] = zeros
        kw_s[blk + t:2 * blk + t, :] = zeros
        vw_s[blk + t:2 * blk + t, :] = zeros
        kw_s[blk:blk + t, :] = _rope(k_ref[...], cos_ref[...], sin_ref[...]).astype(BF16)
        vw_s[blk:blk + t, :] = v_ref[...].astype(BF16)
        kc_s[...] = kc_ref[...].astype(BF16)
        vc_s[...] = vc_ref[...].astype(BF16)

    r0 = pl.multiple_of(n * blk, blk)
    cq = cos_ref[pl.ds(r0, blk), :]
    sq = sin_ref[pl.ds(r0, blk), :]
    kw = kw_s[pl.ds(r0, 3 * blk), :]
    vw = vw_s[pl.ds(r0, 3 * blk), :]
    qi = lax.broadcasted_iota(jnp.int32, (blk, 3 * blk), 0)
    kj = lax.broadcasted_iota(jnp.int32, (blk, 3 * blk), 1)
    kpos = n * blk - blk + kj
    valid = (jnp.abs(kj - blk - qi) <= WINDOW) & (kpos >= 0) & (kpos < t)
    for h in range(N_KV):
        hs = slice(h * HEAD_DIM, (h + 1) * HEAD_DIM)
        qh = (_rope(q_ref[:, h * KV_W:(h + 1) * KV_W], cq, sq) * ATT_SCALE).astype(BF16)
        kh, vh, kch, vch = kw[:, hs], vw[:, hs], kc_s[:, hs], vc_s[:, hs]
        for g in range(Q_PER_KV):
            qg = qh[:, g * HEAD_DIM:(g + 1) * HEAD_DIM]
            s_loc = lax.dot_general(qg, kh, NT_DIMS, preferred_element_type=F32)
            s_loc = jnp.where(valid, s_loc, NEG_INF)
            s_ctx = lax.dot_general(qg, kch, NT_DIMS, preferred_element_type=F32)
            c0 = (h * Q_PER_KV + g) * HEAD_DIM
            o_ref[:, c0:c0 + HEAD_DIM] = _softmax_pv([s_loc, s_ctx], [vh, vch], sink_ref[h * Q_PER_KV + g])


def _rope_tables(t):
    quarter = HEAD_DIM // 4
    freqs = ROPE_BASE ** (-jnp.arange(quarter, dtype=F32) / quarter)
    pos = jnp.arange(t)
    ang_r = (pos // GRID_W).astype(F32)[:, None] * freqs
    ang_c = (pos % GRID_W).astype(F32)[:, None] * freqs
    cos = jnp.concatenate([jnp.cos(ang_r), jnp.cos(ang_r), jnp.cos(ang_c), jnp.cos(ang_c)], axis=-1)
    sin = jnp.concatenate([-jnp.sin(ang_r), jnp.sin(ang_r), -jnp.sin(ang_c), jnp.sin(ang_c)], axis=-1)
    return jnp.tile(cos, (1, N_KV)), jnp.tile(sin, (1, N_KV))


def attn_sample(qkv, row0, sink, n_seq, t, k_ctx, v_ctx):
    dq = N_HEADS * HEAD_DIM
    kcol = dq // KV_W
    nb = t // ATT_BLOCK
    lc = k_ctx.shape[1]
    cos, sin = _rope_tables(t)
    qblk0, sblk0 = row0 // ATT_BLOCK, row0 // t
    return pl.pallas_call(
        functools.partial(_attn_sample_kernel, t=t),
        grid=(n_seq, nb),
        in_specs=[pl.BlockSpec(memory_space=pltpu.SMEM),
                  pl.BlockSpec((ATT_BLOCK, dq), lambda b, n: (qblk0 + b * nb + n, 0)),
                  pl.BlockSpec((t, KV_W), lambda b, n: (sblk0 + b, kcol)),
                  pl.BlockSpec((t, KV_W), lambda b, n: (sblk0 + b, kcol + 1)),
                  pl.BlockSpec((None, lc, KV_W), lambda b, n: (b, 0, 0)),
                  pl.BlockSpec((None, lc, KV_W), lambda b, n: (b, 0, 0)),
                  pl.BlockSpec((t, KV_W), lambda b, n: (0, 0)),
                  pl.BlockSpec((t, KV_W), lambda b, n: (0, 0))],
        out_specs=pl.BlockSpec((ATT_BLOCK, dq), lambda b, n: (b * nb + n, 0)),
        out_shape=jax.ShapeDtypeStruct((n_seq * t, dq), F32),
        scratch_shapes=[pltpu.VMEM((t + 2 * ATT_BLOCK, KV_W), BF16), pltpu.VMEM((t + 2 * ATT_BLOCK, KV_W), BF16),
                        pltpu.VMEM((lc, KV_W), BF16), pltpu.VMEM((lc, KV_W), BF16)],
        compiler_params=_cparams("arbitrary", "arbitrary"),
        name="attn_sample",
    )(sink, qkv, qkv, qkv, k_ctx, v_ctx, cos, sin)


def _router_kernel(x_ref, g_ref, sc_ref, sh_ref, rw_ref, h_ref, aff_ref):
    h = _norm_mod(x_ref[...], g_ref[...], sc_ref[...], sh_ref[...])
    h_hi = h.astype(BF16)
    h_ref[...] = h_hi
    h_lo = (h - h_hi.astype(F32)).astype(BF16)
    rw = rw_ref[...]
    rw_hi = rw.astype(BF16)
    rw_lo = (rw - rw_hi.astype(F32)).astype(BF16)
    logits = (jnp.dot(h_hi, rw_hi, preferred_element_type=F32)
              + (jnp.dot(h_hi, rw_lo, preferred_element_type=F32)
                 + jnp.dot(h_lo, rw_hi, preferred_element_type=F32)))
    e = jnp.exp(logits - jnp.max(logits, axis=-1, keepdims=True))
    aff_ref[...] = e / jnp.sum(e, axis=-1, keepdims=True)


def norm_router(x, gains, mod, rows, layer, router_w):
    m, d = x.shape
    ne = router_w.shape[2]
    tm = rows.tm
    return pl.pallas_call(
        _router_kernel,
        grid=(m // tm,),
        in_specs=[pl.BlockSpec((tm, d), lambda i: (i, 0)),
                  _gain_spec(layer, d),
                  _mod_spec(rows, layer, d, 4, 0),
                  _mod_spec(rows, layer, d, 3, 0),
                  pl.BlockSpec((None, d, ne), lambda i: (layer, 0, 0))],
        out_specs=[pl.BlockSpec((tm, d), lambda i: (i, 0)),
                   pl.BlockSpec((tm, ne), lambda i: (i, 0))],
        out_shape=[jax.ShapeDtypeStruct((m, d), BF16), jax.ShapeDtypeStruct((m, ne), F32)],
        compiler_params=_cparams("arbitrary"),
        name="norm_router",
    )(x, gains, mod, mod, router_w)


def _moe_kernel(x_ref, wg_ref, wu_ref, wd_ref, gt_ref, seg_ref, g2_ref, o_ref, wg_bf, wu_bf, wd_bf, *, n_seg):
    wg_bf[...] = wg_ref[...].astype(BF16)
    wu_bf[...] = wu_ref[...].astype(BF16)
    wd_bf[...] = wd_ref[...].astype(BF16)
    rsub = min(MOE_ROW_SUB, x_ref.shape[0])
    for r in range(x_ref.shape[0] // rsub):
        rs = slice(r * rsub, (r + 1) * rsub)
        x = x_ref[rs, :]
        hg = jnp.dot(x, wg_bf[...], preferred_element_type=F32)
        hu = jnp.dot(x, wu_bf[...], preferred_element_type=F32)
        he = ((hg * jax.nn.sigmoid(hg)) * hu).astype(BF16)
        y = jnp.dot(he, wd_bf[...], preferred_element_type=F32)
        seg = seg_ref[rs, :]
        g2 = jnp.zeros(y.shape, F32)
        for s in range(n_seg):
            g2 = jnp.where(seg == s, g2_ref[s:s + 1, :], g2)
        o_ref[rs, :] = y * (gt_ref[rs, :] * g2)


def moe_experts(xe, w_gate, w_up, w_down, layer, gates, seg, mod, n_seg):
    ne, r, d = xe.shape
    dff = w_gate.shape[3]
    return pl.pallas_call(
        functools.partial(_moe_kernel, n_seg=n_seg),
        grid=(ne,),
        in_specs=[pl.BlockSpec((None, r, d), lambda e: (e, 0, 0)),
                  pl.BlockSpec((None, None, d, dff), lambda e: (layer, e, 0, 0)),
                  pl.BlockSpec((None, None, d, dff), lambda e: (layer, e, 0, 0)),
                  pl.BlockSpec((None, None, dff, d), lambda e: (layer, e, 0, 0)),
                  pl.BlockSpec((None, r, 1), lambda e: (e, 0, 0)),
                  pl.BlockSpec((None, r, 1), lambda e: (e, 0, 0)),
                  pl.BlockSpec((SUBLANES, None, d), lambda e: (layer, 0, 5))],
        out_specs=pl.BlockSpec((None, r, d), lambda e: (e, 0, 0)),
        out_shape=jax.ShapeDtypeStruct((ne, r, d), F32),
        scratch_shapes=[pltpu.VMEM((d, dff), BF16), pltpu.VMEM((d, dff), BF16), pltpu.VMEM((dff, d), BF16)],
        compiler_params=_cparams("arbitrary"),
        name="moe_experts",
    )(xe, w_gate, w_up, w_down, gates, seg, mod)


def moe_layer(x, gains, mod, rows, layer, router_w, w_gate, w_up, w_down, n_seg):
    m, d = x.shape
    h2, aff = norm_router(x, gains, mod, rows, layer, router_w)
    idx_l, gate_l = [], []
    off = 0
    for n in (rows.n_prompt, rows.n_sample):
        cap = (EC_FACTOR * n) // N_EXPERTS
        gt, ix = lax.top_k(aff[off:off + n].T, cap)
        idx_l.append(ix + off)
        gate_l.append(gt)
        off += n
    idx = jnp.concatenate(idx_l, axis=1)
    gates = jnp.concatenate(gate_l, axis=1)
    seg = jnp.where(idx < rows.n_prompt, 0, 1 + (idx - rows.n_prompt) // rows.t_sample)
    xe = h2[idx]
    ye = moe_experts(xe, w_gate, w_up, w_down, layer, gates[..., None], seg[..., None], mod, n_seg)
    return x.at[idx.reshape(-1)].add(ye.reshape(-1, d))


def kernel(x_prompt, x_sample, state_rglru, state_s5, cache_k, cache_v, c, c_ctx, ada_w, ada_b, norm1_g, norm2_g, rg_w_in, rg_conv_w, rg_conv_b, rg_w_a, rg_b_a, rg_w_x, rg_b_x, rg_lambda, rg_w_out, s5_a_re, s5_a_im, s5_log_dt, s5_b_re, s5_b_im, s5_c_re, s5_c_im, s5_d, s5_w_glu, attn_w_qkv, attn_w_o, attn_sink, router_w, moe_w_gate, moe_w_up, moe_w_down, final_norm_g):
    bp_, tp, d = x_prompt.shape
    bs, ts, _ = x_sample.shape
    n_p, n_s = bp_ * tp, bs * ts
    depth = ada_w.shape[0]
    rows = _Rows(n_p, n_s, ts)
    assert bs + 1 <= SUBLANES and n_p % ts == 0

    x = jnp.concatenate([x_prompt.reshape(n_p, d), x_sample.reshape(n_s, d)], axis=0)
    cond = jnp.concatenate([c_ctx[None, :], c, jnp.zeros((SUBLANES - 1 - bs, d), F32)], axis=0)
    mod_all = ada_modulation_all(cond, ada_w, ada_b)
    mod = mod_all.reshape(depth * SUBLANES, 1, 6 * d)
    g1 = norm1_g.reshape(depth, 1, d)
    g2 = norm2_g.reshape(depth, 1, d)

    new_rg, new_s5, new_k, new_v = [], [], [], []
    for l in range(depth):
        kind, j = l % 3, l // 3
        if kind == 0:
            gu = norm_mod_matmul(x, g1, mod, rows, l, rg_w_in, j)
            args = (j, rg_conv_w, rg_conv_b, rg_w_a, rg_b_a, rg_w_x, rg_b_x, rg_lambda)
            r = gu.shape[1] // 2
            yp, fin = rglru_scan(gu, 0, bp_, tp, *args, jnp.zeros((bp_, 1, 2, r), F32), 0)
            ys, _ = rglru_scan(gu, n_p, bs, ts, *args, state_rglru, j)
            new_rg.append(fin)
            x = matmul_gated_residual(yp, ys, rg_w_out, j, x, mod, rows, l)
        elif kind == 1:
            hn = norm_mod(x, g1, mod, rows, l)
            mats = s5_chunk_operators(s5_a_re[j], s5_a_im[j], s5_log_dt[j], s5_b_re[j], s5_b_im[j],
                                s5_c_re[j], s5_c_im[j], s5_d[j])
            hn2 = hn.reshape((n_p + n_s) // S5_L, S5_L * d)
            up, st = s5_mixer_group(hn2, 0, bp_, tp, mats, None)
            us, _ = s5_mixer_group(hn2, n_p, bs, ts, mats, state_s5[:, j])
            new_s5.append(st)
            x = glu_gated_residual(up, us, s5_w_glu, j, x, mod, rows, l)
        else:
            qkv = norm_mod_matmul(x, g1, mod, rows, l, attn_w_qkv, j)
            dq = N_HEADS * HEAD_DIM
            new_k.append(qkv[:n_p, dq:dq + KV_W].reshape(bp_, tp, N_KV, HEAD_DIM))
            new_v.append(qkv[:n_p, dq + KV_W:].reshape(bp_, tp, N_KV, HEAD_DIM))
            op = attn_prompt(qkv, attn_sink[j], bp_, tp)
            lc = cache_k.shape[2]
            os_ = attn_sample(qkv, n_p, attn_sink[j], bs, ts,
                              cache_k[:, j].reshape(bs, lc, KV_W), cache_v[:, j].reshape(bs, lc, KV_W))
            x = matmul_gated_residual(op, os_, attn_w_o, j, x, mod, rows, l)
        x = moe_layer(x, g2, mod, rows, l, router_w, moe_w_gate, moe_w_up, moe_w_down, bs + 1)

    y = final_norm(x, final_norm_g, rows.tm)
    return (y[:n_p].reshape(bp_, tp, d), y[n_p:].reshape(bs, ts, d),
            jnp.stack(new_rg, axis=1), jnp.stack(new_s5, axis=1),
            jnp.stack(new_k, axis=1), jnp.stack(new_v, axis=1))
```

```python
import functools
import math

import jax
import jax.numpy as jnp
from jax import lax
from jax.experimental import pallas as pl
from jax.experimental.pallas import tpu as pltpu

F32 = jnp.float32
BF16 = jnp.bfloat16
HIGHEST = lax.Precision.HIGHEST

EPS = 1e-6
RG_C = 8.0
RG_BS = 128
RG_TILE = 512
S5_H = 16
S5_L = 16
S5_PREP_GROUPS = 4
N_HEADS = 16
N_KV = 4
Q_PER_KV = N_HEADS // N_KV
HEAD_DIM = 64
KV_W = N_KV * HEAD_DIM
GRID_W = 64
WINDOW = 128
ATT_BLOCK = 128
ROPE_BASE = 10000.0
ATT_SCALE = HEAD_DIM ** -0.5
assert math.frexp(ATT_SCALE)[0] == 0.5
NEG_INF = -1e30
N_EXPERTS = 16
EC_FACTOR = 2
SUBLANES = 8
ROW_TILE = 512
MOE_ROW_SUB = 256
VMEM_LIMIT = 56 * 1024 * 1024
NT_DIMS = (((1,), (1,)), ((), ()))


def _cparams(*sem):
    return pltpu.CompilerParams(dimension_semantics=sem, vmem_limit_bytes=VMEM_LIMIT)


def _gelu(x):
    return x * (0.5 * (1.0 + jnp.tanh(math.sqrt(2.0 / math.pi) * (x + 0.044715 * (x * x * x)))))


def _sigmoid(x):
    return 0.5 * jnp.tanh(0.5 * x) + 0.5


def _norm_mod(x, g, sc, sh):
    ms = jnp.mean(x * x, axis=-1, keepdims=True)
    return ((x * lax.rsqrt(ms + EPS)) * g) * (1.0 + sc) + sh


def _mod_kernel(c_ref, w_ref, b_ref, o_ref):
    c = c_ref[...]
    s = (c * jax.nn.sigmoid(c)).astype(BF16)
    o_ref[...] = jnp.dot(s, w_ref[...].astype(BF16), preferred_element_type=F32) + b_ref[...]


def ada_modulation_all(cond, ada_w, ada_b):
    n_layers, d, n = ada_w.shape
    tn = 1536
    return pl.pallas_call(
        _mod_kernel,
        grid=(n_layers, n // tn),
        in_specs=[pl.BlockSpec((SUBLANES, d), lambda l, j: (0, 0)),
                  pl.BlockSpec((None, d, tn), lambda l, j: (l, 0, j)),
                  pl.BlockSpec((None, 1, tn), lambda l, j: (l, 0, j))],
        out_specs=pl.BlockSpec((None, SUBLANES, tn), lambda l, j: (l, 0, j)),
        out_shape=jax.ShapeDtypeStruct((n_layers, SUBLANES, n), F32),
        compiler_params=_cparams("arbitrary", "arbitrary"),
        name="ada_mod",
    )(cond, ada_w, ada_b.reshape(n_layers, 1, n))


class _Rows:
    def __init__(self, n_prompt, n_sample, t_sample):
        self.n_prompt = n_prompt
        self.n_sample = n_sample
        self.t_sample = t_sample
        self.tm = min(ROW_TILE, n_prompt, t_sample)
        assert n_prompt % self.tm == 0 and t_sample % self.tm == 0
        self.prompt_blocks = n_prompt // self.tm
        self.sample_blocks = n_sample // self.tm

    def seg(self, i):
        r = i * self.tm
        return jnp.where(r < self.n_prompt, 0, 1 + lax.div(r - self.n_prompt, self.t_sample))


def _mod_spec(rows, layer, width, chunk, m_axis):
    def imap(*ids):
        return (layer * SUBLANES + rows.seg(ids[m_axis]), 0, chunk)
    return pl.BlockSpec((None, 1, width), imap)


def _gain_spec(layer, d):
    return pl.BlockSpec((None, 1, d), lambda *ids: (layer, 0, 0))


def _nm_kernel(x_ref, g_ref, sc_ref, sh_ref, w_ref, o_ref, wbf_ref):
    @pl.when(pl.program_id(1) == 0)
    def _():
        wbf_ref[...] = w_ref[...].astype(BF16)
    h = _norm_mod(x_ref[...], g_ref[...], sc_ref[...], sh_ref[...])
    o_ref[...] = jnp.dot(h.astype(BF16), wbf_ref[...], preferred_element_type=F32)


def norm_mod_matmul(x, gains, mod, rows, layer, w, wl):
    m, d = x.shape
    n = w.shape[2]
    tm = rows.tm
    tn = n
    return pl.pallas_call(
        _nm_kernel,
        grid=(n // tn, m // tm),
        in_specs=[pl.BlockSpec((tm, d), lambda j, i: (i, 0)),
                  _gain_spec(layer, d),
                  _mod_spec(rows, layer, d, 1, 1),
                  _mod_spec(rows, layer, d, 0, 1),
                  pl.BlockSpec((None, d, tn), lambda j, i: (wl, 0, j))],
        out_specs=pl.BlockSpec((tm, tn), lambda j, i: (i, j)),
        out_shape=jax.ShapeDtypeStruct((m, n), F32),
        scratch_shapes=[pltpu.VMEM((d, tn), BF16)],
        compiler_params=_cparams("arbitrary", "arbitrary"),
        name="norm_mod_matmul",
    )(x, gains, mod, mod, w)


def _norm_only_kernel(x_ref, g_ref, sc_ref, sh_ref, o_ref):
    o_ref[...] = _norm_mod(x_ref[...], g_ref[...], sc_ref[...], sh_ref[...])


def norm_mod(x, gains, mod, rows, layer):
    m, d = x.shape
    tm = rows.tm
    return pl.pallas_call(
        _norm_only_kernel,
        grid=(m // tm,),
        in_specs=[pl.BlockSpec((tm, d), lambda i: (i, 0)),
                  _gain_spec(layer, d),
                  _mod_spec(rows, layer, d, 1, 0),
                  _mod_spec(rows, layer, d, 0, 0)],
        out_specs=pl.BlockSpec((tm, d), lambda i: (i, 0)),
        out_shape=jax.ShapeDtypeStruct((m, d), F32),
        compiler_params=_cparams("arbitrary"),
        name="norm_mod",
    )(x, gains, mod, mod)


def _final_norm_kernel(x_ref, g_ref, o_ref):
    x = x_ref[...]
    ms = jnp.mean(x * x, axis=-1, keepdims=True)
    o_ref[...] = (x * lax.rsqrt(ms + EPS)) * g_ref[...]


def final_norm(x, g, tm):
    m, d = x.shape
    return pl.pallas_call(
        _final_norm_kernel,
        grid=(m // tm,),
        in_specs=[pl.BlockSpec((tm, d), lambda i: (i, 0)),
                  pl.BlockSpec((1, d), lambda i: (0, 0))],
        out_specs=pl.BlockSpec((tm, d), lambda i: (i, 0)),
        out_shape=jax.ShapeDtypeStruct((m, d), F32),
        compiler_params=_cparams("arbitrary"),
        name="final_norm",
    )(x, g.reshape(1, d))


def _router_epilogue(x_new, g_ref, sc_ref, sh_ref, rw_ref, h_ref, aff_ref):
    h = _norm_mod(x_new, g_ref[...], sc_ref[...], sh_ref[...])
    h_hi = h.astype(BF16)
    h_ref[...] = h_hi
    h_lo = (h - h_hi.astype(F32)).astype(BF16)
    rw = rw_ref[...]
    rw_hi = rw.astype(BF16)
    rw_lo = (rw - rw_hi.astype(F32)).astype(BF16)
    logits = (jnp.dot(h_hi, rw_hi, preferred_element_type=F32)
              + (jnp.dot(h_hi, rw_lo, preferred_element_type=F32)
                 + jnp.dot(h_lo, rw_hi, preferred_element_type=F32)))
    e = jnp.exp(logits - jnp.max(logits, axis=-1, keepdims=True))
    aff_ref[...] = e / jnp.sum(e, axis=-1, keepdims=True)


def _mmres_kernel(ap_ref, as_ref, w_ref, r_ref, gt_ref, g_ref, sc_ref, sh_ref, rw_ref,
                  o_ref, h_ref, aff_ref, wbf_ref, *, npb):
    i = pl.program_id(0)

    @pl.when(i == 0)
    def _():
        wbf_ref[...] = w_ref[...].astype(BF16)

    def emit(a_ref):
        acc = jnp.dot(a_ref[...].astype(BF16), wbf_ref[...], preferred_element_type=F32)
        x_new = r_ref[...] + gt_ref[...] * acc
        o_ref[...] = x_new
        _router_epilogue(x_new, g_ref, sc_ref, sh_ref, rw_ref, h_ref, aff_ref)

    pl.when(i < npb)(lambda: emit(ap_ref))
    pl.when(i >= npb)(lambda: emit(as_ref))


def _sublayer2_specs(rows, layer, d, ne):
    tm = rows.tm
    ins = [_gain_spec(layer, d), _mod_spec(rows, layer, d, 4, 0), _mod_spec(rows, layer, d, 3, 0),
           pl.BlockSpec((None, d, ne), lambda i: (layer, 0, 0))]
    outs = [pl.BlockSpec((tm, d), lambda i: (i, 0)), pl.BlockSpec((tm, ne), lambda i: (i, 0))]
    return ins, outs


def _two_group_specs(rows, k):
    npb, nsb, tm = rows.prompt_blocks, rows.sample_blocks, rows.tm
    return [pl.BlockSpec((tm, k), lambda i: (jnp.minimum(i, npb - 1), 0)),
            pl.BlockSpec((tm, k), lambda i: (jnp.clip(i - npb, 0, nsb - 1), 0))]


def matmul_gated_residual(a_p, a_s, w, wl, resid, mod, rows, layer, gains2, router_w):
    k = a_p.shape[1]
    m, d = resid.shape
    tm = rows.tm
    ne = router_w.shape[2]
    r_in, r_out = _sublayer2_specs(rows, layer, d, ne)
    return pl.pallas_call(
        functools.partial(_mmres_kernel, npb=rows.prompt_blocks),
        grid=(m // tm,),
        in_specs=_two_group_specs(rows, k) + [
            pl.BlockSpec((None, k, d), lambda i: (wl, 0, 0)),
            pl.BlockSpec((tm, d), lambda i: (i, 0)),
            _mod_spec(rows, layer, d, 2, 0)] + r_in,
        out_specs=[pl.BlockSpec((tm, d), lambda i: (i, 0))] + r_out,
        out_shape=[jax.ShapeDtypeStruct((m, d), F32), jax.ShapeDtypeStruct((m, d), BF16),
                   jax.ShapeDtypeStruct((m, ne), F32)],
        scratch_shapes=[pltpu.VMEM((k, d), BF16)],
        compiler_params=_cparams("arbitrary"),
        name="matmul_gated_residual",
    )(a_p, a_s, w, resid, mod, gains2, mod, mod, router_w)


def _glures_kernel(ap_ref, as_ref, wv_ref, wg_ref, r_ref, gt_ref, g_ref, sc_ref, sh_ref, rw_ref,
                   o_ref, h_ref, aff_ref, wv_bf, wg_bf, *, npb):
    i = pl.program_id(0)

    @pl.when(i == 0)
    def _():
        wv_bf[...] = wv_ref[...].astype(BF16)
        wg_bf[...] = wg_ref[...].astype(BF16)

    def emit(a_ref):
        a = a_ref[...].astype(BF16)
        v = jnp.dot(a, wv_bf[...], preferred_element_type=F32)
        g = jnp.dot(a, wg_bf[...], preferred_element_type=F32)
        x_new = r_ref[...] + gt_ref[...] * (v * jax.nn.sigmoid(g))
        o_ref[...] = x_new
        _router_epilogue(x_new, g_ref, sc_ref, sh_ref, rw_ref, h_ref, aff_ref)

    pl.when(i < npb)(lambda: emit(ap_ref))
    pl.when(i >= npb)(lambda: emit(as_ref))


def glu_gated_residual(a_p, a_s, w_glu, wl, resid, mod, rows, layer, gains2, router_w):
    k = a_p.shape[1]
    m, d = resid.shape
    tm = rows.tm
    ne = router_w.shape[2]
    r_in, r_out = _sublayer2_specs(rows, layer, d, ne)
    return pl.pallas_call(
        functools.partial(_glures_kernel, npb=rows.prompt_blocks),
        grid=(m // tm,),
        in_specs=_two_group_specs(rows, k) + [
            pl.BlockSpec((None, k, d), lambda i: (wl, 0, 0)),
            pl.BlockSpec((None, k, d), lambda i: (wl, 0, 1)),
            pl.BlockSpec((tm, d), lambda i: (i, 0)),
            _mod_spec(rows, layer, d, 2, 0)] + r_in,
        out_specs=[pl.BlockSpec((tm, d), lambda i: (i, 0))] + r_out,
        out_shape=[jax.ShapeDtypeStruct((m, d), F32), jax.ShapeDtypeStruct((m, d), BF16),
                   jax.ShapeDtypeStruct((m, ne), F32)],
        scratch_shapes=[pltpu.VMEM((k, d), BF16), pltpu.VMEM((k, d), BF16)],
        compiler_params=_cparams("arbitrary"),
        name="glu_gated_residual",
    )(a_p, a_s, w_glu, w_glu, resid, mod, gains2, mod, mod, router_w)


def _rglru_kernel(gate_ref, u_ref, cw_ref, cb_ref, wa_ref, ba_ref, wx_ref, bx_ref, lam_ref, h0_ref,
                  y_ref, fin_ref, af_s, bf_s, ab_s, bb_s, hf_s, hb_s):
    t, cw = u_ref.shape
    u = u_ref[...]
    row = lax.broadcasted_iota(jnp.int32, (t, cw), 0)

    def shifted(x, k):
        if k > 0:
            return jnp.where(row >= k, pltpu.roll(x, k, axis=0), 0.0)
        return jnp.where(row < t + k, pltpu.roll(x, t + k, axis=0), 0.0)

    cwv = cw_ref[...]
    uc = (cwv[0:1] * shifted(u, 2) + cwv[1:2] * shifted(u, 1) + cwv[2:3] * u
          + cwv[3:4] * shifted(u, -1) + cb_ref[...])

    a_scr = (af_s, ab_s)
    b_scr = (bf_s, bb_s)
    for k in range(2):
        nl = -lam_ref[k:k + 1, :]
        sp = jnp.maximum(nl, 0.0) + jnp.log1p(jnp.exp(-jnp.abs(nl)))
        for hh in range(cw // RG_BS):
            sl = slice(hh * RG_BS, (hh + 1) * RG_BS)
            uh = uc[:, sl]
            ub = uh.astype(BF16)
            r = _sigmoid(jnp.dot(ub, wa_ref[k, hh].astype(BF16), preferred_element_type=F32) + ba_ref[k:k + 1, sl])
            i = _sigmoid(jnp.dot(ub, wx_ref[k, hh].astype(BF16), preferred_element_type=F32) + bx_ref[k:k + 1, sl])
            log_a = (-RG_C * r) * sp[:, sl]
            a = jnp.exp(log_a)
            a_scr[k][:, sl] = a
            b_scr[k][:, sl] = jnp.sqrt(jnp.tanh(-log_a) * (a * a + 1.0)) * (i * uh)

    nblk = t // SUBLANES
    srow = lax.broadcasted_iota(jnp.int32, (SUBLANES, cw), 0)

    def body(n, carry):
        cf, cb = carry
        rf = pl.multiple_of(n * SUBLANES, SUBLANES)
        rb = pl.multiple_of((nblk - 1 - n) * SUBLANES, SUBLANES)
        a = af_s[pl.ds(rf, SUBLANES), :]
        b = bf_s[pl.ds(rf, SUBLANES), :]
        a2 = ab_s[pl.ds(rb, SUBLANES), :]
        b2 = bb_s[pl.ds(rb, SUBLANES), :]
        for s in (1, 2, 4):
            m = srow >= s
            b = jnp.where(m, a * pltpu.roll(b, s, axis=0) + b, b)
            a = jnp.where(m, a * pltpu.roll(a, s, axis=0), a)
            m2 = srow < SUBLANES - s
            b2 = jnp.where(m2, a2 * pltpu.roll(b2, SUBLANES - s, axis=0) + b2, b2)
            a2 = jnp.where(m2, a2 * pltpu.roll(a2, SUBLANES - s, axis=0), a2)
        hf = a * cf + b
        hb = a2 * cb + b2
        hf_s[pl.ds(rf, SUBLANES), :] = hf
        hb_s[pl.ds(rb, SUBLANES), :] = hb
        return hf[SUBLANES - 1:SUBLANES, :], hb[0:1, :]

    cf, cb = lax.fori_loop(0, nblk, body, (h0_ref[0:1, :], h0_ref[1:2, :]))
    fin_ref[0:1, :] = cf
    fin_ref[1:2, :] = cb
    y_ref[...] = (hf_s[...] + hb_s[...]) * _gelu(gate_ref[...])


def rglru_scan(gu, row0, n_seq, t, j, conv_w, conv_b, w_a, b_a, w_x, b_x, lam, h0, h0_j, *, cw=RG_TILE):
    r = gu.shape[1] // 2
    nh = cw // RG_BS
    blk0 = row0 // t
    nc = r // cw
    scr = [pltpu.VMEM((t, cw), F32) for _ in range(6)]
    vec2 = pl.BlockSpec((None, 2, cw), lambda b, c: (j, 0, c))
    gatew = pl.BlockSpec((None, 2, nh, RG_BS, RG_BS), lambda b, c: (j, 0, c, 0, 0))
    return pl.pallas_call(
        _rglru_kernel,
        grid=(n_seq, nc),
        in_specs=[pl.BlockSpec((t, cw), lambda b, c: (blk0 + b, c)),
                  pl.BlockSpec((t, cw), lambda b, c: (blk0 + b, nc + c)),
                  pl.BlockSpec((None, 4, cw), lambda b, c: (j, 0, c)),
                  pl.BlockSpec((None, 1, cw), lambda b, c: (j, 0, c)),
                  gatew, vec2, gatew, vec2, vec2,
                  pl.BlockSpec((None, None, 2, cw), lambda b, c: (b, h0_j, 0, c))],
        out_specs=[pl.BlockSpec((t, cw), lambda b, c: (b, c)),
                   pl.BlockSpec((None, 2, cw), lambda b, c: (b, 0, c))],
        out_shape=[jax.ShapeDtypeStruct((n_seq * t, r), F32),
                   jax.ShapeDtypeStruct((n_seq, 2, r), F32)],
        scratch_shapes=scr,
        compiler_params=_cparams("arbitrary", "arbitrary"),
        name="rglru_scan",
    )(gu, gu, conv_w, conv_b.reshape(conv_b.shape[0], 1, r), w_a, b_a, w_x, b_x, lam, h0)


def _s5_kernel(*refs, bp, n_seq, nc, gpb):
    ell, gw = S5_L, S5_H
    x_refs = refs[:ell]
    tm_ref, win_ref, wre_ref, wim_ref, ar_ref, ai_ref, d_ref, s0re_ref, s0im_ref = refs[ell:ell + 9]
    o_refs = refs[ell + 9:2 * ell + 9]
    fin_ref = refs[2 * ell + 9]
    xg_s, ure_s, uim_s, fre_s, fim_s, bre_s, bim_s = refs[2 * ell + 10:]
    m, mp = n_seq * nc, bp * nc
    lanes = x_refs[0].shape[1]
    per_tile = lanes // gw
    half = ure_s.shape[2] // 2
    lane_grp = lax.shift_right_logical(lax.broadcasted_iota(jnp.int32, (m, lanes), 1), gw.bit_length() - 1)

    def perm(shape, chunk_major_axis):
        i = lax.broadcasted_iota(jnp.int32, shape, chunk_major_axis)
        j = lax.broadcasted_iota(jnp.int32, shape, 1 - chunk_major_axis)
        b = i & (bp - 1)
        c = lax.shift_right_logical(i, bp.bit_length() - 1)
        return jnp.where((j == b * nc + c) & (b < n_seq), 1.0, 0.0).astype(BF16)

    to_chunk_major = perm((mp, m), 0)
    to_batch_major = perm((m, mp), 1)

    def block_transpose(v):
        k = per_tile // 2
        while k >= 1:
            low = (lane_grp & k) == 0
            nxt = list(v)
            for i in range(per_tile):
                if i & k == 0:
                    a, b = v[i], v[i + k]
                    nxt[i] = jnp.where(low, a, pltpu.roll(b, k * gw, axis=1))
                    nxt[i + k] = jnp.where(low, pltpu.roll(a, lanes - k * gw, axis=1), b)
            v = nxt
            k //= 2
        return v

    for tile in range(ell // per_tile):
        by_group = block_transpose([x_refs[tile * per_tile + j][...] for j in range(per_tile)])
        for g in range(gpb):
            xg_s[g, :, tile * lanes:(tile + 1) * lanes] = by_group[g]

    for g in range(gpb):
        xg = xg_s[g]
        xp = jnp.dot(to_chunk_major, xg.astype(BF16), preferred_element_type=F32).astype(BF16)
        u = jnp.dot(xp, win_ref[g].astype(BF16), preferred_element_type=F32)
        ure_s[g] = u[:, :2 * half]
        uim_s[g] = u[:, 2 * half:]

    is_fwd = lax.broadcasted_iota(jnp.int32, (bp, 2 * half), 1) < half
    ar = [ar_ref[g] for g in range(gpb)]
    ai = [ai_ref[g] for g in range(gpb)]

    def body(k, carry):
        rf = pl.multiple_of(k * bp, bp)
        rb = pl.multiple_of((nc - 1 - k) * bp, bp)
        out = []
        for g in range(gpb):
            re, im = carry[2 * g], carry[2 * g + 1]
            fre_s[g, pl.ds(rf, bp), :] = re
            fim_s[g, pl.ds(rf, bp), :] = im
            bre_s[g, pl.ds(rb, bp), :] = re
            bim_s[g, pl.ds(rb, bp), :] = im
            ure = jnp.where(is_fwd, ure_s[g, pl.ds(rf, bp), :], ure_s[g, pl.ds(rb, bp), :])
            uim = jnp.where(is_fwd, uim_s[g, pl.ds(rf, bp), :], uim_s[g, pl.ds(rb, bp), :])
            out += [ar[g] * re - ai[g] * im + ure, ar[g] * im + ai[g] * re + uim]
        return tuple(out)

    init = tuple(r[g] for g in range(gpb) for r in (s0re_ref, s0im_ref))
    fin = lax.fori_loop(0, nc, body, init)
    fwd_all = lax.broadcasted_iota(jnp.int32, (mp, 2 * half), 1) < half
    for g in range(gpb):
        fin_ref[g, :, :2 * half] = fin[2 * g]
        fin_ref[g, :, 2 * half:] = fin[2 * g + 1]
        hre = jnp.where(fwd_all, fre_s[g], bre_s[g]).astype(BF16)
        him = jnp.where(fwd_all, fim_s[g], bim_s[g]).astype(BF16)
        hre = jnp.dot(to_batch_major, hre, preferred_element_type=F32).astype(BF16)
        him = jnp.dot(to_batch_major, him, preferred_element_type=F32).astype(BF16)
        xg = xg_s[g]
        y = (jnp.dot(xg.astype(BF16), tm_ref[g].astype(BF16), preferred_element_type=F32)
             + jnp.dot(hre, wre_ref[g].astype(BF16), preferred_element_type=F32)
             + jnp.dot(him, wim_ref[g].astype(BF16), preferred_element_type=F32)
             + d_ref[g] * xg)
        xg_s[g] = _gelu(y)

    for tile in range(ell // per_tile):
        by_token = block_transpose([xg_s[g, :, tile * lanes:(tile + 1) * lanes] for g in range(gpb)])
        for j in range(per_tile):
            o_refs[tile * per_tile + j][...] = by_token[j]


def s5_chunked(hn2, row_blk, n_seq, nc, mats, s0re, s0im, *, bp):
    tmat, win, wre, wim, ar, ai, dg = mats
    g, w, _ = tmat.shape
    p2 = ar.shape[-1]
    ell = S5_L
    lanes = 128
    gpb = lanes // S5_H
    d = hn2.shape[1] // ell
    m, mp = n_seq * nc, bp * nc
    assert bp & (bp - 1) == 0 and S5_H & (S5_H - 1) == 0 and g % gpb == 0
    col_tiles = d // lanes
    blk = lambda shape: pl.BlockSpec((gpb,) + shape, lambda i: (i, 0, 0))
    x_specs = [pl.BlockSpec((m, lanes), functools.partial(lambda i, l: (row_blk, l * col_tiles + i), l=l))
               for l in range(ell)]
    o_specs = [pl.BlockSpec((m, lanes), lambda i: (0, i)) for _ in range(ell)]
    outs = pl.pallas_call(
        functools.partial(_s5_kernel, bp=bp, n_seq=n_seq, nc=nc, gpb=gpb),
        grid=(g // gpb,),
        in_specs=x_specs + [blk((w, w)), blk((w, 2 * p2)), blk((p2, w)), blk((p2, w)),
                            blk((1, p2)), blk((1, p2)), blk((1, w)), blk((bp, p2)), blk((bp, p2))],
        out_specs=o_specs + [blk((bp, 2 * p2))],
        out_shape=[jax.ShapeDtypeStruct((m, d), F32)] * ell + [jax.ShapeDtypeStruct((g, bp, 2 * p2), F32)],
        scratch_shapes=[pltpu.VMEM((gpb, m, w), F32)] + [pltpu.VMEM((gpb, mp, p2), F32) for _ in range(6)],
        compiler_params=_cparams("arbitrary"),
        name="s5_chunked",
    )(*([hn2] * ell), tmat, win, wre, wim, ar, ai, dg, s0re, s0im)
    return outs[:ell], outs[ell]


def _cmul(ar, ai, br, bi):
    return ar * br - ai * bi, ar * bi + ai * br


def _s5_prep_kernel(are_ref, aim_ref, ldt_ref, btr_ref, bti_ref, cr_ref, ci_ref,
                    tm_ref, win_ref, wre_ref, wim_ref, ar_ref, ai_ref):
    for gi in range(cr_ref.shape[0]):
        _s5_prep_group(*(r.at[gi] for r in (are_ref, aim_ref, ldt_ref, btr_ref, bti_ref, cr_ref, ci_ref,
                                           tm_ref, win_ref, wre_ref, wim_ref, ar_ref, ai_ref)))


def _s5_prep_group(are_ref, aim_ref, ldt_ref, btr_ref, bti_ref, cr_ref, ci_ref,
                   tm_ref, win_ref, wre_ref, wim_ref, ar_ref, ai_ref):
    ell = S5_L
    h, p2 = cr_ref.shape
    w = ell * h
    a_re, a_im = are_ref[...], aim_ref[...]
    dt = jnp.exp(ldt_ref[...])
    steps = lax.broadcasted_iota(jnp.int32, (3 * SUBLANES, p2), 0).astype(F32)
    mag = jnp.exp(steps * (a_re * dt))
    ang = steps * (a_im * dt)
    pw_r, pw_i = mag * jnp.cos(ang), mag * jnp.sin(ang)
    nr, ni = pw_r[1:2] - 1.0, pw_i[1:2]
    den = a_re * a_re + a_im * a_im
    qr, qi = (nr * a_re + ni * a_im) / den, (ni * a_re - nr * a_im) / den
    bb_r, bb_i = _cmul(qr, qi, btr_ref[...], bti_ref[...])
    c_r, c_i = cr_ref[...], ci_ref[...]
    fwd = lax.broadcasted_iota(jnp.int32, (1, p2), 1) < p2 // 2

    def power_rows(m_fwd, m_bwd):
        return (jnp.where(fwd, pw_r[m_fwd:m_fwd + 1], pw_r[m_bwd:m_bwd + 1]),
                jnp.where(fwd, pw_i[m_fwd:m_fwd + 1], pw_i[m_bwd:m_bwd + 1]))

    def stack(x_r, x_i, powers):
        parts = [_cmul(x_r, x_i, *power_rows(*powers(l))) for l in range(ell)]
        return (jnp.concatenate([q[0] for q in parts], axis=0), jnp.concatenate([q[1] for q in parts], axis=0))

    win_r, win_i = stack(bb_r, bb_i, lambda l: (ell - 1 - l, l))
    win_ref[...] = jnp.concatenate([win_r, win_i], axis=1).astype(win_ref.dtype)
    z_r, z_i = stack(c_r, c_i, lambda l: (l + 1, ell - l))
    wre_ref[...] = z_r.T.astype(wre_ref.dtype)
    wim_ref[...] = (-z_i).T.astype(wim_ref.dtype)
    k_r, k_i = stack(c_r, c_i, lambda m: (m, ell - 1 - m))
    mask_f = jnp.where(fwd, 1.0, 0.0)

    def lag_rows(mask):
        return (lax.dot_general(bb_r * mask, k_r, NT_DIMS, precision=HIGHEST, preferred_element_type=F32)
                - lax.dot_general(bb_i * mask, k_i, NT_DIMS, precision=HIGHEST, preferred_element_type=F32))

    kf = lag_rows(mask_f)
    kb = lag_rows(1.0 - mask_f)
    lane = lax.broadcasted_iota(jnp.int32, (h, w), 1)
    blocks = []
    for li in range(ell):
        f_part = kf if li == 0 else pltpu.roll(kf, li * h, axis=1)
        s_b = (w - (ell - 1 - li) * h) % w
        b_part = kb if s_b == 0 else pltpu.roll(kb, s_b, axis=1)
        blocks.append(jnp.where(lane >= li * h, f_part, 0.0) + jnp.where(lane < (li + 1) * h, b_part, 0.0))
    tm_ref[...] = jnp.concatenate(blocks, axis=0).astype(tm_ref.dtype)
    ar_ref[...] = pw_r[ell:ell + 1]
    ai_ref[...] = pw_i[ell:ell + 1]


def s5_chunk_operators(a_re, a_im, log_dt, b_re, b_im, c_re, c_im, d):
    _, g, p = a_re.shape
    h = b_re.shape[-1]
    w = S5_L * h
    two_dir = lambda x: jnp.transpose(x, (1, 0, 2)).reshape(g, 1, 2 * p)
    ldt = two_dir(jnp.broadcast_to(log_dt[:, :, None], (2, g, p)))
    bt = lambda x: jnp.transpose(x, (1, 3, 0, 2)).reshape(g, h, 2 * p)
    ct = lambda x: jnp.transpose(x, (1, 2, 0, 3)).reshape(g, h, 2 * p)
    gps = math.gcd(g, S5_PREP_GROUPS)
    per_g = lambda shape: pl.BlockSpec((gps,) + shape, lambda i: (i, 0, 0))
    tmat, win, wre, wim, ar, ai = pl.pallas_call(
        _s5_prep_kernel,
        grid=(g // gps,),
        in_specs=[per_g((1, 2 * p))] * 3 + [per_g((h, 2 * p))] * 4,
        out_specs=[per_g((w, w)), per_g((w, 4 * p)), per_g((2 * p, w)), per_g((2 * p, w)),
                   per_g((1, 2 * p)), per_g((1, 2 * p))],
        out_shape=[jax.ShapeDtypeStruct((g, w, w), BF16), jax.ShapeDtypeStruct((g, w, 4 * p), BF16),
                   jax.ShapeDtypeStruct((g, 2 * p, w), BF16), jax.ShapeDtypeStruct((g, 2 * p, w), BF16),
                   jax.ShapeDtypeStruct((g, 1, 2 * p), F32), jax.ShapeDtypeStruct((g, 1, 2 * p), F32)],
        compiler_params=_cparams("arbitrary"),
        name="s5_chunk_operators",
    )(two_dir(a_re), two_dir(a_im), ldt, bt(b_re), bt(b_im), ct(c_re), ct(c_im))
    dg = jnp.tile(d.reshape(g, 1, h), (1, S5_L, 1)).reshape(g, 1, w)
    return tmat, win, wre, wim, ar, ai, dg


def s5_mixer_group(hn2, row0, n_seq, t, mats, s0):
    ar = mats[4]
    g = ar.shape[0]
    p = ar.shape[-1] // 2
    nc = t // S5_L
    bp = -(-n_seq // SUBLANES) * SUBLANES
    m = n_seq * nc
    assert (row0 // S5_L) % m == 0
    if s0 is None:
        s0re = jnp.zeros((g, bp, 2 * p), F32)
        s0im = s0re
    else:
        st = jnp.transpose(s0, (3, 0, 2, 1, 4)).reshape(g, n_seq, 2, 2 * p)
        st = jnp.pad(st, ((0, 0), (0, bp - n_seq), (0, 0), (0, 0)))
        s0re, s0im = st[:, :, 0], st[:, :, 1]
    u2, fin = s5_chunked(hn2, (row0 // S5_L) // m, n_seq, nc, mats, s0re, s0im, bp=bp)
    u = jnp.stack(u2, axis=1).reshape(m * S5_L, -1)
    fin = fin.reshape(g, bp, 2, 2, p)[:, :n_seq]
    return u, jnp.transpose(fin, (1, 3, 2, 0, 4))


def _softmax_pv(scores, values, sink):
    m = sink
    for s in scores:
        m = jnp.maximum(m, jnp.max(s, axis=-1, keepdims=True))
    den = jnp.exp(sink - m)
    acc = None
    for s, v in zip(scores, values):
        p = jnp.exp(s - m)
        den = den + jnp.sum(p, axis=-1, keepdims=True)
        pv = jnp.dot(p.astype(BF16), v, preferred_element_type=F32)
        acc = pv if acc is None else acc + pv
    return acc / den


def _attn_prompt_kernel(sink_ref, q_ref, k_ref, v_ref, o_ref):
    k = k_ref[...].astype(BF16)
    v = v_ref[...].astype(BF16)
    for h in range(N_KV):
        hs = slice(h * HEAD_DIM, (h + 1) * HEAD_DIM)
        kh, vh = k[:, hs], v[:, hs]
        for g in range(Q_PER_KV):
            c0 = (h * Q_PER_KV + g) * HEAD_DIM
            qg = (q_ref[:, c0:c0 + HEAD_DIM] * ATT_SCALE).astype(BF16)
            s = lax.dot_general(qg, kh, NT_DIMS, preferred_element_type=F32)
            o_ref[:, c0:c0 + HEAD_DIM] = _softmax_pv([s], [vh], sink_ref[h * Q_PER_KV + g])


def attn_prompt(qkv, sink, n_seq, t):
    dq = N_HEADS * HEAD_DIM
    kcol = dq // KV_W
    return pl.pallas_call(
        _attn_prompt_kernel,
        grid=(n_seq,),
        in_specs=[pl.BlockSpec(memory_space=pltpu.SMEM),
                  pl.BlockSpec((t, dq), lambda b: (b, 0)),
                  pl.BlockSpec((t, KV_W), lambda b: (b, kcol)),
                  pl.BlockSpec((t, KV_W), lambda b: (b, kcol + 1))],
        out_specs=pl.BlockSpec((t, dq), lambda b: (b, 0)),
        out_shape=jax.ShapeDtypeStruct((n_seq * t, dq), F32),
        compiler_params=_cparams("arbitrary"),
        name="attn_prompt",
    )(sink, qkv, qkv, qkv)


def _rope(x, cos, sin):
    w = x.shape[1]
    low = (lax.broadcasted_iota(jnp.int32, x.shape, 1) & (HEAD_DIM // 4)) == 0
    partner = jnp.where(low, pltpu.roll(x, w - HEAD_DIM // 4, axis=1), pltpu.roll(x, HEAD_DIM // 4, axis=1))
    return x * cos + partner * sin


def _attn_sample_kernel(sink_ref, q_ref, k_ref, v_ref, kc_ref, vc_ref, cos_ref, sin_ref, o_ref,
                        kw_s, vw_s, kc_s, vc_s, *, t):
    n = pl.program_id(1)
    blk = ATT_BLOCK

    @pl.when(n == 0)
    def _():
        zeros = jnp.zeros((blk, KV_W), BF16)
        kw_s[0:blk, :] = zeros
        vw_s[0:blk, :] = zeros
        kw_s[blk + t:2 * blk + t, :] = zeros
        vw_s[blk + t:2 * blk + t, :] = zeros
        kw_s[blk:blk + t, :] = _rope(k_ref[...], cos_ref[...], sin_ref[...]).astype(BF16)
        vw_s[blk:blk + t, :] = v_ref[...].astype(BF16)
        kc_s[...] = kc_ref[...].astype(BF16)
        vc_s[...] = vc_ref[...].astype(BF16)

    r0 = pl.multiple_of(n * blk, blk)
    cq = cos_ref[pl.ds(r0, blk), :]
    sq = sin_ref[pl.ds(r0, blk), :]
    kw = kw_s[pl.ds(r0, 3 * blk), :]
    vw = vw_s[pl.ds(r0, 3 * blk), :]
    qi = lax.broadcasted_iota(jnp.int32, (blk, 3 * blk), 0)
    kj = lax.broadcasted_iota(jnp.int32, (blk, 3 * blk), 1)
    kpos = n * blk - blk + kj
    valid = (jnp.abs(kj - blk - qi) <= WINDOW) & (kpos >= 0) & (kpos < t)
    for h in range(N_KV):
        hs = slice(h * HEAD_DIM, (h + 1) * HEAD_DIM)
        qh = (_rope(q_ref[:, h * KV_W:(h + 1) * KV_W], cq, sq) * ATT_SCALE).astype(BF16)
        kh, vh, kch, vch = kw[:, hs], vw[:, hs], kc_s[:, hs], vc_s[:, hs]
        for g in range(Q_PER_KV):
            qg = qh[:, g * HEAD_DIM:(g + 1) * HEAD_DIM]
            s_loc = lax.dot_general(qg, kh, NT_DIMS, preferred_element_type=F32)
            s_loc = jnp.where(valid, s_loc, NEG_INF)
            s_ctx = lax.dot_general(qg, kch, NT_DIMS, preferred_element_type=F32)
            c0 = (h * Q_PER_KV + g) * HEAD_DIM
            o_ref[:, c0:c0 + HEAD_DIM] = _softmax_pv([s_loc, s_ctx], [vh, vch], sink_ref[h * Q_PER_KV + g])


def _rope_tables(t):
    quarter = HEAD_DIM // 4
    freqs = ROPE_BASE ** (-jnp.arange(quarter, dtype=F32) / quarter)
    pos = jnp.arange(t)
    ang_r = (pos // GRID_W).astype(F32)[:, None] * freqs
    ang_c = (pos % GRID_W).astype(F32)[:, None] * freqs
    cos = jnp.concatenate([jnp.cos(ang_r), jnp.cos(ang_r), jnp.cos(ang_c), jnp.cos(ang_c)], axis=-1)
    sin = jnp.concatenate([-jnp.sin(ang_r), jnp.sin(ang_r), -jnp.sin(ang_c), jnp.sin(ang_c)], axis=-1)
    return jnp.tile(cos, (1, N_KV)), jnp.tile(sin, (1, N_KV))


def attn_sample(qkv, row0, sink, n_seq, t, k_ctx, v_ctx):
    dq = N_HEADS * HEAD_DIM
    kcol = dq // KV_W
    nb = t // ATT_BLOCK
    lc = k_ctx.shape[1]
    cos, sin = _rope_tables(t)
    qblk0, sblk0 = row0 // ATT_BLOCK, row0 // t
    return pl.pallas_call(
        functools.partial(_attn_sample_kernel, t=t),
        grid=(n_seq, nb),
        in_specs=[pl.BlockSpec(memory_space=pltpu.SMEM),
                  pl.BlockSpec((ATT_BLOCK, dq), lambda b, n: (qblk0 + b * nb + n, 0)),
                  pl.BlockSpec((t, KV_W), lambda b, n: (sblk0 + b, kcol)),
                  pl.BlockSpec((t, KV_W), lambda b, n: (sblk0 + b, kcol + 1)),
                  pl.BlockSpec((None, lc, KV_W), lambda b, n: (b, 0, 0)),
                  pl.BlockSpec((None, lc, KV_W), lambda b, n: (b, 0, 0)),
                  pl.BlockSpec((t, KV_W), lambda b, n: (0, 0)),
                  pl.BlockSpec((t, KV_W), lambda b, n: (0, 0))],
        out_specs=pl.BlockSpec((ATT_BLOCK, dq), lambda b, n: (b * nb + n, 0)),
        out_shape=jax.ShapeDtypeStruct((n_seq * t, dq), F32),
        scratch_shapes=[pltpu.VMEM((t + 2 * ATT_BLOCK, KV_W), BF16), pltpu.VMEM((t + 2 * ATT_BLOCK, KV_W), BF16),
                        pltpu.VMEM((lc, KV_W), BF16), pltpu.VMEM((lc, KV_W), BF16)],
        compiler_params=_cparams("arbitrary", "arbitrary"),
        name="attn_sample",
    )(sink, qkv, qkv, qkv, k_ctx, v_ctx, cos, sin)


def _moe_kernel(x_ref, wg_ref, wu_ref, wd_ref, gt_ref, seg_ref, g2_ref, o_ref, wg_bf, wu_bf, wd_bf, *, n_seg):
    wg_bf[...] = wg_ref[...].astype(BF16)
    wu_bf[...] = wu_ref[...].astype(BF16)
    wd_bf[...] = wd_ref[...].astype(BF16)
    rsub = min(MOE_ROW_SUB, x_ref.shape[0])
    for r in range(x_ref.shape[0] // rsub):
        rs = slice(r * rsub, (r + 1) * rsub)
        x = x_ref[rs, :]
        hg = jnp.dot(x, wg_bf[...], preferred_element_type=F32)
        hu = jnp.dot(x, wu_bf[...], preferred_element_type=F32)
        he = ((hg * jax.nn.sigmoid(hg)) * hu).astype(BF16)
        y = jnp.dot(he, wd_bf[...], preferred_element_type=F32)
        seg = seg_ref[rs, :]
        g2 = jnp.zeros(y.shape, F32)
        for s in range(n_seg):
            g2 = jnp.where(seg == s, g2_ref[s:s + 1, :], g2)
        o_ref[rs, :] = y * (gt_ref[rs, :] * g2)


def moe_experts(xe, w_gate, w_up, w_down, layer, gates, seg, mod, n_seg):
    ne, r, d = xe.shape
    dff = w_gate.shape[3]
    return pl.pallas_call(
        functools.partial(_moe_kernel, n_seg=n_seg),
        grid=(ne,),
        in_specs=[pl.BlockSpec((None, r, d), lambda e: (e, 0, 0)),
                  pl.BlockSpec((None, None, d, dff), lambda e: (layer, e, 0, 0)),
                  pl.BlockSpec((None, None, d, dff), lambda e: (layer, e, 0, 0)),
                  pl.BlockSpec((None, None, dff, d), lambda e: (layer, e, 0, 0)),
                  pl.BlockSpec((None, r, 1), lambda e: (e, 0, 0)),
                  pl.BlockSpec((None, r, 1), lambda e: (e, 0, 0)),
                  pl.BlockSpec((SUBLANES, None, d), lambda e: (layer, 0, 5))],
        out_specs=pl.BlockSpec((None, r, d), lambda e: (e, 0, 0)),
        out_shape=jax.ShapeDtypeStruct((ne, r, d), F32),
        scratch_shapes=[pltpu.VMEM((d, dff), BF16), pltpu.VMEM((d, dff), BF16), pltpu.VMEM((dff, d), BF16)],
        compiler_params=_cparams("arbitrary"),
        name="moe_experts",
    )(xe, w_gate, w_up, w_down, gates, seg, mod)


def _expert_choice(aff, rows):
    ne = aff.shape[1]
    sizes = (rows.n_prompt, rows.n_sample)
    caps = [(EC_FACTOR * n) // ne for n in sizes]
    if sizes[0] == sizes[1]:
        n = sizes[0]
        gt, ix = lax.top_k(jnp.swapaxes(aff.reshape(2, n, ne), 1, 2), caps[0])
        ix = ix + jnp.array([0, n], jnp.int32)[:, None, None]
        return (jnp.concatenate([gt[0], gt[1]], axis=1), jnp.concatenate([ix[0], ix[1]], axis=1))
    gts, ixs, off = [], [], 0
    for n, cap in zip(sizes, caps):
        gt, ix = lax.top_k(aff[off:off + n].T, cap)
        gts.append(gt)
        ixs.append(ix + off)
        off += n
    return jnp.concatenate(gts, axis=1), jnp.concatenate(ixs, axis=1)


def moe_layer(x, h2, aff, mod, rows, layer, w_gate, w_up, w_down, n_seg):
    m, d = x.shape
    gates, idx = _expert_choice(aff, rows)
    seg = jnp.where(idx < rows.n_prompt, 0, 1 + (idx - rows.n_prompt) // rows.t_sample)
    xe = h2[idx]
    ye = moe_experts(xe, w_gate, w_up, w_down, layer, gates[..., None], seg[..., None], mod, n_seg)
    return x.at[idx.reshape(-1)].add(ye.reshape(-1, d))


def kernel(x_prompt, x_sample, state_rglru, state_s5, cache_k, cache_v, c, c_ctx, ada_w, ada_b, norm1_g, norm2_g, rg_w_in, rg_conv_w, rg_conv_b, rg_w_a, rg_b_a, rg_w_x, rg_b_x, rg_lambda, rg_w_out, s5_a_re, s5_a_im, s5_log_dt, s5_b_re, s5_b_im, s5_c_re, s5_c_im, s5_d, s5_w_glu, attn_w_qkv, attn_w_o, attn_sink, router_w, moe_w_gate, moe_w_up, moe_w_down, final_norm_g):
    bp_, tp, d = x_prompt.shape
    bs, ts, _ = x_sample.shape
    n_p, n_s = bp_ * tp, bs * ts
    depth = ada_w.shape[0]
    rows = _Rows(n_p, n_s, ts)
    assert bs + 1 <= SUBLANES and n_p % ts == 0

    x = jnp.concatenate([x_prompt.reshape(n_p, d), x_sample.reshape(n_s, d)], axis=0)
    cond = jnp.concatenate([c_ctx[None, :], c, jnp.zeros((SUBLANES - 1 - bs, d), F32)], axis=0)
    mod_all = ada_modulation_all(cond, ada_w, ada_b)
    mod = mod_all.reshape(depth * SUBLANES, 1, 6 * d)
    g1 = norm1_g.reshape(depth, 1, d)
    g2 = norm2_g.reshape(depth, 1, d)

    new_rg, new_s5, new_k, new_v = [], [], [], []
    for l in range(depth):
        kind, j = l % 3, l // 3
        if kind == 0:
            gu = norm_mod_matmul(x, g1, mod, rows, l, rg_w_in, j)
            args = (j, rg_conv_w, rg_conv_b, rg_w_a, rg_b_a, rg_w_x, rg_b_x, rg_lambda)
            r = gu.shape[1] // 2
            yp, fin = rglru_scan(gu, 0, bp_, tp, *args, jnp.zeros((bp_, 1, 2, r), F32), 0)
            ys, _ = rglru_scan(gu, n_p, bs, ts, *args, state_rglru, j)
            new_rg.append(fin)
            x, h2, aff = matmul_gated_residual(yp, ys, rg_w_out, j, x, mod, rows, l, g2, router_w)
        elif kind == 1:
            hn = norm_mod(x, g1, mod, rows, l)
            mats = s5_chunk_operators(s5_a_re[j], s5_a_im[j], s5_log_dt[j], s5_b_re[j], s5_b_im[j],
                                      s5_c_re[j], s5_c_im[j], s5_d[j])
            hn2 = hn.reshape((n_p + n_s) // S5_L, S5_L * d)
            up, st = s5_mixer_group(hn2, 0, bp_, tp, mats, None)
            us, _ = s5_mixer_group(hn2, n_p, bs, ts, mats, state_s5[:, j])
            new_s5.append(st)
            x, h2, aff = glu_gated_residual(up, us, s5_w_glu, j, x, mod, rows, l, g2, router_w)
        else:
            qkv = norm_mod_matmul(x, g1, mod, rows, l, attn_w_qkv, j)
            dq = N_HEADS * HEAD_DIM
            new_k.append(qkv[:n_p, dq:dq + KV_W].reshape(bp_, tp, N_KV, HEAD_DIM))
            new_v.append(qkv[:n_p, dq + KV_W:].reshape(bp_, tp, N_KV, HEAD_DIM))
            op = attn_prompt(qkv, attn_sink[j], bp_, tp)
            lc = cache_k.shape[2]
            os_ = attn_sample(qkv, n_p, attn_sink[j], bs, ts,
                              cache_k[:, j].reshape(bs, lc, KV_W), cache_v[:, j].reshape(bs, lc, KV_W))
            x, h2, aff = matmul_gated_residual(op, os_, attn_w_o, j, x, mod, rows, l, g2, router_w)
        x = moe_layer(x, h2, aff, mod, rows, l, moe_w_gate, moe_w_up, moe_w_down, bs + 1)

    y = final_norm(x, final_norm_g, rows.tm)
    return (y[:n_p].reshape(bp_, tp, d), y[n_p:].reshape(bs, ts, d),
            jnp.stack(new_rg, axis=1), jnp.stack(new_s5, axis=1),
            jnp.stack(new_k, axis=1), jnp.stack(new_v, axis=1))
```

```python
import functools
import math

import jax
import jax.numpy as jnp
from jax import lax
from jax.experimental import pallas as pl
from jax.experimental.pallas import tpu as pltpu

F32 = jnp.float32
BF16 = jnp.bfloat16
HIGHEST = lax.Precision.HIGHEST

EPS = 1e-6
RG_C = 8.0
RG_BS = 128
RG_TILE = 512
S5_H = 16
S5_L = 16
S5_PREP_GROUPS = 4
N_HEADS = 16
N_KV = 4
Q_PER_KV = N_HEADS // N_KV
HEAD_DIM = 64
KV_W = N_KV * HEAD_DIM
GRID_W = 64
WINDOW = 128
ATT_BLOCK = 128
ROPE_BASE = 10000.0
ATT_SCALE = HEAD_DIM ** -0.5
assert math.frexp(ATT_SCALE)[0] == 0.5
NEG_INF = -1e30
N_EXPERTS = 16
EC_FACTOR = 2
SUBLANES = 8
ROW_TILE = 512
MOE_ROW_SUB = 256
COMBINE_UNROLL = 8
VMEM_LIMIT = 56 * 1024 * 1024
NT_DIMS = (((1,), (1,)), ((), ()))


def _cparams(*sem):
    return pltpu.CompilerParams(dimension_semantics=sem, vmem_limit_bytes=VMEM_LIMIT)


def _gelu(x):
    return x * (0.5 * (1.0 + jnp.tanh(math.sqrt(2.0 / math.pi) * (x + 0.044715 * (x * x * x)))))


def _sigmoid(x):
    return 0.5 * jnp.tanh(0.5 * x) + 0.5


def _norm_mod(x, g, sc, sh):
    ms = jnp.mean(x * x, axis=-1, keepdims=True)
    return ((x * lax.rsqrt(ms + EPS)) * g) * (1.0 + sc) + sh


def _mod_kernel(c_ref, w_ref, b_ref, o_ref):
    c = c_ref[...]
    s = (c * jax.nn.sigmoid(c)).astype(BF16)
    o_ref[...] = jnp.dot(s, w_ref[...].astype(BF16), preferred_element_type=F32) + b_ref[...]


def ada_modulation_all(cond, ada_w, ada_b):
    n_layers, d, n = ada_w.shape
    tn = 1536
    return pl.pallas_call(
        _mod_kernel,
        grid=(n_layers, n // tn),
        in_specs=[pl.BlockSpec((SUBLANES, d), lambda l, j: (0, 0)),
                  pl.BlockSpec((None, d, tn), lambda l, j: (l, 0, j)),
                  pl.BlockSpec((None, 1, tn), lambda l, j: (l, 0, j))],
        out_specs=pl.BlockSpec((None, SUBLANES, tn), lambda l, j: (l, 0, j)),
        out_shape=jax.ShapeDtypeStruct((n_layers, SUBLANES, n), F32),
        compiler_params=_cparams("arbitrary", "arbitrary"),
        name="ada_mod",
    )(cond, ada_w, ada_b.reshape(n_layers, 1, n))


class _Rows:
    def __init__(self, n_prompt, n_sample, t_sample):
        self.n_prompt = n_prompt
        self.n_sample = n_sample
        self.t_sample = t_sample
        self.tm = min(ROW_TILE, n_prompt, t_sample)
        assert n_prompt % self.tm == 0 and t_sample % self.tm == 0
        self.prompt_blocks = n_prompt // self.tm
        self.sample_blocks = n_sample // self.tm

    def seg(self, i):
        r = i * self.tm
        return jnp.where(r < self.n_prompt, 0, 1 + lax.div(r - self.n_prompt, self.t_sample))


def _mod_spec(rows, layer, width, chunk, m_axis):
    def imap(*ids):
        return (layer * SUBLANES + rows.seg(ids[m_axis]), 0, chunk)
    return pl.BlockSpec((None, 1, width), imap)


def _gain_spec(layer, d):
    return pl.BlockSpec((None, 1, d), lambda *ids: (layer, 0, 0))


def _nm_kernel(x_ref, g_ref, sc_ref, sh_ref, w_ref, o_ref, wbf_ref):
    @pl.when(pl.program_id(1) == 0)
    def _():
        wbf_ref[...] = w_ref[...].astype(BF16)
    h = _norm_mod(x_ref[...], g_ref[...], sc_ref[...], sh_ref[...])
    o_ref[...] = jnp.dot(h.astype(BF16), wbf_ref[...], preferred_element_type=F32)


def norm_mod_matmul(x, gains, mod, rows, layer, w, wl):
    m, d = x.shape
    n = w.shape[2]
    tm = rows.tm
    tn = n
    return pl.pallas_call(
        _nm_kernel,
        grid=(n // tn, m // tm),
        in_specs=[pl.BlockSpec((tm, d), lambda j, i: (i, 0)),
                  _gain_spec(layer, d),
                  _mod_spec(rows, layer, d, 1, 1),
                  _mod_spec(rows, layer, d, 0, 1),
                  pl.BlockSpec((None, d, tn), lambda j, i: (wl, 0, j))],
        out_specs=pl.BlockSpec((tm, tn), lambda j, i: (i, j)),
        out_shape=jax.ShapeDtypeStruct((m, n), F32),
        scratch_shapes=[pltpu.VMEM((d, tn), BF16)],
        compiler_params=_cparams("arbitrary", "arbitrary"),
        name="norm_mod_matmul",
    )(x, gains, mod, mod, w)


def _norm_only_kernel(x_ref, g_ref, sc_ref, sh_ref, o_ref):
    o_ref[...] = _norm_mod(x_ref[...], g_ref[...], sc_ref[...], sh_ref[...])


def norm_mod(x, gains, mod, rows, layer):
    m, d = x.shape
    tm = rows.tm
    return pl.pallas_call(
        _norm_only_kernel,
        grid=(m // tm,),
        in_specs=[pl.BlockSpec((tm, d), lambda i: (i, 0)),
                  _gain_spec(layer, d),
                  _mod_spec(rows, layer, d, 1, 0),
                  _mod_spec(rows, layer, d, 0, 0)],
        out_specs=pl.BlockSpec((tm, d), lambda i: (i, 0)),
        out_shape=jax.ShapeDtypeStruct((m, d), F32),
        compiler_params=_cparams("arbitrary"),
        name="norm_mod",
    )(x, gains, mod, mod)


def _final_norm_kernel(x_ref, g_ref, op_ref, os_ref, *, npb):
    i = pl.program_id(0)
    x = x_ref[...]
    ms = jnp.mean(x * x, axis=-1, keepdims=True)
    y = (x * lax.rsqrt(ms + EPS)) * g_ref[...]

    @pl.when(i < npb)
    def _():
        op_ref[...] = y

    @pl.when(i >= npb)
    def _():
        os_ref[...] = y


def final_norm(x, g, rows):
    m, d = x.shape
    tm, npb, nsb = rows.tm, rows.prompt_blocks, rows.sample_blocks
    return pl.pallas_call(
        functools.partial(_final_norm_kernel, npb=npb),
        grid=(m // tm,),
        in_specs=[pl.BlockSpec((tm, d), lambda i: (i, 0)),
                  pl.BlockSpec((1, d), lambda i: (0, 0))],
        out_specs=[pl.BlockSpec((tm, d), lambda i: (jnp.minimum(i, npb - 1), 0)),
                   pl.BlockSpec((tm, d), lambda i: (jnp.clip(i - npb, 0, nsb - 1), 0))],
        out_shape=[jax.ShapeDtypeStruct((rows.n_prompt, d), F32), jax.ShapeDtypeStruct((rows.n_sample, d), F32)],
        compiler_params=_cparams("arbitrary"),
        name="final_norm",
    )(x, g.reshape(1, d))


def _router_epilogue(x_new, g_ref, sc_ref, sh_ref, rw_ref, h_ref, aff_ref):
    h = _norm_mod(x_new, g_ref[...], sc_ref[...], sh_ref[...])
    h_hi = h.astype(BF16)
    h_ref[...] = h_hi
    h_lo = (h - h_hi.astype(F32)).astype(BF16)
    rw = rw_ref[...]
    rw_hi = rw.astype(BF16)
    rw_lo = (rw - rw_hi.astype(F32)).astype(BF16)
    logits = (jnp.dot(h_hi, rw_hi, preferred_element_type=F32)
              + (jnp.dot(h_hi, rw_lo, preferred_element_type=F32)
                 + jnp.dot(h_lo, rw_hi, preferred_element_type=F32)))
    e = jnp.exp(logits - jnp.max(logits, axis=-1, keepdims=True))
    aff_ref[...] = e / jnp.sum(e, axis=-1, keepdims=True)


def _mmres_kernel(ap_ref, as_ref, w_ref, r_ref, gt_ref, g_ref, sc_ref, sh_ref, rw_ref,
                  o_ref, h_ref, aff_ref, wbf_ref, *, npb):
    i = pl.program_id(0)

    @pl.when(i == 0)
    def _():
        wbf_ref[...] = w_ref[...].astype(BF16)

    def emit(a_ref):
        acc = jnp.dot(a_ref[...].astype(BF16), wbf_ref[...], preferred_element_type=F32)
        x_new = r_ref[...] + gt_ref[...] * acc
        o_ref[...] = x_new
        _router_epilogue(x_new, g_ref, sc_ref, sh_ref, rw_ref, h_ref, aff_ref)

    pl.when(i < npb)(lambda: emit(ap_ref))
    pl.when(i >= npb)(lambda: emit(as_ref))


def _sublayer2_specs(rows, layer, d, ne):
    tm = rows.tm
    ins = [_gain_spec(layer, d), _mod_spec(rows, layer, d, 4, 0), _mod_spec(rows, layer, d, 3, 0),
           pl.BlockSpec((None, d, ne), lambda i: (layer, 0, 0))]
    outs = [pl.BlockSpec((tm, d), lambda i: (i, 0)), pl.BlockSpec((tm, ne), lambda i: (i, 0))]
    return ins, outs


def _two_group_specs(rows, k):
    npb, nsb, tm = rows.prompt_blocks, rows.sample_blocks, rows.tm
    return [pl.BlockSpec((tm, k), lambda i: (jnp.minimum(i, npb - 1), 0)),
            pl.BlockSpec((tm, k), lambda i: (jnp.clip(i - npb, 0, nsb - 1), 0))]


def matmul_gated_residual(a_p, a_s, w, wl, resid, mod, rows, layer, gains2, router_w):
    k = a_p.shape[1]
    m, d = resid.shape
    tm = rows.tm
    ne = router_w.shape[2]
    r_in, r_out = _sublayer2_specs(rows, layer, d, ne)
    return pl.pallas_call(
        functools.partial(_mmres_kernel, npb=rows.prompt_blocks),
        grid=(m // tm,),
        in_specs=_two_group_specs(rows, k) + [
            pl.BlockSpec((None, k, d), lambda i: (wl, 0, 0)),
            pl.BlockSpec((tm, d), lambda i: (i, 0)),
            _mod_spec(rows, layer, d, 2, 0)] + r_in,
        out_specs=[pl.BlockSpec((tm, d), lambda i: (i, 0))] + r_out,
        out_shape=[jax.ShapeDtypeStruct((m, d), F32), jax.ShapeDtypeStruct((m, d), BF16),
                   jax.ShapeDtypeStruct((m, ne), F32)],
        scratch_shapes=[pltpu.VMEM((k, d), BF16)],
        compiler_params=_cparams("arbitrary"),
        name="matmul_gated_residual",
    )(a_p, a_s, w, resid, mod, gains2, mod, mod, router_w)


def _glures_kernel(ap_ref, as_ref, wv_ref, wg_ref, r_ref, gt_ref, g_ref, sc_ref, sh_ref, rw_ref,
                   o_ref, h_ref, aff_ref, wv_bf, wg_bf, *, npb):
    i = pl.program_id(0)

    @pl.when(i == 0)
    def _():
        wv_bf[...] = wv_ref[...].astype(BF16)
        wg_bf[...] = wg_ref[...].astype(BF16)

    def emit(a_ref):
        a = a_ref[...].astype(BF16)
        v = jnp.dot(a, wv_bf[...], preferred_element_type=F32)
        g = jnp.dot(a, wg_bf[...], preferred_element_type=F32)
        x_new = r_ref[...] + gt_ref[...] * (v * jax.nn.sigmoid(g))
        o_ref[...] = x_new
        _router_epilogue(x_new, g_ref, sc_ref, sh_ref, rw_ref, h_ref, aff_ref)

    pl.when(i < npb)(lambda: emit(ap_ref))
    pl.when(i >= npb)(lambda: emit(as_ref))


def glu_gated_residual(a_p, a_s, w_glu, wl, resid, mod, rows, layer, gains2, router_w):
    k = a_p.shape[1]
    m, d = resid.shape
    tm = rows.tm
    ne = router_w.shape[2]
    r_in, r_out = _sublayer2_specs(rows, layer, d, ne)
    return pl.pallas_call(
        functools.partial(_glures_kernel, npb=rows.prompt_blocks),
        grid=(m // tm,),
        in_specs=_two_group_specs(rows, k) + [
            pl.BlockSpec((None, k, d), lambda i: (wl, 0, 0)),
            pl.BlockSpec((None, k, d), lambda i: (wl, 0, 1)),
            pl.BlockSpec((tm, d), lambda i: (i, 0)),
            _mod_spec(rows, layer, d, 2, 0)] + r_in,
        out_specs=[pl.BlockSpec((tm, d), lambda i: (i, 0))] + r_out,
        out_shape=[jax.ShapeDtypeStruct((m, d), F32), jax.ShapeDtypeStruct((m, d), BF16),
                   jax.ShapeDtypeStruct((m, ne), F32)],
        scratch_shapes=[pltpu.VMEM((k, d), BF16), pltpu.VMEM((k, d), BF16)],
        compiler_params=_cparams("arbitrary"),
        name="glu_gated_residual",
    )(a_p, a_s, w_glu, w_glu, resid, mod, gains2, mod, mod, router_w)


def _rglru_kernel(gate_ref, u_ref, cw_ref, cb_ref, wa_ref, ba_ref, wx_ref, bx_ref, lam_ref, h0_ref,
                  y_ref, fin_ref, af_s, bf_s, ab_s, bb_s, hf_s, hb_s):
    t, cw = u_ref.shape
    u = u_ref[...]
    row = lax.broadcasted_iota(jnp.int32, (t, cw), 0)

    def shifted(x, k):
        if k > 0:
            return jnp.where(row >= k, pltpu.roll(x, k, axis=0), 0.0)
        return jnp.where(row < t + k, pltpu.roll(x, t + k, axis=0), 0.0)

    cwv = cw_ref[...]
    uc = (cwv[0:1] * shifted(u, 2) + cwv[1:2] * shifted(u, 1) + cwv[2:3] * u
          + cwv[3:4] * shifted(u, -1) + cb_ref[...])

    a_scr = (af_s, ab_s)
    b_scr = (bf_s, bb_s)
    for k in range(2):
        nl = -lam_ref[k:k + 1, :]
        sp = jnp.maximum(nl, 0.0) + jnp.log1p(jnp.exp(-jnp.abs(nl)))
        for hh in range(cw // RG_BS):
            sl = slice(hh * RG_BS, (hh + 1) * RG_BS)
            uh = uc[:, sl]
            ub = uh.astype(BF16)
            r = _sigmoid(jnp.dot(ub, wa_ref[k, hh].astype(BF16), preferred_element_type=F32) + ba_ref[k:k + 1, sl])
            i = _sigmoid(jnp.dot(ub, wx_ref[k, hh].astype(BF16), preferred_element_type=F32) + bx_ref[k:k + 1, sl])
            log_a = (-RG_C * r) * sp[:, sl]
            a = jnp.exp(log_a)
            a_scr[k][:, sl] = a
            b_scr[k][:, sl] = jnp.sqrt(jnp.tanh(-log_a) * (a * a + 1.0)) * (i * uh)

    nblk = t // SUBLANES
    srow = lax.broadcasted_iota(jnp.int32, (SUBLANES, cw), 0)

    def body(n, carry):
        cf, cb = carry
        rf = pl.multiple_of(n * SUBLANES, SUBLANES)
        rb = pl.multiple_of((nblk - 1 - n) * SUBLANES, SUBLANES)
        a = af_s[pl.ds(rf, SUBLANES), :]
        b = bf_s[pl.ds(rf, SUBLANES), :]
        a2 = ab_s[pl.ds(rb, SUBLANES), :]
        b2 = bb_s[pl.ds(rb, SUBLANES), :]
        for s in (1, 2, 4):
            m = srow >= s
            b = jnp.where(m, a * pltpu.roll(b, s, axis=0) + b, b)
            a = jnp.where(m, a * pltpu.roll(a, s, axis=0), a)
            m2 = srow < SUBLANES - s
            b2 = jnp.where(m2, a2 * pltpu.roll(b2, SUBLANES - s, axis=0) + b2, b2)
            a2 = jnp.where(m2, a2 * pltpu.roll(a2, SUBLANES - s, axis=0), a2)
        hf = a * cf + b
        hb = a2 * cb + b2
        hf_s[pl.ds(rf, SUBLANES), :] = hf
        hb_s[pl.ds(rb, SUBLANES), :] = hb
        return hf[SUBLANES - 1:SUBLANES, :], hb[0:1, :]

    cf, cb = lax.fori_loop(0, nblk, body, (h0_ref[0:1, :], h0_ref[1:2, :]))
    fin_ref[0:1, :] = cf
    fin_ref[1:2, :] = cb
    y_ref[...] = (hf_s[...] + hb_s[...]) * _gelu(gate_ref[...])


def rglru_scan(gu, row0, n_seq, t, j, conv_w, conv_b, w_a, b_a, w_x, b_x, lam, h0, h0_j, *, cw=RG_TILE):
    r = gu.shape[1] // 2
    nh = cw // RG_BS
    blk0 = row0 // t
    nc = r // cw
    scr = [pltpu.VMEM((t, cw), F32) for _ in range(6)]
    vec2 = pl.BlockSpec((None, 2, cw), lambda b, c: (j, 0, c))
    gatew = pl.BlockSpec((None, 2, nh, RG_BS, RG_BS), lambda b, c: (j, 0, c, 0, 0))
    return pl.pallas_call(
        _rglru_kernel,
        grid=(n_seq, nc),
        in_specs=[pl.BlockSpec((t, cw), lambda b, c: (blk0 + b, c)),
                  pl.BlockSpec((t, cw), lambda b, c: (blk0 + b, nc + c)),
                  pl.BlockSpec((None, 4, cw), lambda b, c: (j, 0, c)),
                  pl.BlockSpec((None, 1, cw), lambda b, c: (j, 0, c)),
                  gatew, vec2, gatew, vec2, vec2,
                  pl.BlockSpec((None, None, 2, cw), lambda b, c: (b, h0_j, 0, c))],
        out_specs=[pl.BlockSpec((t, cw), lambda b, c: (b, c)),
                   pl.BlockSpec((None, 2, cw), lambda b, c: (b, 0, c))],
        out_shape=[jax.ShapeDtypeStruct((n_seq * t, r), F32),
                   jax.ShapeDtypeStruct((n_seq, 2, r), F32)],
        scratch_shapes=scr,
        compiler_params=_cparams("arbitrary", "arbitrary"),
        name="rglru_scan",
    )(gu, gu, conv_w, conv_b.reshape(conv_b.shape[0], 1, r), w_a, b_a, w_x, b_x, lam, h0)


def _s5_kernel(*refs, bp, n_seq, nc, gpb):
    ell, gw = S5_L, S5_H
    x_refs = refs[:ell]
    tm_ref, win_ref, wre_ref, wim_ref, ar_ref, ai_ref, d_ref, s0re_ref, s0im_ref = refs[ell:ell + 9]
    o_refs = refs[ell + 9:2 * ell + 9]
    fin_ref = refs[2 * ell + 9]
    xg_s, ure_s, uim_s, fre_s, fim_s, bre_s, bim_s = refs[2 * ell + 10:]
    m, mp = n_seq * nc, bp * nc
    lanes = x_refs[0].shape[1]
    per_tile = lanes // gw
    half = ure_s.shape[2] // 2
    lane_grp = lax.shift_right_logical(lax.broadcasted_iota(jnp.int32, (m, lanes), 1), gw.bit_length() - 1)

    def perm(shape, chunk_major_axis):
        i = lax.broadcasted_iota(jnp.int32, shape, chunk_major_axis)
        j = lax.broadcasted_iota(jnp.int32, shape, 1 - chunk_major_axis)
        b = i & (bp - 1)
        c = lax.shift_right_logical(i, bp.bit_length() - 1)
        return jnp.where((j == b * nc + c) & (b < n_seq), 1.0, 0.0).astype(BF16)

    to_chunk_major = perm((mp, m), 0)
    to_batch_major = perm((m, mp), 1)

    def block_transpose(v):
        k = per_tile // 2
        while k >= 1:
            low = (lane_grp & k) == 0
            nxt = list(v)
            for i in range(per_tile):
                if i & k == 0:
                    a, b = v[i], v[i + k]
                    nxt[i] = jnp.where(low, a, pltpu.roll(b, k * gw, axis=1))
                    nxt[i + k] = jnp.where(low, pltpu.roll(a, lanes - k * gw, axis=1), b)
            v = nxt
            k //= 2
        return v

    for tile in range(ell // per_tile):
        by_group = block_transpose([x_refs[tile * per_tile + j][...] for j in range(per_tile)])
        for g in range(gpb):
            xg_s[g, :, tile * lanes:(tile + 1) * lanes] = by_group[g]

    for g in range(gpb):
        xg = xg_s[g]
        xp = jnp.dot(to_chunk_major, xg.astype(BF16), preferred_element_type=F32).astype(BF16)
        u = jnp.dot(xp, win_ref[g].astype(BF16), preferred_element_type=F32)
        ure_s[g] = u[:, :2 * half]
        uim_s[g] = u[:, 2 * half:]

    is_fwd = lax.broadcasted_iota(jnp.int32, (bp, 2 * half), 1) < half
    ar = [ar_ref[g] for g in range(gpb)]
    ai = [ai_ref[g] for g in range(gpb)]

    def body(k, carry):
        rf = pl.multiple_of(k * bp, bp)
        rb = pl.multiple_of((nc - 1 - k) * bp, bp)
        out = []
        for g in range(gpb):
            re, im = carry[2 * g], carry[2 * g + 1]
            fre_s[g, pl.ds(rf, bp), :] = re
            fim_s[g, pl.ds(rf, bp), :] = im
            bre_s[g, pl.ds(rb, bp), :] = re
            bim_s[g, pl.ds(rb, bp), :] = im
            ure = jnp.where(is_fwd, ure_s[g, pl.ds(rf, bp), :], ure_s[g, pl.ds(rb, bp), :])
            uim = jnp.where(is_fwd, uim_s[g, pl.ds(rf, bp), :], uim_s[g, pl.ds(rb, bp), :])
            out += [ar[g] * re - ai[g] * im + ure, ar[g] * im + ai[g] * re + uim]
        return tuple(out)

    init = tuple(r[g] for g in range(gpb) for r in (s0re_ref, s0im_ref))
    fin = lax.fori_loop(0, nc, body, init)
    fwd_all = lax.broadcasted_iota(jnp.int32, (mp, 2 * half), 1) < half
    for g in range(gpb):
        fin_ref[g, :, :2 * half] = fin[2 * g]
        fin_ref[g, :, 2 * half:] = fin[2 * g + 1]
        hre = jnp.where(fwd_all, fre_s[g], bre_s[g]).astype(BF16)
        him = jnp.where(fwd_all, fim_s[g], bim_s[g]).astype(BF16)
        hre = jnp.dot(to_batch_major, hre, preferred_element_type=F32).astype(BF16)
        him = jnp.dot(to_batch_major, him, preferred_element_type=F32).astype(BF16)
        xg = xg_s[g]
        y = (jnp.dot(xg.astype(BF16), tm_ref[g].astype(BF16), preferred_element_type=F32)
             + jnp.dot(hre, wre_ref[g].astype(BF16), preferred_element_type=F32)
             + jnp.dot(him, wim_ref[g].astype(BF16), preferred_element_type=F32)
             + d_ref[g] * xg)
        xg_s[g] = _gelu(y)

    for tile in range(ell // per_tile):
        by_token = block_transpose([xg_s[g, :, tile * lanes:(tile + 1) * lanes] for g in range(gpb)])
        for j in range(per_tile):
            o_refs[tile * per_tile + j][...] = by_token[j]


def s5_chunked(hn2, row_blk, n_seq, nc, mats, s0re, s0im, *, bp):
    tmat, win, wre, wim, ar, ai, dg = mats
    g, w, _ = tmat.shape
    p2 = ar.shape[-1]
    ell = S5_L
    lanes = 128
    gpb = lanes // S5_H
    d = hn2.shape[1] // ell
    m, mp = n_seq * nc, bp * nc
    assert bp & (bp - 1) == 0 and S5_H & (S5_H - 1) == 0 and g % gpb == 0
    col_tiles = d // lanes
    blk = lambda shape: pl.BlockSpec((gpb,) + shape, lambda i: (i, 0, 0))
    x_specs = [pl.BlockSpec((m, lanes), functools.partial(lambda i, l: (row_blk, l * col_tiles + i), l=l))
               for l in range(ell)]
    o_specs = [pl.BlockSpec((m, lanes), lambda i: (0, i)) for _ in range(ell)]
    outs = pl.pallas_call(
        functools.partial(_s5_kernel, bp=bp, n_seq=n_seq, nc=nc, gpb=gpb),
        grid=(g // gpb,),
        in_specs=x_specs + [blk((w, w)), blk((w, 2 * p2)), blk((p2, w)), blk((p2, w)),
                            blk((1, p2)), blk((1, p2)), blk((1, w)), blk((bp, p2)), blk((bp, p2))],
        out_specs=o_specs + [blk((bp, 2 * p2))],
        out_shape=[jax.ShapeDtypeStruct((m, d), F32)] * ell + [jax.ShapeDtypeStruct((g, bp, 2 * p2), F32)],
        scratch_shapes=[pltpu.VMEM((gpb, m, w), F32)] + [pltpu.VMEM((gpb, mp, p2), F32) for _ in range(6)],
        compiler_params=_cparams("arbitrary"),
        name="s5_chunked",
    )(*([hn2] * ell), tmat, win, wre, wim, ar, ai, dg, s0re, s0im)
    return outs[:ell], outs[ell]


def _cmul(ar, ai, br, bi):
    return ar * br - ai * bi, ar * bi + ai * br


def _s5_prep_kernel(are_ref, aim_ref, ldt_ref, btr_ref, bti_ref, cr_ref, ci_ref,
                    tm_ref, win_ref, wre_ref, wim_ref, ar_ref, ai_ref):
    for gi in range(cr_ref.shape[0]):
        _s5_prep_group(*(r.at[gi] for r in (are_ref, aim_ref, ldt_ref, btr_ref, bti_ref, cr_ref, ci_ref,
                                           tm_ref, win_ref, wre_ref, wim_ref, ar_ref, ai_ref)))


def _s5_prep_group(are_ref, aim_ref, ldt_ref, btr_ref, bti_ref, cr_ref, ci_ref,
                   tm_ref, win_ref, wre_ref, wim_ref, ar_ref, ai_ref):
    ell = S5_L
    h, p2 = cr_ref.shape
    w = ell * h
    a_re, a_im = are_ref[...], aim_ref[...]
    dt = jnp.exp(ldt_ref[...])
    steps = lax.broadcasted_iota(jnp.int32, (3 * SUBLANES, p2), 0).astype(F32)
    mag = jnp.exp(steps * (a_re * dt))
    ang = steps * (a_im * dt)
    pw_r, pw_i = mag * jnp.cos(ang), mag * jnp.sin(ang)
    nr, ni = pw_r[1:2] - 1.0, pw_i[1:2]
    den = a_re * a_re + a_im * a_im
    qr, qi = (nr * a_re + ni * a_im) / den, (ni * a_re - nr * a_im) / den
    bb_r, bb_i = _cmul(qr, qi, btr_ref[...], bti_ref[...])
    c_r, c_i = cr_ref[...], ci_ref[...]
    fwd = lax.broadcasted_iota(jnp.int32, (1, p2), 1) < p2 // 2

    def power_rows(m_fwd, m_bwd):
        return (jnp.where(fwd, pw_r[m_fwd:m_fwd + 1], pw_r[m_bwd:m_bwd + 1]),
                jnp.where(fwd, pw_i[m_fwd:m_fwd + 1], pw_i[m_bwd:m_bwd + 1]))

    def stack(x_r, x_i, powers):
        parts = [_cmul(x_r, x_i, *power_rows(*powers(l))) for l in range(ell)]
        return (jnp.concatenate([q[0] for q in parts], axis=0), jnp.concatenate([q[1] for q in parts], axis=0))

    win_r, win_i = stack(bb_r, bb_i, lambda l: (ell - 1 - l, l))
    win_ref[...] = jnp.concatenate([win_r, win_i], axis=1).astype(win_ref.dtype)
    z_r, z_i = stack(c_r, c_i, lambda l: (l + 1, ell - l))
    wre_ref[...] = z_r.T.astype(wre_ref.dtype)
    wim_ref[...] = (-z_i).T.astype(wim_ref.dtype)
    k_r, k_i = stack(c_r, c_i, lambda m: (m, ell - 1 - m))
    mask_f = jnp.where(fwd, 1.0, 0.0)

    def lag_rows(mask):
        return (lax.dot_general(bb_r * mask, k_r, NT_DIMS, precision=HIGHEST, preferred_element_type=F32)
                - lax.dot_general(bb_i * mask, k_i, NT_DIMS, precision=HIGHEST, preferred_element_type=F32))

    kf = lag_rows(mask_f)
    kb = lag_rows(1.0 - mask_f)
    lane = lax.broadcasted_iota(jnp.int32, (h, w), 1)
    blocks = []
    for li in range(ell):
        f_part = kf if li == 0 else pltpu.roll(kf, li * h, axis=1)
        s_b = (w - (ell - 1 - li) * h) % w
        b_part = kb if s_b == 0 else pltpu.roll(kb, s_b, axis=1)
        blocks.append(jnp.where(lane >= li * h, f_part, 0.0) + jnp.where(lane < (li + 1) * h, b_part, 0.0))
    tm_ref[...] = jnp.concatenate(blocks, axis=0).astype(tm_ref.dtype)
    ar_ref[...] = pw_r[ell:ell + 1]
    ai_ref[...] = pw_i[ell:ell + 1]


def s5_chunk_operators(a_re, a_im, log_dt, b_re, b_im, c_re, c_im, d):
    _, g, p = a_re.shape
    h = b_re.shape[-1]
    w = S5_L * h
    two_dir = lambda x: jnp.transpose(x, (1, 0, 2)).reshape(g, 1, 2 * p)
    ldt = two_dir(jnp.broadcast_to(log_dt[:, :, None], (2, g, p)))
    bt = lambda x: jnp.transpose(x, (1, 3, 0, 2)).reshape(g, h, 2 * p)
    ct = lambda x: jnp.transpose(x, (1, 2, 0, 3)).reshape(g, h, 2 * p)
    gps = math.gcd(g, S5_PREP_GROUPS)
    per_g = lambda shape: pl.BlockSpec((gps,) + shape, lambda i: (i, 0, 0))
    tmat, win, wre, wim, ar, ai = pl.pallas_call(
        _s5_prep_kernel,
        grid=(g // gps,),
        in_specs=[per_g((1, 2 * p))] * 3 + [per_g((h, 2 * p))] * 4,
        out_specs=[per_g((w, w)), per_g((w, 4 * p)), per_g((2 * p, w)), per_g((2 * p, w)),
                   per_g((1, 2 * p)), per_g((1, 2 * p))],
        out_shape=[jax.ShapeDtypeStruct((g, w, w), BF16), jax.ShapeDtypeStruct((g, w, 4 * p), BF16),
                   jax.ShapeDtypeStruct((g, 2 * p, w), BF16), jax.ShapeDtypeStruct((g, 2 * p, w), BF16),
                   jax.ShapeDtypeStruct((g, 1, 2 * p), F32), jax.ShapeDtypeStruct((g, 1, 2 * p), F32)],
        compiler_params=_cparams("arbitrary"),
        name="s5_chunk_operators",
    )(two_dir(a_re), two_dir(a_im), ldt, bt(b_re), bt(b_im), ct(c_re), ct(c_im))
    dg = jnp.tile(d.reshape(g, 1, h), (1, S5_L, 1)).reshape(g, 1, w)
    return tmat, win, wre, wim, ar, ai, dg


def s5_mixer_group(hn2, row0, n_seq, t, mats, s0):
    ar = mats[4]
    g = ar.shape[0]
    p = ar.shape[-1] // 2
    nc = t // S5_L
    bp = -(-n_seq // SUBLANES) * SUBLANES
    m = n_seq * nc
    assert (row0 // S5_L) % m == 0
    if s0 is None:
        s0re = jnp.zeros((g, bp, 2 * p), F32)
        s0im = s0re
    else:
        st = jnp.transpose(s0, (3, 0, 2, 1, 4)).reshape(g, n_seq, 2, 2 * p)
        st = jnp.pad(st, ((0, 0), (0, bp - n_seq), (0, 0), (0, 0)))
        s0re, s0im = st[:, :, 0], st[:, :, 1]
    u2, fin = s5_chunked(hn2, (row0 // S5_L) // m, n_seq, nc, mats, s0re, s0im, bp=bp)
    u = jnp.stack(u2, axis=1).reshape(m * S5_L, -1)
    fin = fin.reshape(g, bp, 2, 2, p)[:, :n_seq]
    return u, jnp.transpose(fin, (1, 3, 2, 0, 4))


def _softmax_pv(scores, values, sink):
    m = sink
    for s in scores:
        m = jnp.maximum(m, jnp.max(s, axis=-1, keepdims=True))
    den = jnp.exp(sink - m)
    acc = None
    for s, v in zip(scores, values):
        p = jnp.exp(s - m)
        den = den + jnp.sum(p, axis=-1, keepdims=True)
        pv = jnp.dot(p.astype(BF16), v, preferred_element_type=F32)
        acc = pv if acc is None else acc + pv
    return acc / den


def _attn_prompt_kernel(sink_ref, q_ref, k_ref, v_ref, o_ref):
    k = k_ref[...].astype(BF16)
    v = v_ref[...].astype(BF16)
    for h in range(N_KV):
        hs = slice(h * HEAD_DIM, (h + 1) * HEAD_DIM)
        kh, vh = k[:, hs], v[:, hs]
        for g in range(Q_PER_KV):
            c0 = (h * Q_PER_KV + g) * HEAD_DIM
            qg = (q_ref[:, c0:c0 + HEAD_DIM] * ATT_SCALE).astype(BF16)
            s = lax.dot_general(qg, kh, NT_DIMS, preferred_element_type=F32)
            o_ref[:, c0:c0 + HEAD_DIM] = _softmax_pv([s], [vh], sink_ref[h * Q_PER_KV + g])


def attn_prompt(qkv, sink, n_seq, t):
    dq = N_HEADS * HEAD_DIM
    kcol = dq // KV_W
    return pl.pallas_call(
        _attn_prompt_kernel,
        grid=(n_seq,),
        in_specs=[pl.BlockSpec(memory_space=pltpu.SMEM),
                  pl.BlockSpec((t, dq), lambda b: (b, 0)),
                  pl.BlockSpec((t, KV_W), lambda b: (b, kcol)),
                  pl.BlockSpec((t, KV_W), lambda b: (b, kcol + 1))],
        out_specs=pl.BlockSpec((t, dq), lambda b: (b, 0)),
        out_shape=jax.ShapeDtypeStruct((n_seq * t, dq), F32),
        compiler_params=_cparams("arbitrary"),
        name="attn_prompt",
    )(sink, qkv, qkv, qkv)


def _rope(x, cos, sin):
    w = x.shape[1]
    low = (lax.broadcasted_iota(jnp.int32, x.shape, 1) & (HEAD_DIM // 4)) == 0
    partner = jnp.where(low, pltpu.roll(x, w - HEAD_DIM // 4, axis=1), pltpu.roll(x, HEAD_DIM // 4, axis=1))
    return x * cos + partner * sin


def _attn_sample_kernel(sink_ref, q_ref, k_ref, v_ref, kc_ref, vc_ref, cos_ref, sin_ref, o_ref,
                        kw_s, vw_s, kc_s, vc_s, *, t):
    n = pl.program_id(1)
    blk = ATT_BLOCK

    @pl.when(n == 0)
    def _():
        zeros = jnp.zeros((blk, KV_W), BF16)
        kw_s[0:blk, :] = zeros
        vw_s[0:blk, :] = zeros
        kw_s[blk + t:2 * blk + t, :] = zeros
        vw_s[blk + t:2 * blk + t, :] = zeros
        kw_s[blk:blk + t, :] = _rope(k_ref[...], cos_ref[...], sin_ref[...]).astype(BF16)
        vw_s[blk:blk + t, :] = v_ref[...].astype(BF16)
        kc_s[...] = kc_ref[...].astype(BF16)
        vc_s[...] = vc_ref[...].astype(BF16)

    r0 = pl.multiple_of(n * blk, blk)
    cq = cos_ref[pl.ds(r0, blk), :]
    sq = sin_ref[pl.ds(r0, blk), :]
    kw = kw_s[pl.ds(r0, 3 * blk), :]
    vw = vw_s[pl.ds(r0, 3 * blk), :]
    qi = lax.broadcasted_iota(jnp.int32, (blk, 3 * blk), 0)
    kj = lax.broadcasted_iota(jnp.int32, (blk, 3 * blk), 1)
    kpos = n * blk - blk + kj
    valid = (jnp.abs(kj - blk - qi) <= WINDOW) & (kpos >= 0) & (kpos < t)
    for h in range(N_KV):
        hs = slice(h * HEAD_DIM, (h + 1) * HEAD_DIM)
        qh = (_rope(q_ref[:, h * KV_W:(h + 1) * KV_W], cq, sq) * ATT_SCALE).astype(BF16)
        kh, vh, kch, vch = kw[:, hs], vw[:, hs], kc_s[:, hs], vc_s[:, hs]
        for g in range(Q_PER_KV):
            qg = qh[:, g * HEAD_DIM:(g + 1) * HEAD_DIM]
            s_loc = lax.dot_general(qg, kh, NT_DIMS, preferred_element_type=F32)
            s_loc = jnp.where(valid, s_loc, NEG_INF)
            s_ctx = lax.dot_general(qg, kch, NT_DIMS, preferred_element_type=F32)
            c0 = (h * Q_PER_KV + g) * HEAD_DIM
            o_ref[:, c0:c0 + HEAD_DIM] = _softmax_pv([s_loc, s_ctx], [vh, vch], sink_ref[h * Q_PER_KV + g])


def _rope_tables(t):
    quarter = HEAD_DIM // 4
    freqs = ROPE_BASE ** (-jnp.arange(quarter, dtype=F32) / quarter)
    pos = jnp.arange(t)
    ang_r = (pos // GRID_W).astype(F32)[:, None] * freqs
    ang_c = (pos % GRID_W).astype(F32)[:, None] * freqs
    cos = jnp.concatenate([jnp.cos(ang_r), jnp.cos(ang_r), jnp.cos(ang_c), jnp.cos(ang_c)], axis=-1)
    sin = jnp.concatenate([-jnp.sin(ang_r), jnp.sin(ang_r), -jnp.sin(ang_c), jnp.sin(ang_c)], axis=-1)
    return jnp.tile(cos, (1, N_KV)), jnp.tile(sin, (1, N_KV))


def attn_sample(qkv, row0, sink, n_seq, t, k_ctx, v_ctx):
    dq = N_HEADS * HEAD_DIM
    kcol = dq // KV_W
    nb = t // ATT_BLOCK
    lc = k_ctx.shape[1]
    cos, sin = _rope_tables(t)
    qblk0, sblk0 = row0 // ATT_BLOCK, row0 // t
    return pl.pallas_call(
        functools.partial(_attn_sample_kernel, t=t),
        grid=(n_seq, nb),
        in_specs=[pl.BlockSpec(memory_space=pltpu.SMEM),
                  pl.BlockSpec((ATT_BLOCK, dq), lambda b, n: (qblk0 + b * nb + n, 0)),
                  pl.BlockSpec((t, KV_W), lambda b, n: (sblk0 + b, kcol)),
                  pl.BlockSpec((t, KV_W), lambda b, n: (sblk0 + b, kcol + 1)),
                  pl.BlockSpec((None, lc, KV_W), lambda b, n: (b, 0, 0)),
                  pl.BlockSpec((None, lc, KV_W), lambda b, n: (b, 0, 0)),
                  pl.BlockSpec((t, KV_W), lambda b, n: (0, 0)),
                  pl.BlockSpec((t, KV_W), lambda b, n: (0, 0))],
        out_specs=pl.BlockSpec((ATT_BLOCK, dq), lambda b, n: (b * nb + n, 0)),
        out_shape=jax.ShapeDtypeStruct((n_seq * t, dq), F32),
        scratch_shapes=[pltpu.VMEM((t + 2 * ATT_BLOCK, KV_W), BF16), pltpu.VMEM((t + 2 * ATT_BLOCK, KV_W), BF16),
                        pltpu.VMEM((lc, KV_W), BF16), pltpu.VMEM((lc, KV_W), BF16)],
        compiler_params=_cparams("arbitrary", "arbitrary"),
        name="attn_sample",
    )(sink, qkv, qkv, qkv, k_ctx, v_ctx, cos, sin)


def _moe_kernel(x_ref, wg_ref, wu_ref, wd_ref, gt_ref, seg_ref, g2_ref, o_ref, wg_bf, wu_bf, wd_bf, *, n_seg):
    wg_bf[...] = wg_ref[...].astype(BF16)
    wu_bf[...] = wu_ref[...].astype(BF16)
    wd_bf[...] = wd_ref[...].astype(BF16)
    rsub = min(MOE_ROW_SUB, x_ref.shape[0])
    for r in range(x_ref.shape[0] // rsub):
        rs = slice(r * rsub, (r + 1) * rsub)
        x = x_ref[rs, :]
        hg = jnp.dot(x, wg_bf[...], preferred_element_type=F32)
        hu = jnp.dot(x, wu_bf[...], preferred_element_type=F32)
        he = ((hg * jax.nn.sigmoid(hg)) * hu).astype(BF16)
        y = jnp.dot(he, wd_bf[...], preferred_element_type=F32)
        seg = seg_ref[rs, :]
        g2 = jnp.zeros(y.shape, F32)
        for s in range(n_seg):
            g2 = jnp.where(seg == s, g2_ref[s:s + 1, :], g2)
        o_ref[rs, :] = y * (gt_ref[rs, :] * g2)


def moe_experts(xe, w_gate, w_up, w_down, layer, gates, seg, mod, n_seg):
    ne, r, d = xe.shape
    dff = w_gate.shape[3]
    return pl.pallas_call(
        functools.partial(_moe_kernel, n_seg=n_seg),
        grid=(ne,),
        in_specs=[pl.BlockSpec((None, r, d), lambda e: (e, 0, 0)),
                  pl.BlockSpec((None, None, d, dff), lambda e: (layer, e, 0, 0)),
                  pl.BlockSpec((None, None, d, dff), lambda e: (layer, e, 0, 0)),
                  pl.BlockSpec((None, None, dff, d), lambda e: (layer, e, 0, 0)),
                  pl.BlockSpec((None, r, 1), lambda e: (e, 0, 0)),
                  pl.BlockSpec((None, r, 1), lambda e: (e, 0, 0)),
                  pl.BlockSpec((SUBLANES, None, d), lambda e: (layer, 0, 5))],
        out_specs=pl.BlockSpec((None, r, d), lambda e: (e, 0, 0)),
        out_shape=jax.ShapeDtypeStruct((ne, r, d), F32),
        scratch_shapes=[pltpu.VMEM((d, dff), BF16), pltpu.VMEM((d, dff), BF16), pltpu.VMEM((dff, d), BF16)],
        compiler_params=_cparams("arbitrary"),
        name="moe_experts",
    )(xe, w_gate, w_up, w_down, gates, seg, mod)


def _expert_choice(aff, rows):
    ne = aff.shape[1]
    sizes = (rows.n_prompt, rows.n_sample)
    caps = [(EC_FACTOR * n) // ne for n in sizes]
    if sizes[0] == sizes[1]:
        n = sizes[0]
        gt, ix = lax.top_k(jnp.swapaxes(aff.reshape(2, n, ne), 1, 2), caps[0])
        ix = ix + jnp.array([0, n], jnp.int32)[:, None, None]
        return (jnp.concatenate([gt[0], gt[1]], axis=1), jnp.concatenate([ix[0], ix[1]], axis=1))
    gts, ixs, off = [], [], 0
    for n, cap in zip(sizes, caps):
        gt, ix = lax.top_k(aff[off:off + n].T, cap)
        gts.append(gt)
        ixs.append(ix + off)
        off += n
    return jnp.concatenate(gts, axis=1), jnp.concatenate(ixs, axis=1)


def _combine_kernel(idx_ref, x_hbm, ye_ref, o_hbm, acc, sem, *, n_grp, rows_per):
    g = pl.program_id(0)
    e = pl.program_id(1)
    last = pl.num_programs(1) - 1

    def load(grp):
        return pltpu.make_async_copy(x_hbm.at[pl.ds(grp * n_grp, n_grp), :], acc.at[grp], sem.at[grp])

    def store(grp):
        return pltpu.make_async_copy(acc.at[grp], o_hbm.at[pl.ds(grp * n_grp, n_grp), :], sem.at[2 + grp])

    @pl.when((g == 0) & (e == 0))
    def _():
        load(0).start()
        load(1).start()

    @pl.when(e == 0)
    def _():
        load(g).wait()

    base = (2 * e + g) * rows_per
    first = g * n_grp

    def body(i, carry):
        toks = [idx_ref[base + i * COMBINE_UNROLL + k] - first for k in range(COMBINE_UNROLL)]
        old = [acc[g, pl.ds(t, 1), :] for t in toks]
        add = [ye_ref[pl.ds(i * COMBINE_UNROLL + k, 1), :] for k in range(COMBINE_UNROLL)]
        for t, a, b in zip(toks, old, add):
            acc[g, pl.ds(t, 1), :] = a + b
        return carry

    lax.fori_loop(0, rows_per // COMBINE_UNROLL, body, 0)

    @pl.when(e == last)
    def _():
        store(g).start()

    @pl.when((g == 1) & (e == last))
    def _():
        store(0).wait()
        store(1).wait()


def moe_combine(x, ye, idx, n_grp):
    m, d = x.shape
    ne, r, _ = ye.shape
    rows_per = r // 2
    assert m == 2 * n_grp and rows_per % COMBINE_UNROLL == 0
    grid_spec = pltpu.PrefetchScalarGridSpec(
        num_scalar_prefetch=1,
        grid=(2, ne),
        in_specs=[pl.BlockSpec(memory_space=pl.ANY),
                  pl.BlockSpec((None, rows_per, d), lambda g, e, ix: (e, g, 0))],
        out_specs=pl.BlockSpec(memory_space=pl.ANY),
        scratch_shapes=[pltpu.VMEM((2, n_grp, d), F32), pltpu.SemaphoreType.DMA((4,))])
    return pl.pallas_call(
        functools.partial(_combine_kernel, n_grp=n_grp, rows_per=rows_per),
        grid_spec=grid_spec,
        out_shape=jax.ShapeDtypeStruct((m, d), F32),
        compiler_params=_cparams("arbitrary", "arbitrary"),
        name="moe_combine",
    )(idx.reshape(-1), x, ye)


def moe_layer(x, h2, aff, mod, rows, layer, w_gate, w_up, w_down, n_seg):
    m, d = x.shape
    gates, idx = _expert_choice(aff, rows)
    seg = jnp.where(idx < rows.n_prompt, 0, 1 + (idx - rows.n_prompt) // rows.t_sample)
    xe = h2[idx]
    ye = moe_experts(xe, w_gate, w_up, w_down, layer, gates[..., None], seg[..., None], mod, n_seg)
    if rows.n_prompt == rows.n_sample:
        return moe_combine(x, ye, idx, rows.n_prompt)
    return x.at[idx.reshape(-1)].add(ye.reshape(-1, d))


def kernel(x_prompt, x_sample, state_rglru, state_s5, cache_k, cache_v, c, c_ctx, ada_w, ada_b, norm1_g, norm2_g, rg_w_in, rg_conv_w, rg_conv_b, rg_w_a, rg_b_a, rg_w_x, rg_b_x, rg_lambda, rg_w_out, s5_a_re, s5_a_im, s5_log_dt, s5_b_re, s5_b_im, s5_c_re, s5_c_im, s5_d, s5_w_glu, attn_w_qkv, attn_w_o, attn_sink, router_w, moe_w_gate, moe_w_up, moe_w_down, final_norm_g):
    bp_, tp, d = x_prompt.shape
    bs, ts, _ = x_sample.shape
    n_p, n_s = bp_ * tp, bs * ts
    depth = ada_w.shape[0]
    rows = _Rows(n_p, n_s, ts)
    assert bs + 1 <= SUBLANES and n_p % ts == 0

    x = jnp.concatenate([x_prompt.reshape(n_p, d), x_sample.reshape(n_s, d)], axis=0)
    cond = jnp.concatenate([c_ctx[None, :], c, jnp.zeros((SUBLANES - 1 - bs, d), F32)], axis=0)
    mod_all = ada_modulation_all(cond, ada_w, ada_b)
    mod = mod_all.reshape(depth * SUBLANES, 1, 6 * d)
    g1 = norm1_g.reshape(depth, 1, d)
    g2 = norm2_g.reshape(depth, 1, d)

    new_rg, new_s5, new_k, new_v = [], [], [], []
    for l in range(depth):
        kind, j = l % 3, l // 3
        if kind == 0:
            gu = norm_mod_matmul(x, g1, mod, rows, l, rg_w_in, j)
            args = (j, rg_conv_w, rg_conv_b, rg_w_a, rg_b_a, rg_w_x, rg_b_x, rg_lambda)
            r = gu.shape[1] // 2
            yp, fin = rglru_scan(gu, 0, bp_, tp, *args, jnp.zeros((bp_, 1, 2, r), F32), 0)
            ys, _ = rglru_scan(gu, n_p, bs, ts, *args, state_rglru, j)
            new_rg.append(fin)
            x, h2, aff = matmul_gated_residual(yp, ys, rg_w_out, j, x, mod, rows, l, g2, router_w)
        elif kind == 1:
            hn = norm_mod(x, g1, mod, rows, l)
            mats = s5_chunk_operators(s5_a_re[j], s5_a_im[j], s5_log_dt[j], s5_b_re[j], s5_b_im[j],
                                      s5_c_re[j], s5_c_im[j], s5_d[j])
            hn2 = hn.reshape((n_p + n_s) // S5_L, S5_L * d)
            up, st = s5_mixer_group(hn2, 0, bp_, tp, mats, None)
            us, _ = s5_mixer_group(hn2, n_p, bs, ts, mats, state_s5[:, j])
            new_s5.append(st)
            x, h2, aff = glu_gated_residual(up, us, s5_w_glu, j, x, mod, rows, l, g2, router_w)
        else:
            qkv = norm_mod_matmul(x, g1, mod, rows, l, attn_w_qkv, j)
            dq = N_HEADS * HEAD_DIM
            new_k.append(qkv[:n_p, dq:dq + KV_W].reshape(bp_, tp, N_KV, HEAD_DIM))
            new_v.append(qkv[:n_p, dq + KV_W:].reshape(bp_, tp, N_KV, HEAD_DIM))
            op = attn_prompt(qkv, attn_sink[j], bp_, tp)
            lc = cache_k.shape[2]
            os_ = attn_sample(qkv, n_p, attn_sink[j], bs, ts,
                              cache_k[:, j].reshape(bs, lc, KV_W), cache_v[:, j].reshape(bs, lc, KV_W))
            x, h2, aff = matmul_gated_residual(op, os_, attn_w_o, j, x, mod, rows, l, g2, router_w)
        x = moe_layer(x, h2, aff, mod, rows, l, moe_w_gate, moe_w_up, moe_w_down, bs + 1)

    y_p, y_s = final_norm(x, final_norm_g, rows)
    return (y_p.reshape(bp_, tp, d), y_s.reshape(bs, ts, d),
            jnp.stack(new_rg, axis=1), jnp.stack(new_s5, axis=1),
            jnp.stack(new_k, axis=1), jnp.stack(new_v, axis=1))
```

```python
import functools
import math

import jax
import jax.numpy as jnp
from jax import lax
from jax.experimental import pallas as pl
from jax.experimental.pallas import tpu as pltpu

F32 = jnp.float32
BF16 = jnp.bfloat16
HIGHEST = lax.Precision.HIGHEST

EPS = 1e-6
RG_C = 8.0
RG_BS = 128
RG_TILE = 512
S5_H = 16
S5_L = 16
S5_PREP_GROUPS = 4
N_HEADS = 16
N_KV = 4
Q_PER_KV = N_HEADS // N_KV
HEAD_DIM = 64
KV_W = N_KV * HEAD_DIM
GRID_W = 64
WINDOW = 128
ATT_BLOCK = 128
ROPE_BASE = 10000.0
ATT_SCALE = HEAD_DIM ** -0.5
assert math.frexp(ATT_SCALE)[0] == 0.5
NEG_INF = -1e30
N_EXPERTS = 16
EC_FACTOR = 2
SUBLANES = 8
ROW_TILE = 512
MOE_ROW_SUB = 256
COMBINE_UNROLL = 8
VMEM_LIMIT = 56 * 1024 * 1024
NT_DIMS = (((1,), (1,)), ((), ()))


def _cparams(*sem):
    return pltpu.CompilerParams(dimension_semantics=sem, vmem_limit_bytes=VMEM_LIMIT)


def _gelu(x):
    return x * (0.5 * (1.0 + jnp.tanh(math.sqrt(2.0 / math.pi) * (x + 0.044715 * (x * x * x)))))


def _sigmoid(x):
    return 0.5 * jnp.tanh(0.5 * x) + 0.5


def _norm_mod(x, g, sc, sh):
    ms = jnp.mean(x * x, axis=-1, keepdims=True)
    return ((x * lax.rsqrt(ms + EPS)) * g) * (1.0 + sc) + sh


def _mod_kernel(c_ref, w_ref, b_ref, o_ref):
    c = c_ref[...]
    s = (c * jax.nn.sigmoid(c)).astype(BF16)
    o_ref[...] = jnp.dot(s, w_ref[...].astype(BF16), preferred_element_type=F32) + b_ref[...]


def ada_modulation_all(cond, ada_w, ada_b):
    n_layers, d, n = ada_w.shape
    tn = 1536
    return pl.pallas_call(
        _mod_kernel,
        grid=(n_layers, n // tn),
        in_specs=[pl.BlockSpec((SUBLANES, d), lambda l, j: (0, 0)),
                  pl.BlockSpec((None, d, tn), lambda l, j: (l, 0, j)),
                  pl.BlockSpec((None, 1, tn), lambda l, j: (l, 0, j))],
        out_specs=pl.BlockSpec((None, SUBLANES, tn), lambda l, j: (l, 0, j)),
        out_shape=jax.ShapeDtypeStruct((n_layers, SUBLANES, n), F32),
        compiler_params=_cparams("arbitrary", "arbitrary"),
        name="ada_mod",
    )(cond, ada_w, ada_b.reshape(n_layers, 1, n))


class _Rows:
    def __init__(self, n_prompt, n_sample, t_sample):
        self.n_prompt = n_prompt
        self.n_sample = n_sample
        self.t_sample = t_sample
        self.tm = min(ROW_TILE, n_prompt, t_sample)
        assert n_prompt % self.tm == 0 and t_sample % self.tm == 0
        self.prompt_blocks = n_prompt // self.tm
        self.sample_blocks = n_sample // self.tm

    def seg(self, i):
        r = i * self.tm
        return jnp.where(r < self.n_prompt, 0, 1 + lax.div(r - self.n_prompt, self.t_sample))


def _mod_spec(rows, layer, width, chunk, m_axis):
    def imap(*ids):
        return (layer * SUBLANES + rows.seg(ids[m_axis]), 0, chunk)
    return pl.BlockSpec((None, 1, width), imap)


def _gain_spec(layer, d):
    return pl.BlockSpec((None, 1, d), lambda *ids: (layer, 0, 0))


def _nm_kernel(x_ref, g_ref, sc_ref, sh_ref, w_ref, o_ref, wbf_ref):
    @pl.when(pl.program_id(1) == 0)
    def _():
        wbf_ref[...] = w_ref[...].astype(BF16)
    h = _norm_mod(x_ref[...], g_ref[...], sc_ref[...], sh_ref[...])
    o_ref[...] = jnp.dot(h.astype(BF16), wbf_ref[...], preferred_element_type=F32)


def norm_mod_matmul(x, gains, mod, rows, layer, w, wl):
    m, d = x.shape
    n = w.shape[2]
    tm = rows.tm
    tn = n
    return pl.pallas_call(
        _nm_kernel,
        grid=(n // tn, m // tm),
        in_specs=[pl.BlockSpec((tm, d), lambda j, i: (i, 0)),
                  _gain_spec(layer, d),
                  _mod_spec(rows, layer, d, 1, 1),
                  _mod_spec(rows, layer, d, 0, 1),
                  pl.BlockSpec((None, d, tn), lambda j, i: (wl, 0, j))],
        out_specs=pl.BlockSpec((tm, tn), lambda j, i: (i, j)),
        out_shape=jax.ShapeDtypeStruct((m, n), F32),
        scratch_shapes=[pltpu.VMEM((d, tn), BF16)],
        compiler_params=_cparams("arbitrary", "arbitrary"),
        name="norm_mod_matmul",
    )(x, gains, mod, mod, w)


def _norm_only_kernel(x_ref, g_ref, sc_ref, sh_ref, o_ref):
    o_ref[...] = _norm_mod(x_ref[...], g_ref[...], sc_ref[...], sh_ref[...])


def norm_mod(x, gains, mod, rows, layer):
    m, d = x.shape
    tm = rows.tm
    return pl.pallas_call(
        _norm_only_kernel,
        grid=(m // tm,),
        in_specs=[pl.BlockSpec((tm, d), lambda i: (i, 0)),
                  _gain_spec(layer, d),
                  _mod_spec(rows, layer, d, 1, 0),
                  _mod_spec(rows, layer, d, 0, 0)],
        out_specs=pl.BlockSpec((tm, d), lambda i: (i, 0)),
        out_shape=jax.ShapeDtypeStruct((m, d), F32),
        compiler_params=_cparams("arbitrary"),
        name="norm_mod",
    )(x, gains, mod, mod)


def _final_norm_kernel(x_ref, g_ref, op_ref, os_ref, *, npb):
    i = pl.program_id(0)
    x = x_ref[...]
    ms = jnp.mean(x * x, axis=-1, keepdims=True)
    y = (x * lax.rsqrt(ms + EPS)) * g_ref[...]

    @pl.when(i < npb)
    def _():
        op_ref[...] = y

    @pl.when(i >= npb)
    def _():
        os_ref[...] = y


def final_norm(x, g, rows):
    m, d = x.shape
    tm, npb, nsb = rows.tm, rows.prompt_blocks, rows.sample_blocks
    return pl.pallas_call(
        functools.partial(_final_norm_kernel, npb=npb),
        grid=(m // tm,),
        in_specs=[pl.BlockSpec((tm, d), lambda i: (i, 0)),
                  pl.BlockSpec((1, d), lambda i: (0, 0))],
        out_specs=[pl.BlockSpec((tm, d), lambda i: (jnp.minimum(i, npb - 1), 0)),
                   pl.BlockSpec((tm, d), lambda i: (jnp.clip(i - npb, 0, nsb - 1), 0))],
        out_shape=[jax.ShapeDtypeStruct((rows.n_prompt, d), F32), jax.ShapeDtypeStruct((rows.n_sample, d), F32)],
        compiler_params=_cparams("arbitrary"),
        name="final_norm",
    )(x, g.reshape(1, d))


def _router_epilogue(x_new, g_ref, sc_ref, sh_ref, rw_ref, h_ref, aff_ref):
    h = _norm_mod(x_new, g_ref[...], sc_ref[...], sh_ref[...])
    h_hi = h.astype(BF16)
    h_ref[...] = h_hi
    h_lo = (h - h_hi.astype(F32)).astype(BF16)
    rw = rw_ref[...]
    rw_hi = rw.astype(BF16)
    rw_lo = (rw - rw_hi.astype(F32)).astype(BF16)
    logits = (jnp.dot(h_hi, rw_hi, preferred_element_type=F32)
              + (jnp.dot(h_hi, rw_lo, preferred_element_type=F32)
                 + jnp.dot(h_lo, rw_hi, preferred_element_type=F32)))
    e = jnp.exp(logits - jnp.max(logits, axis=-1, keepdims=True))
    aff_ref[...] = e / jnp.sum(e, axis=-1, keepdims=True)


def _mmres_kernel(ap_ref, as_ref, w_ref, r_ref, gt_ref, g_ref, sc_ref, sh_ref, rw_ref,
                  o_ref, h_ref, aff_ref, wbf_ref, *, npb):
    i = pl.program_id(0)

    @pl.when(i == 0)
    def _():
        wbf_ref[...] = w_ref[...].astype(BF16)

    def emit(a_ref):
        acc = jnp.dot(a_ref[...].astype(BF16), wbf_ref[...], preferred_element_type=F32)
        x_new = r_ref[...] + gt_ref[...] * acc
        o_ref[...] = x_new
        _router_epilogue(x_new, g_ref, sc_ref, sh_ref, rw_ref, h_ref, aff_ref)

    pl.when(i < npb)(lambda: emit(ap_ref))
    pl.when(i >= npb)(lambda: emit(as_ref))


def _sublayer2_specs(rows, layer, d, ne):
    tm = rows.tm
    ins = [_gain_spec(layer, d), _mod_spec(rows, layer, d, 4, 0), _mod_spec(rows, layer, d, 3, 0),
           pl.BlockSpec((None, d, ne), lambda i: (layer, 0, 0))]
    outs = [pl.BlockSpec((tm, d), lambda i: (i, 0)), pl.BlockSpec((tm, ne), lambda i: (i, 0))]
    return ins, outs


def _two_group_specs(rows, k):
    npb, nsb, tm = rows.prompt_blocks, rows.sample_blocks, rows.tm
    return [pl.BlockSpec((tm, k), lambda i: (jnp.minimum(i, npb - 1), 0)),
            pl.BlockSpec((tm, k), lambda i: (jnp.clip(i - npb, 0, nsb - 1), 0))]


def matmul_gated_residual(a_p, a_s, w, wl, resid, mod, rows, layer, gains2, router_w):
    k = a_p.shape[1]
    m, d = resid.shape
    tm = rows.tm
    ne = router_w.shape[2]
    r_in, r_out = _sublayer2_specs(rows, layer, d, ne)
    return pl.pallas_call(
        functools.partial(_mmres_kernel, npb=rows.prompt_blocks),
        grid=(m // tm,),
        in_specs=_two_group_specs(rows, k) + [
            pl.BlockSpec((None, k, d), lambda i: (wl, 0, 0)),
            pl.BlockSpec((tm, d), lambda i: (i, 0)),
            _mod_spec(rows, layer, d, 2, 0)] + r_in,
        out_specs=[pl.BlockSpec((tm, d), lambda i: (i, 0))] + r_out,
        out_shape=[jax.ShapeDtypeStruct((m, d), F32), jax.ShapeDtypeStruct((m, d), BF16),
                   jax.ShapeDtypeStruct((m, ne), F32)],
        scratch_shapes=[pltpu.VMEM((k, d), BF16)],
        compiler_params=_cparams("arbitrary"),
        name="matmul_gated_residual",
    )(a_p, a_s, w, resid, mod, gains2, mod, mod, router_w)


def _glures_kernel(ap_ref, as_ref, wv_ref, wg_ref, r_ref, gt_ref, g_ref, sc_ref, sh_ref, rw_ref,
                   o_ref, h_ref, aff_ref, wv_bf, wg_bf, *, npb):
    i = pl.program_id(0)

    @pl.when(i == 0)
    def _():
        wv_bf[...] = wv_ref[...].astype(BF16)
        wg_bf[...] = wg_ref[...].astype(BF16)

    def emit(a_ref):
        a = a_ref[...].astype(BF16)
        v = jnp.dot(a, wv_bf[...], preferred_element_type=F32)
        g = jnp.dot(a, wg_bf[...], preferred_element_type=F32)
        x_new = r_ref[...] + gt_ref[...] * (v * jax.nn.sigmoid(g))
        o_ref[...] = x_new
        _router_epilogue(x_new, g_ref, sc_ref, sh_ref, rw_ref, h_ref, aff_ref)

    pl.when(i < npb)(lambda: emit(ap_ref))
    pl.when(i >= npb)(lambda: emit(as_ref))


def glu_gated_residual(a_p, a_s, w_glu, wl, resid, mod, rows, layer, gains2, router_w):
    k = a_p.shape[1]
    m, d = resid.shape
    tm = rows.tm
    ne = router_w.shape[2]
    r_in, r_out = _sublayer2_specs(rows, layer, d, ne)
    return pl.pallas_call(
        functools.partial(_glures_kernel, npb=rows.prompt_blocks),
        grid=(m // tm,),
        in_specs=_two_group_specs(rows, k) + [
            pl.BlockSpec((None, k, d), lambda i: (wl, 0, 0)),
            pl.BlockSpec((None, k, d), lambda i: (wl, 0, 1)),
            pl.BlockSpec((tm, d), lambda i: (i, 0)),
            _mod_spec(rows, layer, d, 2, 0)] + r_in,
        out_specs=[pl.BlockSpec((tm, d), lambda i: (i, 0))] + r_out,
        out_shape=[jax.ShapeDtypeStruct((m, d), F32), jax.ShapeDtypeStruct((m, d), BF16),
                   jax.ShapeDtypeStruct((m, ne), F32)],
        scratch_shapes=[pltpu.VMEM((k, d), BF16), pltpu.VMEM((k, d), BF16)],
        compiler_params=_cparams("arbitrary"),
        name="glu_gated_residual",
    )(a_p, a_s, w_glu, w_glu, resid, mod, gains2, mod, mod, router_w)


def _rglru_kernel(gate_ref, u_ref, cw_ref, cb_ref, wa_ref, ba_ref, wx_ref, bx_ref, lam_ref, h0_ref,
                  y_ref, fin_ref, af_s, bf_s, ab_s, bb_s, hf_s, hb_s):
    t, cw = u_ref.shape
    u = u_ref[...]
    row = lax.broadcasted_iota(jnp.int32, (t, cw), 0)

    def shifted(x, k):
        if k > 0:
            return jnp.where(row >= k, pltpu.roll(x, k, axis=0), 0.0)
        return jnp.where(row < t + k, pltpu.roll(x, t + k, axis=0), 0.0)

    cwv = cw_ref[...]
    uc = (cwv[0:1] * shifted(u, 2) + cwv[1:2] * shifted(u, 1) + cwv[2:3] * u
          + cwv[3:4] * shifted(u, -1) + cb_ref[...])

    a_scr = (af_s, ab_s)
    b_scr = (bf_s, bb_s)
    for k in range(2):
        nl = -lam_ref[k:k + 1, :]
        sp = jnp.maximum(nl, 0.0) + jnp.log1p(jnp.exp(-jnp.abs(nl)))
        for hh in range(cw // RG_BS):
            sl = slice(hh * RG_BS, (hh + 1) * RG_BS)
            uh = uc[:, sl]
            ub = uh.astype(BF16)
            r = _sigmoid(jnp.dot(ub, wa_ref[k, hh].astype(BF16), preferred_element_type=F32) + ba_ref[k:k + 1, sl])
            i = _sigmoid(jnp.dot(ub, wx_ref[k, hh].astype(BF16), preferred_element_type=F32) + bx_ref[k:k + 1, sl])
            log_a = (-RG_C * r) * sp[:, sl]
            a = jnp.exp(log_a)
            a_scr[k][:, sl] = a
            b_scr[k][:, sl] = jnp.sqrt(jnp.tanh(-log_a) * (a * a + 1.0)) * (i * uh)

    nblk = t // SUBLANES
    srow = lax.broadcasted_iota(jnp.int32, (SUBLANES, cw), 0)

    def body(n, carry):
        cf, cb = carry
        rf = pl.multiple_of(n * SUBLANES, SUBLANES)
        rb = pl.multiple_of((nblk - 1 - n) * SUBLANES, SUBLANES)
        a = af_s[pl.ds(rf, SUBLANES), :]
        b = bf_s[pl.ds(rf, SUBLANES), :]
        a2 = ab_s[pl.ds(rb, SUBLANES), :]
        b2 = bb_s[pl.ds(rb, SUBLANES), :]
        for s in (1, 2, 4):
            m = srow >= s
            b = jnp.where(m, a * pltpu.roll(b, s, axis=0) + b, b)
            a = jnp.where(m, a * pltpu.roll(a, s, axis=0), a)
            m2 = srow < SUBLANES - s
            b2 = jnp.where(m2, a2 * pltpu.roll(b2, SUBLANES - s, axis=0) + b2, b2)
            a2 = jnp.where(m2, a2 * pltpu.roll(a2, SUBLANES - s, axis=0), a2)
        hf = a * cf + b
        hb = a2 * cb + b2
        hf_s[pl.ds(rf, SUBLANES), :] = hf
        hb_s[pl.ds(rb, SUBLANES), :] = hb
        return hf[SUBLANES - 1:SUBLANES, :], hb[0:1, :]

    cf, cb = lax.fori_loop(0, nblk, body, (h0_ref[0:1, :], h0_ref[1:2, :]))
    fin_ref[0:1, :] = cf
    fin_ref[1:2, :] = cb
    y_ref[...] = (hf_s[...] + hb_s[...]) * _gelu(gate_ref[...])


def rglru_scan(gu, row0, n_seq, t, j, conv_w, conv_b, w_a, b_a, w_x, b_x, lam, h0, h0_j, *, cw=RG_TILE):
    r = gu.shape[1] // 2
    nh = cw // RG_BS
    blk0 = row0 // t
    nc = r // cw
    scr = [pltpu.VMEM((t, cw), F32) for _ in range(6)]
    vec2 = pl.BlockSpec((None, 2, cw), lambda b, c: (j, 0, c))
    gatew = pl.BlockSpec((None, 2, nh, RG_BS, RG_BS), lambda b, c: (j, 0, c, 0, 0))
    return pl.pallas_call(
        _rglru_kernel,
        grid=(n_seq, nc),
        in_specs=[pl.BlockSpec((t, cw), lambda b, c: (blk0 + b, c)),
                  pl.BlockSpec((t, cw), lambda b, c: (blk0 + b, nc + c)),
                  pl.BlockSpec((None, 4, cw), lambda b, c: (j, 0, c)),
                  pl.BlockSpec((None, 1, cw), lambda b, c: (j, 0, c)),
                  gatew, vec2, gatew, vec2, vec2,
                  pl.BlockSpec((None, None, 2, cw), lambda b, c: (b, h0_j, 0, c))],
        out_specs=[pl.BlockSpec((t, cw), lambda b, c: (b, c)),
                   pl.BlockSpec((None, 2, cw), lambda b, c: (b, 0, c))],
        out_shape=[jax.ShapeDtypeStruct((n_seq * t, r), F32),
                   jax.ShapeDtypeStruct((n_seq, 2, r), F32)],
        scratch_shapes=scr,
        compiler_params=_cparams("arbitrary", "arbitrary"),
        name="rglru_scan",
    )(gu, gu, conv_w, conv_b.reshape(conv_b.shape[0], 1, r), w_a, b_a, w_x, b_x, lam, h0)


def _s5_kernel(*refs, bp, n_seq, nc, gpb):
    ell, gw = S5_L, S5_H
    (x_ref, tm_ref, win_ref, wre_ref, wim_ref, ar_ref, ai_ref, d_ref, s0re_ref, s0im_ref,
     o_ref, fin_ref, xg_s, ure_s, uim_s, fre_s, fim_s, bre_s, bim_s) = refs
    m, mp = n_seq * nc, bp * nc
    lanes = x_ref.shape[1]
    token = lambda l: pl.ds(l, m, stride=ell)
    per_tile = lanes // gw
    half = ure_s.shape[2] // 2
    lane_grp = lax.shift_right_logical(lax.broadcasted_iota(jnp.int32, (m, lanes), 1), gw.bit_length() - 1)

    def perm(shape, chunk_major_axis):
        i = lax.broadcasted_iota(jnp.int32, shape, chunk_major_axis)
        j = lax.broadcasted_iota(jnp.int32, shape, 1 - chunk_major_axis)
        b = i & (bp - 1)
        c = lax.shift_right_logical(i, bp.bit_length() - 1)
        return jnp.where((j == b * nc + c) & (b < n_seq), 1.0, 0.0).astype(BF16)

    to_chunk_major = perm((mp, m), 0)
    to_batch_major = perm((m, mp), 1)

    def block_transpose(v):
        k = per_tile // 2
        while k >= 1:
            low = (lane_grp & k) == 0
            nxt = list(v)
            for i in range(per_tile):
                if i & k == 0:
                    a, b = v[i], v[i + k]
                    nxt[i] = jnp.where(low, a, pltpu.roll(b, k * gw, axis=1))
                    nxt[i + k] = jnp.where(low, pltpu.roll(a, lanes - k * gw, axis=1), b)
            v = nxt
            k //= 2
        return v

    for tile in range(ell // per_tile):
        by_group = block_transpose([x_ref[token(tile * per_tile + j), :] for j in range(per_tile)])
        for g in range(gpb):
            xg_s[g, :, tile * lanes:(tile + 1) * lanes] = by_group[g]

    for g in range(gpb):
        xg = xg_s[g]
        xp = jnp.dot(to_chunk_major, xg.astype(BF16), preferred_element_type=F32).astype(BF16)
        u = jnp.dot(xp, win_ref[g].astype(BF16), preferred_element_type=F32)
        ure_s[g] = u[:, :2 * half]
        uim_s[g] = u[:, 2 * half:]

    is_fwd = lax.broadcasted_iota(jnp.int32, (bp, 2 * half), 1) < half
    ar = [ar_ref[g] for g in range(gpb)]
    ai = [ai_ref[g] for g in range(gpb)]

    def body(k, carry):
        rf = pl.multiple_of(k * bp, bp)
        rb = pl.multiple_of((nc - 1 - k) * bp, bp)
        out = []
        for g in range(gpb):
            re, im = carry[2 * g], carry[2 * g + 1]
            fre_s[g, pl.ds(rf, bp), :] = re
            fim_s[g, pl.ds(rf, bp), :] = im
            bre_s[g, pl.ds(rb, bp), :] = re
            bim_s[g, pl.ds(rb, bp), :] = im
            ure = jnp.where(is_fwd, ure_s[g, pl.ds(rf, bp), :], ure_s[g, pl.ds(rb, bp), :])
            uim = jnp.where(is_fwd, uim_s[g, pl.ds(rf, bp), :], uim_s[g, pl.ds(rb, bp), :])
            out += [ar[g] * re - ai[g] * im + ure, ar[g] * im + ai[g] * re + uim]
        return tuple(out)

    init = tuple(r[g] for g in range(gpb) for r in (s0re_ref, s0im_ref))
    fin = lax.fori_loop(0, nc, body, init)
    fwd_all = lax.broadcasted_iota(jnp.int32, (mp, 2 * half), 1) < half
    for g in range(gpb):
        fin_ref[g, :, :2 * half] = fin[2 * g]
        fin_ref[g, :, 2 * half:] = fin[2 * g + 1]
        hre = jnp.where(fwd_all, fre_s[g], bre_s[g]).astype(BF16)
        him = jnp.where(fwd_all, fim_s[g], bim_s[g]).astype(BF16)
        hre = jnp.dot(to_batch_major, hre, preferred_element_type=F32).astype(BF16)
        him = jnp.dot(to_batch_major, him, preferred_element_type=F32).astype(BF16)
        xg = xg_s[g]
        y = (jnp.dot(xg.astype(BF16), tm_ref[g].astype(BF16), preferred_element_type=F32)
             + jnp.dot(hre, wre_ref[g].astype(BF16), preferred_element_type=F32)
             + jnp.dot(him, wim_ref[g].astype(BF16), preferred_element_type=F32)
             + d_ref[g] * xg)
        xg_s[g] = _gelu(y)

    for tile in range(ell // per_tile):
        by_token = block_transpose([xg_s[g, :, tile * lanes:(tile + 1) * lanes] for g in range(gpb)])
        for j in range(per_tile):
            o_ref[token(tile * per_tile + j), :] = by_token[j]


def s5_chunked(hn, row_blk, n_seq, nc, mats, s0re, s0im, *, bp):
    tmat, win, wre, wim, ar, ai, dg = mats
    g, w, _ = tmat.shape
    p2 = ar.shape[-1]
    ell = S5_L
    lanes = 128
    gpb = lanes // S5_H
    d = hn.shape[1]
    m, mp = n_seq * nc, bp * nc
    n = m * ell
    assert bp & (bp - 1) == 0 and S5_H & (S5_H - 1) == 0 and g % gpb == 0
    blk = lambda shape: pl.BlockSpec((gpb,) + shape, lambda i: (i, 0, 0))
    return pl.pallas_call(
        functools.partial(_s5_kernel, bp=bp, n_seq=n_seq, nc=nc, gpb=gpb),
        grid=(g // gpb,),
        in_specs=[pl.BlockSpec((n, lanes), lambda i: (row_blk, i)),
                  blk((w, w)), blk((w, 2 * p2)), blk((p2, w)), blk((p2, w)),
                  blk((1, p2)), blk((1, p2)), blk((1, w)), blk((bp, p2)), blk((bp, p2))],
        out_specs=[pl.BlockSpec((n, lanes), lambda i: (0, i)), blk((bp, 2 * p2))],
        out_shape=[jax.ShapeDtypeStruct((n, d), F32), jax.ShapeDtypeStruct((g, bp, 2 * p2), F32)],
        scratch_shapes=[pltpu.VMEM((gpb, m, w), F32)] + [pltpu.VMEM((gpb, mp, p2), F32) for _ in range(6)],
        compiler_params=_cparams("arbitrary"),
        name="s5_chunked",
    )(hn, tmat, win, wre, wim, ar, ai, dg, s0re, s0im)


def _cmul(ar, ai, br, bi):
    return ar * br - ai * bi, ar * bi + ai * br


def _s5_prep_kernel(are_ref, aim_ref, ldt_ref, btr_ref, bti_ref, cr_ref, ci_ref,
                    tm_ref, win_ref, wre_ref, wim_ref, ar_ref, ai_ref):
    for gi in range(cr_ref.shape[0]):
        _s5_prep_group(*(r.at[gi] for r in (are_ref, aim_ref, ldt_ref, btr_ref, bti_ref, cr_ref, ci_ref,
                                           tm_ref, win_ref, wre_ref, wim_ref, ar_ref, ai_ref)))


def _s5_prep_group(are_ref, aim_ref, ldt_ref, btr_ref, bti_ref, cr_ref, ci_ref,
                   tm_ref, win_ref, wre_ref, wim_ref, ar_ref, ai_ref):
    ell = S5_L
    h, p2 = cr_ref.shape
    w = ell * h
    a_re, a_im = are_ref[...], aim_ref[...]
    dt = jnp.exp(ldt_ref[...])
    steps = lax.broadcasted_iota(jnp.int32, (3 * SUBLANES, p2), 0).astype(F32)
    mag = jnp.exp(steps * (a_re * dt))
    ang = steps * (a_im * dt)
    pw_r, pw_i = mag * jnp.cos(ang), mag * jnp.sin(ang)
    nr, ni = pw_r[1:2] - 1.0, pw_i[1:2]
    den = a_re * a_re + a_im * a_im
    qr, qi = (nr * a_re + ni * a_im) / den, (ni * a_re - nr * a_im) / den
    bb_r, bb_i = _cmul(qr, qi, btr_ref[...], bti_ref[...])
    c_r, c_i = cr_ref[...], ci_ref[...]
    fwd = lax.broadcasted_iota(jnp.int32, (1, p2), 1) < p2 // 2

    def power_rows(m_fwd, m_bwd):
        return (jnp.where(fwd, pw_r[m_fwd:m_fwd + 1], pw_r[m_bwd:m_bwd + 1]),
                jnp.where(fwd, pw_i[m_fwd:m_fwd + 1], pw_i[m_bwd:m_bwd + 1]))

    def stack(x_r, x_i, powers):
        parts = [_cmul(x_r, x_i, *power_rows(*powers(l))) for l in range(ell)]
        return (jnp.concatenate([q[0] for q in parts], axis=0), jnp.concatenate([q[1] for q in parts], axis=0))

    win_r, win_i = stack(bb_r, bb_i, lambda l: (ell - 1 - l, l))
    win_ref[...] = jnp.concatenate([win_r, win_i], axis=1).astype(win_ref.dtype)
    z_r, z_i = stack(c_r, c_i, lambda l: (l + 1, ell - l))
    wre_ref[...] = z_r.T.astype(wre_ref.dtype)
    wim_ref[...] = (-z_i).T.astype(wim_ref.dtype)
    k_r, k_i = stack(c_r, c_i, lambda m: (m, ell - 1 - m))
    mask_f = jnp.where(fwd, 1.0, 0.0)

    def lag_rows(mask):
        return (lax.dot_general(bb_r * mask, k_r, NT_DIMS, precision=HIGHEST, preferred_element_type=F32)
                - lax.dot_general(bb_i * mask, k_i, NT_DIMS, precision=HIGHEST, preferred_element_type=F32))

    kf = lag_rows(mask_f)
    kb = lag_rows(1.0 - mask_f)
    lane = lax.broadcasted_iota(jnp.int32, (h, w), 1)
    blocks = []
    for li in range(ell):
        f_part = kf if li == 0 else pltpu.roll(kf, li * h, axis=1)
        s_b = (w - (ell - 1 - li) * h) % w
        b_part = kb if s_b == 0 else pltpu.roll(kb, s_b, axis=1)
        blocks.append(jnp.where(lane >= li * h, f_part, 0.0) + jnp.where(lane < (li + 1) * h, b_part, 0.0))
    tm_ref[...] = jnp.concatenate(blocks, axis=0).astype(tm_ref.dtype)
    ar_ref[...] = pw_r[ell:ell + 1]
    ai_ref[...] = pw_i[ell:ell + 1]


def s5_chunk_operators(a_re, a_im, log_dt, b_re, b_im, c_re, c_im, d):
    _, g, p = a_re.shape
    h = b_re.shape[-1]
    w = S5_L * h
    two_dir = lambda x: jnp.transpose(x, (1, 0, 2)).reshape(g, 1, 2 * p)
    ldt = two_dir(jnp.broadcast_to(log_dt[:, :, None], (2, g, p)))
    bt = lambda x: jnp.transpose(x, (1, 3, 0, 2)).reshape(g, h, 2 * p)
    ct = lambda x: jnp.transpose(x, (1, 2, 0, 3)).reshape(g, h, 2 * p)
    gps = math.gcd(g, S5_PREP_GROUPS)
    per_g = lambda shape: pl.BlockSpec((gps,) + shape, lambda i: (i, 0, 0))
    tmat, win, wre, wim, ar, ai = pl.pallas_call(
        _s5_prep_kernel,
        grid=(g // gps,),
        in_specs=[per_g((1, 2 * p))] * 3 + [per_g((h, 2 * p))] * 4,
        out_specs=[per_g((w, w)), per_g((w, 4 * p)), per_g((2 * p, w)), per_g((2 * p, w)),
                   per_g((1, 2 * p)), per_g((1, 2 * p))],
        out_shape=[jax.ShapeDtypeStruct((g, w, w), BF16), jax.ShapeDtypeStruct((g, w, 4 * p), BF16),
                   jax.ShapeDtypeStruct((g, 2 * p, w), BF16), jax.ShapeDtypeStruct((g, 2 * p, w), BF16),
                   jax.ShapeDtypeStruct((g, 1, 2 * p), F32), jax.ShapeDtypeStruct((g, 1, 2 * p), F32)],
        compiler_params=_cparams("arbitrary"),
        name="s5_chunk_operators",
    )(two_dir(a_re), two_dir(a_im), ldt, bt(b_re), bt(b_im), ct(c_re), ct(c_im))
    dg = jnp.tile(d.reshape(g, 1, h), (1, S5_L, 1)).reshape(g, 1, w)
    return tmat, win, wre, wim, ar, ai, dg


def s5_mixer_group(hn, row0, n_seq, t, mats, s0):
    ar = mats[4]
    g = ar.shape[0]
    p = ar.shape[-1] // 2
    nc = t // S5_L
    bp = -(-n_seq // SUBLANES) * SUBLANES
    n = n_seq * t
    assert row0 % n == 0
    if s0 is None:
        s0re = jnp.zeros((g, bp, 2 * p), F32)
        s0im = s0re
    else:
        st = jnp.transpose(s0, (3, 0, 2, 1, 4)).reshape(g, n_seq, 2, 2 * p)
        st = jnp.pad(st, ((0, 0), (0, bp - n_seq), (0, 0), (0, 0)))
        s0re, s0im = st[:, :, 0], st[:, :, 1]
    u, fin = s5_chunked(hn, row0 // n, n_seq, nc, mats, s0re, s0im, bp=bp)
    fin = fin.reshape(g, bp, 2, 2, p)[:, :n_seq]
    return u, jnp.transpose(fin, (1, 3, 2, 0, 4))


def _softmax_pv(scores, values, sink):
    m = sink
    for s in scores:
        m = jnp.maximum(m, jnp.max(s, axis=-1, keepdims=True))
    den = jnp.exp(sink - m)
    acc = None
    for s, v in zip(scores, values):
        p = jnp.exp(s - m)
        den = den + jnp.sum(p, axis=-1, keepdims=True)
        pv = jnp.dot(p.astype(BF16), v, preferred_element_type=F32)
        acc = pv if acc is None else acc + pv
    return acc / den


def _attn_prompt_kernel(sink_ref, q_ref, k_ref, v_ref, o_ref):
    k = k_ref[...].astype(BF16)
    v = v_ref[...].astype(BF16)
    for h in range(N_KV):
        hs = slice(h * HEAD_DIM, (h + 1) * HEAD_DIM)
        kh, vh = k[:, hs], v[:, hs]
        for g in range(Q_PER_KV):
            c0 = (h * Q_PER_KV + g) * HEAD_DIM
            qg = (q_ref[:, c0:c0 + HEAD_DIM] * ATT_SCALE).astype(BF16)
            s = lax.dot_general(qg, kh, NT_DIMS, preferred_element_type=F32)
            o_ref[:, c0:c0 + HEAD_DIM] = _softmax_pv([s], [vh], sink_ref[h * Q_PER_KV + g])


def attn_prompt(qkv, sink, n_seq, t):
    dq = N_HEADS * HEAD_DIM
    kcol = dq // KV_W
    return pl.pallas_call(
        _attn_prompt_kernel,
        grid=(n_seq,),
        in_specs=[pl.BlockSpec(memory_space=pltpu.SMEM),
                  pl.BlockSpec((t, dq), lambda b: (b, 0)),
                  pl.BlockSpec((t, KV_W), lambda b: (b, kcol)),
                  pl.BlockSpec((t, KV_W), lambda b: (b, kcol + 1))],
        out_specs=pl.BlockSpec((t, dq), lambda b: (b, 0)),
        out_shape=jax.ShapeDtypeStruct((n_seq * t, dq), F32),
        compiler_params=_cparams("arbitrary"),
        name="attn_prompt",
    )(sink, qkv, qkv, qkv)


def _rope(x, cos, sin):
    w = x.shape[1]
    low = (lax.broadcasted_iota(jnp.int32, x.shape, 1) & (HEAD_DIM // 4)) == 0
    partner = jnp.where(low, pltpu.roll(x, w - HEAD_DIM // 4, axis=1), pltpu.roll(x, HEAD_DIM // 4, axis=1))
    return x * cos + partner * sin


def _attn_sample_kernel(sink_ref, q_ref, k_ref, v_ref, kc_ref, vc_ref, cos_ref, sin_ref, o_ref,
                        kw_s, vw_s, kc_s, vc_s, *, t):
    n = pl.program_id(1)
    blk = ATT_BLOCK

    @pl.when(n == 0)
    def _():
        zeros = jnp.zeros((blk, KV_W), BF16)
        kw_s[0:blk, :] = zeros
        vw_s[0:blk, :] = zeros
        kw_s[blk + t:2 * blk + t, :] = zeros
        vw_s[blk + t:2 * blk + t, :] = zeros
        kw_s[blk:blk + t, :] = _rope(k_ref[...], cos_ref[...], sin_ref[...]).astype(BF16)
        vw_s[blk:blk + t, :] = v_ref[...].astype(BF16)
        kc_s[...] = kc_ref[...].astype(BF16)
        vc_s[...] = vc_ref[...].astype(BF16)

    r0 = pl.multiple_of(n * blk, blk)
    cq = cos_ref[pl.ds(r0, blk), :]
    sq = sin_ref[pl.ds(r0, blk), :]
    kw = kw_s[pl.ds(r0, 3 * blk), :]
    vw = vw_s[pl.ds(r0, 3 * blk), :]
    qi = lax.broadcasted_iota(jnp.int32, (blk, 3 * blk), 0)
    kj = lax.broadcasted_iota(jnp.int32, (blk, 3 * blk), 1)
    kpos = n * blk - blk + kj
    valid = (jnp.abs(kj - blk - qi) <= WINDOW) & (kpos >= 0) & (kpos < t)
    for h in range(N_KV):
        hs = slice(h * HEAD_DIM, (h + 1) * HEAD_DIM)
        qh = (_rope(q_ref[:, h * KV_W:(h + 1) * KV_W], cq, sq) * ATT_SCALE).astype(BF16)
        kh, vh, kch, vch = kw[:, hs], vw[:, hs], kc_s[:, hs], vc_s[:, hs]
        for g in range(Q_PER_KV):
            qg = qh[:, g * HEAD_DIM:(g + 1) * HEAD_DIM]
            s_loc = lax.dot_general(qg, kh, NT_DIMS, preferred_element_type=F32)
            s_loc = jnp.where(valid, s_loc, NEG_INF)
            s_ctx = lax.dot_general(qg, kch, NT_DIMS, preferred_element_type=F32)
            c0 = (h * Q_PER_KV + g) * HEAD_DIM
            o_ref[:, c0:c0 + HEAD_DIM] = _softmax_pv([s_loc, s_ctx], [vh, vch], sink_ref[h * Q_PER_KV + g])


def _rope_tables(t):
    quarter = HEAD_DIM // 4
    freqs = ROPE_BASE ** (-jnp.arange(quarter, dtype=F32) / quarter)
    pos = jnp.arange(t)
    ang_r = (pos // GRID_W).astype(F32)[:, None] * freqs
    ang_c = (pos % GRID_W).astype(F32)[:, None] * freqs
    cos = jnp.concatenate([jnp.cos(ang_r), jnp.cos(ang_r), jnp.cos(ang_c), jnp.cos(ang_c)], axis=-1)
    sin = jnp.concatenate([-jnp.sin(ang_r), jnp.sin(ang_r), -jnp.sin(ang_c), jnp.sin(ang_c)], axis=-1)
    return jnp.tile(cos, (1, N_KV)), jnp.tile(sin, (1, N_KV))


def attn_sample(qkv, row0, sink, n_seq, t, k_ctx, v_ctx):
    dq = N_HEADS * HEAD_DIM
    kcol = dq // KV_W
    nb = t // ATT_BLOCK
    lc = k_ctx.shape[1]
    cos, sin = _rope_tables(t)
    qblk0, sblk0 = row0 // ATT_BLOCK, row0 // t
    return pl.pallas_call(
        functools.partial(_attn_sample_kernel, t=t),
        grid=(n_seq, nb),
        in_specs=[pl.BlockSpec(memory_space=pltpu.SMEM),
                  pl.BlockSpec((ATT_BLOCK, dq), lambda b, n: (qblk0 + b * nb + n, 0)),
                  pl.BlockSpec((t, KV_W), lambda b, n: (sblk0 + b, kcol)),
                  pl.BlockSpec((t, KV_W), lambda b, n: (sblk0 + b, kcol + 1)),
                  pl.BlockSpec((None, lc, KV_W), lambda b, n: (b, 0, 0)),
                  pl.BlockSpec((None, lc, KV_W), lambda b, n: (b, 0, 0)),
                  pl.BlockSpec((t, KV_W), lambda b, n: (0, 0)),
                  pl.BlockSpec((t, KV_W), lambda b, n: (0, 0))],
        out_specs=pl.BlockSpec((ATT_BLOCK, dq), lambda b, n: (b * nb + n, 0)),
        out_shape=jax.ShapeDtypeStruct((n_seq * t, dq), F32),
        scratch_shapes=[pltpu.VMEM((t + 2 * ATT_BLOCK, KV_W), BF16), pltpu.VMEM((t + 2 * ATT_BLOCK, KV_W), BF16),
                        pltpu.VMEM((lc, KV_W), BF16), pltpu.VMEM((lc, KV_W), BF16)],
        compiler_params=_cparams("arbitrary", "arbitrary"),
        name="attn_sample",
    )(sink, qkv, qkv, qkv, k_ctx, v_ctx, cos, sin)


def _moe_kernel(x_ref, wg_ref, wu_ref, wd_ref, gt_ref, seg_ref, g2_ref, o_ref, wg_bf, wu_bf, wd_bf, *, n_seg):
    wg_bf[...] = wg_ref[...].astype(BF16)
    wu_bf[...] = wu_ref[...].astype(BF16)
    wd_bf[...] = wd_ref[...].astype(BF16)
    rsub = min(MOE_ROW_SUB, x_ref.shape[0])
    for r in range(x_ref.shape[0] // rsub):
        rs = slice(r * rsub, (r + 1) * rsub)
        x = x_ref[rs, :]
        hg = jnp.dot(x, wg_bf[...], preferred_element_type=F32)
        hu = jnp.dot(x, wu_bf[...], preferred_element_type=F32)
        he = ((hg * jax.nn.sigmoid(hg)) * hu).astype(BF16)
        y = jnp.dot(he, wd_bf[...], preferred_element_type=F32)
        seg = seg_ref[rs, :]
        g2 = jnp.zeros(y.shape, F32)
        for s in range(n_seg):
            g2 = jnp.where(seg == s, g2_ref[s:s + 1, :], g2)
        o_ref[rs, :] = y * (gt_ref[rs, :] * g2)


def moe_experts(xe, w_gate, w_up, w_down, layer, gates, seg, mod, n_seg):
    ne, r, d = xe.shape
    dff = w_gate.shape[3]
    return pl.pallas_call(
        functools.partial(_moe_kernel, n_seg=n_seg),
        grid=(ne,),
        in_specs=[pl.BlockSpec((None, r, d), lambda e: (e, 0, 0)),
                  pl.BlockSpec((None, None, d, dff), lambda e: (layer, e, 0, 0)),
                  pl.BlockSpec((None, None, d, dff), lambda e: (layer, e, 0, 0)),
                  pl.BlockSpec((None, None, dff, d), lambda e: (layer, e, 0, 0)),
                  pl.BlockSpec((None, r, 1), lambda e: (e, 0, 0)),
                  pl.BlockSpec((None, r, 1), lambda e: (e, 0, 0)),
                  pl.BlockSpec((SUBLANES, None, d), lambda e: (layer, 0, 5))],
        out_specs=pl.BlockSpec((None, r, d), lambda e: (e, 0, 0)),
        out_shape=jax.ShapeDtypeStruct((ne, r, d), F32),
        scratch_shapes=[pltpu.VMEM((d, dff), BF16), pltpu.VMEM((d, dff), BF16), pltpu.VMEM((dff, d), BF16)],
        compiler_params=_cparams("arbitrary"),
        name="moe_experts",
    )(xe, w_gate, w_up, w_down, gates, seg, mod)


def _expert_choice(aff, rows):
    ne = aff.shape[1]
    sizes = (rows.n_prompt, rows.n_sample)
    caps = [(EC_FACTOR * n) // ne for n in sizes]
    if sizes[0] == sizes[1]:
        n = sizes[0]
        gt, ix = lax.top_k(jnp.swapaxes(aff.reshape(2, n, ne), 1, 2), caps[0])
        ix = ix + jnp.array([0, n], jnp.int32)[:, None, None]
        return (jnp.concatenate([gt[0], gt[1]], axis=1), jnp.concatenate([ix[0], ix[1]], axis=1))
    gts, ixs, off = [], [], 0
    for n, cap in zip(sizes, caps):
        gt, ix = lax.top_k(aff[off:off + n].T, cap)
        gts.append(gt)
        ixs.append(ix + off)
        off += n
    return jnp.concatenate(gts, axis=1), jnp.concatenate(ixs, axis=1)


def _combine_kernel(idx_ref, x_hbm, ye_ref, o_hbm, acc, sem, *, n_grp, rows_per):
    g = pl.program_id(0)
    e = pl.program_id(1)
    last = pl.num_programs(1) - 1

    def load(grp):
        return pltpu.make_async_copy(x_hbm.at[pl.ds(grp * n_grp, n_grp), :], acc.at[grp], sem.at[grp])

    def store(grp):
        return pltpu.make_async_copy(acc.at[grp], o_hbm.at[pl.ds(grp * n_grp, n_grp), :], sem.at[2 + grp])

    @pl.when((g == 0) & (e == 0))
    def _():
        load(0).start()
        load(1).start()

    @pl.when(e == 0)
    def _():
        load(g).wait()

    base = (2 * e + g) * rows_per
    first = g * n_grp

    def body(i, carry):
        toks = [idx_ref[base + i * COMBINE_UNROLL + k] - first for k in range(COMBINE_UNROLL)]
        old = [acc[g, pl.ds(t, 1), :] for t in toks]
        add = [ye_ref[pl.ds(i * COMBINE_UNROLL + k, 1), :] for k in range(COMBINE_UNROLL)]
        for t, a, b in zip(toks, old, add):
            acc[g, pl.ds(t, 1), :] = a + b
        return carry

    lax.fori_loop(0, rows_per // COMBINE_UNROLL, body, 0)

    @pl.when(e == last)
    def _():
        store(g).start()

    @pl.when((g == 1) & (e == last))
    def _():
        store(0).wait()
        store(1).wait()


def moe_combine(x, ye, idx, n_grp):
    m, d = x.shape
    ne, r, _ = ye.shape
    rows_per = r // 2
    assert m == 2 * n_grp and rows_per % COMBINE_UNROLL == 0
    grid_spec = pltpu.PrefetchScalarGridSpec(
        num_scalar_prefetch=1,
        grid=(2, ne),
        in_specs=[pl.BlockSpec(memory_space=pl.ANY),
                  pl.BlockSpec((None, rows_per, d), lambda g, e, ix: (e, g, 0))],
        out_specs=pl.BlockSpec(memory_space=pl.ANY),
        scratch_shapes=[pltpu.VMEM((2, n_grp, d), F32), pltpu.SemaphoreType.DMA((4,))])
    return pl.pallas_call(
        functools.partial(_combine_kernel, n_grp=n_grp, rows_per=rows_per),
        grid_spec=grid_spec,
        out_shape=jax.ShapeDtypeStruct((m, d), F32),
        compiler_params=_cparams("arbitrary", "arbitrary"),
        name="moe_combine",
    )(idx.reshape(-1), x, ye)


def moe_layer(x, h2, aff, mod, rows, layer, w_gate, w_up, w_down, n_seg):
    m, d = x.shape
    gates, idx = _expert_choice(aff, rows)
    seg = jnp.where(idx < rows.n_prompt, 0, 1 + (idx - rows.n_prompt) // rows.t_sample)
    xe = h2[idx]
    ye = moe_experts(xe, w_gate, w_up, w_down, layer, gates[..., None], seg[..., None], mod, n_seg)
    if rows.n_prompt == rows.n_sample:
        return moe_combine(x, ye, idx, rows.n_prompt)
    return x.at[idx.reshape(-1)].add(ye.reshape(-1, d))


def kernel(x_prompt, x_sample, state_rglru, state_s5, cache_k, cache_v, c, c_ctx, ada_w, ada_b, norm1_g, norm2_g, rg_w_in, rg_conv_w, rg_conv_b, rg_w_a, rg_b_a, rg_w_x, rg_b_x, rg_lambda, rg_w_out, s5_a_re, s5_a_im, s5_log_dt, s5_b_re, s5_b_im, s5_c_re, s5_c_im, s5_d, s5_w_glu, attn_w_qkv, attn_w_o, attn_sink, router_w, moe_w_gate, moe_w_up, moe_w_down, final_norm_g):
    bp_, tp, d = x_prompt.shape
    bs, ts, _ = x_sample.shape
    n_p, n_s = bp_ * tp, bs * ts
    depth = ada_w.shape[0]
    rows = _Rows(n_p, n_s, ts)
    assert bs + 1 <= SUBLANES and n_p % ts == 0

    x = jnp.concatenate([x_prompt.reshape(n_p, d), x_sample.reshape(n_s, d)], axis=0)
    cond = jnp.concatenate([c_ctx[None, :], c, jnp.zeros((SUBLANES - 1 - bs, d), F32)], axis=0)
    mod_all = ada_modulation_all(cond, ada_w, ada_b)
    mod = mod_all.reshape(depth * SUBLANES, 1, 6 * d)
    g1 = norm1_g.reshape(depth, 1, d)
    g2 = norm2_g.reshape(depth, 1, d)

    new_rg, new_s5, new_k, new_v = [], [], [], []
    for l in range(depth):
        kind, j = l % 3, l // 3
        if kind == 0:
            gu = norm_mod_matmul(x, g1, mod, rows, l, rg_w_in, j)
            args = (j, rg_conv_w, rg_conv_b, rg_w_a, rg_b_a, rg_w_x, rg_b_x, rg_lambda)
            r = gu.shape[1] // 2
            yp, fin = rglru_scan(gu, 0, bp_, tp, *args, jnp.zeros((bp_, 1, 2, r), F32), 0)
            ys, _ = rglru_scan(gu, n_p, bs, ts, *args, state_rglru, j)
            new_rg.append(fin)
            x, h2, aff = matmul_gated_residual(yp, ys, rg_w_out, j, x, mod, rows, l, g2, router_w)
        elif kind == 1:
            hn = norm_mod(x, g1, mod, rows, l)
            mats = s5_chunk_operators(s5_a_re[j], s5_a_im[j], s5_log_dt[j], s5_b_re[j], s5_b_im[j],
                                      s5_c_re[j], s5_c_im[j], s5_d[j])
            up, st = s5_mixer_group(hn, 0, bp_, tp, mats, None)
            us, _ = s5_mixer_group(hn, n_p, bs, ts, mats, state_s5[:, j])
            new_s5.append(st)
            x, h2, aff = glu_gated_residual(up, us, s5_w_glu, j, x, mod, rows, l, g2, router_w)
        else:
            qkv = norm_mod_matmul(x, g1, mod, rows, l, attn_w_qkv, j)
            dq = N_HEADS * HEAD_DIM
            new_k.append(qkv[:n_p, dq:dq + KV_W].reshape(bp_, tp, N_KV, HEAD_DIM))
            new_v.append(qkv[:n_p, dq + KV_W:].reshape(bp_, tp, N_KV, HEAD_DIM))
            op = attn_prompt(qkv, attn_sink[j], bp_, tp)
            lc = cache_k.shape[2]
            os_ = attn_sample(qkv, n_p, attn_sink[j], bs, ts,
                              cache_k[:, j].reshape(bs, lc, KV_W), cache_v[:, j].reshape(bs, lc, KV_W))
            x, h2, aff = matmul_gated_residual(op, os_, attn_w_o, j, x, mod, rows, l, g2, router_w)
        x = moe_layer(x, h2, aff, mod, rows, l, moe_w_gate, moe_w_up, moe_w_down, bs + 1)

    y_p, y_s = final_norm(x, final_norm_g, rows)
    return (y_p.reshape(bp_, tp, d), y_s.reshape(bs, ts, d),
            jnp.stack(new_rg, axis=1), jnp.stack(new_s5, axis=1),
            jnp.stack(new_k, axis=1), jnp.stack(new_v, axis=1))
```

```python
import functools
import math

import jax
import jax.numpy as jnp
from jax import lax
from jax.experimental import pallas as pl
from jax.experimental.pallas import tpu as pltpu

F32 = jnp.float32
BF16 = jnp.bfloat16
HIGHEST = lax.Precision.HIGHEST

EPS = 1e-6
RG_C = 8.0
RG_BS = 128
RG_TILE = 512
S5_H = 16
S5_L = 16
S5_PREP_GROUPS = 4
N_HEADS = 16
N_KV = 4
Q_PER_KV = N_HEADS // N_KV
HEAD_DIM = 64
KV_W = N_KV * HEAD_DIM
GRID_W = 64
WINDOW = 128
ATT_BLOCK = 128
ROPE_BASE = 10000.0
ATT_SCALE = HEAD_DIM ** -0.5
assert math.frexp(ATT_SCALE)[0] == 0.5
NEG_INF = -1e30
N_EXPERTS = 16
EC_FACTOR = 2
SUBLANES = 8
ROW_TILE = 512
MOE_ROW_SUB = 256
COMBINE_UNROLL = 8
VMEM_LIMIT = 56 * 1024 * 1024
NT_DIMS = (((1,), (1,)), ((), ()))


def _cparams(*sem):
    return pltpu.CompilerParams(dimension_semantics=sem, vmem_limit_bytes=VMEM_LIMIT)


def _gelu(x):
    return x * (0.5 * (1.0 + jnp.tanh(math.sqrt(2.0 / math.pi) * (x + 0.044715 * (x * x * x)))))


def _sigmoid(x):
    return 0.5 * jnp.tanh(0.5 * x) + 0.5


def _norm_mod(x, g, sc, sh):
    ms = jnp.mean(x * x, axis=-1, keepdims=True)
    return ((x * lax.rsqrt(ms + EPS)) * g) * (1.0 + sc) + sh


def _mod_kernel(c_ref, w_ref, b_ref, o_ref):
    c = c_ref[...]
    s = (c * jax.nn.sigmoid(c)).astype(BF16)
    o_ref[...] = jnp.dot(s, w_ref[...].astype(BF16), preferred_element_type=F32) + b_ref[...]


def ada_modulation_all(cond, ada_w, ada_b):
    n_layers, d, n = ada_w.shape
    tn = 1536
    return pl.pallas_call(
        _mod_kernel,
        grid=(n_layers, n // tn),
        in_specs=[pl.BlockSpec((SUBLANES, d), lambda l, j: (0, 0)),
                  pl.BlockSpec((None, d, tn), lambda l, j: (l, 0, j)),
                  pl.BlockSpec((None, 1, tn), lambda l, j: (l, 0, j))],
        out_specs=pl.BlockSpec((None, SUBLANES, tn), lambda l, j: (l, 0, j)),
        out_shape=jax.ShapeDtypeStruct((n_layers, SUBLANES, n), F32),
        compiler_params=_cparams("arbitrary", "arbitrary"),
        name="ada_mod",
    )(cond, ada_w, ada_b.reshape(n_layers, 1, n))


class _Rows:
    def __init__(self, n_prompt, n_sample, t_sample):
        self.n_prompt = n_prompt
        self.n_sample = n_sample
        self.t_sample = t_sample
        self.tm = min(ROW_TILE, n_prompt, t_sample)
        assert n_prompt % self.tm == 0 and t_sample % self.tm == 0
        self.prompt_blocks = n_prompt // self.tm
        self.sample_blocks = n_sample // self.tm

    def seg(self, i):
        r = i * self.tm
        return jnp.where(r < self.n_prompt, 0, 1 + lax.div(r - self.n_prompt, self.t_sample))


def _mod_spec(rows, layer, width, chunk, m_axis):
    def imap(*ids):
        return (layer * SUBLANES + rows.seg(ids[m_axis]), 0, chunk)
    return pl.BlockSpec((None, 1, width), imap)


def _gain_spec(layer, d):
    return pl.BlockSpec((None, 1, d), lambda *ids: (layer, 0, 0))


def _nm_kernel(x_ref, g_ref, sc_ref, sh_ref, w_ref, o_ref, wbf_ref):
    @pl.when(pl.program_id(1) == 0)
    def _():
        wbf_ref[...] = w_ref[...].astype(BF16)
    h = _norm_mod(x_ref[...], g_ref[...], sc_ref[...], sh_ref[...])
    o_ref[...] = jnp.dot(h.astype(BF16), wbf_ref[...], preferred_element_type=F32)


def norm_mod_matmul(x, gains, mod, rows, layer, w, wl):
    m, d = x.shape
    n = w.shape[2]
    tm = rows.tm
    tn = n
    return pl.pallas_call(
        _nm_kernel,
        grid=(n // tn, m // tm),
        in_specs=[pl.BlockSpec((tm, d), lambda j, i: (i, 0)),
                  _gain_spec(layer, d),
                  _mod_spec(rows, layer, d, 1, 1),
                  _mod_spec(rows, layer, d, 0, 1),
                  pl.BlockSpec((None, d, tn), lambda j, i: (wl, 0, j))],
        out_specs=pl.BlockSpec((tm, tn), lambda j, i: (i, j)),
        out_shape=jax.ShapeDtypeStruct((m, n), F32),
        scratch_shapes=[pltpu.VMEM((d, tn), BF16)],
        compiler_params=_cparams("arbitrary", "arbitrary"),
        name="norm_mod_matmul",
    )(x, gains, mod, mod, w)


def _norm_only_kernel(x_ref, g_ref, sc_ref, sh_ref, o_ref):
    o_ref[...] = _norm_mod(x_ref[...], g_ref[...], sc_ref[...], sh_ref[...])


def norm_mod(x, gains, mod, rows, layer):
    m, d = x.shape
    tm = rows.tm
    return pl.pallas_call(
        _norm_only_kernel,
        grid=(m // tm,),
        in_specs=[pl.BlockSpec((tm, d), lambda i: (i, 0)),
                  _gain_spec(layer, d),
                  _mod_spec(rows, layer, d, 1, 0),
                  _mod_spec(rows, layer, d, 0, 0)],
        out_specs=pl.BlockSpec((tm, d), lambda i: (i, 0)),
        out_shape=jax.ShapeDtypeStruct((m, d), F32),
        compiler_params=_cparams("arbitrary"),
        name="norm_mod",
    )(x, gains, mod, mod)


def _final_norm_kernel(x_ref, g_ref, op_ref, os_ref, *, npb):
    i = pl.program_id(0)
    x = x_ref[...]
    ms = jnp.mean(x * x, axis=-1, keepdims=True)
    y = (x * lax.rsqrt(ms + EPS)) * g_ref[...]

    @pl.when(i < npb)
    def _():
        op_ref[...] = y

    @pl.when(i >= npb)
    def _():
        os_ref[...] = y


def final_norm(x, g, rows):
    m, d = x.shape
    tm, npb, nsb = rows.tm, rows.prompt_blocks, rows.sample_blocks
    return pl.pallas_call(
        functools.partial(_final_norm_kernel, npb=npb),
        grid=(m // tm,),
        in_specs=[pl.BlockSpec((tm, d), lambda i: (i, 0)),
                  pl.BlockSpec((1, d), lambda i: (0, 0))],
        out_specs=[pl.BlockSpec((tm, d), lambda i: (jnp.minimum(i, npb - 1), 0)),
                   pl.BlockSpec((tm, d), lambda i: (jnp.clip(i - npb, 0, nsb - 1), 0))],
        out_shape=[jax.ShapeDtypeStruct((rows.n_prompt, d), F32), jax.ShapeDtypeStruct((rows.n_sample, d), F32)],
        compiler_params=_cparams("arbitrary"),
        name="final_norm",
    )(x, g.reshape(1, d))


def _router_epilogue(x_new, g_ref, sc_ref, sh_ref, rw_ref, h_ref, aff_ref):
    h = _norm_mod(x_new, g_ref[...], sc_ref[...], sh_ref[...])
    h_hi = h.astype(BF16)
    h_ref[...] = h_hi
    h_lo = (h - h_hi.astype(F32)).astype(BF16)
    rw = rw_ref[...]
    rw_hi = rw.astype(BF16)
    rw_lo = (rw - rw_hi.astype(F32)).astype(BF16)
    ne = rw.shape[1]
    both = jnp.dot(h_hi, jnp.concatenate([rw_hi, rw_lo], axis=1), preferred_element_type=F32)
    logits = both[:, :ne] + (both[:, ne:] + jnp.dot(h_lo, rw_hi, preferred_element_type=F32))
    e = jnp.exp(logits - jnp.max(logits, axis=-1, keepdims=True))
    aff_ref[...] = e / jnp.sum(e, axis=-1, keepdims=True)


def _mmres_kernel(ap_ref, as_ref, w_ref, r_ref, gt_ref, g_ref, sc_ref, sh_ref, rw_ref,
                  o_ref, h_ref, aff_ref, wbf_ref, *, npb):
    i = pl.program_id(0)

    @pl.when(i == 0)
    def _():
        wbf_ref[...] = w_ref[...].astype(BF16)

    def emit(a_ref):
        acc = jnp.dot(a_ref[...].astype(BF16), wbf_ref[...], preferred_element_type=F32)
        x_new = r_ref[...] + gt_ref[...] * acc
        o_ref[...] = x_new
        _router_epilogue(x_new, g_ref, sc_ref, sh_ref, rw_ref, h_ref, aff_ref)

    pl.when(i < npb)(lambda: emit(ap_ref))
    pl.when(i >= npb)(lambda: emit(as_ref))


def _sublayer2_specs(rows, layer, d, ne):
    tm = rows.tm
    ins = [_gain_spec(layer, d), _mod_spec(rows, layer, d, 4, 0), _mod_spec(rows, layer, d, 3, 0),
           pl.BlockSpec((None, d, ne), lambda i: (layer, 0, 0))]
    outs = [pl.BlockSpec((tm, d), lambda i: (i, 0)), pl.BlockSpec((tm, ne), lambda i: (i, 0))]
    return ins, outs


def _two_group_specs(rows, k):
    npb, nsb, tm = rows.prompt_blocks, rows.sample_blocks, rows.tm
    return [pl.BlockSpec((tm, k), lambda i: (jnp.minimum(i, npb - 1), 0)),
            pl.BlockSpec((tm, k), lambda i: (jnp.clip(i - npb, 0, nsb - 1), 0))]


def matmul_gated_residual(a_p, a_s, w, wl, resid, mod, rows, layer, gains2, router_w):
    k = a_p.shape[1]
    m, d = resid.shape
    tm = rows.tm
    ne = router_w.shape[2]
    r_in, r_out = _sublayer2_specs(rows, layer, d, ne)
    return pl.pallas_call(
        functools.partial(_mmres_kernel, npb=rows.prompt_blocks),
        grid=(m // tm,),
        in_specs=_two_group_specs(rows, k) + [
            pl.BlockSpec((None, k, d), lambda i: (wl, 0, 0)),
            pl.BlockSpec((tm, d), lambda i: (i, 0)),
            _mod_spec(rows, layer, d, 2, 0)] + r_in,
        out_specs=[pl.BlockSpec((tm, d), lambda i: (i, 0))] + r_out,
        out_shape=[jax.ShapeDtypeStruct((m, d), F32), jax.ShapeDtypeStruct((m, d), BF16),
                   jax.ShapeDtypeStruct((m, ne), F32)],
        scratch_shapes=[pltpu.VMEM((k, d), BF16)],
        compiler_params=_cparams("arbitrary"),
        name="matmul_gated_residual",
    )(a_p, a_s, w, resid, mod, gains2, mod, mod, router_w)


def _glures_kernel(ap_ref, as_ref, wv_ref, wg_ref, r_ref, gt_ref, g_ref, sc_ref, sh_ref, rw_ref,
                   o_ref, h_ref, aff_ref, wv_bf, wg_bf, *, npb):
    i = pl.program_id(0)

    @pl.when(i == 0)
    def _():
        wv_bf[...] = wv_ref[...].astype(BF16)
        wg_bf[...] = wg_ref[...].astype(BF16)

    def emit(a_ref):
        a = a_ref[...].astype(BF16)
        v = jnp.dot(a, wv_bf[...], preferred_element_type=F32)
        g = jnp.dot(a, wg_bf[...], preferred_element_type=F32)
        x_new = r_ref[...] + gt_ref[...] * (v * jax.nn.sigmoid(g))
        o_ref[...] = x_new
        _router_epilogue(x_new, g_ref, sc_ref, sh_ref, rw_ref, h_ref, aff_ref)

    pl.when(i < npb)(lambda: emit(ap_ref))
    pl.when(i >= npb)(lambda: emit(as_ref))


def glu_gated_residual(a_p, a_s, w_glu, wl, resid, mod, rows, layer, gains2, router_w):
    k = a_p.shape[1]
    m, d = resid.shape
    tm = rows.tm
    ne = router_w.shape[2]
    r_in, r_out = _sublayer2_specs(rows, layer, d, ne)
    return pl.pallas_call(
        functools.partial(_glures_kernel, npb=rows.prompt_blocks),
        grid=(m // tm,),
        in_specs=_two_group_specs(rows, k) + [
            pl.BlockSpec((None, k, d), lambda i: (wl, 0, 0)),
            pl.BlockSpec((None, k, d), lambda i: (wl, 0, 1)),
            pl.BlockSpec((tm, d), lambda i: (i, 0)),
            _mod_spec(rows, layer, d, 2, 0)] + r_in,
        out_specs=[pl.BlockSpec((tm, d), lambda i: (i, 0))] + r_out,
        out_shape=[jax.ShapeDtypeStruct((m, d), F32), jax.ShapeDtypeStruct((m, d), BF16),
                   jax.ShapeDtypeStruct((m, ne), F32)],
        scratch_shapes=[pltpu.VMEM((k, d), BF16), pltpu.VMEM((k, d), BF16)],
        compiler_params=_cparams("arbitrary"),
        name="glu_gated_residual",
    )(a_p, a_s, w_glu, w_glu, resid, mod, gains2, mod, mod, router_w)


def _rglru_kernel(gate_ref, u_ref, cw_ref, cb_ref, wa_ref, ba_ref, wx_ref, bx_ref, lam_ref, h0_ref,
                  y_ref, fin_ref, af_s, bf_s, ab_s, bb_s, hf_s, hb_s):
    t, cw = u_ref.shape
    u = u_ref[...]
    row = lax.broadcasted_iota(jnp.int32, (t, cw), 0)

    def shifted(x, k):
        if k > 0:
            return jnp.where(row >= k, pltpu.roll(x, k, axis=0), 0.0)
        return jnp.where(row < t + k, pltpu.roll(x, t + k, axis=0), 0.0)

    cwv = cw_ref[...]
    uc = (cwv[0:1] * shifted(u, 2) + cwv[1:2] * shifted(u, 1) + cwv[2:3] * u
          + cwv[3:4] * shifted(u, -1) + cb_ref[...])

    a_scr = (af_s, ab_s)
    b_scr = (bf_s, bb_s)
    for k in range(2):
        nl = -lam_ref[k:k + 1, :]
        sp = jnp.maximum(nl, 0.0) + jnp.log1p(jnp.exp(-jnp.abs(nl)))
        for hh in range(cw // RG_BS):
            sl = slice(hh * RG_BS, (hh + 1) * RG_BS)
            uh = uc[:, sl]
            ub = uh.astype(BF16)
            r = _sigmoid(jnp.dot(ub, wa_ref[k, hh].astype(BF16), preferred_element_type=F32) + ba_ref[k:k + 1, sl])
            i = _sigmoid(jnp.dot(ub, wx_ref[k, hh].astype(BF16), preferred_element_type=F32) + bx_ref[k:k + 1, sl])
            log_a = (-RG_C * r) * sp[:, sl]
            a = jnp.exp(log_a)
            a_scr[k][:, sl] = a
            b_scr[k][:, sl] = jnp.sqrt(jnp.tanh(-log_a) * (a * a + 1.0)) * (i * uh)

    nblk = t // SUBLANES
    srow = lax.broadcasted_iota(jnp.int32, (SUBLANES, cw), 0)

    def body(n, carry):
        cf, cb = carry
        rf = pl.multiple_of(n * SUBLANES, SUBLANES)
        rb = pl.multiple_of((nblk - 1 - n) * SUBLANES, SUBLANES)
        a = af_s[pl.ds(rf, SUBLANES), :]
        b = bf_s[pl.ds(rf, SUBLANES), :]
        a2 = ab_s[pl.ds(rb, SUBLANES), :]
        b2 = bb_s[pl.ds(rb, SUBLANES), :]
        for s in (1, 2, 4):
            m = srow >= s
            b = jnp.where(m, a * pltpu.roll(b, s, axis=0) + b, b)
            a = jnp.where(m, a * pltpu.roll(a, s, axis=0), a)
            m2 = srow < SUBLANES - s
            b2 = jnp.where(m2, a2 * pltpu.roll(b2, SUBLANES - s, axis=0) + b2, b2)
            a2 = jnp.where(m2, a2 * pltpu.roll(a2, SUBLANES - s, axis=0), a2)
        hf = a * cf + b
        hb = a2 * cb + b2
        hf_s[pl.ds(rf, SUBLANES), :] = hf
        hb_s[pl.ds(rb, SUBLANES), :] = hb
        return hf[SUBLANES - 1:SUBLANES, :], hb[0:1, :]

    cf, cb = lax.fori_loop(0, nblk, body, (h0_ref[0:1, :], h0_ref[1:2, :]))
    fin_ref[0:1, :] = cf
    fin_ref[1:2, :] = cb
    y_ref[...] = (hf_s[...] + hb_s[...]) * _gelu(gate_ref[...])


def rglru_scan(gu, row0, n_seq, t, j, conv_w, conv_b, w_a, b_a, w_x, b_x, lam, h0, h0_j, *, cw=RG_TILE):
    r = gu.shape[1] // 2
    nh = cw // RG_BS
    blk0 = row0 // t
    nc = r // cw
    scr = [pltpu.VMEM((t, cw), F32) for _ in range(6)]
    vec2 = pl.BlockSpec((None, 2, cw), lambda b, c: (j, 0, c))
    gatew = pl.BlockSpec((None, 2, nh, RG_BS, RG_BS), lambda b, c: (j, 0, c, 0, 0))
    return pl.pallas_call(
        _rglru_kernel,
        grid=(n_seq, nc),
        in_specs=[pl.BlockSpec((t, cw), lambda b, c: (blk0 + b, c)),
                  pl.BlockSpec((t, cw), lambda b, c: (blk0 + b, nc + c)),
                  pl.BlockSpec((None, 4, cw), lambda b, c: (j, 0, c)),
                  pl.BlockSpec((None, 1, cw), lambda b, c: (j, 0, c)),
                  gatew, vec2, gatew, vec2, vec2,
                  pl.BlockSpec((None, None, 2, cw), lambda b, c: (b, h0_j, 0, c))],
        out_specs=[pl.BlockSpec((t, cw), lambda b, c: (b, c)),
                   pl.BlockSpec((None, 2, cw), lambda b, c: (b, 0, c))],
        out_shape=[jax.ShapeDtypeStruct((n_seq * t, r), F32),
                   jax.ShapeDtypeStruct((n_seq, 2, r), F32)],
        scratch_shapes=scr,
        compiler_params=_cparams("arbitrary", "arbitrary"),
        name="rglru_scan",
    )(gu, gu, conv_w, conv_b.reshape(conv_b.shape[0], 1, r), w_a, b_a, w_x, b_x, lam, h0)


def _s5_kernel(*refs, bp, n_seq, nc, gpb):
    ell, gw = S5_L, S5_H
    (x_ref, tm_ref, win_ref, wre_ref, wim_ref, ar_ref, ai_ref, d_ref, s0re_ref, s0im_ref,
     o_ref, fin_ref, xg_s, ure_s, uim_s, fre_s, fim_s, bre_s, bim_s) = refs
    m, mp = n_seq * nc, bp * nc
    lanes = x_ref.shape[1]
    token = lambda l: pl.ds(l, m, stride=ell)
    per_tile = lanes // gw
    half = ure_s.shape[2] // 2
    lane_grp = lax.shift_right_logical(lax.broadcasted_iota(jnp.int32, (m, lanes), 1), gw.bit_length() - 1)

    def perm(shape, chunk_major_axis):
        i = lax.broadcasted_iota(jnp.int32, shape, chunk_major_axis)
        j = lax.broadcasted_iota(jnp.int32, shape, 1 - chunk_major_axis)
        b = i & (bp - 1)
        c = lax.shift_right_logical(i, bp.bit_length() - 1)
        return jnp.where((j == b * nc + c) & (b < n_seq), 1.0, 0.0).astype(BF16)

    to_chunk_major = perm((mp, m), 0)
    to_batch_major = perm((m, mp), 1)

    def block_transpose(v):
        k = per_tile // 2
        while k >= 1:
            low = (lane_grp & k) == 0
            nxt = list(v)
            for i in range(per_tile):
                if i & k == 0:
                    a, b = v[i], v[i + k]
                    nxt[i] = jnp.where(low, a, pltpu.roll(b, k * gw, axis=1))
                    nxt[i + k] = jnp.where(low, pltpu.roll(a, lanes - k * gw, axis=1), b)
            v = nxt
            k //= 2
        return v

    for tile in range(ell // per_tile):
        by_group = block_transpose([x_ref[token(tile * per_tile + j), :] for j in range(per_tile)])
        for g in range(gpb):
            xg_s[g, :, tile * lanes:(tile + 1) * lanes] = by_group[g]

    for g in range(gpb):
        xg = xg_s[g]
        xp = jnp.dot(to_chunk_major, xg.astype(BF16), preferred_element_type=F32).astype(BF16)
        u = jnp.dot(xp, win_ref[g].astype(BF16), preferred_element_type=F32)
        ure_s[g] = u[:, :2 * half]
        uim_s[g] = u[:, 2 * half:]

    is_fwd = lax.broadcasted_iota(jnp.int32, (bp, 2 * half), 1) < half
    ar = [ar_ref[g] for g in range(gpb)]
    ai = [ai_ref[g] for g in range(gpb)]

    def body(k, carry):
        rf = pl.multiple_of(k * bp, bp)
        rb = pl.multiple_of((nc - 1 - k) * bp, bp)
        out = []
        for g in range(gpb):
            re, im = carry[2 * g], carry[2 * g + 1]
            fre_s[g, pl.ds(rf, bp), :] = re
            fim_s[g, pl.ds(rf, bp), :] = im
            bre_s[g, pl.ds(rb, bp), :] = re
            bim_s[g, pl.ds(rb, bp), :] = im
            ure = jnp.where(is_fwd, ure_s[g, pl.ds(rf, bp), :], ure_s[g, pl.ds(rb, bp), :])
            uim = jnp.where(is_fwd, uim_s[g, pl.ds(rf, bp), :], uim_s[g, pl.ds(rb, bp), :])
            out += [ar[g] * re - ai[g] * im + ure, ar[g] * im + ai[g] * re + uim]
        return tuple(out)

    init = tuple(r[g] for g in range(gpb) for r in (s0re_ref, s0im_ref))
    fin = lax.fori_loop(0, nc, body, init)
    fwd_all = lax.broadcasted_iota(jnp.int32, (mp, 2 * half), 1) < half
    for g in range(gpb):
        fin_ref[g, :, :2 * half] = fin[2 * g]
        fin_ref[g, :, 2 * half:] = fin[2 * g + 1]
        hre = jnp.where(fwd_all, fre_s[g], bre_s[g]).astype(BF16)
        him = jnp.where(fwd_all, fim_s[g], bim_s[g]).astype(BF16)
        hre = jnp.dot(to_batch_major, hre, preferred_element_type=F32).astype(BF16)
        him = jnp.dot(to_batch_major, him, preferred_element_type=F32).astype(BF16)
        xg = xg_s[g]
        y = (jnp.dot(xg.astype(BF16), tm_ref[g].astype(BF16), preferred_element_type=F32)
             + jnp.dot(hre, wre_ref[g].astype(BF16), preferred_element_type=F32)
             + jnp.dot(him, wim_ref[g].astype(BF16), preferred_element_type=F32)
             + d_ref[g] * xg)
        xg_s[g] = _gelu(y)

    for tile in range(ell // per_tile):
        by_token = block_transpose([xg_s[g, :, tile * lanes:(tile + 1) * lanes] for g in range(gpb)])
        for j in range(per_tile):
            o_ref[token(tile * per_tile + j), :] = by_token[j]


def s5_chunked(hn, row_blk, n_seq, nc, mats, s0re, s0im, *, bp):
    tmat, win, wre, wim, ar, ai, dg = mats
    g, w, _ = tmat.shape
    p2 = ar.shape[-1]
    ell = S5_L
    lanes = 128
    gpb = lanes // S5_H
    d = hn.shape[1]
    m, mp = n_seq * nc, bp * nc
    n = m * ell
    assert bp & (bp - 1) == 0 and S5_H & (S5_H - 1) == 0 and g % gpb == 0
    blk = lambda shape: pl.BlockSpec((gpb,) + shape, lambda i: (i, 0, 0))
    return pl.pallas_call(
        functools.partial(_s5_kernel, bp=bp, n_seq=n_seq, nc=nc, gpb=gpb),
        grid=(g // gpb,),
        in_specs=[pl.BlockSpec((n, lanes), lambda i: (row_blk, i)),
                  blk((w, w)), blk((w, 2 * p2)), blk((p2, w)), blk((p2, w)),
                  blk((1, p2)), blk((1, p2)), blk((1, w)), blk((bp, p2)), blk((bp, p2))],
        out_specs=[pl.BlockSpec((n, lanes), lambda i: (0, i)), blk((bp, 2 * p2))],
        out_shape=[jax.ShapeDtypeStruct((n, d), F32), jax.ShapeDtypeStruct((g, bp, 2 * p2), F32)],
        scratch_shapes=[pltpu.VMEM((gpb, m, w), F32)] + [pltpu.VMEM((gpb, mp, p2), F32) for _ in range(6)],
        compiler_params=_cparams("arbitrary"),
        name="s5_chunked",
    )(hn, tmat, win, wre, wim, ar, ai, dg, s0re, s0im)


def _cmul(ar, ai, br, bi):
    return ar * br - ai * bi, ar * bi + ai * br


def _s5_prep_kernel(are_ref, aim_ref, ldt_ref, btr_ref, bti_ref, cr_ref, ci_ref,
                    tm_ref, win_ref, wre_ref, wim_ref, ar_ref, ai_ref):
    for gi in range(cr_ref.shape[0]):
        _s5_prep_group(*(r.at[gi] for r in (are_ref, aim_ref, ldt_ref, btr_ref, bti_ref, cr_ref, ci_ref,
                                           tm_ref, win_ref, wre_ref, wim_ref, ar_ref, ai_ref)))


def _s5_prep_group(are_ref, aim_ref, ldt_ref, btr_ref, bti_ref, cr_ref, ci_ref,
                   tm_ref, win_ref, wre_ref, wim_ref, ar_ref, ai_ref):
    ell = S5_L
    h, p2 = cr_ref.shape
    w = ell * h
    a_re, a_im = are_ref[...], aim_ref[...]
    dt = jnp.exp(ldt_ref[...])
    steps = lax.broadcasted_iota(jnp.int32, (3 * SUBLANES, p2), 0).astype(F32)
    mag = jnp.exp(steps * (a_re * dt))
    ang = steps * (a_im * dt)
    pw_r, pw_i = mag * jnp.cos(ang), mag * jnp.sin(ang)
    nr, ni = pw_r[1:2] - 1.0, pw_i[1:2]
    den = a_re * a_re + a_im * a_im
    qr, qi = (nr * a_re + ni * a_im) / den, (ni * a_re - nr * a_im) / den
    bb_r, bb_i = _cmul(qr, qi, btr_ref[...], bti_ref[...])
    c_r, c_i = cr_ref[...], ci_ref[...]
    fwd = lax.broadcasted_iota(jnp.int32, (1, p2), 1) < p2 // 2

    def power_rows(m_fwd, m_bwd):
        return (jnp.where(fwd, pw_r[m_fwd:m_fwd + 1], pw_r[m_bwd:m_bwd + 1]),
                jnp.where(fwd, pw_i[m_fwd:m_fwd + 1], pw_i[m_bwd:m_bwd + 1]))

    def stack(x_r, x_i, powers):
        parts = [_cmul(x_r, x_i, *power_rows(*powers(l))) for l in range(ell)]
        return (jnp.concatenate([q[0] for q in parts], axis=0), jnp.concatenate([q[1] for q in parts], axis=0))

    win_r, win_i = stack(bb_r, bb_i, lambda l: (ell - 1 - l, l))
    win_ref[...] = jnp.concatenate([win_r, win_i], axis=1).astype(win_ref.dtype)
    z_r, z_i = stack(c_r, c_i, lambda l: (l + 1, ell - l))
    wre_ref[...] = z_r.T.astype(wre_ref.dtype)
    wim_ref[...] = (-z_i).T.astype(wim_ref.dtype)
    k_r, k_i = stack(c_r, c_i, lambda m: (m, ell - 1 - m))
    mask_f = jnp.where(fwd, 1.0, 0.0)

    def lag_rows(mask):
        return (lax.dot_general(bb_r * mask, k_r, NT_DIMS, precision=HIGHEST, preferred_element_type=F32)
                - lax.dot_general(bb_i * mask, k_i, NT_DIMS, precision=HIGHEST, preferred_element_type=F32))

    kf = lag_rows(mask_f)
    kb = lag_rows(1.0 - mask_f)
    lane = lax.broadcasted_iota(jnp.int32, (h, w), 1)
    blocks = []
    for li in range(ell):
        f_part = kf if li == 0 else pltpu.roll(kf, li * h, axis=1)
        s_b = (w - (ell - 1 - li) * h) % w
        b_part = kb if s_b == 0 else pltpu.roll(kb, s_b, axis=1)
        blocks.append(jnp.where(lane >= li * h, f_part, 0.0) + jnp.where(lane < (li + 1) * h, b_part, 0.0))
    tm_ref[...] = jnp.concatenate(blocks, axis=0).astype(tm_ref.dtype)
    ar_ref[...] = pw_r[ell:ell + 1]
    ai_ref[...] = pw_i[ell:ell + 1]


def s5_chunk_operators(a_re, a_im, log_dt, b_re, b_im, c_re, c_im, d):
    _, g, p = a_re.shape
    h = b_re.shape[-1]
    w = S5_L * h
    two_dir = lambda x: jnp.transpose(x, (1, 0, 2)).reshape(g, 1, 2 * p)
    ldt = two_dir(jnp.broadcast_to(log_dt[:, :, None], (2, g, p)))
    bt = lambda x: jnp.transpose(x, (1, 3, 0, 2)).reshape(g, h, 2 * p)
    ct = lambda x: jnp.transpose(x, (1, 2, 0, 3)).reshape(g, h, 2 * p)
    gps = math.gcd(g, S5_PREP_GROUPS)
    per_g = lambda shape: pl.BlockSpec((gps,) + shape, lambda i: (i, 0, 0))
    tmat, win, wre, wim, ar, ai = pl.pallas_call(
        _s5_prep_kernel,
        grid=(g // gps,),
        in_specs=[per_g((1, 2 * p))] * 3 + [per_g((h, 2 * p))] * 4,
        out_specs=[per_g((w, w)), per_g((w, 4 * p)), per_g((2 * p, w)), per_g((2 * p, w)),
                   per_g((1, 2 * p)), per_g((1, 2 * p))],
        out_shape=[jax.ShapeDtypeStruct((g, w, w), BF16), jax.ShapeDtypeStruct((g, w, 4 * p), BF16),
                   jax.ShapeDtypeStruct((g, 2 * p, w), BF16), jax.ShapeDtypeStruct((g, 2 * p, w), BF16),
                   jax.ShapeDtypeStruct((g, 1, 2 * p), F32), jax.ShapeDtypeStruct((g, 1, 2 * p), F32)],
        compiler_params=_cparams("arbitrary"),
        name="s5_chunk_operators",
    )(two_dir(a_re), two_dir(a_im), ldt, bt(b_re), bt(b_im), ct(c_re), ct(c_im))
    dg = jnp.tile(d.reshape(g, 1, h), (1, S5_L, 1)).reshape(g, 1, w)
    return tmat, win, wre, wim, ar, ai, dg


def s5_mixer_group(hn, row0, n_seq, t, mats, s0):
    ar = mats[4]
    g = ar.shape[0]
    p = ar.shape[-1] // 2
    nc = t // S5_L
    bp = -(-n_seq // SUBLANES) * SUBLANES
    n = n_seq * t
    assert row0 % n == 0
    if s0 is None:
        s0re = jnp.zeros((g, bp, 2 * p), F32)
        s0im = s0re
    else:
        st = jnp.transpose(s0, (3, 0, 2, 1, 4)).reshape(g, n_seq, 2, 2 * p)
        st = jnp.pad(st, ((0, 0), (0, bp - n_seq), (0, 0), (0, 0)))
        s0re, s0im = st[:, :, 0], st[:, :, 1]
    u, fin = s5_chunked(hn, row0 // n, n_seq, nc, mats, s0re, s0im, bp=bp)
    fin = fin.reshape(g, bp, 2, 2, p)[:, :n_seq]
    return u, jnp.transpose(fin, (1, 3, 2, 0, 4))


def _softmax_pv(scores, values, sink):
    m = sink
    for s in scores:
        m = jnp.maximum(m, jnp.max(s, axis=-1, keepdims=True))
    den = jnp.exp(sink - m)
    acc = None
    for s, v in zip(scores, values):
        p = jnp.exp(s - m)
        den = den + jnp.sum(p, axis=-1, keepdims=True)
        pv = jnp.dot(p.astype(BF16), v, preferred_element_type=F32)
        acc = pv if acc is None else acc + pv
    return acc / den


def _attn_prompt_kernel(sink_ref, q_ref, k_ref, v_ref, o_ref):
    k = k_ref[...].astype(BF16)
    v = v_ref[...].astype(BF16)
    for h in range(N_KV):
        hs = slice(h * HEAD_DIM, (h + 1) * HEAD_DIM)
        kh, vh = k[:, hs], v[:, hs]
        for g in range(Q_PER_KV):
            c0 = (h * Q_PER_KV + g) * HEAD_DIM
            qg = (q_ref[:, c0:c0 + HEAD_DIM] * ATT_SCALE).astype(BF16)
            s = lax.dot_general(qg, kh, NT_DIMS, preferred_element_type=F32)
            o_ref[:, c0:c0 + HEAD_DIM] = _softmax_pv([s], [vh], sink_ref[h * Q_PER_KV + g])


def attn_prompt(qkv, sink, n_seq, t):
    dq = N_HEADS * HEAD_DIM
    kcol = dq // KV_W
    return pl.pallas_call(
        _attn_prompt_kernel,
        grid=(n_seq,),
        in_specs=[pl.BlockSpec(memory_space=pltpu.SMEM),
                  pl.BlockSpec((t, dq), lambda b: (b, 0)),
                  pl.BlockSpec((t, KV_W), lambda b: (b, kcol)),
                  pl.BlockSpec((t, KV_W), lambda b: (b, kcol + 1))],
        out_specs=pl.BlockSpec((t, dq), lambda b: (b, 0)),
        out_shape=jax.ShapeDtypeStruct((n_seq * t, dq), F32),
        compiler_params=_cparams("arbitrary"),
        name="attn_prompt",
    )(sink, qkv, qkv, qkv)


def _rope(x, cos, sin):
    w = x.shape[1]
    low = (lax.broadcasted_iota(jnp.int32, x.shape, 1) & (HEAD_DIM // 4)) == 0
    partner = jnp.where(low, pltpu.roll(x, w - HEAD_DIM // 4, axis=1), pltpu.roll(x, HEAD_DIM // 4, axis=1))
    return x * cos + partner * sin


def _attn_sample_kernel(sink_ref, q_ref, k_ref, v_ref, kc_ref, vc_ref, cos_ref, sin_ref, o_ref,
                        kw_s, vw_s, kc_s, vc_s, *, t):
    n = pl.program_id(1)
    blk = ATT_BLOCK

    @pl.when(n == 0)
    def _():
        zeros = jnp.zeros((blk, KV_W), BF16)
        kw_s[0:blk, :] = zeros
        vw_s[0:blk, :] = zeros
        kw_s[blk + t:2 * blk + t, :] = zeros
        vw_s[blk + t:2 * blk + t, :] = zeros
        kw_s[blk:blk + t, :] = _rope(k_ref[...], cos_ref[...], sin_ref[...]).astype(BF16)
        vw_s[blk:blk + t, :] = v_ref[...].astype(BF16)
        kc_s[...] = kc_ref[...].astype(BF16)
        vc_s[...] = vc_ref[...].astype(BF16)

    r0 = pl.multiple_of(n * blk, blk)
    cq = cos_ref[pl.ds(r0, blk), :]
    sq = sin_ref[pl.ds(r0, blk), :]
    kw = kw_s[pl.ds(r0, 3 * blk), :]
    vw = vw_s[pl.ds(r0, 3 * blk), :]
    qi = lax.broadcasted_iota(jnp.int32, (blk, 3 * blk), 0)
    kj = lax.broadcasted_iota(jnp.int32, (blk, 3 * blk), 1)
    kpos = n * blk - blk + kj
    valid = (jnp.abs(kj - blk - qi) <= WINDOW) & (kpos >= 0) & (kpos < t)
    for h in range(N_KV):
        hs = slice(h * HEAD_DIM, (h + 1) * HEAD_DIM)
        qh = (_rope(q_ref[:, h * KV_W:(h + 1) * KV_W], cq, sq) * ATT_SCALE).astype(BF16)
        kh, vh, kch, vch = kw[:, hs], vw[:, hs], kc_s[:, hs], vc_s[:, hs]
        for g in range(Q_PER_KV):
            qg = qh[:, g * HEAD_DIM:(g + 1) * HEAD_DIM]
            s_loc = lax.dot_general(qg, kh, NT_DIMS, preferred_element_type=F32)
            s_loc = jnp.where(valid, s_loc, NEG_INF)
            s_ctx = lax.dot_general(qg, kch, NT_DIMS, preferred_element_type=F32)
            c0 = (h * Q_PER_KV + g) * HEAD_DIM
            o_ref[:, c0:c0 + HEAD_DIM] = _softmax_pv([s_loc, s_ctx], [vh, vch], sink_ref[h * Q_PER_KV + g])


def _rope_tables(t):
    quarter = HEAD_DIM // 4
    freqs = ROPE_BASE ** (-jnp.arange(quarter, dtype=F32) / quarter)
    pos = jnp.arange(t)
    ang_r = (pos // GRID_W).astype(F32)[:, None] * freqs
    ang_c = (pos % GRID_W).astype(F32)[:, None] * freqs
    cos = jnp.concatenate([jnp.cos(ang_r), jnp.cos(ang_r), jnp.cos(ang_c), jnp.cos(ang_c)], axis=-1)
    sin = jnp.concatenate([-jnp.sin(ang_r), jnp.sin(ang_r), -jnp.sin(ang_c), jnp.sin(ang_c)], axis=-1)
    return jnp.tile(cos, (1, N_KV)), jnp.tile(sin, (1, N_KV))


def attn_sample(qkv, row0, sink, n_seq, t, k_ctx, v_ctx):
    dq = N_HEADS * HEAD_DIM
    kcol = dq // KV_W
    nb = t // ATT_BLOCK
    lc = k_ctx.shape[1]
    cos, sin = _rope_tables(t)
    qblk0, sblk0 = row0 // ATT_BLOCK, row0 // t
    return pl.pallas_call(
        functools.partial(_attn_sample_kernel, t=t),
        grid=(n_seq, nb),
        in_specs=[pl.BlockSpec(memory_space=pltpu.SMEM),
                  pl.BlockSpec((ATT_BLOCK, dq), lambda b, n: (qblk0 + b * nb + n, 0)),
                  pl.BlockSpec((t, KV_W), lambda b, n: (sblk0 + b, kcol)),
                  pl.BlockSpec((t, KV_W), lambda b, n: (sblk0 + b, kcol + 1)),
                  pl.BlockSpec((None, lc, KV_W), lambda b, n: (b, 0, 0)),
                  pl.BlockSpec((None, lc, KV_W), lambda b, n: (b, 0, 0)),
                  pl.BlockSpec((t, KV_W), lambda b, n: (0, 0)),
                  pl.BlockSpec((t, KV_W), lambda b, n: (0, 0))],
        out_specs=pl.BlockSpec((ATT_BLOCK, dq), lambda b, n: (b * nb + n, 0)),
        out_shape=jax.ShapeDtypeStruct((n_seq * t, dq), F32),
        scratch_shapes=[pltpu.VMEM((t + 2 * ATT_BLOCK, KV_W), BF16), pltpu.VMEM((t + 2 * ATT_BLOCK, KV_W), BF16),
                        pltpu.VMEM((lc, KV_W), BF16), pltpu.VMEM((lc, KV_W), BF16)],
        compiler_params=_cparams("arbitrary", "arbitrary"),
        name="attn_sample",
    )(sink, qkv, qkv, qkv, k_ctx, v_ctx, cos, sin)


def _moe_kernel(x_ref, wg_ref, wu_ref, wd_ref, gt_ref, seg_ref, g2_ref, o_ref, wg_bf, wu_bf, wd_bf, *, n_seg):
    wg_bf[...] = wg_ref[...].astype(BF16)
    wu_bf[...] = wu_ref[...].astype(BF16)
    wd_bf[...] = wd_ref[...].astype(BF16)
    rsub = min(MOE_ROW_SUB, x_ref.shape[0])
    for r in range(x_ref.shape[0] // rsub):
        rs = slice(r * rsub, (r + 1) * rsub)
        x = x_ref[rs, :]
        hg = jnp.dot(x, wg_bf[...], preferred_element_type=F32)
        hu = jnp.dot(x, wu_bf[...], preferred_element_type=F32)
        he = ((hg * jax.nn.sigmoid(hg)) * hu).astype(BF16)
        y = jnp.dot(he, wd_bf[...], preferred_element_type=F32)
        seg = seg_ref[rs, :]
        g2 = jnp.zeros(y.shape, F32)
        for s in range(n_seg):
            g2 = jnp.where(seg == s, g2_ref[s:s + 1, :], g2)
        o_ref[rs, :] = y * (gt_ref[rs, :] * g2)


def moe_experts(xe, w_gate, w_up, w_down, layer, gates, seg, mod, n_seg):
    ne, r, d = xe.shape
    dff = w_gate.shape[3]
    return pl.pallas_call(
        functools.partial(_moe_kernel, n_seg=n_seg),
        grid=(ne,),
        in_specs=[pl.BlockSpec((None, r, d), lambda e: (e, 0, 0)),
                  pl.BlockSpec((None, None, d, dff), lambda e: (layer, e, 0, 0)),
                  pl.BlockSpec((None, None, d, dff), lambda e: (layer, e, 0, 0)),
                  pl.BlockSpec((None, None, dff, d), lambda e: (layer, e, 0, 0)),
                  pl.BlockSpec((None, r, 1), lambda e: (e, 0, 0)),
                  pl.BlockSpec((None, r, 1), lambda e: (e, 0, 0)),
                  pl.BlockSpec((SUBLANES, None, d), lambda e: (layer, 0, 5))],
        out_specs=pl.BlockSpec((None, r, d), lambda e: (e, 0, 0)),
        out_shape=jax.ShapeDtypeStruct((ne, r, d), F32),
        scratch_shapes=[pltpu.VMEM((d, dff), BF16), pltpu.VMEM((d, dff), BF16), pltpu.VMEM((dff, d), BF16)],
        compiler_params=_cparams("arbitrary"),
        name="moe_experts",
    )(xe, w_gate, w_up, w_down, gates, seg, mod)


def _expert_choice(aff, rows):
    ne = aff.shape[1]
    sizes = (rows.n_prompt, rows.n_sample)
    caps = [(EC_FACTOR * n) // ne for n in sizes]
    if sizes[0] == sizes[1]:
        n = sizes[0]
        gt, ix = lax.top_k(jnp.swapaxes(aff.reshape(2, n, ne), 1, 2), caps[0])
        ix = ix + jnp.array([0, n], jnp.int32)[:, None, None]
        return (jnp.concatenate([gt[0], gt[1]], axis=1), jnp.concatenate([ix[0], ix[1]], axis=1))
    gts, ixs, off = [], [], 0
    for n, cap in zip(sizes, caps):
        gt, ix = lax.top_k(aff[off:off + n].T, cap)
        gts.append(gt)
        ixs.append(ix + off)
        off += n
    return jnp.concatenate(gts, axis=1), jnp.concatenate(ixs, axis=1)


def _combine_kernel(idx_ref, x_hbm, ye_ref, o_hbm, acc, sem, *, n_grp, rows_per):
    g = pl.program_id(0)
    e = pl.program_id(1)
    last = pl.num_programs(1) - 1

    def load(grp):
        return pltpu.make_async_copy(x_hbm.at[pl.ds(grp * n_grp, n_grp), :], acc.at[grp], sem.at[grp])

    def store(grp):
        return pltpu.make_async_copy(acc.at[grp], o_hbm.at[pl.ds(grp * n_grp, n_grp), :], sem.at[2 + grp])

    @pl.when((g == 0) & (e == 0))
    def _():
        load(0).start()
        load(1).start()

    @pl.when(e == 0)
    def _():
        load(g).wait()

    base = (2 * e + g) * rows_per
    first = g * n_grp

    def body(i, carry):
        toks = [idx_ref[base + i * COMBINE_UNROLL + k] - first for k in range(COMBINE_UNROLL)]
        old = [acc[g, pl.ds(t, 1), :] for t in toks]
        add = [ye_ref[pl.ds(i * COMBINE_UNROLL + k, 1), :] for k in range(COMBINE_UNROLL)]
        for t, a, b in zip(toks, old, add):
            acc[g, pl.ds(t, 1), :] = a + b
        return carry

    lax.fori_loop(0, rows_per // COMBINE_UNROLL, body, 0)

    @pl.when(e == last)
    def _():
        store(g).start()

    @pl.when((g == 1) & (e == last))
    def _():
        store(0).wait()
        store(1).wait()


def moe_combine(x, ye, idx, n_grp):
    m, d = x.shape
    ne, r, _ = ye.shape
    rows_per = r // 2
    assert m == 2 * n_grp and rows_per % COMBINE_UNROLL == 0
    grid_spec = pltpu.PrefetchScalarGridSpec(
        num_scalar_prefetch=1,
        grid=(2, ne),
        in_specs=[pl.BlockSpec(memory_space=pl.ANY),
                  pl.BlockSpec((None, rows_per, d), lambda g, e, ix: (e, g, 0))],
        out_specs=pl.BlockSpec(memory_space=pl.ANY),
        scratch_shapes=[pltpu.VMEM((2, n_grp, d), F32), pltpu.SemaphoreType.DMA((4,))])
    return pl.pallas_call(
        functools.partial(_combine_kernel, n_grp=n_grp, rows_per=rows_per),
        grid_spec=grid_spec,
        out_shape=jax.ShapeDtypeStruct((m, d), F32),
        compiler_params=_cparams("arbitrary", "arbitrary"),
        name="moe_combine",
    )(idx.reshape(-1), x, ye)


def moe_layer(x, h2, aff, mod, rows, layer, w_gate, w_up, w_down, n_seg):
    m, d = x.shape
    gates, idx = _expert_choice(aff, rows)
    seg = jnp.where(idx < rows.n_prompt, 0, 1 + (idx - rows.n_prompt) // rows.t_sample)
    xe = h2[idx]
    ye = moe_experts(xe, w_gate, w_up, w_down, layer, gates[..., None], seg[..., None], mod, n_seg)
    if rows.n_prompt == rows.n_sample:
        return moe_combine(x, ye, idx, rows.n_prompt)
    return x.at[idx.reshape(-1)].add(ye.reshape(-1, d))


def kernel(x_prompt, x_sample, state_rglru, state_s5, cache_k, cache_v, c, c_ctx, ada_w, ada_b, norm1_g, norm2_g, rg_w_in, rg_conv_w, rg_conv_b, rg_w_a, rg_b_a, rg_w_x, rg_b_x, rg_lambda, rg_w_out, s5_a_re, s5_a_im, s5_log_dt, s5_b_re, s5_b_im, s5_c_re, s5_c_im, s5_d, s5_w_glu, attn_w_qkv, attn_w_o, attn_sink, router_w, moe_w_gate, moe_w_up, moe_w_down, final_norm_g):
    bp_, tp, d = x_prompt.shape
    bs, ts, _ = x_sample.shape
    n_p, n_s = bp_ * tp, bs * ts
    depth = ada_w.shape[0]
    rows = _Rows(n_p, n_s, ts)
    assert bs + 1 <= SUBLANES and n_p % ts == 0

    x = jnp.concatenate([x_prompt.reshape(n_p, d), x_sample.reshape(n_s, d)], axis=0)
    cond = jnp.concatenate([c_ctx[None, :], c, jnp.zeros((SUBLANES - 1 - bs, d), F32)], axis=0)
    mod_all = ada_modulation_all(cond, ada_w, ada_b)
    mod = mod_all.reshape(depth * SUBLANES, 1, 6 * d)
    g1 = norm1_g.reshape(depth, 1, d)
    g2 = norm2_g.reshape(depth, 1, d)

    new_rg, new_s5, new_k, new_v = [], [], [], []
    for l in range(depth):
        kind, j = l % 3, l // 3
        if kind == 0:
            gu = norm_mod_matmul(x, g1, mod, rows, l, rg_w_in, j)
            args = (j, rg_conv_w, rg_conv_b, rg_w_a, rg_b_a, rg_w_x, rg_b_x, rg_lambda)
            r = gu.shape[1] // 2
            yp, fin = rglru_scan(gu, 0, bp_, tp, *args, jnp.zeros((bp_, 1, 2, r), F32), 0)
            ys, _ = rglru_scan(gu, n_p, bs, ts, *args, state_rglru, j)
            new_rg.append(fin)
            x, h2, aff = matmul_gated_residual(yp, ys, rg_w_out, j, x, mod, rows, l, g2, router_w)
        elif kind == 1:
            hn = norm_mod(x, g1, mod, rows, l)
            mats = s5_chunk_operators(s5_a_re[j], s5_a_im[j], s5_log_dt[j], s5_b_re[j], s5_b_im[j],
                                      s5_c_re[j], s5_c_im[j], s5_d[j])
            up, st = s5_mixer_group(hn, 0, bp_, tp, mats, None)
            us, _ = s5_mixer_group(hn, n_p, bs, ts, mats, state_s5[:, j])
            new_s5.append(st)
            x, h2, aff = glu_gated_residual(up, us, s5_w_glu, j, x, mod, rows, l, g2, router_w)
        else:
            qkv = norm_mod_matmul(x, g1, mod, rows, l, attn_w_qkv, j)
            dq = N_HEADS * HEAD_DIM
            new_k.append(qkv[:n_p, dq:dq + KV_W].reshape(bp_, tp, N_KV, HEAD_DIM))
            new_v.append(qkv[:n_p, dq + KV_W:].reshape(bp_, tp, N_KV, HEAD_DIM))
            op = attn_prompt(qkv, attn_sink[j], bp_, tp)
            lc = cache_k.shape[2]
            os_ = attn_sample(qkv, n_p, attn_sink[j], bs, ts,
                              cache_k[:, j].reshape(bs, lc, KV_W), cache_v[:, j].reshape(bs, lc, KV_W))
            x, h2, aff = matmul_gated_residual(op, os_, attn_w_o, j, x, mod, rows, l, g2, router_w)
        x = moe_layer(x, h2, aff, mod, rows, l, moe_w_gate, moe_w_up, moe_w_down, bs + 1)

    y_p, y_s = final_norm(x, final_norm_g, rows)
    return (y_p.reshape(bp_, tp, d), y_s.reshape(bs, ts, d),
            jnp.stack(new_rg, axis=1), jnp.stack(new_s5, axis=1),
            jnp.stack(new_k, axis=1), jnp.stack(new_v, axis=1))
```

```python
import functools
import math

import jax
import jax.numpy as jnp
from jax import lax
from jax.experimental import pallas as pl
from jax.experimental.pallas import tpu as pltpu

F32 = jnp.float32
BF16 = jnp.bfloat16
HIGHEST = lax.Precision.HIGHEST

EPS = 1e-6
RG_C = 8.0
RG_BS = 128
RG_TILE = 512
S5_H = 16
S5_L = 16
S5_PREP_GROUPS = 4
N_HEADS = 16
N_KV = 4
Q_PER_KV = N_HEADS // N_KV
HEAD_DIM = 64
KV_W = N_KV * HEAD_DIM
GRID_W = 64
WINDOW = 128
ATT_BLOCK = 128
ROPE_BASE = 10000.0
ATT_SCALE = HEAD_DIM ** -0.5
assert math.frexp(ATT_SCALE)[0] == 0.5
NEG_INF = -1e30
N_EXPERTS = 16
EC_FACTOR = 2
SUBLANES = 8
ROW_TILE = 512
MOE_ROW_SUB = 256
COMBINE_UNROLL = 8
VMEM_LIMIT = 56 * 1024 * 1024
NT_DIMS = (((1,), (1,)), ((), ()))


def _cparams(*sem):
    return pltpu.CompilerParams(dimension_semantics=sem, vmem_limit_bytes=VMEM_LIMIT)


def _gelu(x):
    return x * (0.5 * (1.0 + jnp.tanh(math.sqrt(2.0 / math.pi) * (x + 0.044715 * (x * x * x)))))


def _sigmoid(x):
    return 0.5 * jnp.tanh(0.5 * x) + 0.5


def _norm_mod(x, g, sc, sh):
    ms = jnp.mean(x * x, axis=-1, keepdims=True)
    return ((x * lax.rsqrt(ms + EPS)) * g) * (1.0 + sc) + sh


def _mod_kernel(c_ref, w_ref, b_ref, o_ref):
    c = c_ref[...]
    s = (c * jax.nn.sigmoid(c)).astype(BF16)
    o_ref[...] = jnp.dot(s, w_ref[...].astype(BF16), preferred_element_type=F32) + b_ref[...]


def ada_modulation_all(cond, ada_w, ada_b):
    n_layers, d, n = ada_w.shape
    tn = 1536
    return pl.pallas_call(
        _mod_kernel,
        grid=(n_layers, n // tn),
        in_specs=[pl.BlockSpec((SUBLANES, d), lambda l, j: (0, 0)),
                  pl.BlockSpec((None, d, tn), lambda l, j: (l, 0, j)),
                  pl.BlockSpec((None, 1, tn), lambda l, j: (l, 0, j))],
        out_specs=pl.BlockSpec((None, SUBLANES, tn), lambda l, j: (l, 0, j)),
        out_shape=jax.ShapeDtypeStruct((n_layers, SUBLANES, n), F32),
        compiler_params=_cparams("arbitrary", "arbitrary"),
        name="ada_mod",
    )(cond, ada_w, ada_b.reshape(n_layers, 1, n))


class _Rows:
    def __init__(self, n_prompt, n_sample, t_sample):
        self.n_prompt = n_prompt
        self.n_sample = n_sample
        self.t_sample = t_sample
        self.tm = min(ROW_TILE, n_prompt, t_sample)
        assert n_prompt % self.tm == 0 and t_sample % self.tm == 0
        self.prompt_blocks = n_prompt // self.tm
        self.sample_blocks = n_sample // self.tm

    def seg(self, i):
        r = i * self.tm
        return jnp.where(r < self.n_prompt, 0, 1 + lax.div(r - self.n_prompt, self.t_sample))


def _mod_spec(rows, layer, width, chunk, m_axis):
    def imap(*ids):
        return (layer * SUBLANES + rows.seg(ids[m_axis]), 0, chunk)
    return pl.BlockSpec((None, 1, width), imap)


def _gain_spec(layer, d):
    return pl.BlockSpec((None, 1, d), lambda *ids: (layer, 0, 0))


def _nm_kernel(x_ref, g_ref, sc_ref, sh_ref, w_ref, o_ref, wbf_ref):
    @pl.when(pl.program_id(1) == 0)
    def _():
        wbf_ref[...] = w_ref[...].astype(BF16)
    h = _norm_mod(x_ref[...], g_ref[...], sc_ref[...], sh_ref[...])
    o_ref[...] = jnp.dot(h.astype(BF16), wbf_ref[...], preferred_element_type=F32)


def norm_mod_matmul(x, gains, mod, rows, layer, w, wl):
    m, d = x.shape
    n = w.shape[2]
    tm = rows.tm
    tn = n
    return pl.pallas_call(
        _nm_kernel,
        grid=(n // tn, m // tm),
        in_specs=[pl.BlockSpec((tm, d), lambda j, i: (i, 0)),
                  _gain_spec(layer, d),
                  _mod_spec(rows, layer, d, 1, 1),
                  _mod_spec(rows, layer, d, 0, 1),
                  pl.BlockSpec((None, d, tn), lambda j, i: (wl, 0, j))],
        out_specs=pl.BlockSpec((tm, tn), lambda j, i: (i, j)),
        out_shape=jax.ShapeDtypeStruct((m, n), F32),
        scratch_shapes=[pltpu.VMEM((d, tn), BF16)],
        compiler_params=_cparams("arbitrary", "arbitrary"),
        name="norm_mod_matmul",
    )(x, gains, mod, mod, w)


def _norm_only_kernel(x_ref, g_ref, sc_ref, sh_ref, o_ref):
    o_ref[...] = _norm_mod(x_ref[...], g_ref[...], sc_ref[...], sh_ref[...])


def norm_mod(x, gains, mod, rows, layer):
    m, d = x.shape
    tm = rows.tm
    return pl.pallas_call(
        _norm_only_kernel,
        grid=(m // tm,),
        in_specs=[pl.BlockSpec((tm, d), lambda i: (i, 0)),
                  _gain_spec(layer, d),
                  _mod_spec(rows, layer, d, 1, 0),
                  _mod_spec(rows, layer, d, 0, 0)],
        out_specs=pl.BlockSpec((tm, d), lambda i: (i, 0)),
        out_shape=jax.ShapeDtypeStruct((m, d), F32),
        compiler_params=_cparams("arbitrary"),
        name="norm_mod",
    )(x, gains, mod, mod)


def _final_norm_kernel(x_ref, g_ref, op_ref, os_ref, *, npb):
    i = pl.program_id(0)
    x = x_ref[...]
    ms = jnp.mean(x * x, axis=-1, keepdims=True)
    y = (x * lax.rsqrt(ms + EPS)) * g_ref[...]

    @pl.when(i < npb)
    def _():
        op_ref[...] = y

    @pl.when(i >= npb)
    def _():
        os_ref[...] = y


def final_norm(x, g, rows):
    m, d = x.shape
    tm, npb, nsb = rows.tm, rows.prompt_blocks, rows.sample_blocks
    return pl.pallas_call(
        functools.partial(_final_norm_kernel, npb=npb),
        grid=(m // tm,),
        in_specs=[pl.BlockSpec((tm, d), lambda i: (i, 0)),
                  pl.BlockSpec((1, d), lambda i: (0, 0))],
        out_specs=[pl.BlockSpec((tm, d), lambda i: (jnp.minimum(i, npb - 1), 0)),
                   pl.BlockSpec((tm, d), lambda i: (jnp.clip(i - npb, 0, nsb - 1), 0))],
        out_shape=[jax.ShapeDtypeStruct((rows.n_prompt, d), F32), jax.ShapeDtypeStruct((rows.n_sample, d), F32)],
        compiler_params=_cparams("arbitrary"),
        name="final_norm",
    )(x, g.reshape(1, d))


def _router_epilogue(x_new, g_ref, sc_ref, sh_ref, rw_ref, h_ref, aff_ref):
    h = _norm_mod(x_new, g_ref[...], sc_ref[...], sh_ref[...])
    h_hi = h.astype(BF16)
    h_ref[...] = h_hi
    rw = rw_ref[...]
    rw_hi = rw.astype(BF16)
    rw_lo = (rw - rw_hi.astype(F32)).astype(BF16)
    ne = rw.shape[1]
    both = jnp.dot(h_hi, jnp.concatenate([rw_hi, rw_lo], axis=1), preferred_element_type=F32)
    logits = both[:, :ne] + both[:, ne:]
    e = jnp.exp(logits - jnp.max(logits, axis=-1, keepdims=True))
    aff_ref[...] = e / jnp.sum(e, axis=-1, keepdims=True)


def _mmres_kernel(ap_ref, as_ref, w_ref, r_ref, gt_ref, g_ref, sc_ref, sh_ref, rw_ref,
                  o_ref, h_ref, aff_ref, wbf_ref, *, npb):
    i = pl.program_id(0)

    @pl.when(i == 0)
    def _():
        wbf_ref[...] = w_ref[...].astype(BF16)

    def emit(a_ref):
        acc = jnp.dot(a_ref[...].astype(BF16), wbf_ref[...], preferred_element_type=F32)
        x_new = r_ref[...] + gt_ref[...] * acc
        o_ref[...] = x_new
        _router_epilogue(x_new, g_ref, sc_ref, sh_ref, rw_ref, h_ref, aff_ref)

    pl.when(i < npb)(lambda: emit(ap_ref))
    pl.when(i >= npb)(lambda: emit(as_ref))


def _sublayer2_specs(rows, layer, d, ne):
    tm = rows.tm
    ins = [_gain_spec(layer, d), _mod_spec(rows, layer, d, 4, 0), _mod_spec(rows, layer, d, 3, 0),
           pl.BlockSpec((None, d, ne), lambda i: (layer, 0, 0))]
    outs = [pl.BlockSpec((tm, d), lambda i: (i, 0)), pl.BlockSpec((tm, ne), lambda i: (i, 0))]
    return ins, outs


def _two_group_specs(rows, k):
    npb, nsb, tm = rows.prompt_blocks, rows.sample_blocks, rows.tm
    return [pl.BlockSpec((tm, k), lambda i: (jnp.minimum(i, npb - 1), 0)),
            pl.BlockSpec((tm, k), lambda i: (jnp.clip(i - npb, 0, nsb - 1), 0))]


def matmul_gated_residual(a_p, a_s, w, wl, resid, mod, rows, layer, gains2, router_w):
    k = a_p.shape[1]
    m, d = resid.shape
    tm = rows.tm
    ne = router_w.shape[2]
    r_in, r_out = _sublayer2_specs(rows, layer, d, ne)
    return pl.pallas_call(
        functools.partial(_mmres_kernel, npb=rows.prompt_blocks),
        grid=(m // tm,),
        in_specs=_two_group_specs(rows, k) + [
            pl.BlockSpec((None, k, d), lambda i: (wl, 0, 0)),
            pl.BlockSpec((tm, d), lambda i: (i, 0)),
            _mod_spec(rows, layer, d, 2, 0)] + r_in,
        out_specs=[pl.BlockSpec((tm, d), lambda i: (i, 0))] + r_out,
        out_shape=[jax.ShapeDtypeStruct((m, d), F32), jax.ShapeDtypeStruct((m, d), BF16),
                   jax.ShapeDtypeStruct((m, ne), F32)],
        scratch_shapes=[pltpu.VMEM((k, d), BF16)],
        compiler_params=_cparams("arbitrary"),
        name="matmul_gated_residual",
    )(a_p, a_s, w, resid, mod, gains2, mod, mod, router_w)


def _glures_kernel(ap_ref, as_ref, wv_ref, wg_ref, r_ref, gt_ref, g_ref, sc_ref, sh_ref, rw_ref,
                   o_ref, h_ref, aff_ref, wv_bf, wg_bf, *, npb):
    i = pl.program_id(0)

    @pl.when(i == 0)
    def _():
        wv_bf[...] = wv_ref[...].astype(BF16)
        wg_bf[...] = wg_ref[...].astype(BF16)

    def emit(a_ref):
        a = a_ref[...].astype(BF16)
        v = jnp.dot(a, wv_bf[...], preferred_element_type=F32)
        g = jnp.dot(a, wg_bf[...], preferred_element_type=F32)
        x_new = r_ref[...] + gt_ref[...] * (v * jax.nn.sigmoid(g))
        o_ref[...] = x_new
        _router_epilogue(x_new, g_ref, sc_ref, sh_ref, rw_ref, h_ref, aff_ref)

    pl.when(i < npb)(lambda: emit(ap_ref))
    pl.when(i >= npb)(lambda: emit(as_ref))


def glu_gated_residual(a_p, a_s, w_glu, wl, resid, mod, rows, layer, gains2, router_w):
    k = a_p.shape[1]
    m, d = resid.shape
    tm = rows.tm
    ne = router_w.shape[2]
    r_in, r_out = _sublayer2_specs(rows, layer, d, ne)
    return pl.pallas_call(
        functools.partial(_glures_kernel, npb=rows.prompt_blocks),
        grid=(m // tm,),
        in_specs=_two_group_specs(rows, k) + [
            pl.BlockSpec((None, k, d), lambda i: (wl, 0, 0)),
            pl.BlockSpec((None, k, d), lambda i: (wl, 0, 1)),
            pl.BlockSpec((tm, d), lambda i: (i, 0)),
            _mod_spec(rows, layer, d, 2, 0)] + r_in,
        out_specs=[pl.BlockSpec((tm, d), lambda i: (i, 0))] + r_out,
        out_shape=[jax.ShapeDtypeStruct((m, d), F32), jax.ShapeDtypeStruct((m, d), BF16),
                   jax.ShapeDtypeStruct((m, ne), F32)],
        scratch_shapes=[pltpu.VMEM((k, d), BF16), pltpu.VMEM((k, d), BF16)],
        compiler_params=_cparams("arbitrary"),
        name="glu_gated_residual",
    )(a_p, a_s, w_glu, w_glu, resid, mod, gains2, mod, mod, router_w)


def _rglru_kernel(gate_ref, u_ref, cw_ref, cb_ref, wa_ref, ba_ref, wx_ref, bx_ref, lam_ref, h0_ref,
                  y_ref, fin_ref, af_s, bf_s, ab_s, bb_s, hf_s, hb_s):
    t, cw = u_ref.shape
    u = u_ref[...]
    row = lax.broadcasted_iota(jnp.int32, (t, cw), 0)

    def shifted(x, k):
        if k > 0:
            return jnp.where(row >= k, pltpu.roll(x, k, axis=0), 0.0)
        return jnp.where(row < t + k, pltpu.roll(x, t + k, axis=0), 0.0)

    cwv = cw_ref[...]
    uc = (cwv[0:1] * shifted(u, 2) + cwv[1:2] * shifted(u, 1) + cwv[2:3] * u
          + cwv[3:4] * shifted(u, -1) + cb_ref[...])

    a_scr = (af_s, ab_s)
    b_scr = (bf_s, bb_s)
    for k in range(2):
        nl = -lam_ref[k:k + 1, :]
        sp = jnp.maximum(nl, 0.0) + jnp.log1p(jnp.exp(-jnp.abs(nl)))
        for hh in range(cw // RG_BS):
            sl = slice(hh * RG_BS, (hh + 1) * RG_BS)
            uh = uc[:, sl]
            ub = uh.astype(BF16)
            r = _sigmoid(jnp.dot(ub, wa_ref[k, hh].astype(BF16), preferred_element_type=F32) + ba_ref[k:k + 1, sl])
            i = _sigmoid(jnp.dot(ub, wx_ref[k, hh].astype(BF16), preferred_element_type=F32) + bx_ref[k:k + 1, sl])
            log_a = (-RG_C * r) * sp[:, sl]
            a = jnp.exp(log_a)
            a_scr[k][:, sl] = a
            b_scr[k][:, sl] = jnp.sqrt(jnp.tanh(-log_a) * (a * a + 1.0)) * (i * uh)

    nblk = t // SUBLANES
    srow = lax.broadcasted_iota(jnp.int32, (SUBLANES, cw), 0)

    def body(n, carry):
        cf, cb = carry
        rf = pl.multiple_of(n * SUBLANES, SUBLANES)
        rb = pl.multiple_of((nblk - 1 - n) * SUBLANES, SUBLANES)
        a = af_s[pl.ds(rf, SUBLANES), :]
        b = bf_s[pl.ds(rf, SUBLANES), :]
        a2 = ab_s[pl.ds(rb, SUBLANES), :]
        b2 = bb_s[pl.ds(rb, SUBLANES), :]
        for s in (1, 2, 4):
            m = srow >= s
            b = jnp.where(m, a * pltpu.roll(b, s, axis=0) + b, b)
            a = jnp.where(m, a * pltpu.roll(a, s, axis=0), a)
            m2 = srow < SUBLANES - s
            b2 = jnp.where(m2, a2 * pltpu.roll(b2, SUBLANES - s, axis=0) + b2, b2)
            a2 = jnp.where(m2, a2 * pltpu.roll(a2, SUBLANES - s, axis=0), a2)
        hf = a * cf + b
        hb = a2 * cb + b2
        hf_s[pl.ds(rf, SUBLANES), :] = hf
        hb_s[pl.ds(rb, SUBLANES), :] = hb
        return hf[SUBLANES - 1:SUBLANES, :], hb[0:1, :]

    cf, cb = lax.fori_loop(0, nblk, body, (h0_ref[0:1, :], h0_ref[1:2, :]))
    fin_ref[0:1, :] = cf
    fin_ref[1:2, :] = cb
    y_ref[...] = (hf_s[...] + hb_s[...]) * _gelu(gate_ref[...])


def rglru_scan(gu, row0, n_seq, t, j, conv_w, conv_b, w_a, b_a, w_x, b_x, lam, h0, h0_j, *, cw=RG_TILE):
    r = gu.shape[1] // 2
    nh = cw // RG_BS
    blk0 = row0 // t
    nc = r // cw
    scr = [pltpu.VMEM((t, cw), F32) for _ in range(6)]
    vec2 = pl.BlockSpec((None, 2, cw), lambda b, c: (j, 0, c))
    gatew = pl.BlockSpec((None, 2, nh, RG_BS, RG_BS), lambda b, c: (j, 0, c, 0, 0))
    return pl.pallas_call(
        _rglru_kernel,
        grid=(n_seq, nc),
        in_specs=[pl.BlockSpec((t, cw), lambda b, c: (blk0 + b, c)),
                  pl.BlockSpec((t, cw), lambda b, c: (blk0 + b, nc + c)),
                  pl.BlockSpec((None, 4, cw), lambda b, c: (j, 0, c)),
                  pl.BlockSpec((None, 1, cw), lambda b, c: (j, 0, c)),
                  gatew, vec2, gatew, vec2, vec2,
                  pl.BlockSpec((None, None, 2, cw), lambda b, c: (b, h0_j, 0, c))],
        out_specs=[pl.BlockSpec((t, cw), lambda b, c: (b, c)),
                   pl.BlockSpec((None, 2, cw), lambda b, c: (b, 0, c))],
        out_shape=[jax.ShapeDtypeStruct((n_seq * t, r), F32),
                   jax.ShapeDtypeStruct((n_seq, 2, r), F32)],
        scratch_shapes=scr,
        compiler_params=_cparams("arbitrary", "arbitrary"),
        name="rglru_scan",
    )(gu, gu, conv_w, conv_b.reshape(conv_b.shape[0], 1, r), w_a, b_a, w_x, b_x, lam, h0)


def _s5_kernel(*refs, bp, n_seq, nc, gpb):
    ell, gw = S5_L, S5_H
    (x_ref, tm_ref, win_ref, wre_ref, wim_ref, ar_ref, ai_ref, d_ref, s0re_ref, s0im_ref,
     o_ref, fin_ref, xg_s, ure_s, uim_s, fre_s, fim_s, bre_s, bim_s) = refs
    m, mp = n_seq * nc, bp * nc
    lanes = x_ref.shape[1]
    token = lambda l: pl.ds(l, m, stride=ell)
    per_tile = lanes // gw
    half = ure_s.shape[2] // 2
    lane_grp = lax.shift_right_logical(lax.broadcasted_iota(jnp.int32, (m, lanes), 1), gw.bit_length() - 1)

    def perm(shape, chunk_major_axis):
        i = lax.broadcasted_iota(jnp.int32, shape, chunk_major_axis)
        j = lax.broadcasted_iota(jnp.int32, shape, 1 - chunk_major_axis)
        b = i & (bp - 1)
        c = lax.shift_right_logical(i, bp.bit_length() - 1)
        return jnp.where((j == b * nc + c) & (b < n_seq), 1.0, 0.0).astype(BF16)

    to_chunk_major = perm((mp, m), 0)
    to_batch_major = perm((m, mp), 1)

    def block_transpose(v):
        k = per_tile // 2
        while k >= 1:
            low = (lane_grp & k) == 0
            nxt = list(v)
            for i in range(per_tile):
                if i & k == 0:
                    a, b = v[i], v[i + k]
                    nxt[i] = jnp.where(low, a, pltpu.roll(b, k * gw, axis=1))
                    nxt[i + k] = jnp.where(low, pltpu.roll(a, lanes - k * gw, axis=1), b)
            v = nxt
            k //= 2
        return v

    for tile in range(ell // per_tile):
        by_group = block_transpose([x_ref[token(tile * per_tile + j), :] for j in range(per_tile)])
        for g in range(gpb):
            xg_s[g, :, tile * lanes:(tile + 1) * lanes] = by_group[g]

    for g in range(gpb):
        xg = xg_s[g]
        xp = jnp.dot(to_chunk_major, xg.astype(BF16), preferred_element_type=F32).astype(BF16)
        u = jnp.dot(xp, win_ref[g].astype(BF16), preferred_element_type=F32)
        ure_s[g] = u[:, :2 * half]
        uim_s[g] = u[:, 2 * half:]

    is_fwd = lax.broadcasted_iota(jnp.int32, (bp, 2 * half), 1) < half
    ar = [ar_ref[g] for g in range(gpb)]
    ai = [ai_ref[g] for g in range(gpb)]

    def body(k, carry):
        rf = pl.multiple_of(k * bp, bp)
        rb = pl.multiple_of((nc - 1 - k) * bp, bp)
        out = []
        for g in range(gpb):
            re, im = carry[2 * g], carry[2 * g + 1]
            fre_s[g, pl.ds(rf, bp), :] = re
            fim_s[g, pl.ds(rf, bp), :] = im
            bre_s[g, pl.ds(rb, bp), :] = re
            bim_s[g, pl.ds(rb, bp), :] = im
            ure = jnp.where(is_fwd, ure_s[g, pl.ds(rf, bp), :], ure_s[g, pl.ds(rb, bp), :])
            uim = jnp.where(is_fwd, uim_s[g, pl.ds(rf, bp), :], uim_s[g, pl.ds(rb, bp), :])
            out += [ar[g] * re - ai[g] * im + ure, ar[g] * im + ai[g] * re + uim]
        return tuple(out)

    init = tuple(r[g] for g in range(gpb) for r in (s0re_ref, s0im_ref))
    fin = lax.fori_loop(0, nc, body, init)
    fwd_all = lax.broadcasted_iota(jnp.int32, (mp, 2 * half), 1) < half
    for g in range(gpb):
        fin_ref[g, :, :2 * half] = fin[2 * g]
        fin_ref[g, :, 2 * half:] = fin[2 * g + 1]
        hre = jnp.where(fwd_all, fre_s[g], bre_s[g]).astype(BF16)
        him = jnp.where(fwd_all, fim_s[g], bim_s[g]).astype(BF16)
        hre = jnp.dot(to_batch_major, hre, preferred_element_type=F32).astype(BF16)
        him = jnp.dot(to_batch_major, him, preferred_element_type=F32).astype(BF16)
        xg = xg_s[g]
        y = (jnp.dot(xg.astype(BF16), tm_ref[g].astype(BF16), preferred_element_type=F32)
             + jnp.dot(hre, wre_ref[g].astype(BF16), preferred_element_type=F32)
             + jnp.dot(him, wim_ref[g].astype(BF16), preferred_element_type=F32)
             + d_ref[g] * xg)
        xg_s[g] = _gelu(y)

    for tile in range(ell // per_tile):
        by_token = block_transpose([xg_s[g, :, tile * lanes:(tile + 1) * lanes] for g in range(gpb)])
        for j in range(per_tile):
            o_ref[token(tile * per_tile + j), :] = by_token[j]


def s5_chunked(hn, row_blk, n_seq, nc, mats, s0re, s0im, *, bp):
    tmat, win, wre, wim, ar, ai, dg = mats
    g, w, _ = tmat.shape
    p2 = ar.shape[-1]
    ell = S5_L
    lanes = 128
    gpb = lanes // S5_H
    d = hn.shape[1]
    m, mp = n_seq * nc, bp * nc
    n = m * ell
    assert bp & (bp - 1) == 0 and S5_H & (S5_H - 1) == 0 and g % gpb == 0
    blk = lambda shape: pl.BlockSpec((gpb,) + shape, lambda i: (i, 0, 0))
    return pl.pallas_call(
        functools.partial(_s5_kernel, bp=bp, n_seq=n_seq, nc=nc, gpb=gpb),
        grid=(g // gpb,),
        in_specs=[pl.BlockSpec((n, lanes), lambda i: (row_blk, i)),
                  blk((w, w)), blk((w, 2 * p2)), blk((p2, w)), blk((p2, w)),
                  blk((1, p2)), blk((1, p2)), blk((1, w)), blk((bp, p2)), blk((bp, p2))],
        out_specs=[pl.BlockSpec((n, lanes), lambda i: (0, i)), blk((bp, 2 * p2))],
        out_shape=[jax.ShapeDtypeStruct((n, d), F32), jax.ShapeDtypeStruct((g, bp, 2 * p2), F32)],
        scratch_shapes=[pltpu.VMEM((gpb, m, w), F32)] + [pltpu.VMEM((gpb, mp, p2), F32) for _ in range(6)],
        compiler_params=_cparams("arbitrary"),
        name="s5_chunked",
    )(hn, tmat, win, wre, wim, ar, ai, dg, s0re, s0im)


def _cmul(ar, ai, br, bi):
    return ar * br - ai * bi, ar * bi + ai * br


def _s5_prep_kernel(are_ref, aim_ref, ldt_ref, btr_ref, bti_ref, cr_ref, ci_ref,
                    tm_ref, win_ref, wre_ref, wim_ref, ar_ref, ai_ref):
    for gi in range(cr_ref.shape[0]):
        _s5_prep_group(*(r.at[gi] for r in (are_ref, aim_ref, ldt_ref, btr_ref, bti_ref, cr_ref, ci_ref,
                                           tm_ref, win_ref, wre_ref, wim_ref, ar_ref, ai_ref)))


def _s5_prep_group(are_ref, aim_ref, ldt_ref, btr_ref, bti_ref, cr_ref, ci_ref,
                   tm_ref, win_ref, wre_ref, wim_ref, ar_ref, ai_ref):
    ell = S5_L
    h, p2 = cr_ref.shape
    w = ell * h
    a_re, a_im = are_ref[...], aim_ref[...]
    dt = jnp.exp(ldt_ref[...])
    steps = lax.broadcasted_iota(jnp.int32, (3 * SUBLANES, p2), 0).astype(F32)
    mag = jnp.exp(steps * (a_re * dt))
    ang = steps * (a_im * dt)
    pw_r, pw_i = mag * jnp.cos(ang), mag * jnp.sin(ang)
    nr, ni = pw_r[1:2] - 1.0, pw_i[1:2]
    den = a_re * a_re + a_im * a_im
    qr, qi = (nr * a_re + ni * a_im) / den, (ni * a_re - nr * a_im) / den
    bb_r, bb_i = _cmul(qr, qi, btr_ref[...], bti_ref[...])
    c_r, c_i = cr_ref[...], ci_ref[...]
    fwd = lax.broadcasted_iota(jnp.int32, (1, p2), 1) < p2 // 2

    def power_rows(m_fwd, m_bwd):
        return (jnp.where(fwd, pw_r[m_fwd:m_fwd + 1], pw_r[m_bwd:m_bwd + 1]),
                jnp.where(fwd, pw_i[m_fwd:m_fwd + 1], pw_i[m_bwd:m_bwd + 1]))

    def stack(x_r, x_i, powers):
        parts = [_cmul(x_r, x_i, *power_rows(*powers(l))) for l in range(ell)]
        return (jnp.concatenate([q[0] for q in parts], axis=0), jnp.concatenate([q[1] for q in parts], axis=0))

    win_r, win_i = stack(bb_r, bb_i, lambda l: (ell - 1 - l, l))
    win_ref[...] = jnp.concatenate([win_r, win_i], axis=1).astype(win_ref.dtype)
    z_r, z_i = stack(c_r, c_i, lambda l: (l + 1, ell - l))
    wre_ref[...] = z_r.T.astype(wre_ref.dtype)
    wim_ref[...] = (-z_i).T.astype(wim_ref.dtype)
    k_r, k_i = stack(c_r, c_i, lambda m: (m, ell - 1 - m))
    mask_f = jnp.where(fwd, 1.0, 0.0)

    def lag_rows(mask):
        return (lax.dot_general(bb_r * mask, k_r, NT_DIMS, precision=HIGHEST, preferred_element_type=F32)
                - lax.dot_general(bb_i * mask, k_i, NT_DIMS, precision=HIGHEST, preferred_element_type=F32))

    kf = lag_rows(mask_f)
    kb = lag_rows(1.0 - mask_f)
    lane = lax.broadcasted_iota(jnp.int32, (h, w), 1)
    blocks = []
    for li in range(ell):
        f_part = kf if li == 0 else pltpu.roll(kf, li * h, axis=1)
        s_b = (w - (ell - 1 - li) * h) % w
        b_part = kb if s_b == 0 else pltpu.roll(kb, s_b, axis=1)
        blocks.append(jnp.where(lane >= li * h, f_part, 0.0) + jnp.where(lane < (li + 1) * h, b_part, 0.0))
    tm_ref[...] = jnp.concatenate(blocks, axis=0).astype(tm_ref.dtype)
    ar_ref[...] = pw_r[ell:ell + 1]
    ai_ref[...] = pw_i[ell:ell + 1]


def s5_chunk_operators(a_re, a_im, log_dt, b_re, b_im, c_re, c_im, d):
    _, g, p = a_re.shape
    h = b_re.shape[-1]
    w = S5_L * h
    two_dir = lambda x: jnp.transpose(x, (1, 0, 2)).reshape(g, 1, 2 * p)
    ldt = two_dir(jnp.broadcast_to(log_dt[:, :, None], (2, g, p)))
    bt = lambda x: jnp.transpose(x, (1, 3, 0, 2)).reshape(g, h, 2 * p)
    ct = lambda x: jnp.transpose(x, (1, 2, 0, 3)).reshape(g, h, 2 * p)
    gps = math.gcd(g, S5_PREP_GROUPS)
    per_g = lambda shape: pl.BlockSpec((gps,) + shape, lambda i: (i, 0, 0))
    tmat, win, wre, wim, ar, ai = pl.pallas_call(
        _s5_prep_kernel,
        grid=(g // gps,),
        in_specs=[per_g((1, 2 * p))] * 3 + [per_g((h, 2 * p))] * 4,
        out_specs=[per_g((w, w)), per_g((w, 4 * p)), per_g((2 * p, w)), per_g((2 * p, w)),
                   per_g((1, 2 * p)), per_g((1, 2 * p))],
        out_shape=[jax.ShapeDtypeStruct((g, w, w), BF16), jax.ShapeDtypeStruct((g, w, 4 * p), BF16),
                   jax.ShapeDtypeStruct((g, 2 * p, w), BF16), jax.ShapeDtypeStruct((g, 2 * p, w), BF16),
                   jax.ShapeDtypeStruct((g, 1, 2 * p), F32), jax.ShapeDtypeStruct((g, 1, 2 * p), F32)],
        compiler_params=_cparams("arbitrary"),
        name="s5_chunk_operators",
    )(two_dir(a_re), two_dir(a_im), ldt, bt(b_re), bt(b_im), ct(c_re), ct(c_im))
    dg = jnp.tile(d.reshape(g, 1, h), (1, S5_L, 1)).reshape(g, 1, w)
    return tmat, win, wre, wim, ar, ai, dg


def s5_mixer_group(hn, row0, n_seq, t, mats, s0):
    ar = mats[4]
    g = ar.shape[0]
    p = ar.shape[-1] // 2
    nc = t // S5_L
    bp = -(-n_seq // SUBLANES) * SUBLANES
    n = n_seq * t
    assert row0 % n == 0
    if s0 is None:
        s0re = jnp.zeros((g, bp, 2 * p), F32)
        s0im = s0re
    else:
        st = jnp.transpose(s0, (3, 0, 2, 1, 4)).reshape(g, n_seq, 2, 2 * p)
        st = jnp.pad(st, ((0, 0), (0, bp - n_seq), (0, 0), (0, 0)))
        s0re, s0im = st[:, :, 0], st[:, :, 1]
    u, fin = s5_chunked(hn, row0 // n, n_seq, nc, mats, s0re, s0im, bp=bp)
    fin = fin.reshape(g, bp, 2, 2, p)[:, :n_seq]
    return u, jnp.transpose(fin, (1, 3, 2, 0, 4))


def _softmax_pv(scores, values, sink):
    m = sink
    for s in scores:
        m = jnp.maximum(m, jnp.max(s, axis=-1, keepdims=True))
    den = jnp.exp(sink - m)
    acc = None
    for s, v in zip(scores, values):
        p = jnp.exp(s - m)
        den = den + jnp.sum(p, axis=-1, keepdims=True)
        pv = jnp.dot(p.astype(BF16), v, preferred_element_type=F32)
        acc = pv if acc is None else acc + pv
    return acc / den


def _attn_prompt_kernel(sink_ref, q_ref, k_ref, v_ref, o_ref):
    k = k_ref[...].astype(BF16)
    v = v_ref[...].astype(BF16)
    for h in range(N_KV):
        hs = slice(h * HEAD_DIM, (h + 1) * HEAD_DIM)
        kh, vh = k[:, hs], v[:, hs]
        for g in range(Q_PER_KV):
            c0 = (h * Q_PER_KV + g) * HEAD_DIM
            qg = (q_ref[:, c0:c0 + HEAD_DIM] * ATT_SCALE).astype(BF16)
            s = lax.dot_general(qg, kh, NT_DIMS, preferred_element_type=F32)
            o_ref[:, c0:c0 + HEAD_DIM] = _softmax_pv([s], [vh], sink_ref[h * Q_PER_KV + g])


def attn_prompt(qkv, sink, n_seq, t):
    dq = N_HEADS * HEAD_DIM
    kcol = dq // KV_W
    return pl.pallas_call(
        _attn_prompt_kernel,
        grid=(n_seq,),
        in_specs=[pl.BlockSpec(memory_space=pltpu.SMEM),
                  pl.BlockSpec((t, dq), lambda b: (b, 0)),
                  pl.BlockSpec((t, KV_W), lambda b: (b, kcol)),
                  pl.BlockSpec((t, KV_W), lambda b: (b, kcol + 1))],
        out_specs=pl.BlockSpec((t, dq), lambda b: (b, 0)),
        out_shape=jax.ShapeDtypeStruct((n_seq * t, dq), F32),
        compiler_params=_cparams("arbitrary"),
        name="attn_prompt",
    )(sink, qkv, qkv, qkv)


def _rope(x, cos, sin):
    w = x.shape[1]
    low = (lax.broadcasted_iota(jnp.int32, x.shape, 1) & (HEAD_DIM // 4)) == 0
    partner = jnp.where(low, pltpu.roll(x, w - HEAD_DIM // 4, axis=1), pltpu.roll(x, HEAD_DIM // 4, axis=1))
    return x * cos + partner * sin


def _attn_sample_kernel(sink_ref, q_ref, k_ref, v_ref, kc_ref, vc_ref, cos_ref, sin_ref, o_ref,
                        kw_s, vw_s, kc_s, vc_s, *, t):
    n = pl.program_id(1)
    blk = ATT_BLOCK

    @pl.when(n == 0)
    def _():
        zeros = jnp.zeros((blk, KV_W), BF16)
        kw_s[0:blk, :] = zeros
        vw_s[0:blk, :] = zeros
        kw_s[blk + t:2 * blk + t, :] = zeros
        vw_s[blk + t:2 * blk + t, :] = zeros
        kw_s[blk:blk + t, :] = _rope(k_ref[...], cos_ref[...], sin_ref[...]).astype(BF16)
        vw_s[blk:blk + t, :] = v_ref[...].astype(BF16)
        kc_s[...] = kc_ref[...].astype(BF16)
        vc_s[...] = vc_ref[...].astype(BF16)

    r0 = pl.multiple_of(n * blk, blk)
    cq = cos_ref[pl.ds(r0, blk), :]
    sq = sin_ref[pl.ds(r0, blk), :]
    kw = kw_s[pl.ds(r0, 3 * blk), :]
    vw = vw_s[pl.ds(r0, 3 * blk), :]
    qi = lax.broadcasted_iota(jnp.int32, (blk, 3 * blk), 0)
    kj = lax.broadcasted_iota(jnp.int32, (blk, 3 * blk), 1)
    kpos = n * blk - blk + kj
    valid = (jnp.abs(kj - blk - qi) <= WINDOW) & (kpos >= 0) & (kpos < t)
    for h in range(N_KV):
        hs = slice(h * HEAD_DIM, (h + 1) * HEAD_DIM)
        qh = (_rope(q_ref[:, h * KV_W:(h + 1) * KV_W], cq, sq) * ATT_SCALE).astype(BF16)
        kh, vh, kch, vch = kw[:, hs], vw[:, hs], kc_s[:, hs], vc_s[:, hs]
        for g in range(Q_PER_KV):
            qg = qh[:, g * HEAD_DIM:(g + 1) * HEAD_DIM]
            s_loc = lax.dot_general(qg, kh, NT_DIMS, preferred_element_type=F32)
            s_loc = jnp.where(valid, s_loc, NEG_INF)
            s_ctx = lax.dot_general(qg, kch, NT_DIMS, preferred_element_type=F32)
            c0 = (h * Q_PER_KV + g) * HEAD_DIM
            o_ref[:, c0:c0 + HEAD_DIM] = _softmax_pv([s_loc, s_ctx], [vh, vch], sink_ref[h * Q_PER_KV + g])


def _rope_tables(t):
    quarter = HEAD_DIM // 4
    freqs = ROPE_BASE ** (-jnp.arange(quarter, dtype=F32) / quarter)
    pos = jnp.arange(t)
    ang_r = (pos // GRID_W).astype(F32)[:, None] * freqs
    ang_c = (pos % GRID_W).astype(F32)[:, None] * freqs
    cos = jnp.concatenate([jnp.cos(ang_r), jnp.cos(ang_r), jnp.cos(ang_c), jnp.cos(ang_c)], axis=-1)
    sin = jnp.concatenate([-jnp.sin(ang_r), jnp.sin(ang_r), -jnp.sin(ang_c), jnp.sin(ang_c)], axis=-1)
    return jnp.tile(cos, (1, N_KV)), jnp.tile(sin, (1, N_KV))


def attn_sample(qkv, row0, sink, n_seq, t, k_ctx, v_ctx):
    dq = N_HEADS * HEAD_DIM
    kcol = dq // KV_W
    nb = t // ATT_BLOCK
    lc = k_ctx.shape[1]
    cos, sin = _rope_tables(t)
    qblk0, sblk0 = row0 // ATT_BLOCK, row0 // t
    return pl.pallas_call(
        functools.partial(_attn_sample_kernel, t=t),
        grid=(n_seq, nb),
        in_specs=[pl.BlockSpec(memory_space=pltpu.SMEM),
                  pl.BlockSpec((ATT_BLOCK, dq), lambda b, n: (qblk0 + b * nb + n, 0)),
                  pl.BlockSpec((t, KV_W), lambda b, n: (sblk0 + b, kcol)),
                  pl.BlockSpec((t, KV_W), lambda b, n: (sblk0 + b, kcol + 1)),
                  pl.BlockSpec((None, lc, KV_W), lambda b, n: (b, 0, 0)),
                  pl.BlockSpec((None, lc, KV_W), lambda b, n: (b, 0, 0)),
                  pl.BlockSpec((t, KV_W), lambda b, n: (0, 0)),
                  pl.BlockSpec((t, KV_W), lambda b, n: (0, 0))],
        out_specs=pl.BlockSpec((ATT_BLOCK, dq), lambda b, n: (b * nb + n, 0)),
        out_shape=jax.ShapeDtypeStruct((n_seq * t, dq), F32),
        scratch_shapes=[pltpu.VMEM((t + 2 * ATT_BLOCK, KV_W), BF16), pltpu.VMEM((t + 2 * ATT_BLOCK, KV_W), BF16),
                        pltpu.VMEM((lc, KV_W), BF16), pltpu.VMEM((lc, KV_W), BF16)],
        compiler_params=_cparams("arbitrary", "arbitrary"),
        name="attn_sample",
    )(sink, qkv, qkv, qkv, k_ctx, v_ctx, cos, sin)


def _moe_kernel(x_ref, wg_ref, wu_ref, wd_ref, gt_ref, seg_ref, g2_ref, o_ref, wg_bf, wu_bf, wd_bf, *, n_seg):
    wg_bf[...] = wg_ref[...].astype(BF16)
    wu_bf[...] = wu_ref[...].astype(BF16)
    wd_bf[...] = wd_ref[...].astype(BF16)
    rsub = min(MOE_ROW_SUB, x_ref.shape[0])
    for r in range(x_ref.shape[0] // rsub):
        rs = slice(r * rsub, (r + 1) * rsub)
        x = x_ref[rs, :]
        hg = jnp.dot(x, wg_bf[...], preferred_element_type=F32)
        hu = jnp.dot(x, wu_bf[...], preferred_element_type=F32)
        he = ((hg * jax.nn.sigmoid(hg)) * hu).astype(BF16)
        y = jnp.dot(he, wd_bf[...], preferred_element_type=F32)
        seg = seg_ref[rs, :]
        g2 = jnp.zeros(y.shape, F32)
        for s in range(n_seg):
            g2 = jnp.where(seg == s, g2_ref[s:s + 1, :], g2)
        o_ref[rs, :] = y * (gt_ref[rs, :] * g2)


def moe_experts(xe, w_gate, w_up, w_down, layer, gates, seg, mod, n_seg):
    ne, r, d = xe.shape
    dff = w_gate.shape[3]
    return pl.pallas_call(
        functools.partial(_moe_kernel, n_seg=n_seg),
        grid=(ne,),
        in_specs=[pl.BlockSpec((None, r, d), lambda e: (e, 0, 0)),
                  pl.BlockSpec((None, None, d, dff), lambda e: (layer, e, 0, 0)),
                  pl.BlockSpec((None, None, d, dff), lambda e: (layer, e, 0, 0)),
                  pl.BlockSpec((None, None, dff, d), lambda e: (layer, e, 0, 0)),
                  pl.BlockSpec((None, r, 1), lambda e: (e, 0, 0)),
                  pl.BlockSpec((None, r, 1), lambda e: (e, 0, 0)),
                  pl.BlockSpec((SUBLANES, None, d), lambda e: (layer, 0, 5))],
        out_specs=pl.BlockSpec((None, r, d), lambda e: (e, 0, 0)),
        out_shape=jax.ShapeDtypeStruct((ne, r, d), F32),
        scratch_shapes=[pltpu.VMEM((d, dff), BF16), pltpu.VMEM((d, dff), BF16), pltpu.VMEM((dff, d), BF16)],
        compiler_params=_cparams("arbitrary"),
        name="moe_experts",
    )(xe, w_gate, w_up, w_down, gates, seg, mod)


def _expert_choice(aff, rows):
    ne = aff.shape[1]
    sizes = (rows.n_prompt, rows.n_sample)
    caps = [(EC_FACTOR * n) // ne for n in sizes]
    if sizes[0] == sizes[1]:
        n = sizes[0]
        gt, ix = lax.top_k(jnp.swapaxes(aff.reshape(2, n, ne), 1, 2), caps[0])
        ix = ix + jnp.array([0, n], jnp.int32)[:, None, None]
        return (jnp.concatenate([gt[0], gt[1]], axis=1), jnp.concatenate([ix[0], ix[1]], axis=1))
    gts, ixs, off = [], [], 0
    for n, cap in zip(sizes, caps):
        gt, ix = lax.top_k(aff[off:off + n].T, cap)
        gts.append(gt)
        ixs.append(ix + off)
        off += n
    return jnp.concatenate(gts, axis=1), jnp.concatenate(ixs, axis=1)


def _combine_kernel(idx_ref, x_hbm, ye_ref, o_hbm, acc, sem, *, n_grp, rows_per):
    g = pl.program_id(0)
    e = pl.program_id(1)
    last = pl.num_programs(1) - 1

    def load(grp):
        return pltpu.make_async_copy(x_hbm.at[pl.ds(grp * n_grp, n_grp), :], acc.at[grp], sem.at[grp])

    def store(grp):
        return pltpu.make_async_copy(acc.at[grp], o_hbm.at[pl.ds(grp * n_grp, n_grp), :], sem.at[2 + grp])

    @pl.when((g == 0) & (e == 0))
    def _():
        load(0).start()
        load(1).start()

    @pl.when(e == 0)
    def _():
        load(g).wait()

    base = (2 * e + g) * rows_per
    first = g * n_grp

    def body(i, carry):
        toks = [idx_ref[base + i * COMBINE_UNROLL + k] - first for k in range(COMBINE_UNROLL)]
        old = [acc[g, pl.ds(t, 1), :] for t in toks]
        add = [ye_ref[pl.ds(i * COMBINE_UNROLL + k, 1), :] for k in range(COMBINE_UNROLL)]
        for t, a, b in zip(toks, old, add):
            acc[g, pl.ds(t, 1), :] = a + b
        return carry

    lax.fori_loop(0, rows_per // COMBINE_UNROLL, body, 0)

    @pl.when(e == last)
    def _():
        store(g).start()

    @pl.when((g == 1) & (e == last))
    def _():
        store(0).wait()
        store(1).wait()


def moe_combine(x, ye, idx, n_grp):
    m, d = x.shape
    ne, r, _ = ye.shape
    rows_per = r // 2
    assert m == 2 * n_grp and rows_per % COMBINE_UNROLL == 0
    grid_spec = pltpu.PrefetchScalarGridSpec(
        num_scalar_prefetch=1,
        grid=(2, ne),
        in_specs=[pl.BlockSpec(memory_space=pl.ANY),
                  pl.BlockSpec((None, rows_per, d), lambda g, e, ix: (e, g, 0))],
        out_specs=pl.BlockSpec(memory_space=pl.ANY),
        scratch_shapes=[pltpu.VMEM((2, n_grp, d), F32), pltpu.SemaphoreType.DMA((4,))])
    return pl.pallas_call(
        functools.partial(_combine_kernel, n_grp=n_grp, rows_per=rows_per),
        grid_spec=grid_spec,
        out_shape=jax.ShapeDtypeStruct((m, d), F32),
        compiler_params=_cparams("arbitrary", "arbitrary"),
        name="moe_combine",
    )(idx.reshape(-1), x, ye)


def moe_layer(x, h2, aff, mod, rows, layer, w_gate, w_up, w_down, n_seg):
    m, d = x.shape
    gates, idx = _expert_choice(aff, rows)
    seg = jnp.where(idx < rows.n_prompt, 0, 1 + (idx - rows.n_prompt) // rows.t_sample)
    xe = h2[idx]
    ye = moe_experts(xe, w_gate, w_up, w_down, layer, gates[..., None], seg[..., None], mod, n_seg)
    if rows.n_prompt == rows.n_sample:
        return moe_combine(x, ye, idx, rows.n_prompt)
    return x.at[idx.reshape(-1)].add(ye.reshape(-1, d))


def kernel(x_prompt, x_sample, state_rglru, state_s5, cache_k, cache_v, c, c_ctx, ada_w, ada_b, norm1_g, norm2_g, rg_w_in, rg_conv_w, rg_conv_b, rg_w_a, rg_b_a, rg_w_x, rg_b_x, rg_lambda, rg_w_out, s5_a_re, s5_a_im, s5_log_dt, s5_b_re, s5_b_im, s5_c_re, s5_c_im, s5_d, s5_w_glu, attn_w_qkv, attn_w_o, attn_sink, router_w, moe_w_gate, moe_w_up, moe_w_down, final_norm_g):
    bp_, tp, d = x_prompt.shape
    bs, ts, _ = x_sample.shape
    n_p, n_s = bp_ * tp, bs * ts
    depth = ada_w.shape[0]
    rows = _Rows(n_p, n_s, ts)
    assert bs + 1 <= SUBLANES and n_p % ts == 0

    x = jnp.concatenate([x_prompt.reshape(n_p, d), x_sample.reshape(n_s, d)], axis=0)
    cond = jnp.concatenate([c_ctx[None, :], c, jnp.zeros((SUBLANES - 1 - bs, d), F32)], axis=0)
    mod_all = ada_modulation_all(cond, ada_w, ada_b)
    mod = mod_all.reshape(depth * SUBLANES, 1, 6 * d)
    g1 = norm1_g.reshape(depth, 1, d)
    g2 = norm2_g.reshape(depth, 1, d)

    new_rg, new_s5, new_k, new_v = [], [], [], []
    for l in range(depth):
        kind, j = l % 3, l // 3
        if kind == 0:
            gu = norm_mod_matmul(x, g1, mod, rows, l, rg_w_in, j)
            args = (j, rg_conv_w, rg_conv_b, rg_w_a, rg_b_a, rg_w_x, rg_b_x, rg_lambda)
            r = gu.shape[1] // 2
            yp, fin = rglru_scan(gu, 0, bp_, tp, *args, jnp.zeros((bp_, 1, 2, r), F32), 0)
            ys, _ = rglru_scan(gu, n_p, bs, ts, *args, state_rglru, j)
            new_rg.append(fin)
            x, h2, aff = matmul_gated_residual(yp, ys, rg_w_out, j, x, mod, rows, l, g2, router_w)
        elif kind == 1:
            hn = norm_mod(x, g1, mod, rows, l)
            mats = s5_chunk_operators(s5_a_re[j], s5_a_im[j], s5_log_dt[j], s5_b_re[j], s5_b_im[j],
                                      s5_c_re[j], s5_c_im[j], s5_d[j])
            up, st = s5_mixer_group(hn, 0, bp_, tp, mats, None)
            us, _ = s5_mixer_group(hn, n_p, bs, ts, mats, state_s5[:, j])
            new_s5.append(st)
            x, h2, aff = glu_gated_residual(up, us, s5_w_glu, j, x, mod, rows, l, g2, router_w)
        else:
            qkv = norm_mod_matmul(x, g1, mod, rows, l, attn_w_qkv, j)
            dq = N_HEADS * HEAD_DIM
            new_k.append(qkv[:n_p, dq:dq + KV_W].reshape(bp_, tp, N_KV, HEAD_DIM))
            new_v.append(qkv[:n_p, dq + KV_W:].reshape(bp_, tp, N_KV, HEAD_DIM))
            op = attn_prompt(qkv, attn_sink[j], bp_, tp)
            lc = cache_k.shape[2]
            os_ = attn_sample(qkv, n_p, attn_sink[j], bs, ts,
                              cache_k[:, j].reshape(bs, lc, KV_W), cache_v[:, j].reshape(bs, lc, KV_W))
            x, h2, aff = matmul_gated_residual(op, os_, attn_w_o, j, x, mod, rows, l, g2, router_w)
        x = moe_layer(x, h2, aff, mod, rows, l, moe_w_gate, moe_w_up, moe_w_down, bs + 1)

    y_p, y_s = final_norm(x, final_norm_g, rows)
    return (y_p.reshape(bp_, tp, d), y_s.reshape(bs, ts, d),
            jnp.stack(new_rg, axis=1), jnp.stack(new_s5, axis=1),
            jnp.stack(new_k, axis=1), jnp.stack(new_v, axis=1))
```

```python
import functools
import math

import jax
import jax.numpy as jnp
from jax import lax
from jax.experimental import pallas as pl
from jax.experimental.pallas import tpu as pltpu

F32 = jnp.float32
BF16 = jnp.bfloat16
HIGHEST = lax.Precision.HIGHEST

EPS = 1e-6
RG_C = 8.0
RG_BS = 128
RG_TILE = 512
S5_H = 16
S5_L = 16
S5_PREP_GROUPS = 4
N_HEADS = 16
N_KV = 4
Q_PER_KV = N_HEADS // N_KV
HEAD_DIM = 64
KV_W = N_KV * HEAD_DIM
GRID_W = 64
WINDOW = 128
ATT_BLOCK = 128
ROPE_BASE = 10000.0
ATT_SCALE = HEAD_DIM ** -0.5
assert math.frexp(ATT_SCALE)[0] == 0.5
NEG_INF = -1e30
N_EXPERTS = 16
EC_FACTOR = 2
SUBLANES = 8
ROW_TILE = 512
MOE_ROW_SUB = 256
COMBINE_UNROLL = 8
VMEM_LIMIT = 56 * 1024 * 1024
NT_DIMS = (((1,), (1,)), ((), ()))


def _cparams(*sem):
    return pltpu.CompilerParams(dimension_semantics=sem, vmem_limit_bytes=VMEM_LIMIT)


def _gelu(x):
    return x * (0.5 * (1.0 + jnp.tanh(math.sqrt(2.0 / math.pi) * (x + 0.044715 * (x * x * x)))))


def _sigmoid(x):
    return 0.5 * jnp.tanh(0.5 * x) + 0.5


def _norm_mod(x, g, sc, sh):
    ms = jnp.mean(x * x, axis=-1, keepdims=True)
    return ((x * lax.rsqrt(ms + EPS)) * g) * (1.0 + sc) + sh


def _mod_kernel(c_ref, w_ref, b_ref, o_ref):
    c = c_ref[...]
    s = (c * jax.nn.sigmoid(c)).astype(BF16)
    o_ref[...] = jnp.dot(s, w_ref[...].astype(BF16), preferred_element_type=F32) + b_ref[...]


def ada_modulation_all(cond, ada_w, ada_b):
    n_layers, d, n = ada_w.shape
    tn = 1536
    return pl.pallas_call(
        _mod_kernel,
        grid=(n_layers, n // tn),
        in_specs=[pl.BlockSpec((SUBLANES, d), lambda l, j: (0, 0)),
                  pl.BlockSpec((None, d, tn), lambda l, j: (l, 0, j)),
                  pl.BlockSpec((None, 1, tn), lambda l, j: (l, 0, j))],
        out_specs=pl.BlockSpec((None, SUBLANES, tn), lambda l, j: (l, 0, j)),
        out_shape=jax.ShapeDtypeStruct((n_layers, SUBLANES, n), F32),
        compiler_params=_cparams("arbitrary", "arbitrary"),
        name="ada_mod",
    )(cond, ada_w, ada_b.reshape(n_layers, 1, n))


class _Rows:
    def __init__(self, n_prompt, n_sample, t_sample):
        self.n_prompt = n_prompt
        self.n_sample = n_sample
        self.t_sample = t_sample
        self.tm = min(ROW_TILE, n_prompt, t_sample)
        assert n_prompt % self.tm == 0 and t_sample % self.tm == 0
        self.prompt_blocks = n_prompt // self.tm
        self.sample_blocks = n_sample // self.tm

    def seg(self, i):
        r = i * self.tm
        return jnp.where(r < self.n_prompt, 0, 1 + lax.div(r - self.n_prompt, self.t_sample))


def _mod_spec(rows, layer, width, chunk, m_axis):
    def imap(*ids):
        return (layer * SUBLANES + rows.seg(ids[m_axis]), 0, chunk)
    return pl.BlockSpec((None, 1, width), imap)


def _gain_spec(layer, d):
    return pl.BlockSpec((None, 1, d), lambda *ids: (layer, 0, 0))


def _stream_specs(rows, xs, m_axis):
    top, _, bot0 = xs
    npb, nsb, tm = rows.prompt_blocks, rows.sample_blocks, rows.tm
    d = top.shape[1]
    return [pl.BlockSpec((tm, d), lambda *ids: (jnp.minimum(ids[m_axis], npb - 1), 0)),
            pl.BlockSpec((tm, d), lambda *ids: (bot0 + jnp.clip(ids[m_axis] - npb, 0, nsb - 1), 0))]


def _nm_kernel(xt_ref, xb_ref, g_ref, sc_ref, sh_ref, w_ref, o_ref, wbf_ref, *, npb):
    i = pl.program_id(1)

    @pl.when(i == 0)
    def _():
        wbf_ref[...] = w_ref[...].astype(BF16)
    x = jnp.where(i < npb, xt_ref[...], xb_ref[...])
    h = _norm_mod(x, g_ref[...], sc_ref[...], sh_ref[...])
    o_ref[...] = jnp.dot(h.astype(BF16), wbf_ref[...], preferred_element_type=F32)


def norm_mod_matmul(xs, gains, mod, rows, layer, w, wl):
    m, d = rows.n_prompt + rows.n_sample, xs[0].shape[1]
    n = w.shape[2]
    tm = rows.tm
    tn = n
    return pl.pallas_call(
        functools.partial(_nm_kernel, npb=rows.prompt_blocks),
        grid=(n // tn, m // tm),
        in_specs=_stream_specs(rows, xs, 1) + [
                  _gain_spec(layer, d),
                  _mod_spec(rows, layer, d, 1, 1),
                  _mod_spec(rows, layer, d, 0, 1),
                  pl.BlockSpec((None, d, tn), lambda j, i: (wl, 0, j))],
        out_specs=pl.BlockSpec((tm, tn), lambda j, i: (i, j)),
        out_shape=jax.ShapeDtypeStruct((m, n), F32),
        scratch_shapes=[pltpu.VMEM((d, tn), BF16)],
        compiler_params=_cparams("arbitrary", "arbitrary"),
        name="norm_mod_matmul",
    )(xs[0], xs[1], gains, mod, mod, w)


def _norm_only_kernel(x_ref, g_ref, sc_ref, sh_ref, o_ref):
    o_ref[...] = _norm_mod(x_ref[...], g_ref[...], sc_ref[...], sh_ref[...])


def norm_mod(x, gains, mod, rows, layer):
    m, d = x.shape
    tm = rows.tm
    return pl.pallas_call(
        _norm_only_kernel,
        grid=(m // tm,),
        in_specs=[pl.BlockSpec((tm, d), lambda i: (i, 0)),
                  _gain_spec(layer, d),
                  _mod_spec(rows, layer, d, 1, 0),
                  _mod_spec(rows, layer, d, 0, 0)],
        out_specs=pl.BlockSpec((tm, d), lambda i: (i, 0)),
        out_shape=jax.ShapeDtypeStruct((m, d), F32),
        compiler_params=_cparams("arbitrary"),
        name="norm_mod",
    )(x, gains, mod, mod)


def _final_norm_kernel(x_ref, g_ref, op_ref, os_ref, *, npb):
    i = pl.program_id(0)
    x = x_ref[...]
    ms = jnp.mean(x * x, axis=-1, keepdims=True)
    y = (x * lax.rsqrt(ms + EPS)) * g_ref[...]

    @pl.when(i < npb)
    def _():
        op_ref[...] = y

    @pl.when(i >= npb)
    def _():
        os_ref[...] = y


def final_norm(x, g, rows):
    m, d = x.shape
    tm, npb, nsb = rows.tm, rows.prompt_blocks, rows.sample_blocks
    return pl.pallas_call(
        functools.partial(_final_norm_kernel, npb=npb),
        grid=(m // tm,),
        in_specs=[pl.BlockSpec((tm, d), lambda i: (i, 0)),
                  pl.BlockSpec((1, d), lambda i: (0, 0))],
        out_specs=[pl.BlockSpec((tm, d), lambda i: (jnp.minimum(i, npb - 1), 0)),
                   pl.BlockSpec((tm, d), lambda i: (jnp.clip(i - npb, 0, nsb - 1), 0))],
        out_shape=[jax.ShapeDtypeStruct((rows.n_prompt, d), F32), jax.ShapeDtypeStruct((rows.n_sample, d), F32)],
        compiler_params=_cparams("arbitrary"),
        name="final_norm",
    )(x, g.reshape(1, d))


def _router_epilogue(x_new, g_ref, sc_ref, sh_ref, rw_ref, h_ref, aff_ref):
    h = _norm_mod(x_new, g_ref[...], sc_ref[...], sh_ref[...])
    h_hi = h.astype(BF16)
    h_ref[...] = h_hi
    h_lo = (h - h_hi.astype(F32)).astype(BF16)
    rw = rw_ref[...]
    rw_hi = rw.astype(BF16)
    rw_lo = (rw - rw_hi.astype(F32)).astype(BF16)
    ne = rw.shape[1]
    both = jnp.dot(h_hi, jnp.concatenate([rw_hi, rw_lo], axis=1), preferred_element_type=F32)
    logits = both[:, :ne] + (both[:, ne:] + jnp.dot(h_lo, rw_hi, preferred_element_type=F32))
    e = jnp.exp(logits - jnp.max(logits, axis=-1, keepdims=True))
    aff_ref[...] = e / jnp.sum(e, axis=-1, keepdims=True)


def _mmres_kernel(ap_ref, as_ref, w_ref, rt_ref, rb_ref, gt_ref, g_ref, sc_ref, sh_ref, rw_ref,
                  o_ref, h_ref, aff_ref, wbf_ref, *, npb):
    i = pl.program_id(0)

    @pl.when(i == 0)
    def _():
        wbf_ref[...] = w_ref[...].astype(BF16)

    def emit(a_ref, r_ref):
        acc = jnp.dot(a_ref[...].astype(BF16), wbf_ref[...], preferred_element_type=F32)
        x_new = r_ref[...] + gt_ref[...] * acc
        o_ref[...] = x_new
        _router_epilogue(x_new, g_ref, sc_ref, sh_ref, rw_ref, h_ref, aff_ref)

    pl.when(i < npb)(lambda: emit(ap_ref, rt_ref))
    pl.when(i >= npb)(lambda: emit(as_ref, rb_ref))


def _sublayer2_specs(rows, layer, d, ne):
    tm = rows.tm
    ins = [_gain_spec(layer, d), _mod_spec(rows, layer, d, 4, 0), _mod_spec(rows, layer, d, 3, 0),
           pl.BlockSpec((None, d, ne), lambda i: (layer, 0, 0))]
    outs = [pl.BlockSpec((tm, d), lambda i: (i, 0)), pl.BlockSpec((tm, ne), lambda i: (i, 0))]
    return ins, outs


def _two_group_specs(rows, k):
    npb, nsb, tm = rows.prompt_blocks, rows.sample_blocks, rows.tm
    return [pl.BlockSpec((tm, k), lambda i: (jnp.minimum(i, npb - 1), 0)),
            pl.BlockSpec((tm, k), lambda i: (jnp.clip(i - npb, 0, nsb - 1), 0))]


def matmul_gated_residual(a_p, a_s, w, wl, xs, mod, rows, layer, gains2, router_w):
    k = a_p.shape[1]
    m, d = rows.n_prompt + rows.n_sample, xs[0].shape[1]
    tm = rows.tm
    ne = router_w.shape[2]
    r_in, r_out = _sublayer2_specs(rows, layer, d, ne)
    return pl.pallas_call(
        functools.partial(_mmres_kernel, npb=rows.prompt_blocks),
        grid=(m // tm,),
        in_specs=_two_group_specs(rows, k) + [
            pl.BlockSpec((None, k, d), lambda i: (wl, 0, 0))] + _stream_specs(rows, xs, 0) + [
            _mod_spec(rows, layer, d, 2, 0)] + r_in,
        out_specs=[pl.BlockSpec((tm, d), lambda i: (i, 0))] + r_out,
        out_shape=[jax.ShapeDtypeStruct((m, d), F32), jax.ShapeDtypeStruct((m, d), BF16),
                   jax.ShapeDtypeStruct((m, ne), F32)],
        scratch_shapes=[pltpu.VMEM((k, d), BF16)],
        compiler_params=_cparams("arbitrary"),
        name="matmul_gated_residual",
    )(a_p, a_s, w, xs[0], xs[1], mod, gains2, mod, mod, router_w)


def _glures_kernel(ap_ref, as_ref, wv_ref, wg_ref, r_ref, gt_ref, g_ref, sc_ref, sh_ref, rw_ref,
                   o_ref, h_ref, aff_ref, wv_bf, wg_bf, *, npb):
    i = pl.program_id(0)

    @pl.when(i == 0)
    def _():
        wv_bf[...] = wv_ref[...].astype(BF16)
        wg_bf[...] = wg_ref[...].astype(BF16)

    def emit(a_ref):
        a = a_ref[...].astype(BF16)
        v = jnp.dot(a, wv_bf[...], preferred_element_type=F32)
        g = jnp.dot(a, wg_bf[...], preferred_element_type=F32)
        x_new = r_ref[...] + gt_ref[...] * (v * jax.nn.sigmoid(g))
        o_ref[...] = x_new
        _router_epilogue(x_new, g_ref, sc_ref, sh_ref, rw_ref, h_ref, aff_ref)

    pl.when(i < npb)(lambda: emit(ap_ref))
    pl.when(i >= npb)(lambda: emit(as_ref))


def glu_gated_residual(a_p, a_s, w_glu, wl, resid, mod, rows, layer, gains2, router_w):
    k = a_p.shape[1]
    m, d = resid.shape
    tm = rows.tm
    ne = router_w.shape[2]
    r_in, r_out = _sublayer2_specs(rows, layer, d, ne)
    return pl.pallas_call(
        functools.partial(_glures_kernel, npb=rows.prompt_blocks),
        grid=(m // tm,),
        in_specs=_two_group_specs(rows, k) + [
            pl.BlockSpec((None, k, d), lambda i: (wl, 0, 0)),
            pl.BlockSpec((None, k, d), lambda i: (wl, 0, 1)),
            pl.BlockSpec((tm, d), lambda i: (i, 0)),
            _mod_spec(rows, layer, d, 2, 0)] + r_in,
        out_specs=[pl.BlockSpec((tm, d), lambda i: (i, 0))] + r_out,
        out_shape=[jax.ShapeDtypeStruct((m, d), F32), jax.ShapeDtypeStruct((m, d), BF16),
                   jax.ShapeDtypeStruct((m, ne), F32)],
        scratch_shapes=[pltpu.VMEM((k, d), BF16), pltpu.VMEM((k, d), BF16)],
        compiler_params=_cparams("arbitrary"),
        name="glu_gated_residual",
    )(a_p, a_s, w_glu, w_glu, resid, mod, gains2, mod, mod, router_w)


def _rglru_kernel(gate_ref, u_ref, cw_ref, cb_ref, wa_ref, ba_ref, wx_ref, bx_ref, lam_ref, h0_ref,
                  y_ref, fin_ref, af_s, bf_s, ab_s, bb_s, hf_s, hb_s):
    t, cw = u_ref.shape
    u = u_ref[...]
    row = lax.broadcasted_iota(jnp.int32, (t, cw), 0)

    def shifted(x, k):
        if k > 0:
            return jnp.where(row >= k, pltpu.roll(x, k, axis=0), 0.0)
        return jnp.where(row < t + k, pltpu.roll(x, t + k, axis=0), 0.0)

    cwv = cw_ref[...]
    uc = (cwv[0:1] * shifted(u, 2) + cwv[1:2] * shifted(u, 1) + cwv[2:3] * u
          + cwv[3:4] * shifted(u, -1) + cb_ref[...])

    a_scr = (af_s, ab_s)
    b_scr = (bf_s, bb_s)
    for k in range(2):
        nl = -lam_ref[k:k + 1, :]
        sp = jnp.maximum(nl, 0.0) + jnp.log1p(jnp.exp(-jnp.abs(nl)))
        for hh in range(cw // RG_BS):
            sl = slice(hh * RG_BS, (hh + 1) * RG_BS)
            uh = uc[:, sl]
            ub = uh.astype(BF16)
            r = _sigmoid(jnp.dot(ub, wa_ref[k, hh].astype(BF16), preferred_element_type=F32) + ba_ref[k:k + 1, sl])
            i = _sigmoid(jnp.dot(ub, wx_ref[k, hh].astype(BF16), preferred_element_type=F32) + bx_ref[k:k + 1, sl])
            log_a = (-RG_C * r) * sp[:, sl]
            a = jnp.exp(log_a)
            a_scr[k][:, sl] = a
            b_scr[k][:, sl] = jnp.sqrt(jnp.tanh(-log_a) * (a * a + 1.0)) * (i * uh)

    nblk = t // SUBLANES
    srow = lax.broadcasted_iota(jnp.int32, (SUBLANES, cw), 0)

    def body(n, carry):
        cf, cb = carry
        rf = pl.multiple_of(n * SUBLANES, SUBLANES)
        rb = pl.multiple_of((nblk - 1 - n) * SUBLANES, SUBLANES)
        a = af_s[pl.ds(rf, SUBLANES), :]
        b = bf_s[pl.ds(rf, SUBLANES), :]
        a2 = ab_s[pl.ds(rb, SUBLANES), :]
        b2 = bb_s[pl.ds(rb, SUBLANES), :]
        for s in (1, 2, 4):
            m = srow >= s
            b = jnp.where(m, a * pltpu.roll(b, s, axis=0) + b, b)
            a = jnp.where(m, a * pltpu.roll(a, s, axis=0), a)
            m2 = srow < SUBLANES - s
            b2 = jnp.where(m2, a2 * pltpu.roll(b2, SUBLANES - s, axis=0) + b2, b2)
            a2 = jnp.where(m2, a2 * pltpu.roll(a2, SUBLANES - s, axis=0), a2)
        hf = a * cf + b
        hb = a2 * cb + b2
        hf_s[pl.ds(rf, SUBLANES), :] = hf
        hb_s[pl.ds(rb, SUBLANES), :] = hb
        return hf[SUBLANES - 1:SUBLANES, :], hb[0:1, :]

    cf, cb = lax.fori_loop(0, nblk, body, (h0_ref[0:1, :], h0_ref[1:2, :]))
    fin_ref[0:1, :] = cf
    fin_ref[1:2, :] = cb
    y_ref[...] = (hf_s[...] + hb_s[...]) * _gelu(gate_ref[...])


def rglru_scan(gu, row0, n_seq, t, j, conv_w, conv_b, w_a, b_a, w_x, b_x, lam, h0, h0_j, *, cw=RG_TILE):
    r = gu.shape[1] // 2
    nh = cw // RG_BS
    blk0 = row0 // t
    nc = r // cw
    scr = [pltpu.VMEM((t, cw), F32) for _ in range(6)]
    vec2 = pl.BlockSpec((None, 2, cw), lambda b, c: (j, 0, c))
    gatew = pl.BlockSpec((None, 2, nh, RG_BS, RG_BS), lambda b, c: (j, 0, c, 0, 0))
    return pl.pallas_call(
        _rglru_kernel,
        grid=(n_seq, nc),
        in_specs=[pl.BlockSpec((t, cw), lambda b, c: (blk0 + b, c)),
                  pl.BlockSpec((t, cw), lambda b, c: (blk0 + b, nc + c)),
                  pl.BlockSpec((None, 4, cw), lambda b, c: (j, 0, c)),
                  pl.BlockSpec((None, 1, cw), lambda b, c: (j, 0, c)),
                  gatew, vec2, gatew, vec2, vec2,
                  pl.BlockSpec((None, None, 2, cw), lambda b, c: (b, h0_j, 0, c))],
        out_specs=[pl.BlockSpec((t, cw), lambda b, c: (b, c)),
                   pl.BlockSpec((None, 2, cw), lambda b, c: (b, 0, c))],
        out_shape=[jax.ShapeDtypeStruct((n_seq * t, r), F32),
                   jax.ShapeDtypeStruct((n_seq, 2, r), F32)],
        scratch_shapes=scr,
        compiler_params=_cparams("arbitrary", "arbitrary"),
        name="rglru_scan",
    )(gu, gu, conv_w, conv_b.reshape(conv_b.shape[0], 1, r), w_a, b_a, w_x, b_x, lam, h0)


def _s5_kernel(*refs, bp, n_seq, nc, gpb):
    ell, gw = S5_L, S5_H
    (x_ref, tm_ref, win_ref, wre_ref, wim_ref, ar_ref, ai_ref, d_ref, s0re_ref, s0im_ref,
     o_ref, fin_ref, xg_s, ure_s, uim_s, fre_s, fim_s, bre_s, bim_s) = refs
    m, mp = n_seq * nc, bp * nc
    lanes = x_ref.shape[1]
    token = lambda l: pl.ds(l, m, stride=ell)
    per_tile = lanes // gw
    half = ure_s.shape[2] // 2
    lane_grp = lax.shift_right_logical(lax.broadcasted_iota(jnp.int32, (m, lanes), 1), gw.bit_length() - 1)

    def perm(shape, chunk_major_axis):
        i = lax.broadcasted_iota(jnp.int32, shape, chunk_major_axis)
        j = lax.broadcasted_iota(jnp.int32, shape, 1 - chunk_major_axis)
        b = i & (bp - 1)
        c = lax.shift_right_logical(i, bp.bit_length() - 1)
        return jnp.where((j == b * nc + c) & (b < n_seq), 1.0, 0.0).astype(BF16)

    to_chunk_major = perm((mp, m), 0)
    to_batch_major = perm((m, mp), 1)

    def block_transpose(v):
        k = per_tile // 2
        while k >= 1:
            low = (lane_grp & k) == 0
            nxt = list(v)
            for i in range(per_tile):
                if i & k == 0:
                    a, b = v[i], v[i + k]
                    nxt[i] = jnp.where(low, a, pltpu.roll(b, k * gw, axis=1))
                    nxt[i + k] = jnp.where(low, pltpu.roll(a, lanes - k * gw, axis=1), b)
            v = nxt
            k //= 2
        return v

    for tile in range(ell // per_tile):
        by_group = block_transpose([x_ref[token(tile * per_tile + j), :] for j in range(per_tile)])
        for g in range(gpb):
            xg_s[g, :, tile * lanes:(tile + 1) * lanes] = by_group[g]

    for g in range(gpb):
        xg = xg_s[g]
        xp = jnp.dot(to_chunk_major, xg.astype(BF16), preferred_element_type=F32).astype(BF16)
        u = jnp.dot(xp, win_ref[g].astype(BF16), preferred_element_type=F32)
        ure_s[g] = u[:, :2 * half]
        uim_s[g] = u[:, 2 * half:]

    is_fwd = lax.broadcasted_iota(jnp.int32, (bp, 2 * half), 1) < half
    ar = [ar_ref[g] for g in range(gpb)]
    ai = [ai_ref[g] for g in range(gpb)]

    def body(k, carry):
        rf = pl.multiple_of(k * bp, bp)
        rb = pl.multiple_of((nc - 1 - k) * bp, bp)
        out = []
        for g in range(gpb):
            re, im = carry[2 * g], carry[2 * g + 1]
            fre_s[g, pl.ds(rf, bp), :] = re
            fim_s[g, pl.ds(rf, bp), :] = im
            bre_s[g, pl.ds(rb, bp), :] = re
            bim_s[g, pl.ds(rb, bp), :] = im
            ure = jnp.where(is_fwd, ure_s[g, pl.ds(rf, bp), :], ure_s[g, pl.ds(rb, bp), :])
            uim = jnp.where(is_fwd, uim_s[g, pl.ds(rf, bp), :], uim_s[g, pl.ds(rb, bp), :])
            out += [ar[g] * re - ai[g] * im + ure, ar[g] * im + ai[g] * re + uim]
        return tuple(out)

    init = tuple(r[g] for g in range(gpb) for r in (s0re_ref, s0im_ref))
    fin = lax.fori_loop(0, nc, body, init)
    fwd_all = lax.broadcasted_iota(jnp.int32, (mp, 2 * half), 1) < half
    for g in range(gpb):
        fin_ref[g, :, :2 * half] = fin[2 * g]
        fin_ref[g, :, 2 * half:] = fin[2 * g + 1]
        hre = jnp.where(fwd_all, fre_s[g], bre_s[g]).astype(BF16)
        him = jnp.where(fwd_all, fim_s[g], bim_s[g]).astype(BF16)
        hre = jnp.dot(to_batch_major, hre, preferred_element_type=F32).astype(BF16)
        him = jnp.dot(to_batch_major, him, preferred_element_type=F32).astype(BF16)
        xg = xg_s[g]
        y = (jnp.dot(xg.astype(BF16), tm_ref[g].astype(BF16), preferred_element_type=F32)
             + jnp.dot(hre, wre_ref[g].astype(BF16), preferred_element_type=F32)
             + jnp.dot(him, wim_ref[g].astype(BF16), preferred_element_type=F32)
             + d_ref[g] * xg)
        xg_s[g] = _gelu(y)

    for tile in range(ell // per_tile):
        by_token = block_transpose([xg_s[g, :, tile * lanes:(tile + 1) * lanes] for g in range(gpb)])
        for j in range(per_tile):
            o_ref[token(tile * per_tile + j), :] = by_token[j]


def s5_chunked(hn, row_blk, n_seq, nc, mats, s0re, s0im, *, bp):
    tmat, win, wre, wim, ar, ai, dg = mats
    g, w, _ = tmat.shape
    p2 = ar.shape[-1]
    ell = S5_L
    lanes = 128
    gpb = lanes // S5_H
    d = hn.shape[1]
    m, mp = n_seq * nc, bp * nc
    n = m * ell
    assert bp & (bp - 1) == 0 and S5_H & (S5_H - 1) == 0 and g % gpb == 0
    blk = lambda shape: pl.BlockSpec((gpb,) + shape, lambda i: (i, 0, 0))
    return pl.pallas_call(
        functools.partial(_s5_kernel, bp=bp, n_seq=n_seq, nc=nc, gpb=gpb),
        grid=(g // gpb,),
        in_specs=[pl.BlockSpec((n, lanes), lambda i: (row_blk, i)),
                  blk((w, w)), blk((w, 2 * p2)), blk((p2, w)), blk((p2, w)),
                  blk((1, p2)), blk((1, p2)), blk((1, w)), blk((bp, p2)), blk((bp, p2))],
        out_specs=[pl.BlockSpec((n, lanes), lambda i: (0, i)), blk((bp, 2 * p2))],
        out_shape=[jax.ShapeDtypeStruct((n, d), F32), jax.ShapeDtypeStruct((g, bp, 2 * p2), F32)],
        scratch_shapes=[pltpu.VMEM((gpb, m, w), F32)] + [pltpu.VMEM((gpb, mp, p2), F32) for _ in range(6)],
        compiler_params=_cparams("arbitrary"),
        name="s5_chunked",
    )(hn, tmat, win, wre, wim, ar, ai, dg, s0re, s0im)


def _cmul(ar, ai, br, bi):
    return ar * br - ai * bi, ar * bi + ai * br


def _s5_prep_kernel(are_ref, aim_ref, ldt_ref, btr_ref, bti_ref, cr_ref, ci_ref,
                    tm_ref, win_ref, wre_ref, wim_ref, ar_ref, ai_ref):
    for gi in range(cr_ref.shape[0]):
        _s5_prep_group(*(r.at[gi] for r in (are_ref, aim_ref, ldt_ref, btr_ref, bti_ref, cr_ref, ci_ref,
                                           tm_ref, win_ref, wre_ref, wim_ref, ar_ref, ai_ref)))


def _s5_prep_group(are_ref, aim_ref, ldt_ref, btr_ref, bti_ref, cr_ref, ci_ref,
                   tm_ref, win_ref, wre_ref, wim_ref, ar_ref, ai_ref):
    ell = S5_L
    h, p2 = cr_ref.shape
    w = ell * h
    a_re, a_im = are_ref[...], aim_ref[...]
    dt = jnp.exp(ldt_ref[...])
    steps = lax.broadcasted_iota(jnp.int32, (3 * SUBLANES, p2), 0).astype(F32)
    mag = jnp.exp(steps * (a_re * dt))
    ang = steps * (a_im * dt)
    pw_r, pw_i = mag * jnp.cos(ang), mag * jnp.sin(ang)
    nr, ni = pw_r[1:2] - 1.0, pw_i[1:2]
    den = a_re * a_re + a_im * a_im
    qr, qi = (nr * a_re + ni * a_im) / den, (ni * a_re - nr * a_im) / den
    bb_r, bb_i = _cmul(qr, qi, btr_ref[...], bti_ref[...])
    c_r, c_i = cr_ref[...], ci_ref[...]
    fwd = lax.broadcasted_iota(jnp.int32, (1, p2), 1) < p2 // 2

    def power_rows(m_fwd, m_bwd):
        return (jnp.where(fwd, pw_r[m_fwd:m_fwd + 1], pw_r[m_bwd:m_bwd + 1]),
                jnp.where(fwd, pw_i[m_fwd:m_fwd + 1], pw_i[m_bwd:m_bwd + 1]))

    def stack(x_r, x_i, powers):
        parts = [_cmul(x_r, x_i, *power_rows(*powers(l))) for l in range(ell)]
        return (jnp.concatenate([q[0] for q in parts], axis=0), jnp.concatenate([q[1] for q in parts], axis=0))

    win_r, win_i = stack(bb_r, bb_i, lambda l: (ell - 1 - l, l))
    win_ref[...] = jnp.concatenate([win_r, win_i], axis=1).astype(win_ref.dtype)
    z_r, z_i = stack(c_r, c_i, lambda l: (l + 1, ell - l))
    wre_ref[...] = z_r.T.astype(wre_ref.dtype)
    wim_ref[...] = (-z_i).T.astype(wim_ref.dtype)
    k_r, k_i = stack(c_r, c_i, lambda m: (m, ell - 1 - m))
    mask_f = jnp.where(fwd, 1.0, 0.0)

    def lag_rows(mask):
        return (lax.dot_general(bb_r * mask, k_r, NT_DIMS, precision=HIGHEST, preferred_element_type=F32)
                - lax.dot_general(bb_i * mask, k_i, NT_DIMS, precision=HIGHEST, preferred_element_type=F32))

    kf = lag_rows(mask_f)
    kb = lag_rows(1.0 - mask_f)
    lane = lax.broadcasted_iota(jnp.int32, (h, w), 1)
    blocks = []
    for li in range(ell):
        f_part = kf if li == 0 else pltpu.roll(kf, li * h, axis=1)
        s_b = (w - (ell - 1 - li) * h) % w
        b_part = kb if s_b == 0 else pltpu.roll(kb, s_b, axis=1)
        blocks.append(jnp.where(lane >= li * h, f_part, 0.0) + jnp.where(lane < (li + 1) * h, b_part, 0.0))
    tm_ref[...] = jnp.concatenate(blocks, axis=0).astype(tm_ref.dtype)
    ar_ref[...] = pw_r[ell:ell + 1]
    ai_ref[...] = pw_i[ell:ell + 1]


def s5_chunk_operators(a_re, a_im, log_dt, b_re, b_im, c_re, c_im, d):
    _, g, p = a_re.shape
    h = b_re.shape[-1]
    w = S5_L * h
    two_dir = lambda x: jnp.transpose(x, (1, 0, 2)).reshape(g, 1, 2 * p)
    ldt = two_dir(jnp.broadcast_to(log_dt[:, :, None], (2, g, p)))
    bt = lambda x: jnp.transpose(x, (1, 3, 0, 2)).reshape(g, h, 2 * p)
    ct = lambda x: jnp.transpose(x, (1, 2, 0, 3)).reshape(g, h, 2 * p)
    gps = math.gcd(g, S5_PREP_GROUPS)
    per_g = lambda shape: pl.BlockSpec((gps,) + shape, lambda i: (i, 0, 0))
    tmat, win, wre, wim, ar, ai = pl.pallas_call(
        _s5_prep_kernel,
        grid=(g // gps,),
        in_specs=[per_g((1, 2 * p))] * 3 + [per_g((h, 2 * p))] * 4,
        out_specs=[per_g((w, w)), per_g((w, 4 * p)), per_g((2 * p, w)), per_g((2 * p, w)),
                   per_g((1, 2 * p)), per_g((1, 2 * p))],
        out_shape=[jax.ShapeDtypeStruct((g, w, w), BF16), jax.ShapeDtypeStruct((g, w, 4 * p), BF16),
                   jax.ShapeDtypeStruct((g, 2 * p, w), BF16), jax.ShapeDtypeStruct((g, 2 * p, w), BF16),
                   jax.ShapeDtypeStruct((g, 1, 2 * p), F32), jax.ShapeDtypeStruct((g, 1, 2 * p), F32)],
        compiler_params=_cparams("arbitrary"),
        name="s5_chunk_operators",
    )(two_dir(a_re), two_dir(a_im), ldt, bt(b_re), bt(b_im), ct(c_re), ct(c_im))
    dg = jnp.tile(d.reshape(g, 1, h), (1, S5_L, 1)).reshape(g, 1, w)
    return tmat, win, wre, wim, ar, ai, dg


def s5_mixer_group(hn, row0, n_seq, t, mats, s0):
    ar = mats[4]
    g = ar.shape[0]
    p = ar.shape[-1] // 2
    nc = t // S5_L
    bp = -(-n_seq // SUBLANES) * SUBLANES
    n = n_seq * t
    assert row0 % n == 0
    if s0 is None:
        s0re = jnp.zeros((g, bp, 2 * p), F32)
        s0im = s0re
    else:
        st = jnp.transpose(s0, (3, 0, 2, 1, 4)).reshape(g, n_seq, 2, 2 * p)
        st = jnp.pad(st, ((0, 0), (0, bp - n_seq), (0, 0), (0, 0)))
        s0re, s0im = st[:, :, 0], st[:, :, 1]
    u, fin = s5_chunked(hn, row0 // n, n_seq, nc, mats, s0re, s0im, bp=bp)
    fin = fin.reshape(g, bp, 2, 2, p)[:, :n_seq]
    return u, jnp.transpose(fin, (1, 3, 2, 0, 4))


def _softmax_pv(scores, values, sink):
    m = sink
    for s in scores:
        m = jnp.maximum(m, jnp.max(s, axis=-1, keepdims=True))
    den = jnp.exp(sink - m)
    acc = None
    for s, v in zip(scores, values):
        p = jnp.exp(s - m)
        den = den + jnp.sum(p, axis=-1, keepdims=True)
        pv = jnp.dot(p.astype(BF16), v, preferred_element_type=F32)
        acc = pv if acc is None else acc + pv
    return acc / den


def _attn_prompt_kernel(sink_ref, q_ref, k_ref, v_ref, o_ref):
    k = k_ref[...].astype(BF16)
    v = v_ref[...].astype(BF16)
    for h in range(N_KV):
        hs = slice(h * HEAD_DIM, (h + 1) * HEAD_DIM)
        kh, vh = k[:, hs], v[:, hs]
        for g in range(Q_PER_KV):
            c0 = (h * Q_PER_KV + g) * HEAD_DIM
            qg = (q_ref[:, c0:c0 + HEAD_DIM] * ATT_SCALE).astype(BF16)
            s = lax.dot_general(qg, kh, NT_DIMS, preferred_element_type=F32)
            o_ref[:, c0:c0 + HEAD_DIM] = _softmax_pv([s], [vh], sink_ref[h * Q_PER_KV + g])


def attn_prompt(qkv, sink, n_seq, t):
    dq = N_HEADS * HEAD_DIM
    kcol = dq // KV_W
    return pl.pallas_call(
        _attn_prompt_kernel,
        grid=(n_seq,),
        in_specs=[pl.BlockSpec(memory_space=pltpu.SMEM),
                  pl.BlockSpec((t, dq), lambda b: (b, 0)),
                  pl.BlockSpec((t, KV_W), lambda b: (b, kcol)),
                  pl.BlockSpec((t, KV_W), lambda b: (b, kcol + 1))],
        out_specs=pl.BlockSpec((t, dq), lambda b: (b, 0)),
        out_shape=jax.ShapeDtypeStruct((n_seq * t, dq), F32),
        compiler_params=_cparams("arbitrary"),
        name="attn_prompt",
    )(sink, qkv, qkv, qkv)


def _rope(x, cos, sin):
    w = x.shape[1]
    low = (lax.broadcasted_iota(jnp.int32, x.shape, 1) & (HEAD_DIM // 4)) == 0
    partner = jnp.where(low, pltpu.roll(x, w - HEAD_DIM // 4, axis=1), pltpu.roll(x, HEAD_DIM // 4, axis=1))
    return x * cos + partner * sin


def _attn_sample_kernel(sink_ref, q_ref, k_ref, v_ref, kc_ref, vc_ref, cos_ref, sin_ref, o_ref,
                        kw_s, vw_s, kc_s, vc_s, *, t):
    n = pl.program_id(1)
    blk = ATT_BLOCK

    @pl.when(n == 0)
    def _():
        zeros = jnp.zeros((blk, KV_W), BF16)
        kw_s[0:blk, :] = zeros
        vw_s[0:blk, :] = zeros
        kw_s[blk + t:2 * blk + t, :] = zeros
        vw_s[blk + t:2 * blk + t, :] = zeros
        kw_s[blk:blk + t, :] = _rope(k_ref[...], cos_ref[...], sin_ref[...]).astype(BF16)
        vw_s[blk:blk + t, :] = v_ref[...].astype(BF16)
        kc_s[...] = kc_ref[...].astype(BF16)
        vc_s[...] = vc_ref[...].astype(BF16)

    r0 = pl.multiple_of(n * blk, blk)
    cq = cos_ref[pl.ds(r0, blk), :]
    sq = sin_ref[pl.ds(r0, blk), :]
    kw = kw_s[pl.ds(r0, 3 * blk), :]
    vw = vw_s[pl.ds(r0, 3 * blk), :]
    qi = lax.broadcasted_iota(jnp.int32, (blk, 3 * blk), 0)
    kj = lax.broadcasted_iota(jnp.int32, (blk, 3 * blk), 1)
    kpos = n * blk - blk + kj
    valid = (jnp.abs(kj - blk - qi) <= WINDOW) & (kpos >= 0) & (kpos < t)
    for h in range(N_KV):
        hs = slice(h * HEAD_DIM, (h + 1) * HEAD_DIM)
        qh = (_rope(q_ref[:, h * KV_W:(h + 1) * KV_W], cq, sq) * ATT_SCALE).astype(BF16)
        kh, vh, kch, vch = kw[:, hs], vw[:, hs], kc_s[:, hs], vc_s[:, hs]
        for g in range(Q_PER_KV):
            qg = qh[:, g * HEAD_DIM:(g + 1) * HEAD_DIM]
            s_loc = lax.dot_general(qg, kh, NT_DIMS, preferred_element_type=F32)
            s_loc = jnp.where(valid, s_loc, NEG_INF)
            s_ctx = lax.dot_general(qg, kch, NT_DIMS, preferred_element_type=F32)
            c0 = (h * Q_PER_KV + g) * HEAD_DIM
            o_ref[:, c0:c0 + HEAD_DIM] = _softmax_pv([s_loc, s_ctx], [vh, vch], sink_ref[h * Q_PER_KV + g])


def _rope_tables(t):
    quarter = HEAD_DIM // 4
    freqs = ROPE_BASE ** (-jnp.arange(quarter, dtype=F32) / quarter)
    pos = jnp.arange(t)
    ang_r = (pos // GRID_W).astype(F32)[:, None] * freqs
    ang_c = (pos % GRID_W).astype(F32)[:, None] * freqs
    cos = jnp.concatenate([jnp.cos(ang_r), jnp.cos(ang_r), jnp.cos(ang_c), jnp.cos(ang_c)], axis=-1)
    sin = jnp.concatenate([-jnp.sin(ang_r), jnp.sin(ang_r), -jnp.sin(ang_c), jnp.sin(ang_c)], axis=-1)
    return jnp.tile(cos, (1, N_KV)), jnp.tile(sin, (1, N_KV))


def attn_sample(qkv, row0, sink, n_seq, t, k_ctx, v_ctx):
    dq = N_HEADS * HEAD_DIM
    kcol = dq // KV_W
    nb = t // ATT_BLOCK
    lc = k_ctx.shape[1]
    cos, sin = _rope_tables(t)
    qblk0, sblk0 = row0 // ATT_BLOCK, row0 // t
    return pl.pallas_call(
        functools.partial(_attn_sample_kernel, t=t),
        grid=(n_seq, nb),
        in_specs=[pl.BlockSpec(memory_space=pltpu.SMEM),
                  pl.BlockSpec((ATT_BLOCK, dq), lambda b, n: (qblk0 + b * nb + n, 0)),
                  pl.BlockSpec((t, KV_W), lambda b, n: (sblk0 + b, kcol)),
                  pl.BlockSpec((t, KV_W), lambda b, n: (sblk0 + b, kcol + 1)),
                  pl.BlockSpec((None, lc, KV_W), lambda b, n: (b, 0, 0)),
                  pl.BlockSpec((None, lc, KV_W), lambda b, n: (b, 0, 0)),
                  pl.BlockSpec((t, KV_W), lambda b, n: (0, 0)),
                  pl.BlockSpec((t, KV_W), lambda b, n: (0, 0))],
        out_specs=pl.BlockSpec((ATT_BLOCK, dq), lambda b, n: (b * nb + n, 0)),
        out_shape=jax.ShapeDtypeStruct((n_seq * t, dq), F32),
        scratch_shapes=[pltpu.VMEM((t + 2 * ATT_BLOCK, KV_W), BF16), pltpu.VMEM((t + 2 * ATT_BLOCK, KV_W), BF16),
                        pltpu.VMEM((lc, KV_W), BF16), pltpu.VMEM((lc, KV_W), BF16)],
        compiler_params=_cparams("arbitrary", "arbitrary"),
        name="attn_sample",
    )(sink, qkv, qkv, qkv, k_ctx, v_ctx, cos, sin)


def _moe_kernel(x_ref, wg_ref, wu_ref, wd_ref, gs_ref, g2_ref, o_ref, wg_bf, wu_bf, wd_bf, *, n_seg):
    wg_bf[...] = wg_ref[...].astype(BF16)
    wu_bf[...] = wu_ref[...].astype(BF16)
    wd_bf[...] = wd_ref[...].astype(BF16)
    rsub = min(MOE_ROW_SUB, x_ref.shape[0])
    for r in range(x_ref.shape[0] // rsub):
        rs = slice(r * rsub, (r + 1) * rsub)
        x = x_ref[rs, :]
        hg = jnp.dot(x, wg_bf[...], preferred_element_type=F32)
        hu = jnp.dot(x, wu_bf[...], preferred_element_type=F32)
        he = ((hg * jax.nn.sigmoid(hg)) * hu).astype(BF16)
        y = jnp.dot(he, wd_bf[...], preferred_element_type=F32)
        gs = gs_ref[rs, :]
        seg = gs[:, 1:2]
        g2 = jnp.zeros(y.shape, F32)
        for s in range(n_seg):
            g2 = jnp.where(seg == float(s), g2_ref[s:s + 1, :], g2)
        o_ref[rs, :] = y * (gs[:, 0:1] * g2)


def moe_experts(xe, w_gate, w_up, w_down, layer, gate_seg, mod, n_seg):
    ne, r, d = xe.shape
    dff = w_gate.shape[3]
    return pl.pallas_call(
        functools.partial(_moe_kernel, n_seg=n_seg),
        grid=(ne,),
        in_specs=[pl.BlockSpec((None, r, d), lambda e: (e, 0, 0)),
                  pl.BlockSpec((None, None, d, dff), lambda e: (layer, e, 0, 0)),
                  pl.BlockSpec((None, None, d, dff), lambda e: (layer, e, 0, 0)),
                  pl.BlockSpec((None, None, dff, d), lambda e: (layer, e, 0, 0)),
                  pl.BlockSpec((None, r, 2), lambda e: (e, 0, 0)),
                  pl.BlockSpec((SUBLANES, None, d), lambda e: (layer, 0, 5))],
        out_specs=pl.BlockSpec((None, r, d), lambda e: (e, 0, 0)),
        out_shape=jax.ShapeDtypeStruct((ne, r, d), F32),
        scratch_shapes=[pltpu.VMEM((d, dff), BF16), pltpu.VMEM((d, dff), BF16), pltpu.VMEM((dff, d), BF16)],
        compiler_params=_cparams("arbitrary"),
        name="moe_experts",
    )(xe, w_gate, w_up, w_down, gate_seg, mod)


def _expert_choice(aff, rows):
    ne = aff.shape[1]
    sizes = (rows.n_prompt, rows.n_sample)
    caps = [(EC_FACTOR * n) // ne for n in sizes]
    if sizes[0] == sizes[1]:
        n = sizes[0]
        gt, ix = lax.top_k(jnp.swapaxes(aff.reshape(2, n, ne), 1, 2), caps[0])
        ix = ix + jnp.array([0, n], jnp.int32)[:, None, None]
        return (jnp.concatenate([gt[0], gt[1]], axis=1), jnp.concatenate([ix[0], ix[1]], axis=1))
    gts, ixs, off = [], [], 0
    for n, cap in zip(sizes, caps):
        gt, ix = lax.top_k(aff[off:off + n].T, cap)
        gts.append(gt)
        ixs.append(ix + off)
        off += n
    return jnp.concatenate(gts, axis=1), jnp.concatenate(ixs, axis=1)


def _combine_kernel(idx_ref, x_hbm, ye_ref, o_hbm, acc, sem, *, n_grp, rows_per):
    g = pl.program_id(0)
    e = pl.program_id(1)
    last = pl.num_programs(1) - 1

    def load(grp):
        return pltpu.make_async_copy(x_hbm.at[pl.ds(grp * n_grp, n_grp), :], acc.at[grp], sem.at[grp])

    def store(grp):
        return pltpu.make_async_copy(acc.at[grp], o_hbm.at[pl.ds(grp * n_grp, n_grp), :], sem.at[2 + grp])

    @pl.when((g == 0) & (e == 0))
    def _():
        load(0).start()
        load(1).start()

    @pl.when(e == 0)
    def _():
        load(g).wait()

    base = (2 * e + g) * rows_per
    first = g * n_grp

    def body(i, carry):
        toks = [idx_ref[base + i * COMBINE_UNROLL + k] - first for k in range(COMBINE_UNROLL)]
        old = [acc[g, pl.ds(t, 1), :] for t in toks]
        add = [ye_ref[pl.ds(i * COMBINE_UNROLL + k, 1), :] for k in range(COMBINE_UNROLL)]
        for t, a, b in zip(toks, old, add):
            acc[g, pl.ds(t, 1), :] = a + b
        return carry

    lax.fori_loop(0, rows_per // COMBINE_UNROLL, body, 0)

    @pl.when(e == last)
    def _():
        store(g).start()

    @pl.when((g == 1) & (e == last))
    def _():
        store(0).wait()
        store(1).wait()


def moe_combine(x, ye, idx, n_grp):
    m, d = x.shape
    ne, r, _ = ye.shape
    rows_per = r // 2
    assert m == 2 * n_grp and rows_per % COMBINE_UNROLL == 0
    grid_spec = pltpu.PrefetchScalarGridSpec(
        num_scalar_prefetch=1,
        grid=(2, ne),
        in_specs=[pl.BlockSpec(memory_space=pl.ANY),
                  pl.BlockSpec((None, rows_per, d), lambda g, e, ix: (e, g, 0))],
        out_specs=pl.BlockSpec(memory_space=pl.ANY),
        scratch_shapes=[pltpu.VMEM((2, n_grp, d), F32), pltpu.SemaphoreType.DMA((4,))])
    return pl.pallas_call(
        functools.partial(_combine_kernel, n_grp=n_grp, rows_per=rows_per),
        grid_spec=grid_spec,
        out_shape=jax.ShapeDtypeStruct((m, d), F32),
        compiler_params=_cparams("arbitrary", "arbitrary"),
        name="moe_combine",
    )(idx.reshape(-1), x, ye)


def moe_layer(x, h2, aff, mod, rows, layer, w_gate, w_up, w_down, n_seg):
    m, d = x.shape
    gates, idx = _expert_choice(aff, rows)
    seg = jnp.where(idx < rows.n_prompt, 0, 1 + (idx - rows.n_prompt) // rows.t_sample)
    xe = h2[idx]
    gate_seg = jnp.stack([gates, seg.astype(F32)], axis=-1)
    ye = moe_experts(xe, w_gate, w_up, w_down, layer, gate_seg, mod, n_seg)
    if rows.n_prompt == rows.n_sample:
        return moe_combine(x, ye, idx, rows.n_prompt)
    return x.at[idx.reshape(-1)].add(ye.reshape(-1, d))


def kernel(x_prompt, x_sample, state_rglru, state_s5, cache_k, cache_v, c, c_ctx, ada_w, ada_b, norm1_g, norm2_g, rg_w_in, rg_conv_w, rg_conv_b, rg_w_a, rg_b_a, rg_w_x, rg_b_x, rg_lambda, rg_w_out, s5_a_re, s5_a_im, s5_log_dt, s5_b_re, s5_b_im, s5_c_re, s5_c_im, s5_d, s5_w_glu, attn_w_qkv, attn_w_o, attn_sink, router_w, moe_w_gate, moe_w_up, moe_w_down, final_norm_g):
    bp_, tp, d = x_prompt.shape
    bs, ts, _ = x_sample.shape
    n_p, n_s = bp_ * tp, bs * ts
    depth = ada_w.shape[0]
    rows = _Rows(n_p, n_s, ts)
    assert bs + 1 <= SUBLANES and n_p % ts == 0

    xs = (x_prompt.reshape(n_p, d), x_sample.reshape(n_s, d), 0)
    cond = jnp.concatenate([c_ctx[None, :], c, jnp.zeros((SUBLANES - 1 - bs, d), F32)], axis=0)
    mod_all = ada_modulation_all(cond, ada_w, ada_b)
    mod = mod_all.reshape(depth * SUBLANES, 1, 6 * d)
    g1 = norm1_g.reshape(depth, 1, d)
    g2 = norm2_g.reshape(depth, 1, d)

    new_rg, new_s5, new_k, new_v = [], [], [], []
    for l in range(depth):
        kind, j = l % 3, l // 3
        if kind == 0:
            gu = norm_mod_matmul(xs, g1, mod, rows, l, rg_w_in, j)
            args = (j, rg_conv_w, rg_conv_b, rg_w_a, rg_b_a, rg_w_x, rg_b_x, rg_lambda)
            r = gu.shape[1] // 2
            yp, fin = rglru_scan(gu, 0, bp_, tp, *args, jnp.zeros((bp_, 1, 2, r), F32), 0)
            ys, _ = rglru_scan(gu, n_p, bs, ts, *args, state_rglru, j)
            new_rg.append(fin)
            x, h2, aff = matmul_gated_residual(yp, ys, rg_w_out, j, xs, mod, rows, l, g2, router_w)
        elif kind == 1:
            hn = norm_mod(x, g1, mod, rows, l)
            mats = s5_chunk_operators(s5_a_re[j], s5_a_im[j], s5_log_dt[j], s5_b_re[j], s5_b_im[j],
                                      s5_c_re[j], s5_c_im[j], s5_d[j])
            up, st = s5_mixer_group(hn, 0, bp_, tp, mats, None)
            us, _ = s5_mixer_group(hn, n_p, bs, ts, mats, state_s5[:, j])
            new_s5.append(st)
            x, h2, aff = glu_gated_residual(up, us, s5_w_glu, j, x, mod, rows, l, g2, router_w)
        else:
            qkv = norm_mod_matmul(xs, g1, mod, rows, l, attn_w_qkv, j)
            dq = N_HEADS * HEAD_DIM
            new_k.append(qkv[:n_p, dq:dq + KV_W].reshape(bp_, tp, N_KV, HEAD_DIM))
            new_v.append(qkv[:n_p, dq + KV_W:].reshape(bp_, tp, N_KV, HEAD_DIM))
            op = attn_prompt(qkv, attn_sink[j], bp_, tp)
            lc = cache_k.shape[2]
            os_ = attn_sample(qkv, n_p, attn_sink[j], bs, ts,
                              cache_k[:, j].reshape(bs, lc, KV_W), cache_v[:, j].reshape(bs, lc, KV_W))
            x, h2, aff = matmul_gated_residual(op, os_, attn_w_o, j, xs, mod, rows, l, g2, router_w)
        x = moe_layer(x, h2, aff, mod, rows, l, moe_w_gate, moe_w_up, moe_w_down, bs + 1)
        xs = (x, x, rows.prompt_blocks)

    y_p, y_s = final_norm(x, final_norm_g, rows)
    return (y_p.reshape(bp_, tp, d), y_s.reshape(bs, ts, d),
            jnp.stack(new_rg, axis=1), jnp.stack(new_s5, axis=1),
            jnp.stack(new_k, axis=1), jnp.stack(new_v, axis=1))
```

```python
import functools
import math

import jax
import jax.numpy as jnp
from jax import lax
from jax.experimental import pallas as pl
from jax.experimental.pallas import tpu as pltpu

F32 = jnp.float32
BF16 = jnp.bfloat16
HIGHEST = lax.Precision.HIGHEST

EPS = 1e-6
RG_C = 8.0
RG_BS = 128
RG_TILE = 512
S5_H = 16
S5_L = 16
S5_PREP_GROUPS = 4
N_HEADS = 16
N_KV = 4
Q_PER_KV = N_HEADS // N_KV
HEAD_DIM = 64
KV_W = N_KV * HEAD_DIM
GRID_W = 64
WINDOW = 128
ATT_BLOCK = 128
ROPE_BASE = 10000.0
ATT_SCALE = HEAD_DIM ** -0.5
assert math.frexp(ATT_SCALE)[0] == 0.5
NEG_INF = -1e30
EC_FACTOR = 2
SUBLANES = 8
ROW_TILE = 512
MOE_ROW_SUB = 256
COMBINE_UNROLL = 16
VMEM_LIMIT = 56 * 1024 * 1024
NT_DIMS = (((1,), (1,)), ((), ()))


def _cparams(*sem):
    return pltpu.CompilerParams(dimension_semantics=sem, vmem_limit_bytes=VMEM_LIMIT)


def _gelu(x):
    return x * (0.5 * (1.0 + jnp.tanh(math.sqrt(2.0 / math.pi) * (x + 0.044715 * (x * x * x)))))


def _sigmoid(x):
    return 0.5 * jnp.tanh(0.5 * x) + 0.5


def _norm_mod(x, g, sc, sh):
    ms = jnp.mean(x * x, axis=-1, keepdims=True)
    return ((x * lax.rsqrt(ms + EPS)) * g) * (1.0 + sc) + sh


def _mod_kernel(c_ref, w_ref, b_ref, o_ref):
    c = c_ref[...]
    s = (c * jax.nn.sigmoid(c)).astype(BF16)
    o_ref[...] = jnp.dot(s, w_ref[...].astype(BF16), preferred_element_type=F32) + b_ref[...]


def ada_modulation_all(cond, ada_w, ada_b):
    n_layers, d, n = ada_w.shape
    tn = 1536
    return pl.pallas_call(
        _mod_kernel,
        grid=(n_layers, n // tn),
        in_specs=[pl.BlockSpec((SUBLANES, d), lambda l, j: (0, 0)),
                  pl.BlockSpec((None, d, tn), lambda l, j: (l, 0, j)),
                  pl.BlockSpec((None, 1, tn), lambda l, j: (l, 0, j))],
        out_specs=pl.BlockSpec((None, SUBLANES, tn), lambda l, j: (l, 0, j)),
        out_shape=jax.ShapeDtypeStruct((n_layers, SUBLANES, n), F32),
        compiler_params=_cparams("arbitrary", "arbitrary"),
        name="ada_mod",
    )(cond, ada_w, ada_b.reshape(n_layers, 1, n))


class _Rows:
    def __init__(self, n_prompt, n_sample, t_sample):
        self.n_prompt = n_prompt
        self.n_sample = n_sample
        self.t_sample = t_sample
        self.tm = min(ROW_TILE, n_prompt, t_sample)
        assert n_prompt % self.tm == 0 and t_sample % self.tm == 0
        self.prompt_blocks = n_prompt // self.tm
        self.sample_blocks = n_sample // self.tm

    def seg(self, i):
        r = i * self.tm
        return jnp.where(r < self.n_prompt, 0, 1 + lax.div(r - self.n_prompt, self.t_sample))


def _mod_spec(rows, layer, width, chunk, m_axis):
    def imap(*ids):
        return (layer * SUBLANES + rows.seg(ids[m_axis]), 0, chunk)
    return pl.BlockSpec((None, 1, width), imap)


def _gain_spec(layer, d):
    return pl.BlockSpec((None, 1, d), lambda *ids: (layer, 0, 0))


def _stream_specs(rows, xs, m_axis):
    top, _, bot0 = xs
    npb, nsb, tm = rows.prompt_blocks, rows.sample_blocks, rows.tm
    d = top.shape[1]
    return [pl.BlockSpec((tm, d), lambda *ids: (jnp.minimum(ids[m_axis], npb - 1), 0)),
            pl.BlockSpec((tm, d), lambda *ids: (bot0 + jnp.clip(ids[m_axis] - npb, 0, nsb - 1), 0))]


def _nm_kernel(xt_ref, xb_ref, g_ref, sc_ref, sh_ref, w_ref, o_ref, wbf_ref, *, npb):
    i = pl.program_id(1)

    @pl.when(i == 0)
    def _():
        wbf_ref[...] = w_ref[...].astype(BF16)
    x = jnp.where(i < npb, xt_ref[...], xb_ref[...])
    h = _norm_mod(x, g_ref[...], sc_ref[...], sh_ref[...])
    o_ref[...] = jnp.dot(h.astype(BF16), wbf_ref[...], preferred_element_type=F32)


def norm_mod_matmul(xs, gains, mod, rows, layer, w, wl):
    m, d = rows.n_prompt + rows.n_sample, xs[0].shape[1]
    n = w.shape[2]
    tm = rows.tm
    tn = n
    return pl.pallas_call(
        functools.partial(_nm_kernel, npb=rows.prompt_blocks),
        grid=(n // tn, m // tm),
        in_specs=_stream_specs(rows, xs, 1) + [
                  _gain_spec(layer, d),
                  _mod_spec(rows, layer, d, 1, 1),
                  _mod_spec(rows, layer, d, 0, 1),
                  pl.BlockSpec((None, d, tn), lambda j, i: (wl, 0, j))],
        out_specs=pl.BlockSpec((tm, tn), lambda j, i: (i, j)),
        out_shape=jax.ShapeDtypeStruct((m, n), F32),
        scratch_shapes=[pltpu.VMEM((d, tn), BF16)],
        compiler_params=_cparams("arbitrary", "arbitrary"),
        name="norm_mod_matmul",
    )(xs[0], xs[1], gains, mod, mod, w)


def _norm_only_kernel(x_ref, g_ref, sc_ref, sh_ref, o_ref):
    o_ref[...] = _norm_mod(x_ref[...], g_ref[...], sc_ref[...], sh_ref[...])


def norm_mod(x, gains, mod, rows, layer):
    m, d = x.shape
    tm = rows.tm
    return pl.pallas_call(
        _norm_only_kernel,
        grid=(m // tm,),
        in_specs=[pl.BlockSpec((tm, d), lambda i: (i, 0)),
                  _gain_spec(layer, d),
                  _mod_spec(rows, layer, d, 1, 0),
                  _mod_spec(rows, layer, d, 0, 0)],
        out_specs=pl.BlockSpec((tm, d), lambda i: (i, 0)),
        out_shape=jax.ShapeDtypeStruct((m, d), F32),
        compiler_params=_cparams("arbitrary"),
        name="norm_mod",
    )(x, gains, mod, mod)


def _final_norm_kernel(x_ref, g_ref, op_ref, os_ref, *, npb):
    i = pl.program_id(0)
    x = x_ref[...]
    ms = jnp.mean(x * x, axis=-1, keepdims=True)
    y = (x * lax.rsqrt(ms + EPS)) * g_ref[...]

    @pl.when(i < npb)
    def _():
        op_ref[...] = y

    @pl.when(i >= npb)
    def _():
        os_ref[...] = y


def final_norm(x, g, rows):
    m, d = x.shape
    tm, npb, nsb = rows.tm, rows.prompt_blocks, rows.sample_blocks
    return pl.pallas_call(
        functools.partial(_final_norm_kernel, npb=npb),
        grid=(m // tm,),
        in_specs=[pl.BlockSpec((tm, d), lambda i: (i, 0)),
                  pl.BlockSpec((1, d), lambda i: (0, 0))],
        out_specs=[pl.BlockSpec((tm, d), lambda i: (jnp.minimum(i, npb - 1), 0)),
                   pl.BlockSpec((tm, d), lambda i: (jnp.clip(i - npb, 0, nsb - 1), 0))],
        out_shape=[jax.ShapeDtypeStruct((rows.n_prompt, d), F32), jax.ShapeDtypeStruct((rows.n_sample, d), F32)],
        compiler_params=_cparams("arbitrary"),
        name="final_norm",
    )(x, g.reshape(1, d))


def _router_epilogue(x_new, g_ref, sc_ref, sh_ref, rw_ref, h_ref, aff_ref):
    h = _norm_mod(x_new, g_ref[...], sc_ref[...], sh_ref[...])
    h_hi = h.astype(BF16)
    h_ref[...] = h_hi
    h_lo = (h - h_hi.astype(F32)).astype(BF16)
    rw = rw_ref[...]
    rw_hi = rw.astype(BF16)
    rw_lo = (rw - rw_hi.astype(F32)).astype(BF16)
    ne = rw.shape[1]
    both = jnp.dot(h_hi, jnp.concatenate([rw_hi, rw_lo], axis=1), preferred_element_type=F32)
    logits = both[:, :ne] + (both[:, ne:] + jnp.dot(h_lo, rw_hi, preferred_element_type=F32))
    e = jnp.exp(logits - jnp.max(logits, axis=-1, keepdims=True))
    aff_ref[...] = e / jnp.sum(e, axis=-1, keepdims=True)


def _mmres_kernel(ap_ref, as_ref, w_ref, rt_ref, rb_ref, gt_ref, g_ref, sc_ref, sh_ref, rw_ref,
                  o_ref, h_ref, aff_ref, wbf_ref, *, npb):
    i = pl.program_id(0)

    @pl.when(i == 0)
    def _():
        wbf_ref[...] = w_ref[...].astype(BF16)

    def emit(a_ref, r_ref):
        acc = jnp.dot(a_ref[...].astype(BF16), wbf_ref[...], preferred_element_type=F32)
        x_new = r_ref[...] + gt_ref[...] * acc
        o_ref[...] = x_new
        _router_epilogue(x_new, g_ref, sc_ref, sh_ref, rw_ref, h_ref, aff_ref)

    pl.when(i < npb)(lambda: emit(ap_ref, rt_ref))
    pl.when(i >= npb)(lambda: emit(as_ref, rb_ref))


def _sublayer2_specs(rows, layer, d, ne):
    tm = rows.tm
    ins = [_gain_spec(layer, d), _mod_spec(rows, layer, d, 4, 0), _mod_spec(rows, layer, d, 3, 0),
           pl.BlockSpec((None, d, ne), lambda i: (layer, 0, 0))]
    outs = [pl.BlockSpec((tm, d), lambda i: (i, 0)), pl.BlockSpec((tm, ne), lambda i: (i, 0))]
    return ins, outs


def _two_group_specs(rows, k):
    npb, nsb, tm = rows.prompt_blocks, rows.sample_blocks, rows.tm
    return [pl.BlockSpec((tm, k), lambda i: (jnp.minimum(i, npb - 1), 0)),
            pl.BlockSpec((tm, k), lambda i: (jnp.clip(i - npb, 0, nsb - 1), 0))]


def matmul_gated_residual(a_p, a_s, w, wl, xs, mod, rows, layer, gains2, router_w):
    k = a_p.shape[1]
    m, d = rows.n_prompt + rows.n_sample, xs[0].shape[1]
    tm = rows.tm
    ne = router_w.shape[2]
    r_in, r_out = _sublayer2_specs(rows, layer, d, ne)
    return pl.pallas_call(
        functools.partial(_mmres_kernel, npb=rows.prompt_blocks),
        grid=(m // tm,),
        in_specs=_two_group_specs(rows, k) + [
            pl.BlockSpec((None, k, d), lambda i: (wl, 0, 0))] + _stream_specs(rows, xs, 0) + [
            _mod_spec(rows, layer, d, 2, 0)] + r_in,
        out_specs=[pl.BlockSpec((tm, d), lambda i: (i, 0))] + r_out,
        out_shape=[jax.ShapeDtypeStruct((m, d), F32), jax.ShapeDtypeStruct((m, d), BF16),
                   jax.ShapeDtypeStruct((m, ne), F32)],
        scratch_shapes=[pltpu.VMEM((k, d), BF16)],
        compiler_params=_cparams("arbitrary"),
        name="matmul_gated_residual",
    )(a_p, a_s, w, xs[0], xs[1], mod, gains2, mod, mod, router_w)


def _glures_kernel(ap_ref, as_ref, wv_ref, wg_ref, r_ref, gt_ref, g_ref, sc_ref, sh_ref, rw_ref,
                   o_ref, h_ref, aff_ref, wv_bf, wg_bf, *, npb):
    i = pl.program_id(0)

    @pl.when(i == 0)
    def _():
        wv_bf[...] = wv_ref[...].astype(BF16)
        wg_bf[...] = wg_ref[...].astype(BF16)

    def emit(a_ref):
        a = a_ref[...].astype(BF16)
        v = jnp.dot(a, wv_bf[...], preferred_element_type=F32)
        g = jnp.dot(a, wg_bf[...], preferred_element_type=F32)
        x_new = r_ref[...] + gt_ref[...] * (v * jax.nn.sigmoid(g))
        o_ref[...] = x_new
        _router_epilogue(x_new, g_ref, sc_ref, sh_ref, rw_ref, h_ref, aff_ref)

    pl.when(i < npb)(lambda: emit(ap_ref))
    pl.when(i >= npb)(lambda: emit(as_ref))


def glu_gated_residual(a_p, a_s, w_glu, wl, resid, mod, rows, layer, gains2, router_w):
    k = a_p.shape[1]
    m, d = resid.shape
    tm = rows.tm
    ne = router_w.shape[2]
    r_in, r_out = _sublayer2_specs(rows, layer, d, ne)
    return pl.pallas_call(
        functools.partial(_glures_kernel, npb=rows.prompt_blocks),
        grid=(m // tm,),
        in_specs=_two_group_specs(rows, k) + [
            pl.BlockSpec((None, k, d), lambda i: (wl, 0, 0)),
            pl.BlockSpec((None, k, d), lambda i: (wl, 0, 1)),
            pl.BlockSpec((tm, d), lambda i: (i, 0)),
            _mod_spec(rows, layer, d, 2, 0)] + r_in,
        out_specs=[pl.BlockSpec((tm, d), lambda i: (i, 0))] + r_out,
        out_shape=[jax.ShapeDtypeStruct((m, d), F32), jax.ShapeDtypeStruct((m, d), BF16),
                   jax.ShapeDtypeStruct((m, ne), F32)],
        scratch_shapes=[pltpu.VMEM((k, d), BF16), pltpu.VMEM((k, d), BF16)],
        compiler_params=_cparams("arbitrary"),
        name="glu_gated_residual",
    )(a_p, a_s, w_glu, w_glu, resid, mod, gains2, mod, mod, router_w)


def _rglru_kernel(gate_ref, u_ref, cw_ref, cb_ref, wa_ref, ba_ref, wx_ref, bx_ref, lam_ref, h0_ref,
                  y_ref, fin_ref, af_s, bf_s, ab_s, bb_s, hf_s, hb_s):
    t, cw = u_ref.shape
    u = u_ref[...]
    row = lax.broadcasted_iota(jnp.int32, (t, cw), 0)

    def shifted(x, k):
        if k > 0:
            return jnp.where(row >= k, pltpu.roll(x, k, axis=0), 0.0)
        return jnp.where(row < t + k, pltpu.roll(x, t + k, axis=0), 0.0)

    cwv = cw_ref[...]
    uc = (cwv[0:1] * shifted(u, 2) + cwv[1:2] * shifted(u, 1) + cwv[2:3] * u
          + cwv[3:4] * shifted(u, -1) + cb_ref[...])

    a_scr = (af_s, ab_s)
    b_scr = (bf_s, bb_s)
    for k in range(2):
        nl = -lam_ref[k:k + 1, :]
        sp = jnp.maximum(nl, 0.0) + jnp.log1p(jnp.exp(-jnp.abs(nl)))
        for hh in range(cw // RG_BS):
            sl = slice(hh * RG_BS, (hh + 1) * RG_BS)
            uh = uc[:, sl]
            ub = uh.astype(BF16)
            r = _sigmoid(jnp.dot(ub, wa_ref[k, hh].astype(BF16), preferred_element_type=F32) + ba_ref[k:k + 1, sl])
            i = _sigmoid(jnp.dot(ub, wx_ref[k, hh].astype(BF16), preferred_element_type=F32) + bx_ref[k:k + 1, sl])
            log_a = (-RG_C * r) * sp[:, sl]
            a = jnp.exp(log_a)
            a_scr[k][:, sl] = a
            b_scr[k][:, sl] = jnp.sqrt(jnp.tanh(-log_a) * (a * a + 1.0)) * (i * uh)

    nblk = t // SUBLANES
    srow = lax.broadcasted_iota(jnp.int32, (SUBLANES, cw), 0)

    def body(n, carry):
        cf, cb = carry
        rf = pl.multiple_of(n * SUBLANES, SUBLANES)
        rb = pl.multiple_of((nblk - 1 - n) * SUBLANES, SUBLANES)
        a = af_s[pl.ds(rf, SUBLANES), :]
        b = bf_s[pl.ds(rf, SUBLANES), :]
        a2 = ab_s[pl.ds(rb, SUBLANES), :]
        b2 = bb_s[pl.ds(rb, SUBLANES), :]
        for s in (1, 2, 4):
            m = srow >= s
            b = jnp.where(m, a * pltpu.roll(b, s, axis=0) + b, b)
            a = jnp.where(m, a * pltpu.roll(a, s, axis=0), a)
            m2 = srow < SUBLANES - s
            b2 = jnp.where(m2, a2 * pltpu.roll(b2, SUBLANES - s, axis=0) + b2, b2)
            a2 = jnp.where(m2, a2 * pltpu.roll(a2, SUBLANES - s, axis=0), a2)
        hf = a * cf + b
        hb = a2 * cb + b2
        hf_s[pl.ds(rf, SUBLANES), :] = hf
        hb_s[pl.ds(rb, SUBLANES), :] = hb
        return hf[SUBLANES - 1:SUBLANES, :], hb[0:1, :]

    cf, cb = lax.fori_loop(0, nblk, body, (h0_ref[0:1, :], h0_ref[1:2, :]))
    fin_ref[0:1, :] = cf
    fin_ref[1:2, :] = cb
    y_ref[...] = (hf_s[...] + hb_s[...]) * _gelu(gate_ref[...])


def rglru_scan(gu, row0, n_seq, t, j, conv_w, conv_b, w_a, b_a, w_x, b_x, lam, h0, h0_j, *, cw=RG_TILE):
    r = gu.shape[1] // 2
    nh = cw // RG_BS
    blk0 = row0 // t
    nc = r // cw
    scr = [pltpu.VMEM((t, cw), F32) for _ in range(6)]
    vec2 = pl.BlockSpec((None, 2, cw), lambda b, c: (j, 0, c))
    gatew = pl.BlockSpec((None, 2, nh, RG_BS, RG_BS), lambda b, c: (j, 0, c, 0, 0))
    return pl.pallas_call(
        _rglru_kernel,
        grid=(n_seq, nc),
        in_specs=[pl.BlockSpec((t, cw), lambda b, c: (blk0 + b, c)),
                  pl.BlockSpec((t, cw), lambda b, c: (blk0 + b, nc + c)),
                  pl.BlockSpec((None, 4, cw), lambda b, c: (j, 0, c)),
                  pl.BlockSpec((None, 1, cw), lambda b, c: (j, 0, c)),
                  gatew, vec2, gatew, vec2, vec2,
                  pl.BlockSpec((None, None, 2, cw), lambda b, c: (b, h0_j, 0, c))],
        out_specs=[pl.BlockSpec((t, cw), lambda b, c: (b, c)),
                   pl.BlockSpec((None, 2, cw), lambda b, c: (b, 0, c))],
        out_shape=[jax.ShapeDtypeStruct((n_seq * t, r), F32),
                   jax.ShapeDtypeStruct((n_seq, 2, r), F32)],
        scratch_shapes=scr,
        compiler_params=_cparams("arbitrary", "arbitrary"),
        name="rglru_scan",
    )(gu, gu, conv_w, conv_b.reshape(conv_b.shape[0], 1, r), w_a, b_a, w_x, b_x, lam, h0)


def _s5_kernel(*refs, bp, n_seq, nc, gpb):
    ell, gw = S5_L, S5_H
    (x_ref, tm_ref, win_ref, wre_ref, wim_ref, ar_ref, ai_ref, d_ref, s0re_ref, s0im_ref,
     o_ref, fin_ref, xg_s, ure_s, uim_s, fre_s, fim_s, bre_s, bim_s) = refs
    m, mp = n_seq * nc, bp * nc
    lanes = x_ref.shape[1]
    token = lambda l: pl.ds(l, m, stride=ell)
    per_tile = lanes // gw
    half = ure_s.shape[2] // 2
    lane_grp = lax.shift_right_logical(lax.broadcasted_iota(jnp.int32, (m, lanes), 1), gw.bit_length() - 1)

    def perm(shape, chunk_major_axis):
        i = lax.broadcasted_iota(jnp.int32, shape, chunk_major_axis)
        j = lax.broadcasted_iota(jnp.int32, shape, 1 - chunk_major_axis)
        b = i & (bp - 1)
        c = lax.shift_right_logical(i, bp.bit_length() - 1)
        return jnp.where((j == b * nc + c) & (b < n_seq), 1.0, 0.0).astype(BF16)

    to_chunk_major = perm((mp, m), 0)
    to_batch_major = perm((m, mp), 1)

    def block_transpose(v):
        k = per_tile // 2
        while k >= 1:
            low = (lane_grp & k) == 0
            nxt = list(v)
            for i in range(per_tile):
                if i & k == 0:
                    a, b = v[i], v[i + k]
                    nxt[i] = jnp.where(low, a, pltpu.roll(b, k * gw, axis=1))
                    nxt[i + k] = jnp.where(low, pltpu.roll(a, lanes - k * gw, axis=1), b)
            v = nxt
            k //= 2
        return v

    for tile in range(ell // per_tile):
        by_group = block_transpose([x_ref[token(tile * per_tile + j), :] for j in range(per_tile)])
        for g in range(gpb):
            xg_s[g, :, tile * lanes:(tile + 1) * lanes] = by_group[g]

    for g in range(gpb):
        xg = xg_s[g]
        xp = jnp.dot(to_chunk_major, xg.astype(BF16), preferred_element_type=F32).astype(BF16)
        u = jnp.dot(xp, win_ref[g].astype(BF16), preferred_element_type=F32)
        ure_s[g] = u[:, :2 * half]
        uim_s[g] = u[:, 2 * half:]

    is_fwd = lax.broadcasted_iota(jnp.int32, (bp, 2 * half), 1) < half
    ar = [ar_ref[g] for g in range(gpb)]
    ai = [ai_ref[g] for g in range(gpb)]

    def body(k, carry):
        rf = pl.multiple_of(k * bp, bp)
        rb = pl.multiple_of((nc - 1 - k) * bp, bp)
        out = []
        for g in range(gpb):
            re, im = carry[2 * g], carry[2 * g + 1]
            fre_s[g, pl.ds(rf, bp), :] = re
            fim_s[g, pl.ds(rf, bp), :] = im
            bre_s[g, pl.ds(rb, bp), :] = re
            bim_s[g, pl.ds(rb, bp), :] = im
            ure = jnp.where(is_fwd, ure_s[g, pl.ds(rf, bp), :], ure_s[g, pl.ds(rb, bp), :])
            uim = jnp.where(is_fwd, uim_s[g, pl.ds(rf, bp), :], uim_s[g, pl.ds(rb, bp), :])
            out += [ar[g] * re - ai[g] * im + ure, ar[g] * im + ai[g] * re + uim]
        return tuple(out)

    init = tuple(r[g] for g in range(gpb) for r in (s0re_ref, s0im_ref))
    fin = lax.fori_loop(0, nc, body, init)
    fwd_all = lax.broadcasted_iota(jnp.int32, (mp, 2 * half), 1) < half
    for g in range(gpb):
        fin_ref[g, :, :2 * half] = fin[2 * g]
        fin_ref[g, :, 2 * half:] = fin[2 * g + 1]
        hre = jnp.where(fwd_all, fre_s[g], bre_s[g]).astype(BF16)
        him = jnp.where(fwd_all, fim_s[g], bim_s[g]).astype(BF16)
        hre = jnp.dot(to_batch_major, hre, preferred_element_type=F32).astype(BF16)
        him = jnp.dot(to_batch_major, him, preferred_element_type=F32).astype(BF16)
        xg = xg_s[g]
        y = (jnp.dot(xg.astype(BF16), tm_ref[g].astype(BF16), preferred_element_type=F32)
             + jnp.dot(hre, wre_ref[g].astype(BF16), preferred_element_type=F32)
             + jnp.dot(him, wim_ref[g].astype(BF16), preferred_element_type=F32)
             + d_ref[g] * xg)
        xg_s[g] = _gelu(y)

    for tile in range(ell // per_tile):
        by_token = block_transpose([xg_s[g, :, tile * lanes:(tile + 1) * lanes] for g in range(gpb)])
        for j in range(per_tile):
            o_ref[token(tile * per_tile + j), :] = by_token[j]


def s5_chunked(hn, row_blk, n_seq, nc, mats, s0re, s0im, *, bp):
    tmat, win, wre, wim, ar, ai, dg = mats
    g, w, _ = tmat.shape
    p2 = ar.shape[-1]
    ell = S5_L
    lanes = 128
    gpb = lanes // S5_H
    d = hn.shape[1]
    m, mp = n_seq * nc, bp * nc
    n = m * ell
    assert bp & (bp - 1) == 0 and S5_H & (S5_H - 1) == 0 and g % gpb == 0
    blk = lambda shape: pl.BlockSpec((gpb,) + shape, lambda i: (i, 0, 0))
    return pl.pallas_call(
        functools.partial(_s5_kernel, bp=bp, n_seq=n_seq, nc=nc, gpb=gpb),
        grid=(g // gpb,),
        in_specs=[pl.BlockSpec((n, lanes), lambda i: (row_blk, i)),
                  blk((w, w)), blk((w, 2 * p2)), blk((p2, w)), blk((p2, w)),
                  blk((1, p2)), blk((1, p2)), blk((1, w)), blk((bp, p2)), blk((bp, p2))],
        out_specs=[pl.BlockSpec((n, lanes), lambda i: (0, i)), blk((bp, 2 * p2))],
        out_shape=[jax.ShapeDtypeStruct((n, d), F32), jax.ShapeDtypeStruct((g, bp, 2 * p2), F32)],
        scratch_shapes=[pltpu.VMEM((gpb, m, w), F32)] + [pltpu.VMEM((gpb, mp, p2), F32) for _ in range(6)],
        compiler_params=_cparams("arbitrary"),
        name="s5_chunked",
    )(hn, tmat, win, wre, wim, ar, ai, dg, s0re, s0im)


def _cmul(ar, ai, br, bi):
    return ar * br - ai * bi, ar * bi + ai * br


def _s5_prep_kernel(are_ref, aim_ref, ldt_ref, btr_ref, bti_ref, cr_ref, ci_ref,
                    tm_ref, win_ref, wre_ref, wim_ref, ar_ref, ai_ref):
    for gi in range(cr_ref.shape[0]):
        _s5_prep_group(*(r.at[gi] for r in (are_ref, aim_ref, ldt_ref, btr_ref, bti_ref, cr_ref, ci_ref,
                                           tm_ref, win_ref, wre_ref, wim_ref, ar_ref, ai_ref)))


def _s5_prep_group(are_ref, aim_ref, ldt_ref, btr_ref, bti_ref, cr_ref, ci_ref,
                   tm_ref, win_ref, wre_ref, wim_ref, ar_ref, ai_ref):
    ell = S5_L
    h, p2 = cr_ref.shape
    w = ell * h
    a_re, a_im = are_ref[...], aim_ref[...]
    dt = jnp.exp(ldt_ref[...])
    steps = lax.broadcasted_iota(jnp.int32, (3 * SUBLANES, p2), 0).astype(F32)
    mag = jnp.exp(steps * (a_re * dt))
    ang = steps * (a_im * dt)
    pw_r, pw_i = mag * jnp.cos(ang), mag * jnp.sin(ang)
    nr, ni = pw_r[1:2] - 1.0, pw_i[1:2]
    den = a_re * a_re + a_im * a_im
    qr, qi = (nr * a_re + ni * a_im) / den, (ni * a_re - nr * a_im) / den
    bb_r, bb_i = _cmul(qr, qi, btr_ref[...], bti_ref[...])
    c_r, c_i = cr_ref[...], ci_ref[...]
    fwd = lax.broadcasted_iota(jnp.int32, (1, p2), 1) < p2 // 2

    def power_rows(m_fwd, m_bwd):
        return (jnp.where(fwd, pw_r[m_fwd:m_fwd + 1], pw_r[m_bwd:m_bwd + 1]),
                jnp.where(fwd, pw_i[m_fwd:m_fwd + 1], pw_i[m_bwd:m_bwd + 1]))

    def stack(x_r, x_i, powers):
        parts = [_cmul(x_r, x_i, *power_rows(*powers(l))) for l in range(ell)]
        return (jnp.concatenate([q[0] for q in parts], axis=0), jnp.concatenate([q[1] for q in parts], axis=0))

    win_r, win_i = stack(bb_r, bb_i, lambda l: (ell - 1 - l, l))
    win_ref[...] = jnp.concatenate([win_r, win_i], axis=1).astype(win_ref.dtype)
    z_r, z_i = stack(c_r, c_i, lambda l: (l + 1, ell - l))
    wre_ref[...] = z_r.T.astype(wre_ref.dtype)
    wim_ref[...] = (-z_i).T.astype(wim_ref.dtype)
    k_r, k_i = stack(c_r, c_i, lambda m: (m, ell - 1 - m))
    mask_f = jnp.where(fwd, 1.0, 0.0)

    def lag_rows(mask):
        return (lax.dot_general(bb_r * mask, k_r, NT_DIMS, precision=HIGHEST, preferred_element_type=F32)
                - lax.dot_general(bb_i * mask, k_i, NT_DIMS, precision=HIGHEST, preferred_element_type=F32))

    kf = lag_rows(mask_f)
    kb = lag_rows(1.0 - mask_f)
    lane = lax.broadcasted_iota(jnp.int32, (h, w), 1)
    blocks = []
    for li in range(ell):
        f_part = kf if li == 0 else pltpu.roll(kf, li * h, axis=1)
        s_b = (w - (ell - 1 - li) * h) % w
        b_part = kb if s_b == 0 else pltpu.roll(kb, s_b, axis=1)
        blocks.append(jnp.where(lane >= li * h, f_part, 0.0) + jnp.where(lane < (li + 1) * h, b_part, 0.0))
    tm_ref[...] = jnp.concatenate(blocks, axis=0).astype(tm_ref.dtype)
    ar_ref[...] = pw_r[ell:ell + 1]
    ai_ref[...] = pw_i[ell:ell + 1]


def s5_chunk_operators(a_re, a_im, log_dt, b_re, b_im, c_re, c_im, d):
    _, g, p = a_re.shape
    h = b_re.shape[-1]
    w = S5_L * h
    two_dir = lambda x: jnp.transpose(x, (1, 0, 2)).reshape(g, 1, 2 * p)
    ldt = two_dir(jnp.broadcast_to(log_dt[:, :, None], (2, g, p)))
    bt = lambda x: jnp.transpose(x, (1, 3, 0, 2)).reshape(g, h, 2 * p)
    ct = lambda x: jnp.transpose(x, (1, 2, 0, 3)).reshape(g, h, 2 * p)
    gps = math.gcd(g, S5_PREP_GROUPS)
    per_g = lambda shape: pl.BlockSpec((gps,) + shape, lambda i: (i, 0, 0))
    tmat, win, wre, wim, ar, ai = pl.pallas_call(
        _s5_prep_kernel,
        grid=(g // gps,),
        in_specs=[per_g((1, 2 * p))] * 3 + [per_g((h, 2 * p))] * 4,
        out_specs=[per_g((w, w)), per_g((w, 4 * p)), per_g((2 * p, w)), per_g((2 * p, w)),
                   per_g((1, 2 * p)), per_g((1, 2 * p))],
        out_shape=[jax.ShapeDtypeStruct((g, w, w), BF16), jax.ShapeDtypeStruct((g, w, 4 * p), BF16),
                   jax.ShapeDtypeStruct((g, 2 * p, w), BF16), jax.ShapeDtypeStruct((g, 2 * p, w), BF16),
                   jax.ShapeDtypeStruct((g, 1, 2 * p), F32), jax.ShapeDtypeStruct((g, 1, 2 * p), F32)],
        compiler_params=_cparams("arbitrary"),
        name="s5_chunk_operators",
    )(two_dir(a_re), two_dir(a_im), ldt, bt(b_re), bt(b_im), ct(c_re), ct(c_im))
    dg = jnp.tile(d.reshape(g, 1, h), (1, S5_L, 1)).reshape(g, 1, w)
    return tmat, win, wre, wim, ar, ai, dg


def s5_mixer_group(hn, row0, n_seq, t, mats, s0):
    ar = mats[4]
    g = ar.shape[0]
    p = ar.shape[-1] // 2
    nc = t // S5_L
    bp = -(-n_seq // SUBLANES) * SUBLANES
    n = n_seq * t
    assert row0 % n == 0
    if s0 is None:
        s0re = jnp.zeros((g, bp, 2 * p), F32)
        s0im = s0re
    else:
        st = jnp.transpose(s0, (3, 0, 2, 1, 4)).reshape(g, n_seq, 2, 2 * p)
        st = jnp.pad(st, ((0, 0), (0, bp - n_seq), (0, 0), (0, 0)))
        s0re, s0im = st[:, :, 0], st[:, :, 1]
    u, fin = s5_chunked(hn, row0 // n, n_seq, nc, mats, s0re, s0im, bp=bp)
    fin = fin.reshape(g, bp, 2, 2, p)[:, :n_seq]
    return u, jnp.transpose(fin, (1, 3, 2, 0, 4))


def _softmax_pv(scores, values, sink):
    m = sink
    for s in scores:
        m = jnp.maximum(m, jnp.max(s, axis=-1, keepdims=True))
    den = jnp.exp(sink - m)
    acc = None
    for s, v in zip(scores, values):
        p = jnp.exp(s - m)
        den = den + jnp.sum(p, axis=-1, keepdims=True)
        pv = jnp.dot(p.astype(BF16), v, preferred_element_type=F32)
        acc = pv if acc is None else acc + pv
    return acc / den


def _attn_prompt_kernel(sink_ref, q_ref, k_ref, v_ref, o_ref):
    k = k_ref[...].astype(BF16)
    v = v_ref[...].astype(BF16)
    for h in range(N_KV):
        hs = slice(h * HEAD_DIM, (h + 1) * HEAD_DIM)
        kh, vh = k[:, hs], v[:, hs]
        for g in range(Q_PER_KV):
            c0 = (h * Q_PER_KV + g) * HEAD_DIM
            qg = (q_ref[:, c0:c0 + HEAD_DIM] * ATT_SCALE).astype(BF16)
            s = lax.dot_general(qg, kh, NT_DIMS, preferred_element_type=F32)
            o_ref[:, c0:c0 + HEAD_DIM] = _softmax_pv([s], [vh], sink_ref[h * Q_PER_KV + g])


def attn_prompt(qkv, sink, n_seq, t):
    dq = N_HEADS * HEAD_DIM
    kcol = dq // KV_W
    return pl.pallas_call(
        _attn_prompt_kernel,
        grid=(n_seq,),
        in_specs=[pl.BlockSpec(memory_space=pltpu.SMEM),
                  pl.BlockSpec((t, dq), lambda b: (b, 0)),
                  pl.BlockSpec((t, KV_W), lambda b: (b, kcol)),
                  pl.BlockSpec((t, KV_W), lambda b: (b, kcol + 1))],
        out_specs=pl.BlockSpec((t, dq), lambda b: (b, 0)),
        out_shape=jax.ShapeDtypeStruct((n_seq * t, dq), F32),
        compiler_params=_cparams("arbitrary"),
        name="attn_prompt",
    )(sink, qkv, qkv, qkv)


def _rope(x, cos, sin):
    w = x.shape[1]
    low = (lax.broadcasted_iota(jnp.int32, x.shape, 1) & (HEAD_DIM // 4)) == 0
    partner = jnp.where(low, pltpu.roll(x, w - HEAD_DIM // 4, axis=1), pltpu.roll(x, HEAD_DIM // 4, axis=1))
    return x * cos + partner * sin


def _attn_sample_kernel(sink_ref, q_ref, k_ref, v_ref, kc_ref, vc_ref, cos_ref, sin_ref, o_ref,
                        kw_s, vw_s, kc_s, vc_s, *, t):
    n = pl.program_id(1)
    blk = ATT_BLOCK

    @pl.when(n == 0)
    def _():
        zeros = jnp.zeros((blk, KV_W), BF16)
        kw_s[0:blk, :] = zeros
        vw_s[0:blk, :] = zeros
        kw_s[blk + t:2 * blk + t, :] = zeros
        vw_s[blk + t:2 * blk + t, :] = zeros
        kw_s[blk:blk + t, :] = _rope(k_ref[...], cos_ref[...], sin_ref[...]).astype(BF16)
        vw_s[blk:blk + t, :] = v_ref[...].astype(BF16)
        kc_s[...] = kc_ref[...].astype(BF16)
        vc_s[...] = vc_ref[...].astype(BF16)

    r0 = pl.multiple_of(n * blk, blk)
    cq = cos_ref[pl.ds(r0, blk), :]
    sq = sin_ref[pl.ds(r0, blk), :]
    kw = kw_s[pl.ds(r0, 3 * blk), :]
    vw = vw_s[pl.ds(r0, 3 * blk), :]
    qi = lax.broadcasted_iota(jnp.int32, (blk, 3 * blk), 0)
    kj = lax.broadcasted_iota(jnp.int32, (blk, 3 * blk), 1)
    kpos = n * blk - blk + kj
    valid = (jnp.abs(kj - blk - qi) <= WINDOW) & (kpos >= 0) & (kpos < t)
    for h in range(N_KV):
        hs = slice(h * HEAD_DIM, (h + 1) * HEAD_DIM)
        qh = (_rope(q_ref[:, h * KV_W:(h + 1) * KV_W], cq, sq) * ATT_SCALE).astype(BF16)
        kh, vh, kch, vch = kw[:, hs], vw[:, hs], kc_s[:, hs], vc_s[:, hs]
        for g in range(Q_PER_KV):
            qg = qh[:, g * HEAD_DIM:(g + 1) * HEAD_DIM]
            s_loc = lax.dot_general(qg, kh, NT_DIMS, preferred_element_type=F32)
            s_loc = jnp.where(valid, s_loc, NEG_INF)
            s_ctx = lax.dot_general(qg, kch, NT_DIMS, preferred_element_type=F32)
            c0 = (h * Q_PER_KV + g) * HEAD_DIM
            o_ref[:, c0:c0 + HEAD_DIM] = _softmax_pv([s_loc, s_ctx], [vh, vch], sink_ref[h * Q_PER_KV + g])


def _rope_tables(t):
    quarter = HEAD_DIM // 4
    freqs = ROPE_BASE ** (-jnp.arange(quarter, dtype=F32) / quarter)
    pos = jnp.arange(t)
    ang_r = (pos // GRID_W).astype(F32)[:, None] * freqs
    ang_c = (pos % GRID_W).astype(F32)[:, None] * freqs
    cos = jnp.concatenate([jnp.cos(ang_r), jnp.cos(ang_r), jnp.cos(ang_c), jnp.cos(ang_c)], axis=-1)
    sin = jnp.concatenate([-jnp.sin(ang_r), jnp.sin(ang_r), -jnp.sin(ang_c), jnp.sin(ang_c)], axis=-1)
    return jnp.tile(cos, (1, N_KV)), jnp.tile(sin, (1, N_KV))


def attn_sample(qkv, row0, sink, n_seq, t, k_ctx, v_ctx):
    dq = N_HEADS * HEAD_DIM
    kcol = dq // KV_W
    nb = t // ATT_BLOCK
    lc = k_ctx.shape[1]
    cos, sin = _rope_tables(t)
    qblk0, sblk0 = row0 // ATT_BLOCK, row0 // t
    return pl.pallas_call(
        functools.partial(_attn_sample_kernel, t=t),
        grid=(n_seq, nb),
        in_specs=[pl.BlockSpec(memory_space=pltpu.SMEM),
                  pl.BlockSpec((ATT_BLOCK, dq), lambda b, n: (qblk0 + b * nb + n, 0)),
                  pl.BlockSpec((t, KV_W), lambda b, n: (sblk0 + b, kcol)),
                  pl.BlockSpec((t, KV_W), lambda b, n: (sblk0 + b, kcol + 1)),
                  pl.BlockSpec((None, lc, KV_W), lambda b, n: (b, 0, 0)),
                  pl.BlockSpec((None, lc, KV_W), lambda b, n: (b, 0, 0)),
                  pl.BlockSpec((t, KV_W), lambda b, n: (0, 0)),
                  pl.BlockSpec((t, KV_W), lambda b, n: (0, 0))],
        out_specs=pl.BlockSpec((ATT_BLOCK, dq), lambda b, n: (b * nb + n, 0)),
        out_shape=jax.ShapeDtypeStruct((n_seq * t, dq), F32),
        scratch_shapes=[pltpu.VMEM((t + 2 * ATT_BLOCK, KV_W), BF16), pltpu.VMEM((t + 2 * ATT_BLOCK, KV_W), BF16),
                        pltpu.VMEM((lc, KV_W), BF16), pltpu.VMEM((lc, KV_W), BF16)],
        compiler_params=_cparams("arbitrary", "arbitrary"),
        name="attn_sample",
    )(sink, qkv, qkv, qkv, k_ctx, v_ctx, cos, sin)


def _moe_kernel(x_ref, wg_ref, wu_ref, wd_ref, gs_ref, g2_ref, o_ref, wg_bf, wu_bf, wd_bf, *, n_seg):
    wg_bf[...] = wg_ref[...].astype(BF16)
    wu_bf[...] = wu_ref[...].astype(BF16)
    wd_bf[...] = wd_ref[...].astype(BF16)
    rsub = min(MOE_ROW_SUB, x_ref.shape[0])
    for r in range(x_ref.shape[0] // rsub):
        rs = slice(r * rsub, (r + 1) * rsub)
        x = x_ref[rs, :]
        hg = jnp.dot(x, wg_bf[...], preferred_element_type=F32)
        hu = jnp.dot(x, wu_bf[...], preferred_element_type=F32)
        he = ((hg * jax.nn.sigmoid(hg)) * hu).astype(BF16)
        y = jnp.dot(he, wd_bf[...], preferred_element_type=F32)
        gs = gs_ref[rs, :]
        seg = gs[:, 1:2]
        g2 = jnp.zeros(y.shape, F32)
        for s in range(n_seg):
            g2 = jnp.where(seg == float(s), g2_ref[s:s + 1, :], g2)
        o_ref[rs, :] = y * (gs[:, 0:1] * g2)


def moe_experts(xe, w_gate, w_up, w_down, layer, gate_seg, mod, n_seg):
    ne, r, d = xe.shape
    dff = w_gate.shape[3]
    return pl.pallas_call(
        functools.partial(_moe_kernel, n_seg=n_seg),
        grid=(ne,),
        in_specs=[pl.BlockSpec((None, r, d), lambda e: (e, 0, 0)),
                  pl.BlockSpec((None, None, d, dff), lambda e: (layer, e, 0, 0)),
                  pl.BlockSpec((None, None, d, dff), lambda e: (layer, e, 0, 0)),
                  pl.BlockSpec((None, None, dff, d), lambda e: (layer, e, 0, 0)),
                  pl.BlockSpec((None, r, 2), lambda e: (e, 0, 0)),
                  pl.BlockSpec((SUBLANES, None, d), lambda e: (layer, 0, 5))],
        out_specs=pl.BlockSpec((None, r, d), lambda e: (e, 0, 0)),
        out_shape=jax.ShapeDtypeStruct((ne, r, d), F32),
        scratch_shapes=[pltpu.VMEM((d, dff), BF16), pltpu.VMEM((d, dff), BF16), pltpu.VMEM((dff, d), BF16)],
        compiler_params=_cparams("arbitrary"),
        name="moe_experts",
    )(xe, w_gate, w_up, w_down, gate_seg, mod)


def _expert_choice(aff, rows):
    ne = aff.shape[1]
    sizes = (rows.n_prompt, rows.n_sample)
    caps = [(EC_FACTOR * n) // ne for n in sizes]
    if sizes[0] == sizes[1]:
        n = sizes[0]
        gt, ix = lax.top_k(jnp.swapaxes(aff.reshape(2, n, ne), 1, 2), caps[0])
        ix = ix + jnp.array([0, n], jnp.int32)[:, None, None]
        return (jnp.concatenate([gt[0], gt[1]], axis=1), jnp.concatenate([ix[0], ix[1]], axis=1))
    gts, ixs, off = [], [], 0
    for n, cap in zip(sizes, caps):
        gt, ix = lax.top_k(aff[off:off + n].T, cap)
        gts.append(gt)
        ixs.append(ix + off)
        off += n
    return jnp.concatenate(gts, axis=1), jnp.concatenate(ixs, axis=1)


def _combine_kernel(idx_ref, x_hbm, ye_ref, o_hbm, acc, sem, *, n_grp, rows_per):
    g = pl.program_id(0)
    e = pl.program_id(1)
    last = pl.num_programs(1) - 1

    def load(grp):
        return pltpu.make_async_copy(x_hbm.at[pl.ds(grp * n_grp, n_grp), :], acc.at[grp], sem.at[grp])

    def store(grp):
        return pltpu.make_async_copy(acc.at[grp], o_hbm.at[pl.ds(grp * n_grp, n_grp), :], sem.at[2 + grp])

    @pl.when((g == 0) & (e == 0))
    def _():
        load(0).start()
        load(1).start()

    @pl.when(e == 0)
    def _():
        load(g).wait()

    def add_rows(grp):
        base = (2 * e + grp) * rows_per

        def body(i, carry):
            toks = [idx_ref[base + i * COMBINE_UNROLL + k] for k in range(COMBINE_UNROLL)]
            old = [acc[grp, pl.ds(t, 1), :] for t in toks]
            add = [ye_ref[pl.ds(i * COMBINE_UNROLL + k, 1), :] for k in range(COMBINE_UNROLL)]
            for t, a, b in zip(toks, old, add):
                acc[grp, pl.ds(t, 1), :] = a + b
            return carry

        lax.fori_loop(0, rows_per // COMBINE_UNROLL, body, 0)

    pl.when(g == 0)(lambda: add_rows(0))
    pl.when(g == 1)(lambda: add_rows(1))

    @pl.when(e == last)
    def _():
        store(g).start()

    @pl.when((g == 1) & (e == last))
    def _():
        store(0).wait()
        store(1).wait()


def moe_combine(x, ye, idx, n_grp):
    m, d = x.shape
    ne, r, _ = ye.shape
    rows_per = r // 2
    assert m == 2 * n_grp and rows_per % COMBINE_UNROLL == 0
    grid_spec = pltpu.PrefetchScalarGridSpec(
        num_scalar_prefetch=1,
        grid=(2, ne),
        in_specs=[pl.BlockSpec(memory_space=pl.ANY),
                  pl.BlockSpec((None, rows_per, d), lambda g, e, ix: (e, g, 0))],
        out_specs=pl.BlockSpec(memory_space=pl.ANY),
        scratch_shapes=[pltpu.VMEM((2, n_grp, d), F32), pltpu.SemaphoreType.DMA((4,))])
    return pl.pallas_call(
        functools.partial(_combine_kernel, n_grp=n_grp, rows_per=rows_per),
        grid_spec=grid_spec,
        out_shape=jax.ShapeDtypeStruct((m, d), F32),
        compiler_params=_cparams("arbitrary", "arbitrary"),
        name="moe_combine",
    )(idx.reshape(-1), x, ye)


def moe_layer(x, h2, aff, mod, rows, layer, w_gate, w_up, w_down, n_seg):
    m, d = x.shape
    gates, idx = _expert_choice(aff, rows)
    seg = jnp.where(idx < rows.n_prompt, 0, 1 + (idx - rows.n_prompt) // rows.t_sample)
    xe = h2[idx]
    gate_seg = jnp.stack([gates, seg.astype(F32)], axis=-1)
    ye = moe_experts(xe, w_gate, w_up, w_down, layer, gate_seg, mod, n_seg)
    if rows.n_prompt == rows.n_sample:
        return moe_combine(x, ye, idx % rows.n_prompt, rows.n_prompt)
    return x.at[idx.reshape(-1)].add(ye.reshape(-1, d))


def kernel(x_prompt, x_sample, state_rglru, state_s5, cache_k, cache_v, c, c_ctx, ada_w, ada_b, norm1_g, norm2_g, rg_w_in, rg_conv_w, rg_conv_b, rg_w_a, rg_b_a, rg_w_x, rg_b_x, rg_lambda, rg_w_out, s5_a_re, s5_a_im, s5_log_dt, s5_b_re, s5_b_im, s5_c_re, s5_c_im, s5_d, s5_w_glu, attn_w_qkv, attn_w_o, attn_sink, router_w, moe_w_gate, moe_w_up, moe_w_down, final_norm_g):
    bp_, tp, d = x_prompt.shape
    bs, ts, _ = x_sample.shape
    n_p, n_s = bp_ * tp, bs * ts
    depth = ada_w.shape[0]
    rows = _Rows(n_p, n_s, ts)
    assert bs + 1 <= SUBLANES and n_p % ts == 0

    xs = (x_prompt.reshape(n_p, d), x_sample.reshape(n_s, d), 0)
    cond = jnp.concatenate([c_ctx[None, :], c, jnp.zeros((SUBLANES - 1 - bs, d), F32)], axis=0)
    mod_all = ada_modulation_all(cond, ada_w, ada_b)
    mod = mod_all.reshape(depth * SUBLANES, 1, 6 * d)
    g1 = norm1_g.reshape(depth, 1, d)
    g2 = norm2_g.reshape(depth, 1, d)

    new_rg, new_s5, new_k, new_v = [], [], [], []
    for l in range(depth):
        kind, j = l % 3, l // 3
        if kind == 0:
            gu = norm_mod_matmul(xs, g1, mod, rows, l, rg_w_in, j)
            args = (j, rg_conv_w, rg_conv_b, rg_w_a, rg_b_a, rg_w_x, rg_b_x, rg_lambda)
            r = gu.shape[1] // 2
            yp, fin = rglru_scan(gu, 0, bp_, tp, *args, jnp.zeros((bp_, 1, 2, r), F32), 0)
            ys, _ = rglru_scan(gu, n_p, bs, ts, *args, state_rglru, j)
            new_rg.append(fin)
            x, h2, aff = matmul_gated_residual(yp, ys, rg_w_out, j, xs, mod, rows, l, g2, router_w)
        elif kind == 1:
            hn = norm_mod(x, g1, mod, rows, l)
            mats = s5_chunk_operators(s5_a_re[j], s5_a_im[j], s5_log_dt[j], s5_b_re[j], s5_b_im[j],
                                      s5_c_re[j], s5_c_im[j], s5_d[j])
            up, st = s5_mixer_group(hn, 0, bp_, tp, mats, None)
            us, _ = s5_mixer_group(hn, n_p, bs, ts, mats, state_s5[:, j])
            new_s5.append(st)
            x, h2, aff = glu_gated_residual(up, us, s5_w_glu, j, x, mod, rows, l, g2, router_w)
        else:
            qkv = norm_mod_matmul(xs, g1, mod, rows, l, attn_w_qkv, j)
            dq = N_HEADS * HEAD_DIM
            new_k.append(qkv[:n_p, dq:dq + KV_W].reshape(bp_, tp, N_KV, HEAD_DIM))
            new_v.append(qkv[:n_p, dq + KV_W:].reshape(bp_, tp, N_KV, HEAD_DIM))
            op = attn_prompt(qkv, attn_sink[j], bp_, tp)
            lc = cache_k.shape[2]
            os_ = attn_sample(qkv, n_p, attn_sink[j], bs, ts,
                              cache_k[:, j].reshape(bs, lc, KV_W), cache_v[:, j].reshape(bs, lc, KV_W))
            x, h2, aff = matmul_gated_residual(op, os_, attn_w_o, j, xs, mod, rows, l, g2, router_w)
        x = moe_layer(x, h2, aff, mod, rows, l, moe_w_gate, moe_w_up, moe_w_down, bs + 1)
        xs = (x, x, rows.prompt_blocks)

    y_p, y_s = final_norm(x, final_norm_g, rows)
    return (y_p.reshape(bp_, tp, d), y_s.reshape(bs, ts, d),
            jnp.stack(new_rg, axis=1), jnp.stack(new_s5, axis=1),
            jnp.stack(new_k, axis=1), jnp.stack(new_v, axis=1))
```

```python
import functools
import math

import jax
import jax.numpy as jnp
from jax import lax
from jax.experimental import pallas as pl
from jax.experimental.pallas import tpu as pltpu

F32 = jnp.float32
BF16 = jnp.bfloat16
HIGHEST = lax.Precision.HIGHEST

EPS = 1e-6
RG_C = 8.0
RG_BS = 128
RG_TILE = 512
S5_H = 16
S5_L = 16
S5_PREP_GROUPS = 4
N_HEADS = 16
N_KV = 4
Q_PER_KV = N_HEADS // N_KV
HEAD_DIM = 64
KV_W = N_KV * HEAD_DIM
GRID_W = 64
WINDOW = 128
ATT_BLOCK = 128
ROPE_BASE = 10000.0
ATT_SCALE = HEAD_DIM ** -0.5
assert math.frexp(ATT_SCALE)[0] == 0.5
NEG_INF = -1e30
EC_FACTOR = 2
SUBLANES = 8
ROW_TILE = 512
MOE_ROW_SUB = 256
COMBINE_UNROLL = 16
VMEM_LIMIT = 56 * 1024 * 1024
NT_DIMS = (((1,), (1,)), ((), ()))


def _cparams(*sem):
    return pltpu.CompilerParams(dimension_semantics=sem, vmem_limit_bytes=VMEM_LIMIT)


def _gelu(x):
    return x * (0.5 * (1.0 + jnp.tanh(math.sqrt(2.0 / math.pi) * (x + 0.044715 * (x * x * x)))))


def _sigmoid(x):
    return 0.5 * jnp.tanh(0.5 * x) + 0.5


def _norm_mod(x, g, sc, sh):
    ms = jnp.mean(x * x, axis=-1, keepdims=True)
    return ((x * lax.rsqrt(ms + EPS)) * g) * (1.0 + sc) + sh


def _mod_kernel(c_ref, w_ref, b_ref, o_ref):
    c = c_ref[...]
    s = (c * jax.nn.sigmoid(c)).astype(BF16)
    o_ref[...] = jnp.dot(s, w_ref[...].astype(BF16), preferred_element_type=F32) + b_ref[...]


def ada_modulation_all(cond, ada_w, ada_b):
    n_layers, d, n = ada_w.shape
    tn = 1536
    return pl.pallas_call(
        _mod_kernel,
        grid=(n_layers, n // tn),
        in_specs=[pl.BlockSpec((SUBLANES, d), lambda l, j: (0, 0)),
                  pl.BlockSpec((None, d, tn), lambda l, j: (l, 0, j)),
                  pl.BlockSpec((None, 1, tn), lambda l, j: (l, 0, j))],
        out_specs=pl.BlockSpec((None, SUBLANES, tn), lambda l, j: (l, 0, j)),
        out_shape=jax.ShapeDtypeStruct((n_layers, SUBLANES, n), F32),
        compiler_params=_cparams("arbitrary", "arbitrary"),
        name="ada_mod",
    )(cond, ada_w, ada_b.reshape(n_layers, 1, n))


class _Rows:
    def __init__(self, n_prompt, n_sample, t_sample):
        self.n_prompt = n_prompt
        self.n_sample = n_sample
        self.t_sample = t_sample
        self.tm = min(ROW_TILE, n_prompt, t_sample)
        assert n_prompt % self.tm == 0 and t_sample % self.tm == 0
        self.prompt_blocks = n_prompt // self.tm
        self.sample_blocks = n_sample // self.tm

    def seg(self, i):
        r = i * self.tm
        return jnp.where(r < self.n_prompt, 0, 1 + lax.div(r - self.n_prompt, self.t_sample))


def _mod_spec(rows, layer, width, chunk, m_axis):
    def imap(*ids):
        return (layer * SUBLANES + rows.seg(ids[m_axis]), 0, chunk)
    return pl.BlockSpec((None, 1, width), imap)


def _gain_spec(layer, d):
    return pl.BlockSpec((None, 1, d), lambda *ids: (layer, 0, 0))


def _stream_specs(rows, xs, m_axis):
    top, _, bot0 = xs
    npb, nsb, tm = rows.prompt_blocks, rows.sample_blocks, rows.tm
    d = top.shape[1]
    return [pl.BlockSpec((tm, d), lambda *ids: (jnp.minimum(ids[m_axis], npb - 1), 0)),
            pl.BlockSpec((tm, d), lambda *ids: (bot0 + jnp.clip(ids[m_axis] - npb, 0, nsb - 1), 0))]


def _nm_kernel(xt_ref, xb_ref, g_ref, sc_ref, sh_ref, w_ref, o_ref, wbf_ref, *, npb):
    i = pl.program_id(1)

    @pl.when(i == 0)
    def _():
        wbf_ref[...] = w_ref[...].astype(BF16)
    x = jnp.where(i < npb, xt_ref[...], xb_ref[...])
    h = _norm_mod(x, g_ref[...], sc_ref[...], sh_ref[...])
    o_ref[...] = jnp.dot(h.astype(BF16), wbf_ref[...], preferred_element_type=F32)


def norm_mod_matmul(xs, gains, mod, rows, layer, w, wl):
    m, d = rows.n_prompt + rows.n_sample, xs[0].shape[1]
    n = w.shape[2]
    tm = rows.tm
    tn = n
    return pl.pallas_call(
        functools.partial(_nm_kernel, npb=rows.prompt_blocks),
        grid=(n // tn, m // tm),
        in_specs=_stream_specs(rows, xs, 1) + [
                  _gain_spec(layer, d),
                  _mod_spec(rows, layer, d, 1, 1),
                  _mod_spec(rows, layer, d, 0, 1),
                  pl.BlockSpec((None, d, tn), lambda j, i: (wl, 0, j))],
        out_specs=pl.BlockSpec((tm, tn), lambda j, i: (i, j)),
        out_shape=jax.ShapeDtypeStruct((m, n), F32),
        scratch_shapes=[pltpu.VMEM((d, tn), BF16)],
        compiler_params=_cparams("arbitrary", "arbitrary"),
        name="norm_mod_matmul",
    )(xs[0], xs[1], gains, mod, mod, w)


def _norm_only_kernel(x_ref, g_ref, sc_ref, sh_ref, o_ref):
    o_ref[...] = _norm_mod(x_ref[...], g_ref[...], sc_ref[...], sh_ref[...])


def norm_mod(x, gains, mod, rows, layer):
    m, d = x.shape
    tm = rows.tm
    return pl.pallas_call(
        _norm_only_kernel,
        grid=(m // tm,),
        in_specs=[pl.BlockSpec((tm, d), lambda i: (i, 0)),
                  _gain_spec(layer, d),
                  _mod_spec(rows, layer, d, 1, 0),
                  _mod_spec(rows, layer, d, 0, 0)],
        out_specs=pl.BlockSpec((tm, d), lambda i: (i, 0)),
        out_shape=jax.ShapeDtypeStruct((m, d), F32),
        compiler_params=_cparams("arbitrary"),
        name="norm_mod",
    )(x, gains, mod, mod)


def _final_norm_kernel(x_ref, g_ref, op_ref, os_ref, *, npb):
    i = pl.program_id(0)
    x = x_ref[...]
    ms = jnp.mean(x * x, axis=-1, keepdims=True)
    y = (x * lax.rsqrt(ms + EPS)) * g_ref[...]

    @pl.when(i < npb)
    def _():
        op_ref[...] = y

    @pl.when(i >= npb)
    def _():
        os_ref[...] = y


def final_norm(x, g, rows):
    m, d = x.shape
    tm, npb, nsb = rows.tm, rows.prompt_blocks, rows.sample_blocks
    return pl.pallas_call(
        functools.partial(_final_norm_kernel, npb=npb),
        grid=(m // tm,),
        in_specs=[pl.BlockSpec((tm, d), lambda i: (i, 0)),
                  pl.BlockSpec((1, d), lambda i: (0, 0))],
        out_specs=[pl.BlockSpec((tm, d), lambda i: (jnp.minimum(i, npb - 1), 0)),
                   pl.BlockSpec((tm, d), lambda i: (jnp.clip(i - npb, 0, nsb - 1), 0))],
        out_shape=[jax.ShapeDtypeStruct((rows.n_prompt, d), F32), jax.ShapeDtypeStruct((rows.n_sample, d), F32)],
        compiler_params=_cparams("arbitrary"),
        name="final_norm",
    )(x, g.reshape(1, d))


def _router_epilogue(x_new, g_ref, sc_ref, sh_ref, rw_ref, h_ref, aff_ref):
    h = _norm_mod(x_new, g_ref[...], sc_ref[...], sh_ref[...])
    h_hi = h.astype(BF16)
    h_ref[...] = h_hi
    h_lo = (h - h_hi.astype(F32)).astype(BF16)
    rw = rw_ref[...]
    rw_hi = rw.astype(BF16)
    rw_lo = (rw - rw_hi.astype(F32)).astype(BF16)
    ne = rw.shape[1]
    both = jnp.dot(h_hi, jnp.concatenate([rw_hi, rw_lo], axis=1), preferred_element_type=F32)
    logits = both[:, :ne] + (both[:, ne:] + jnp.dot(h_lo, rw_hi, preferred_element_type=F32))
    e = jnp.exp(logits - jnp.max(logits, axis=-1, keepdims=True))
    aff_ref[...] = e / jnp.sum(e, axis=-1, keepdims=True)


def _mmres_kernel(ap_ref, as_ref, w_ref, rt_ref, rb_ref, gt_ref, g_ref, sc_ref, sh_ref, rw_ref,
                  o_ref, h_ref, aff_ref, wbf_ref, *, npb):
    i = pl.program_id(0)

    @pl.when(i == 0)
    def _():
        wbf_ref[...] = w_ref[...].astype(BF16)

    def emit(a_ref, r_ref):
        acc = jnp.dot(a_ref[...].astype(BF16), wbf_ref[...], preferred_element_type=F32)
        x_new = r_ref[...] + gt_ref[...] * acc
        o_ref[...] = x_new
        _router_epilogue(x_new, g_ref, sc_ref, sh_ref, rw_ref, h_ref, aff_ref)

    pl.when(i < npb)(lambda: emit(ap_ref, rt_ref))
    pl.when(i >= npb)(lambda: emit(as_ref, rb_ref))


def _sublayer2_specs(rows, layer, d, ne):
    tm = rows.tm
    ins = [_gain_spec(layer, d), _mod_spec(rows, layer, d, 4, 0), _mod_spec(rows, layer, d, 3, 0),
           pl.BlockSpec((None, d, ne), lambda i: (layer, 0, 0))]
    outs = [pl.BlockSpec((tm, d), lambda i: (i, 0)), pl.BlockSpec((tm, ne), lambda i: (i, 0))]
    return ins, outs


def _two_group_specs(rows, k):
    npb, nsb, tm = rows.prompt_blocks, rows.sample_blocks, rows.tm
    return [pl.BlockSpec((tm, k), lambda i: (jnp.minimum(i, npb - 1), 0)),
            pl.BlockSpec((tm, k), lambda i: (jnp.clip(i - npb, 0, nsb - 1), 0))]


def matmul_gated_residual(a_p, a_s, w, wl, xs, mod, rows, layer, gains2, router_w):
    k = a_p.shape[1]
    m, d = rows.n_prompt + rows.n_sample, xs[0].shape[1]
    tm = rows.tm
    ne = router_w.shape[2]
    r_in, r_out = _sublayer2_specs(rows, layer, d, ne)
    return pl.pallas_call(
        functools.partial(_mmres_kernel, npb=rows.prompt_blocks),
        grid=(m // tm,),
        in_specs=_two_group_specs(rows, k) + [
            pl.BlockSpec((None, k, d), lambda i: (wl, 0, 0))] + _stream_specs(rows, xs, 0) + [
            _mod_spec(rows, layer, d, 2, 0)] + r_in,
        out_specs=[pl.BlockSpec((tm, d), lambda i: (i, 0))] + r_out,
        out_shape=[jax.ShapeDtypeStruct((m, d), F32), jax.ShapeDtypeStruct((m, d), BF16),
                   jax.ShapeDtypeStruct((m, ne), F32)],
        scratch_shapes=[pltpu.VMEM((k, d), BF16)],
        compiler_params=_cparams("arbitrary"),
        name="matmul_gated_residual",
    )(a_p, a_s, w, xs[0], xs[1], mod, gains2, mod, mod, router_w)


def _glures_kernel(ap_ref, as_ref, wv_ref, wg_ref, r_ref, gt_ref, g_ref, sc_ref, sh_ref, rw_ref,
                   o_ref, h_ref, aff_ref, wv_bf, wg_bf, *, npb):
    i = pl.program_id(0)

    @pl.when(i == 0)
    def _():
        wv_bf[...] = wv_ref[...].astype(BF16)
        wg_bf[...] = wg_ref[...].astype(BF16)

    def emit(a_ref):
        a = a_ref[...].astype(BF16)
        v = jnp.dot(a, wv_bf[...], preferred_element_type=F32)
        g = jnp.dot(a, wg_bf[...], preferred_element_type=F32)
        x_new = r_ref[...] + gt_ref[...] * (v * jax.nn.sigmoid(g))
        o_ref[...] = x_new
        _router_epilogue(x_new, g_ref, sc_ref, sh_ref, rw_ref, h_ref, aff_ref)

    pl.when(i < npb)(lambda: emit(ap_ref))
    pl.when(i >= npb)(lambda: emit(as_ref))


def glu_gated_residual(a_p, a_s, w_glu, wl, resid, mod, rows, layer, gains2, router_w):
    k = a_p.shape[1]
    m, d = resid.shape
    tm = rows.tm
    ne = router_w.shape[2]
    r_in, r_out = _sublayer2_specs(rows, layer, d, ne)
    return pl.pallas_call(
        functools.partial(_glures_kernel, npb=rows.prompt_blocks),
        grid=(m // tm,),
        in_specs=_two_group_specs(rows, k) + [
            pl.BlockSpec((None, k, d), lambda i: (wl, 0, 0)),
            pl.BlockSpec((None, k, d), lambda i: (wl, 0, 1)),
            pl.BlockSpec((tm, d), lambda i: (i, 0)),
            _mod_spec(rows, layer, d, 2, 0)] + r_in,
        out_specs=[pl.BlockSpec((tm, d), lambda i: (i, 0))] + r_out,
        out_shape=[jax.ShapeDtypeStruct((m, d), F32), jax.ShapeDtypeStruct((m, d), BF16),
                   jax.ShapeDtypeStruct((m, ne), F32)],
        scratch_shapes=[pltpu.VMEM((k, d), BF16), pltpu.VMEM((k, d), BF16)],
        compiler_params=_cparams("arbitrary"),
        name="glu_gated_residual",
    )(a_p, a_s, w_glu, w_glu, resid, mod, gains2, mod, mod, router_w)


def _rglru_kernel(gate_ref, u_ref, cw_ref, cb_ref, wa_ref, ba_ref, wx_ref, bx_ref, lam_ref, h0_ref,
                  y_ref, fin_ref, af_s, bf_s, ab_s, bb_s, hf_s, hb_s):
    t, cw = u_ref.shape
    u = u_ref[...]
    row = lax.broadcasted_iota(jnp.int32, (t, cw), 0)

    def shifted(x, k):
        if k > 0:
            return jnp.where(row >= k, pltpu.roll(x, k, axis=0), 0.0)
        return jnp.where(row < t + k, pltpu.roll(x, t + k, axis=0), 0.0)

    cwv = cw_ref[...]
    uc = (cwv[0:1] * shifted(u, 2) + cwv[1:2] * shifted(u, 1) + cwv[2:3] * u
          + cwv[3:4] * shifted(u, -1) + cb_ref[...])

    a_scr = (af_s, ab_s)
    b_scr = (bf_s, bb_s)
    for k in range(2):
        nl = -lam_ref[k:k + 1, :]
        sp = jnp.maximum(nl, 0.0) + jnp.log1p(jnp.exp(-jnp.abs(nl)))
        for hh in range(cw // RG_BS):
            sl = slice(hh * RG_BS, (hh + 1) * RG_BS)
            uh = uc[:, sl]
            ub = uh.astype(BF16)
            r = _sigmoid(jnp.dot(ub, wa_ref[k, hh].astype(BF16), preferred_element_type=F32) + ba_ref[k:k + 1, sl])
            i = _sigmoid(jnp.dot(ub, wx_ref[k, hh].astype(BF16), preferred_element_type=F32) + bx_ref[k:k + 1, sl])
            log_a = (-RG_C * r) * sp[:, sl]
            a = jnp.exp(log_a)
            a_scr[k][:, sl] = a
            b_scr[k][:, sl] = jnp.sqrt(jnp.tanh(-log_a) * (a * a + 1.0)) * (i * uh)

    nblk = t // SUBLANES
    srow = lax.broadcasted_iota(jnp.int32, (SUBLANES, cw), 0)

    def body(n, carry):
        cf, cb = carry
        rf = pl.multiple_of(n * SUBLANES, SUBLANES)
        rb = pl.multiple_of((nblk - 1 - n) * SUBLANES, SUBLANES)
        a = af_s[pl.ds(rf, SUBLANES), :]
        b = bf_s[pl.ds(rf, SUBLANES), :]
        a2 = ab_s[pl.ds(rb, SUBLANES), :]
        b2 = bb_s[pl.ds(rb, SUBLANES), :]
        for s in (1, 2, 4):
            m = srow >= s
            b = jnp.where(m, a * pltpu.roll(b, s, axis=0) + b, b)
            a = jnp.where(m, a * pltpu.roll(a, s, axis=0), a)
            m2 = srow < SUBLANES - s
            b2 = jnp.where(m2, a2 * pltpu.roll(b2, SUBLANES - s, axis=0) + b2, b2)
            a2 = jnp.where(m2, a2 * pltpu.roll(a2, SUBLANES - s, axis=0), a2)
        hf = a * cf + b
        hb = a2 * cb + b2
        hf_s[pl.ds(rf, SUBLANES), :] = hf
        hb_s[pl.ds(rb, SUBLANES), :] = hb
        return hf[SUBLANES - 1:SUBLANES, :], hb[0:1, :]

    cf, cb = lax.fori_loop(0, nblk, body, (h0_ref[0:1, :], h0_ref[1:2, :]))
    fin_ref[0:1, :] = cf
    fin_ref[1:2, :] = cb
    y_ref[...] = (hf_s[...] + hb_s[...]) * _gelu(gate_ref[...])


def rglru_scan(gu, row0, n_seq, t, j, conv_w, conv_b, w_a, b_a, w_x, b_x, lam, h0, h0_j, *, cw=RG_TILE):
    r = gu.shape[1] // 2
    nh = cw // RG_BS
    blk0 = row0 // t
    nc = r // cw
    scr = [pltpu.VMEM((t, cw), F32) for _ in range(6)]
    vec2 = pl.BlockSpec((None, 2, cw), lambda b, c: (j, 0, c))
    gatew = pl.BlockSpec((None, 2, nh, RG_BS, RG_BS), lambda b, c: (j, 0, c, 0, 0))
    return pl.pallas_call(
        _rglru_kernel,
        grid=(n_seq, nc),
        in_specs=[pl.BlockSpec((t, cw), lambda b, c: (blk0 + b, c)),
                  pl.BlockSpec((t, cw), lambda b, c: (blk0 + b, nc + c)),
                  pl.BlockSpec((None, 4, cw), lambda b, c: (j, 0, c)),
                  pl.BlockSpec((None, 1, cw), lambda b, c: (j, 0, c)),
                  gatew, vec2, gatew, vec2, vec2,
                  pl.BlockSpec((None, None, 2, cw), lambda b, c: (b, h0_j, 0, c))],
        out_specs=[pl.BlockSpec((t, cw), lambda b, c: (b, c)),
                   pl.BlockSpec((None, 2, cw), lambda b, c: (b, 0, c))],
        out_shape=[jax.ShapeDtypeStruct((n_seq * t, r), F32),
                   jax.ShapeDtypeStruct((n_seq, 2, r), F32)],
        scratch_shapes=scr,
        compiler_params=_cparams("arbitrary", "arbitrary"),
        name="rglru_scan",
    )(gu, gu, conv_w, conv_b.reshape(conv_b.shape[0], 1, r), w_a, b_a, w_x, b_x, lam, h0)


def _s5_kernel(*refs, bp, n_seq, nc, gpb):
    ell, gw = S5_L, S5_H
    (x_ref, tm_ref, win_ref, wre_ref, wim_ref, ar_ref, ai_ref, d_ref, s0re_ref, s0im_ref,
     o_ref, fin_ref, xg_s, ure_s, uim_s, fre_s, fim_s, bre_s, bim_s) = refs
    m, mp = n_seq * nc, bp * nc
    lanes = x_ref.shape[1]
    token = lambda l: pl.ds(l, m, stride=ell)
    per_tile = lanes // gw
    half = ure_s.shape[2] // 2
    lane_grp = lax.shift_right_logical(lax.broadcasted_iota(jnp.int32, (m, lanes), 1), gw.bit_length() - 1)

    def perm(shape, chunk_major_axis):
        i = lax.broadcasted_iota(jnp.int32, shape, chunk_major_axis)
        j = lax.broadcasted_iota(jnp.int32, shape, 1 - chunk_major_axis)
        b = i & (bp - 1)
        c = lax.shift_right_logical(i, bp.bit_length() - 1)
        return jnp.where((j == b * nc + c) & (b < n_seq), 1.0, 0.0).astype(BF16)

    to_chunk_major = perm((mp, m), 0)
    to_batch_major = perm((m, mp), 1)

    def block_transpose(v):
        k = per_tile // 2
        while k >= 1:
            low = (lane_grp & k) == 0
            nxt = list(v)
            for i in range(per_tile):
                if i & k == 0:
                    a, b = v[i], v[i + k]
                    nxt[i] = jnp.where(low, a, pltpu.roll(b, k * gw, axis=1))
                    nxt[i + k] = jnp.where(low, pltpu.roll(a, lanes - k * gw, axis=1), b)
            v = nxt
            k //= 2
        return v

    for tile in range(ell // per_tile):
        by_group = block_transpose([x_ref[token(tile * per_tile + j), :] for j in range(per_tile)])
        for g in range(gpb):
            xg_s[g, :, tile * lanes:(tile + 1) * lanes] = by_group[g]

    for g in range(gpb):
        xg = xg_s[g]
        xp = jnp.dot(to_chunk_major, xg.astype(BF16), preferred_element_type=F32).astype(BF16)
        u = jnp.dot(xp, win_ref[g].astype(BF16), preferred_element_type=F32)
        ure_s[g] = u[:, :2 * half]
        uim_s[g] = u[:, 2 * half:]

    is_fwd = lax.broadcasted_iota(jnp.int32, (bp, 2 * half), 1) < half
    ar = [ar_ref[g] for g in range(gpb)]
    ai = [ai_ref[g] for g in range(gpb)]

    def body(k, carry):
        rf = pl.multiple_of(k * bp, bp)
        rb = pl.multiple_of((nc - 1 - k) * bp, bp)
        out = []
        for g in range(gpb):
            re, im = carry[2 * g], carry[2 * g + 1]
            fre_s[g, pl.ds(rf, bp), :] = re
            fim_s[g, pl.ds(rf, bp), :] = im
            bre_s[g, pl.ds(rb, bp), :] = re
            bim_s[g, pl.ds(rb, bp), :] = im
            ure = jnp.where(is_fwd, ure_s[g, pl.ds(rf, bp), :], ure_s[g, pl.ds(rb, bp), :])
            uim = jnp.where(is_fwd, uim_s[g, pl.ds(rf, bp), :], uim_s[g, pl.ds(rb, bp), :])
            out += [ar[g] * re - ai[g] * im + ure, ar[g] * im + ai[g] * re + uim]
        return tuple(out)

    init = tuple(r[g] for g in range(gpb) for r in (s0re_ref, s0im_ref))
    fin = lax.fori_loop(0, nc, body, init)
    fwd_all = lax.broadcasted_iota(jnp.int32, (mp, 2 * half), 1) < half
    for g in range(gpb):
        fin_ref[g, :, :2 * half] = fin[2 * g]
        fin_ref[g, :, 2 * half:] = fin[2 * g + 1]
        hre = jnp.where(fwd_all, fre_s[g], bre_s[g]).astype(BF16)
        him = jnp.where(fwd_all, fim_s[g], bim_s[g]).astype(BF16)
        hre = jnp.dot(to_batch_major, hre, preferred_element_type=F32).astype(BF16)
        him = jnp.dot(to_batch_major, him, preferred_element_type=F32).astype(BF16)
        xg = xg_s[g]
        y = (jnp.dot(xg.astype(BF16), tm_ref[g].astype(BF16), preferred_element_type=F32)
             + jnp.dot(hre, wre_ref[g].astype(BF16), preferred_element_type=F32)
             + jnp.dot(him, wim_ref[g].astype(BF16), preferred_element_type=F32)
             + d_ref[g] * xg)
        xg_s[g] = _gelu(y)

    for tile in range(ell // per_tile):
        by_token = block_transpose([xg_s[g, :, tile * lanes:(tile + 1) * lanes] for g in range(gpb)])
        for j in range(per_tile):
            o_ref[token(tile * per_tile + j), :] = by_token[j]


def s5_chunked(hn, row_blk, n_seq, nc, mats, s0re, s0im, *, bp):
    tmat, win, wre, wim, ar, ai, dg = mats
    g, w, _ = tmat.shape
    p2 = ar.shape[-1]
    ell = S5_L
    lanes = 128
    gpb = lanes // S5_H
    d = hn.shape[1]
    m, mp = n_seq * nc, bp * nc
    n = m * ell
    assert bp & (bp - 1) == 0 and S5_H & (S5_H - 1) == 0 and g % gpb == 0
    blk = lambda shape: pl.BlockSpec((gpb,) + shape, lambda i: (i, 0, 0))
    return pl.pallas_call(
        functools.partial(_s5_kernel, bp=bp, n_seq=n_seq, nc=nc, gpb=gpb),
        grid=(g // gpb,),
        in_specs=[pl.BlockSpec((n, lanes), lambda i: (row_blk, i)),
                  blk((w, w)), blk((w, 2 * p2)), blk((p2, w)), blk((p2, w)),
                  blk((1, p2)), blk((1, p2)), blk((1, w)), blk((bp, p2)), blk((bp, p2))],
        out_specs=[pl.BlockSpec((n, lanes), lambda i: (0, i)), blk((bp, 2 * p2))],
        out_shape=[jax.ShapeDtypeStruct((n, d), F32), jax.ShapeDtypeStruct((g, bp, 2 * p2), F32)],
        scratch_shapes=[pltpu.VMEM((gpb, m, w), F32)] + [pltpu.VMEM((gpb, mp, p2), F32) for _ in range(6)],
        compiler_params=_cparams("arbitrary"),
        name="s5_chunked",
    )(hn, tmat, win, wre, wim, ar, ai, dg, s0re, s0im)


def _cmul(ar, ai, br, bi):
    return ar * br - ai * bi, ar * bi + ai * br


def _s5_prep_kernel(are_ref, aim_ref, ldt_ref, btr_ref, bti_ref, cr_ref, ci_ref,
                    tm_ref, win_ref, wre_ref, wim_ref, ar_ref, ai_ref):
    for gi in range(cr_ref.shape[0]):
        _s5_prep_group(*(r.at[gi] for r in (are_ref, aim_ref, ldt_ref, btr_ref, bti_ref, cr_ref, ci_ref,
                                           tm_ref, win_ref, wre_ref, wim_ref, ar_ref, ai_ref)))


def _s5_prep_group(are_ref, aim_ref, ldt_ref, btr_ref, bti_ref, cr_ref, ci_ref,
                   tm_ref, win_ref, wre_ref, wim_ref, ar_ref, ai_ref):
    ell = S5_L
    h, p2 = cr_ref.shape
    w = ell * h
    a_re, a_im = are_ref[...], aim_ref[...]
    dt = jnp.exp(ldt_ref[...])
    steps = lax.broadcasted_iota(jnp.int32, (3 * SUBLANES, p2), 0).astype(F32)
    mag = jnp.exp(steps * (a_re * dt))
    ang = steps * (a_im * dt)
    pw_r, pw_i = mag * jnp.cos(ang), mag * jnp.sin(ang)
    nr, ni = pw_r[1:2] - 1.0, pw_i[1:2]
    den = a_re * a_re + a_im * a_im
    qr, qi = (nr * a_re + ni * a_im) / den, (ni * a_re - nr * a_im) / den
    bb_r, bb_i = _cmul(qr, qi, btr_ref[...], bti_ref[...])
    c_r, c_i = cr_ref[...], ci_ref[...]
    fwd = lax.broadcasted_iota(jnp.int32, (1, p2), 1) < p2 // 2

    def power_rows(m_fwd, m_bwd):
        return (jnp.where(fwd, pw_r[m_fwd:m_fwd + 1], pw_r[m_bwd:m_bwd + 1]),
                jnp.where(fwd, pw_i[m_fwd:m_fwd + 1], pw_i[m_bwd:m_bwd + 1]))

    def stack(x_r, x_i, powers):
        parts = [_cmul(x_r, x_i, *power_rows(*powers(l))) for l in range(ell)]
        return (jnp.concatenate([q[0] for q in parts], axis=0), jnp.concatenate([q[1] for q in parts], axis=0))

    win_r, win_i = stack(bb_r, bb_i, lambda l: (ell - 1 - l, l))
    win_ref[...] = jnp.concatenate([win_r, win_i], axis=1).astype(win_ref.dtype)
    z_r, z_i = stack(c_r, c_i, lambda l: (l + 1, ell - l))
    wre_ref[...] = z_r.T.astype(wre_ref.dtype)
    wim_ref[...] = (-z_i).T.astype(wim_ref.dtype)
    k_r, k_i = stack(c_r, c_i, lambda m: (m, ell - 1 - m))
    mask_f = jnp.where(fwd, 1.0, 0.0)

    def lag_rows(mask):
        return (lax.dot_general(bb_r * mask, k_r, NT_DIMS, precision=HIGHEST, preferred_element_type=F32)
                - lax.dot_general(bb_i * mask, k_i, NT_DIMS, precision=HIGHEST, preferred_element_type=F32))

    kf = lag_rows(mask_f)
    kb = lag_rows(1.0 - mask_f)
    lane = lax.broadcasted_iota(jnp.int32, (h, w), 1)
    blocks = []
    for li in range(ell):
        f_part = kf if li == 0 else pltpu.roll(kf, li * h, axis=1)
        s_b = (w - (ell - 1 - li) * h) % w
        b_part = kb if s_b == 0 else pltpu.roll(kb, s_b, axis=1)
        blocks.append(jnp.where(lane >= li * h, f_part, 0.0) + jnp.where(lane < (li + 1) * h, b_part, 0.0))
    tm_ref[...] = jnp.concatenate(blocks, axis=0).astype(tm_ref.dtype)
    ar_ref[...] = pw_r[ell:ell + 1]
    ai_ref[...] = pw_i[ell:ell + 1]


def s5_chunk_operators(a_re, a_im, log_dt, b_re, b_im, c_re, c_im, d):
    _, g, p = a_re.shape
    h = b_re.shape[-1]
    w = S5_L * h
    two_dir = lambda x: jnp.transpose(x, (1, 0, 2)).reshape(g, 1, 2 * p)
    ldt = two_dir(jnp.broadcast_to(log_dt[:, :, None], (2, g, p)))
    bt = lambda x: jnp.transpose(x, (1, 3, 0, 2)).reshape(g, h, 2 * p)
    ct = lambda x: jnp.transpose(x, (1, 2, 0, 3)).reshape(g, h, 2 * p)
    gps = math.gcd(g, S5_PREP_GROUPS)
    per_g = lambda shape: pl.BlockSpec((gps,) + shape, lambda i: (i, 0, 0))
    tmat, win, wre, wim, ar, ai = pl.pallas_call(
        _s5_prep_kernel,
        grid=(g // gps,),
        in_specs=[per_g((1, 2 * p))] * 3 + [per_g((h, 2 * p))] * 4,
        out_specs=[per_g((w, w)), per_g((w, 4 * p)), per_g((2 * p, w)), per_g((2 * p, w)),
                   per_g((1, 2 * p)), per_g((1, 2 * p))],
        out_shape=[jax.ShapeDtypeStruct((g, w, w), BF16), jax.ShapeDtypeStruct((g, w, 4 * p), BF16),
                   jax.ShapeDtypeStruct((g, 2 * p, w), BF16), jax.ShapeDtypeStruct((g, 2 * p, w), BF16),
                   jax.ShapeDtypeStruct((g, 1, 2 * p), F32), jax.ShapeDtypeStruct((g, 1, 2 * p), F32)],
        compiler_params=_cparams("arbitrary"),
        name="s5_chunk_operators",
    )(two_dir(a_re), two_dir(a_im), ldt, bt(b_re), bt(b_im), ct(c_re), ct(c_im))
    dg = jnp.tile(d.reshape(g, 1, h), (1, S5_L, 1)).reshape(g, 1, w)
    return tmat, win, wre, wim, ar, ai, dg


def s5_mixer_group(hn, row0, n_seq, t, mats, s0):
    ar = mats[4]
    g = ar.shape[0]
    p = ar.shape[-1] // 2
    nc = t // S5_L
    bp = -(-n_seq // SUBLANES) * SUBLANES
    n = n_seq * t
    assert row0 % n == 0
    if s0 is None:
        s0re = jnp.zeros((g, bp, 2 * p), F32)
        s0im = s0re
    else:
        st = jnp.transpose(s0, (3, 0, 2, 1, 4)).reshape(g, n_seq, 2, 2 * p)
        st = jnp.pad(st, ((0, 0), (0, bp - n_seq), (0, 0), (0, 0)))
        s0re, s0im = st[:, :, 0], st[:, :, 1]
    u, fin = s5_chunked(hn, row0 // n, n_seq, nc, mats, s0re, s0im, bp=bp)
    fin = fin.reshape(g, bp, 2, 2, p)[:, :n_seq]
    return u, jnp.transpose(fin, (1, 3, 2, 0, 4))


def _softmax_pv(scores, values, sink):
    m = sink
    for s in scores:
        m = jnp.maximum(m, jnp.max(s, axis=-1, keepdims=True))
    den = jnp.exp(sink - m)
    acc = None
    for s, v in zip(scores, values):
        p = jnp.exp(s - m)
        den = den + jnp.sum(p, axis=-1, keepdims=True)
        pv = jnp.dot(p.astype(BF16), v, preferred_element_type=F32)
        acc = pv if acc is None else acc + pv
    return acc / den


def _attn_prompt_kernel(sink_ref, q_ref, k_ref, v_ref, o_ref):
    k = k_ref[...].astype(BF16)
    v = v_ref[...].astype(BF16)
    for h in range(N_KV):
        hs = slice(h * HEAD_DIM, (h + 1) * HEAD_DIM)
        kh, vh = k[:, hs], v[:, hs]
        for g in range(Q_PER_KV):
            c0 = (h * Q_PER_KV + g) * HEAD_DIM
            qg = (q_ref[:, c0:c0 + HEAD_DIM] * ATT_SCALE).astype(BF16)
            s = lax.dot_general(qg, kh, NT_DIMS, preferred_element_type=F32)
            o_ref[:, c0:c0 + HEAD_DIM] = _softmax_pv([s], [vh], sink_ref[h * Q_PER_KV + g])


def attn_prompt(qkv, sink, n_seq, t):
    dq = N_HEADS * HEAD_DIM
    kcol = dq // KV_W
    return pl.pallas_call(
        _attn_prompt_kernel,
        grid=(n_seq,),
        in_specs=[pl.BlockSpec(memory_space=pltpu.SMEM),
                  pl.BlockSpec((t, dq), lambda b: (b, 0)),
                  pl.BlockSpec((t, KV_W), lambda b: (b, kcol)),
                  pl.BlockSpec((t, KV_W), lambda b: (b, kcol + 1))],
        out_specs=pl.BlockSpec((t, dq), lambda b: (b, 0)),
        out_shape=jax.ShapeDtypeStruct((n_seq * t, dq), F32),
        compiler_params=_cparams("arbitrary"),
        name="attn_prompt",
    )(sink, qkv, qkv, qkv)


def _rope(x, cos, sin):
    w = x.shape[1]
    low = (lax.broadcasted_iota(jnp.int32, x.shape, 1) & (HEAD_DIM // 4)) == 0
    partner = jnp.where(low, pltpu.roll(x, w - HEAD_DIM // 4, axis=1), pltpu.roll(x, HEAD_DIM // 4, axis=1))
    return x * cos + partner * sin


def _attn_sample_kernel(sink_ref, q_ref, k_ref, v_ref, kc_ref, vc_ref, cos_ref, sin_ref, o_ref,
                        kw_s, vw_s, kc_s, vc_s, *, t):
    n = pl.program_id(1)
    blk = ATT_BLOCK

    @pl.when(n == 0)
    def _():
        zeros = jnp.zeros((blk, KV_W), BF16)
        kw_s[0:blk, :] = zeros
        vw_s[0:blk, :] = zeros
        kw_s[blk + t:2 * blk + t, :] = zeros
        vw_s[blk + t:2 * blk + t, :] = zeros
        kw_s[blk:blk + t, :] = _rope(k_ref[...], cos_ref[...], sin_ref[...]).astype(BF16)
        vw_s[blk:blk + t, :] = v_ref[...].astype(BF16)
        kc_s[...] = kc_ref[...].astype(BF16)
        vc_s[...] = vc_ref[...].astype(BF16)

    r0 = pl.multiple_of(n * blk, blk)
    cq = cos_ref[pl.ds(r0, blk), :]
    sq = sin_ref[pl.ds(r0, blk), :]
    kw = kw_s[pl.ds(r0, 3 * blk), :]
    vw = vw_s[pl.ds(r0, 3 * blk), :]
    qi = lax.broadcasted_iota(jnp.int32, (blk, 3 * blk), 0)
    kj = lax.broadcasted_iota(jnp.int32, (blk, 3 * blk), 1)
    kpos = n * blk - blk + kj
    valid = (jnp.abs(kj - blk - qi) <= WINDOW) & (kpos >= 0) & (kpos < t)
    for h in range(N_KV):
        hs = slice(h * HEAD_DIM, (h + 1) * HEAD_DIM)
        qh = (_rope(q_ref[:, h * KV_W:(h + 1) * KV_W], cq, sq) * ATT_SCALE).astype(BF16)
        kh, vh, kch, vch = kw[:, hs], vw[:, hs], kc_s[:, hs], vc_s[:, hs]
        for g in range(Q_PER_KV):
            qg = qh[:, g * HEAD_DIM:(g + 1) * HEAD_DIM]
            s_loc = lax.dot_general(qg, kh, NT_DIMS, preferred_element_type=F32)
            s_loc = jnp.where(valid, s_loc, NEG_INF)
            s_ctx = lax.dot_general(qg, kch, NT_DIMS, preferred_element_type=F32)
            c0 = (h * Q_PER_KV + g) * HEAD_DIM
            o_ref[:, c0:c0 + HEAD_DIM] = _softmax_pv([s_loc, s_ctx], [vh, vch], sink_ref[h * Q_PER_KV + g])


def _rope_tables(t):
    quarter = HEAD_DIM // 4
    freqs = ROPE_BASE ** (-jnp.arange(quarter, dtype=F32) / quarter)
    pos = jnp.arange(t)
    ang_r = (pos // GRID_W).astype(F32)[:, None] * freqs
    ang_c = (pos % GRID_W).astype(F32)[:, None] * freqs
    cos = jnp.concatenate([jnp.cos(ang_r), jnp.cos(ang_r), jnp.cos(ang_c), jnp.cos(ang_c)], axis=-1)
    sin = jnp.concatenate([-jnp.sin(ang_r), jnp.sin(ang_r), -jnp.sin(ang_c), jnp.sin(ang_c)], axis=-1)
    return jnp.tile(cos, (1, N_KV)), jnp.tile(sin, (1, N_KV))


def attn_sample(qkv, row0, sink, n_seq, t, k_ctx, v_ctx):
    dq = N_HEADS * HEAD_DIM
    kcol = dq // KV_W
    nb = t // ATT_BLOCK
    lc = k_ctx.shape[1]
    cos, sin = _rope_tables(t)
    qblk0, sblk0 = row0 // ATT_BLOCK, row0 // t
    return pl.pallas_call(
        functools.partial(_attn_sample_kernel, t=t),
        grid=(n_seq, nb),
        in_specs=[pl.BlockSpec(memory_space=pltpu.SMEM),
                  pl.BlockSpec((ATT_BLOCK, dq), lambda b, n: (qblk0 + b * nb + n, 0)),
                  pl.BlockSpec((t, KV_W), lambda b, n: (sblk0 + b, kcol)),
                  pl.BlockSpec((t, KV_W), lambda b, n: (sblk0 + b, kcol + 1)),
                  pl.BlockSpec((None, lc, KV_W), lambda b, n: (b, 0, 0)),
                  pl.BlockSpec((None, lc, KV_W), lambda b, n: (b, 0, 0)),
                  pl.BlockSpec((t, KV_W), lambda b, n: (0, 0)),
                  pl.BlockSpec((t, KV_W), lambda b, n: (0, 0))],
        out_specs=pl.BlockSpec((ATT_BLOCK, dq), lambda b, n: (b * nb + n, 0)),
        out_shape=jax.ShapeDtypeStruct((n_seq * t, dq), F32),
        scratch_shapes=[pltpu.VMEM((t + 2 * ATT_BLOCK, KV_W), BF16), pltpu.VMEM((t + 2 * ATT_BLOCK, KV_W), BF16),
                        pltpu.VMEM((lc, KV_W), BF16), pltpu.VMEM((lc, KV_W), BF16)],
        compiler_params=_cparams("arbitrary", "arbitrary"),
        name="attn_sample",
    )(sink, qkv, qkv, qkv, k_ctx, v_ctx, cos, sin)


def _moe_kernel(x_ref, wg_ref, wu_ref, wd_ref, gs_ref, g2_ref, o_ref, wg_bf, wu_bf, wd_bf, *, n_seg):
    wg_bf[...] = wg_ref[...].astype(BF16)
    wu_bf[...] = wu_ref[...].astype(BF16)
    wd_bf[...] = wd_ref[...].astype(BF16)
    rsub = min(MOE_ROW_SUB, x_ref.shape[0])
    for r in range(x_ref.shape[0] // rsub):
        rs = slice(r * rsub, (r + 1) * rsub)
        x = x_ref[rs, :]
        hg = jnp.dot(x, wg_bf[...], preferred_element_type=F32)
        hu = jnp.dot(x, wu_bf[...], preferred_element_type=F32)
        he = ((hg * jax.nn.sigmoid(hg)) * hu).astype(BF16)
        y = jnp.dot(he, wd_bf[...], preferred_element_type=F32)
        gs = gs_ref[rs, :]
        seg = gs[:, 1:2]
        g2 = jnp.zeros(y.shape, F32)
        for s in range(n_seg):
            g2 = jnp.where(seg == float(s), g2_ref[s:s + 1, :], g2)
        o_ref[rs, :] = y * (gs[:, 0:1] * g2)


def moe_experts(xe, w_gate, w_up, w_down, layer, gate_seg, mod, n_seg):
    ne, r, d = xe.shape
    dff = w_gate.shape[3]
    return pl.pallas_call(
        functools.partial(_moe_kernel, n_seg=n_seg),
        grid=(ne,),
        in_specs=[pl.BlockSpec((None, r, d), lambda e: (e, 0, 0)),
                  pl.BlockSpec((None, None, d, dff), lambda e: (layer, e, 0, 0)),
                  pl.BlockSpec((None, None, d, dff), lambda e: (layer, e, 0, 0)),
                  pl.BlockSpec((None, None, dff, d), lambda e: (layer, e, 0, 0)),
                  pl.BlockSpec((None, r, 2), lambda e: (e, 0, 0)),
                  pl.BlockSpec((SUBLANES, None, d), lambda e: (layer, 0, 5))],
        out_specs=pl.BlockSpec((None, r, d), lambda e: (e, 0, 0)),
        out_shape=jax.ShapeDtypeStruct((ne, r, d), F32),
        scratch_shapes=[pltpu.VMEM((d, dff), BF16), pltpu.VMEM((d, dff), BF16), pltpu.VMEM((dff, d), BF16)],
        compiler_params=_cparams("arbitrary"),
        name="moe_experts",
    )(xe, w_gate, w_up, w_down, gate_seg, mod)


def _expert_choice(aff, rows):
    ne = aff.shape[1]
    sizes = (rows.n_prompt, rows.n_sample)
    caps = [(EC_FACTOR * n) // ne for n in sizes]
    if sizes[0] == sizes[1]:
        n = sizes[0]
        gt, ix = lax.top_k(jnp.swapaxes(aff.reshape(2, n, ne), 1, 2), caps[0])
        ix = ix + jnp.array([0, n], jnp.int32)[:, None, None]
        return (jnp.concatenate([gt[0], gt[1]], axis=1), jnp.concatenate([ix[0], ix[1]], axis=1))
    gts, ixs, off = [], [], 0
    for n, cap in zip(sizes, caps):
        gt, ix = lax.top_k(aff[off:off + n].T, cap)
        gts.append(gt)
        ixs.append(ix + off)
        off += n
    return jnp.concatenate(gts, axis=1), jnp.concatenate(ixs, axis=1)


def _combine_kernel(idx_ref, x_hbm, ye_ref, o_hbm, acc, sem, *, n_grp, rows_per):
    g = pl.program_id(0)
    e = pl.program_id(1)
    last = pl.num_programs(1) - 1

    def load(grp):
        return pltpu.make_async_copy(x_hbm.at[pl.ds(grp * n_grp, n_grp), :], acc.at[grp], sem.at[grp])

    def store(grp):
        return pltpu.make_async_copy(acc.at[grp], o_hbm.at[pl.ds(grp * n_grp, n_grp), :], sem.at[2 + grp])

    @pl.when((g == 0) & (e == 0))
    def _():
        load(0).start()
        load(1).start()

    @pl.when(e == 0)
    def _():
        load(g).wait()

    def add_rows(grp):
        base = (2 * e + grp) * rows_per

        def body(i, carry):
            toks = [idx_ref[base + i * COMBINE_UNROLL + k] for k in range(COMBINE_UNROLL)]
            old = [acc[grp, pl.ds(t, 1), :] for t in toks]
            add = [ye_ref[pl.ds(i * COMBINE_UNROLL + k, 1), :] for k in range(COMBINE_UNROLL)]
            for t, a, b in zip(toks, old, add):
                acc[grp, pl.ds(t, 1), :] = a + b
            return carry

        lax.fori_loop(0, rows_per // COMBINE_UNROLL, body, 0)

    pl.when(g == 0)(lambda: add_rows(0))
    pl.when(g == 1)(lambda: add_rows(1))

    @pl.when(e == last)
    def _():
        store(g).start()

    @pl.when((g == 1) & (e == last))
    def _():
        store(0).wait()
        store(1).wait()


def moe_combine(x, ye, idx, n_grp):
    m, d = x.shape
    ne, r, _ = ye.shape
    rows_per = r // 2
    assert m == 2 * n_grp and rows_per % COMBINE_UNROLL == 0
    grid_spec = pltpu.PrefetchScalarGridSpec(
        num_scalar_prefetch=1,
        grid=(2, ne),
        in_specs=[pl.BlockSpec(memory_space=pl.ANY),
                  pl.BlockSpec((None, rows_per, d), lambda g, e, ix: (e, g, 0))],
        out_specs=pl.BlockSpec(memory_space=pl.ANY),
        scratch_shapes=[pltpu.VMEM((2, n_grp, d), F32), pltpu.SemaphoreType.DMA((4,))])
    return pl.pallas_call(
        functools.partial(_combine_kernel, n_grp=n_grp, rows_per=rows_per),
        grid_spec=grid_spec,
        out_shape=jax.ShapeDtypeStruct((m, d), F32),
        compiler_params=_cparams("arbitrary", "arbitrary"),
        name="moe_combine",
    )(idx.reshape(-1), x, ye)


def moe_layer(x, h2, aff, mod, rows, layer, w_gate, w_up, w_down, n_seg):
    m, d = x.shape
    gates, idx = _expert_choice(aff, rows)
    seg = jnp.where(idx < rows.n_prompt, 0, 1 + (idx - rows.n_prompt) // rows.t_sample)
    xe = h2.at[idx].get(mode="promise_in_bounds")
    gate_seg = jnp.stack([gates, seg.astype(F32)], axis=-1)
    ye = moe_experts(xe, w_gate, w_up, w_down, layer, gate_seg, mod, n_seg)
    if rows.n_prompt == rows.n_sample:
        return moe_combine(x, ye, idx % rows.n_prompt, rows.n_prompt)
    return x.at[idx.reshape(-1)].add(ye.reshape(-1, d))


def kernel(x_prompt, x_sample, state_rglru, state_s5, cache_k, cache_v, c, c_ctx, ada_w, ada_b, norm1_g, norm2_g, rg_w_in, rg_conv_w, rg_conv_b, rg_w_a, rg_b_a, rg_w_x, rg_b_x, rg_lambda, rg_w_out, s5_a_re, s5_a_im, s5_log_dt, s5_b_re, s5_b_im, s5_c_re, s5_c_im, s5_d, s5_w_glu, attn_w_qkv, attn_w_o, attn_sink, router_w, moe_w_gate, moe_w_up, moe_w_down, final_norm_g):
    bp_, tp, d = x_prompt.shape
    bs, ts, _ = x_sample.shape
    n_p, n_s = bp_ * tp, bs * ts
    depth = ada_w.shape[0]
    rows = _Rows(n_p, n_s, ts)
    assert bs + 1 <= SUBLANES and n_p % ts == 0

    xs = (x_prompt.reshape(n_p, d), x_sample.reshape(n_s, d), 0)
    cond = jnp.concatenate([c_ctx[None, :], c, jnp.zeros((SUBLANES - 1 - bs, d), F32)], axis=0)
    mod_all = ada_modulation_all(cond, ada_w, ada_b)
    mod = mod_all.reshape(depth * SUBLANES, 1, 6 * d)
    g1 = norm1_g.reshape(depth, 1, d)
    g2 = norm2_g.reshape(depth, 1, d)

    new_rg, new_s5, new_k, new_v = [], [], [], []
    for l in range(depth):
        kind, j = l % 3, l // 3
        if kind == 0:
            gu = norm_mod_matmul(xs, g1, mod, rows, l, rg_w_in, j)
            args = (j, rg_conv_w, rg_conv_b, rg_w_a, rg_b_a, rg_w_x, rg_b_x, rg_lambda)
            r = gu.shape[1] // 2
            yp, fin = rglru_scan(gu, 0, bp_, tp, *args, jnp.zeros((bp_, 1, 2, r), F32), 0)
            ys, _ = rglru_scan(gu, n_p, bs, ts, *args, state_rglru, j)
            new_rg.append(fin)
            x, h2, aff = matmul_gated_residual(yp, ys, rg_w_out, j, xs, mod, rows, l, g2, router_w)
        elif kind == 1:
            hn = norm_mod(x, g1, mod, rows, l)
            mats = s5_chunk_operators(s5_a_re[j], s5_a_im[j], s5_log_dt[j], s5_b_re[j], s5_b_im[j],
                                      s5_c_re[j], s5_c_im[j], s5_d[j])
            up, st = s5_mixer_group(hn, 0, bp_, tp, mats, None)
            us, _ = s5_mixer_group(hn, n_p, bs, ts, mats, state_s5[:, j])
            new_s5.append(st)
            x, h2, aff = glu_gated_residual(up, us, s5_w_glu, j, x, mod, rows, l, g2, router_w)
        else:
            qkv = norm_mod_matmul(xs, g1, mod, rows, l, attn_w_qkv, j)
            dq = N_HEADS * HEAD_DIM
            new_k.append(qkv[:n_p, dq:dq + KV_W].reshape(bp_, tp, N_KV, HEAD_DIM))
            new_v.append(qkv[:n_p, dq + KV_W:].reshape(bp_, tp, N_KV, HEAD_DIM))
            op = attn_prompt(qkv, attn_sink[j], bp_, tp)
            lc = cache_k.shape[2]
            os_ = attn_sample(qkv, n_p, attn_sink[j], bs, ts,
                              cache_k[:, j].reshape(bs, lc, KV_W), cache_v[:, j].reshape(bs, lc, KV_W))
            x, h2, aff = matmul_gated_residual(op, os_, attn_w_o, j, xs, mod, rows, l, g2, router_w)
        x = moe_layer(x, h2, aff, mod, rows, l, moe_w_gate, moe_w_up, moe_w_down, bs + 1)
        xs = (x, x, rows.prompt_blocks)

    y_p, y_s = final_norm(x, final_norm_g, rows)
    return (y_p.reshape(bp_, tp, d), y_s.reshape(bs, ts, d),
            jnp.stack(new_rg, axis=1), jnp.stack(new_s5, axis=1),
            jnp.stack(new_k, axis=1), jnp.stack(new_v, axis=1))
```

```python
import functools
import math

import jax
import jax.numpy as jnp
from jax import lax
from jax.experimental import pallas as pl
from jax.experimental.pallas import tpu as pltpu

F32 = jnp.float32
BF16 = jnp.bfloat16
HIGHEST = lax.Precision.HIGHEST

EPS = 1e-6
RG_C = 8.0
RG_BS = 128
RG_TILE = 512
S5_H = 16
S5_L = 16
S5_PREP_GROUPS = 4
N_HEADS = 16
N_KV = 4
Q_PER_KV = N_HEADS // N_KV
HEAD_DIM = 64
KV_W = N_KV * HEAD_DIM
GRID_W = 64
WINDOW = 128
ATT_BLOCK = 128
ROPE_BASE = 10000.0
ATT_SCALE = HEAD_DIM ** -0.5
assert math.frexp(ATT_SCALE)[0] == 0.5
NEG_INF = -1e30
EC_FACTOR = 2
SUBLANES = 8
ROW_TILE = 512
MOE_ROW_SUB = 256
COMBINE_UNROLL = 16
VMEM_LIMIT = 56 * 1024 * 1024
NT_DIMS = (((1,), (1,)), ((), ()))


def _cparams(*sem):
    return pltpu.CompilerParams(dimension_semantics=sem, vmem_limit_bytes=VMEM_LIMIT)


def _gelu(x):
    return x * (0.5 * (1.0 + jnp.tanh(math.sqrt(2.0 / math.pi) * (x + 0.044715 * (x * x * x)))))


def _sigmoid(x):
    return 0.5 * jnp.tanh(0.5 * x) + 0.5


def _norm_mod(x, g, sc, sh):
    ms = jnp.mean(x * x, axis=-1, keepdims=True)
    return ((x * lax.rsqrt(ms + EPS)) * g) * (1.0 + sc) + sh


def _mod_kernel(c_ref, w_ref, b_ref, o_ref):
    c = c_ref[...]
    s = (c * jax.nn.sigmoid(c)).astype(BF16)
    o_ref[...] = jnp.dot(s, w_ref[...].astype(BF16), preferred_element_type=F32) + b_ref[...]


def ada_modulation_all(cond, ada_w, ada_b):
    n_layers, d, n = ada_w.shape
    tn = 1536
    return pl.pallas_call(
        _mod_kernel,
        grid=(n_layers, n // tn),
        in_specs=[pl.BlockSpec((SUBLANES, d), lambda l, j: (0, 0)),
                  pl.BlockSpec((None, d, tn), lambda l, j: (l, 0, j)),
                  pl.BlockSpec((None, 1, tn), lambda l, j: (l, 0, j))],
        out_specs=pl.BlockSpec((None, SUBLANES, tn), lambda l, j: (l, 0, j)),
        out_shape=jax.ShapeDtypeStruct((n_layers, SUBLANES, n), F32),
        compiler_params=_cparams("arbitrary", "arbitrary"),
        name="ada_mod",
    )(cond, ada_w, ada_b.reshape(n_layers, 1, n))


class _Rows:
    def __init__(self, n_prompt, n_sample, t_sample):
        self.n_prompt = n_prompt
        self.n_sample = n_sample
        self.t_sample = t_sample
        self.tm = min(ROW_TILE, n_prompt, t_sample)
        assert n_prompt % self.tm == 0 and t_sample % self.tm == 0
        self.prompt_blocks = n_prompt // self.tm
        self.sample_blocks = n_sample // self.tm

    def seg(self, i):
        r = i * self.tm
        return jnp.where(r < self.n_prompt, 0, 1 + lax.div(r - self.n_prompt, self.t_sample))


def _mod_spec(rows, layer, width, chunk, m_axis):
    def imap(*ids):
        return (layer * SUBLANES + rows.seg(ids[m_axis]), 0, chunk)
    return pl.BlockSpec((None, 1, width), imap)


def _gain_spec(layer, d):
    return pl.BlockSpec((None, 1, d), lambda *ids: (layer, 0, 0))


def _stream_specs(rows, xs, m_axis):
    top, _, bot0 = xs
    npb, nsb, tm = rows.prompt_blocks, rows.sample_blocks, rows.tm
    d = top.shape[1]
    return [pl.BlockSpec((tm, d), lambda *ids: (jnp.minimum(ids[m_axis], npb - 1), 0)),
            pl.BlockSpec((tm, d), lambda *ids: (bot0 + jnp.clip(ids[m_axis] - npb, 0, nsb - 1), 0))]


def _nm_kernel(xt_ref, xb_ref, g_ref, sc_ref, sh_ref, w_ref, o_ref, wbf_ref, *, npb):
    i = pl.program_id(1)

    @pl.when(i == 0)
    def _():
        wbf_ref[...] = w_ref[...].astype(BF16)
    x = jnp.where(i < npb, xt_ref[...], xb_ref[...])
    h = _norm_mod(x, g_ref[...], sc_ref[...], sh_ref[...])
    o_ref[...] = jnp.dot(h.astype(BF16), wbf_ref[...], preferred_element_type=F32)


def norm_mod_matmul(xs, gains, mod, rows, layer, w, wl):
    m, d = rows.n_prompt + rows.n_sample, xs[0].shape[1]
    n = w.shape[2]
    tm = rows.tm
    tn = n
    return pl.pallas_call(
        functools.partial(_nm_kernel, npb=rows.prompt_blocks),
        grid=(n // tn, m // tm),
        in_specs=_stream_specs(rows, xs, 1) + [
                  _gain_spec(layer, d),
                  _mod_spec(rows, layer, d, 1, 1),
                  _mod_spec(rows, layer, d, 0, 1),
                  pl.BlockSpec((None, d, tn), lambda j, i: (wl, 0, j))],
        out_specs=pl.BlockSpec((tm, tn), lambda j, i: (i, j)),
        out_shape=jax.ShapeDtypeStruct((m, n), F32),
        scratch_shapes=[pltpu.VMEM((d, tn), BF16)],
        compiler_params=_cparams("arbitrary", "arbitrary"),
        name="norm_mod_matmul",
    )(xs[0], xs[1], gains, mod, mod, w)


def _norm_only_kernel(x_ref, g_ref, sc_ref, sh_ref, o_ref):
    o_ref[...] = _norm_mod(x_ref[...], g_ref[...], sc_ref[...], sh_ref[...])


def norm_mod(x, gains, mod, rows, layer):
    m, d = x.shape
    tm = rows.tm
    return pl.pallas_call(
        _norm_only_kernel,
        grid=(m // tm,),
        in_specs=[pl.BlockSpec((tm, d), lambda i: (i, 0)),
                  _gain_spec(layer, d),
                  _mod_spec(rows, layer, d, 1, 0),
                  _mod_spec(rows, layer, d, 0, 0)],
        out_specs=pl.BlockSpec((tm, d), lambda i: (i, 0)),
        out_shape=jax.ShapeDtypeStruct((m, d), F32),
        compiler_params=_cparams("arbitrary"),
        name="norm_mod",
    )(x, gains, mod, mod)


def _final_norm_kernel(x_ref, g_ref, op_ref, os_ref, *, npb):
    i = pl.program_id(0)
    x = x_ref[...]
    ms = jnp.mean(x * x, axis=-1, keepdims=True)
    y = (x * lax.rsqrt(ms + EPS)) * g_ref[...]

    @pl.when(i < npb)
    def _():
        op_ref[...] = y

    @pl.when(i >= npb)
    def _():
        os_ref[...] = y


def final_norm(x, g, rows):
    m, d = x.shape
    tm, npb, nsb = rows.tm, rows.prompt_blocks, rows.sample_blocks
    return pl.pallas_call(
        functools.partial(_final_norm_kernel, npb=npb),
        grid=(m // tm,),
        in_specs=[pl.BlockSpec((tm, d), lambda i: (i, 0)),
                  pl.BlockSpec((1, d), lambda i: (0, 0))],
        out_specs=[pl.BlockSpec((tm, d), lambda i: (jnp.minimum(i, npb - 1), 0)),
                   pl.BlockSpec((tm, d), lambda i: (jnp.clip(i - npb, 0, nsb - 1), 0))],
        out_shape=[jax.ShapeDtypeStruct((rows.n_prompt, d), F32), jax.ShapeDtypeStruct((rows.n_sample, d), F32)],
        compiler_params=_cparams("arbitrary"),
        name="final_norm",
    )(x, g.reshape(1, d))


def _router_epilogue(x_new, g_ref, sc_ref, sh_ref, rw_ref, h_ref, aff_ref):
    h = _norm_mod(x_new, g_ref[...], sc_ref[...], sh_ref[...])
    h_hi = h.astype(BF16)
    h_ref[...] = h_hi
    h_lo = (h - h_hi.astype(F32)).astype(BF16)
    rw = rw_ref[...]
    rw_hi = rw.astype(BF16)
    rw_lo = (rw - rw_hi.astype(F32)).astype(BF16)
    ne = rw.shape[1]
    both = jnp.dot(h_hi, jnp.concatenate([rw_hi, rw_lo], axis=1), preferred_element_type=F32)
    logits = both[:, :ne] + (both[:, ne:] + jnp.dot(h_lo, rw_hi, preferred_element_type=F32))
    e = jnp.exp(logits - jnp.max(logits, axis=-1, keepdims=True))
    aff_ref[...] = e / jnp.sum(e, axis=-1, keepdims=True)


def _mmres_kernel(ap_ref, as_ref, w_ref, rt_ref, rb_ref, gt_ref, g_ref, sc_ref, sh_ref, rw_ref,
                  o_ref, h_ref, aff_ref, wbf_ref, *, npb):
    i = pl.program_id(0)

    @pl.when(i == 0)
    def _():
        wbf_ref[...] = w_ref[...].astype(BF16)

    def emit(a_ref, r_ref):
        acc = jnp.dot(a_ref[...].astype(BF16), wbf_ref[...], preferred_element_type=F32)
        x_new = r_ref[...] + gt_ref[...] * acc
        o_ref[...] = x_new
        _router_epilogue(x_new, g_ref, sc_ref, sh_ref, rw_ref, h_ref, aff_ref)

    pl.when(i < npb)(lambda: emit(ap_ref, rt_ref))
    pl.when(i >= npb)(lambda: emit(as_ref, rb_ref))


def _sublayer2_specs(rows, layer, d, ne):
    tm = rows.tm
    ins = [_gain_spec(layer, d), _mod_spec(rows, layer, d, 4, 0), _mod_spec(rows, layer, d, 3, 0),
           pl.BlockSpec((None, d, ne), lambda i: (layer, 0, 0))]
    outs = [pl.BlockSpec((tm, d), lambda i: (i, 0)), pl.BlockSpec((tm, ne), lambda i: (i, 0))]
    return ins, outs


def _two_group_specs(rows, k):
    npb, nsb, tm = rows.prompt_blocks, rows.sample_blocks, rows.tm
    return [pl.BlockSpec((tm, k), lambda i: (jnp.minimum(i, npb - 1), 0)),
            pl.BlockSpec((tm, k), lambda i: (jnp.clip(i - npb, 0, nsb - 1), 0))]


def matmul_gated_residual(a_p, a_s, w, wl, xs, mod, rows, layer, gains2, router_w):
    k = a_p.shape[1]
    m, d = rows.n_prompt + rows.n_sample, xs[0].shape[1]
    tm = rows.tm
    ne = router_w.shape[2]
    r_in, r_out = _sublayer2_specs(rows, layer, d, ne)
    return pl.pallas_call(
        functools.partial(_mmres_kernel, npb=rows.prompt_blocks),
        grid=(m // tm,),
        in_specs=_two_group_specs(rows, k) + [
            pl.BlockSpec((None, k, d), lambda i: (wl, 0, 0))] + _stream_specs(rows, xs, 0) + [
            _mod_spec(rows, layer, d, 2, 0)] + r_in,
        out_specs=[pl.BlockSpec((tm, d), lambda i: (i, 0))] + r_out,
        out_shape=[jax.ShapeDtypeStruct((m, d), F32), jax.ShapeDtypeStruct((m, d), BF16),
                   jax.ShapeDtypeStruct((m, ne), F32)],
        scratch_shapes=[pltpu.VMEM((k, d), BF16)],
        compiler_params=_cparams("arbitrary"),
        name="matmul_gated_residual",
    )(a_p, a_s, w, xs[0], xs[1], mod, gains2, mod, mod, router_w)


def _glures_kernel(ap_ref, as_ref, wv_ref, wg_ref, r_ref, gt_ref, g_ref, sc_ref, sh_ref, rw_ref,
                   o_ref, h_ref, aff_ref, wv_bf, wg_bf, *, npb):
    i = pl.program_id(0)

    @pl.when(i == 0)
    def _():
        wv_bf[...] = wv_ref[...].astype(BF16)
        wg_bf[...] = wg_ref[...].astype(BF16)

    def emit(a_ref):
        a = a_ref[...].astype(BF16)
        v = jnp.dot(a, wv_bf[...], preferred_element_type=F32)
        g = jnp.dot(a, wg_bf[...], preferred_element_type=F32)
        x_new = r_ref[...] + gt_ref[...] * (v * jax.nn.sigmoid(g))
        o_ref[...] = x_new
        _router_epilogue(x_new, g_ref, sc_ref, sh_ref, rw_ref, h_ref, aff_ref)

    pl.when(i < npb)(lambda: emit(ap_ref))
    pl.when(i >= npb)(lambda: emit(as_ref))


def glu_gated_residual(a_p, a_s, w_glu, wl, resid, mod, rows, layer, gains2, router_w):
    k = a_p.shape[1]
    m, d = resid.shape
    tm = rows.tm
    ne = router_w.shape[2]
    r_in, r_out = _sublayer2_specs(rows, layer, d, ne)
    return pl.pallas_call(
        functools.partial(_glures_kernel, npb=rows.prompt_blocks),
        grid=(m // tm,),
        in_specs=_two_group_specs(rows, k) + [
            pl.BlockSpec((None, k, d), lambda i: (wl, 0, 0)),
            pl.BlockSpec((None, k, d), lambda i: (wl, 0, 1)),
            pl.BlockSpec((tm, d), lambda i: (i, 0)),
            _mod_spec(rows, layer, d, 2, 0)] + r_in,
        out_specs=[pl.BlockSpec((tm, d), lambda i: (i, 0))] + r_out,
        out_shape=[jax.ShapeDtypeStruct((m, d), F32), jax.ShapeDtypeStruct((m, d), BF16),
                   jax.ShapeDtypeStruct((m, ne), F32)],
        scratch_shapes=[pltpu.VMEM((k, d), BF16), pltpu.VMEM((k, d), BF16)],
        compiler_params=_cparams("arbitrary"),
        name="glu_gated_residual",
    )(a_p, a_s, w_glu, w_glu, resid, mod, gains2, mod, mod, router_w)


def _rglru_kernel(gate_ref, u_ref, cw_ref, cb_ref, wa_ref, ba_ref, wx_ref, bx_ref, lam_ref, h0_ref,
                  y_ref, fin_ref, af_s, bf_s, ab_s, bb_s, hf_s, hb_s):
    t, cw = u_ref.shape
    u = u_ref[...]
    row = lax.broadcasted_iota(jnp.int32, (t, cw), 0)

    def shifted(x, k):
        if k > 0:
            return jnp.where(row >= k, pltpu.roll(x, k, axis=0), 0.0)
        return jnp.where(row < t + k, pltpu.roll(x, t + k, axis=0), 0.0)

    cwv = cw_ref[...]
    uc = (cwv[0:1] * shifted(u, 2) + cwv[1:2] * shifted(u, 1) + cwv[2:3] * u
          + cwv[3:4] * shifted(u, -1) + cb_ref[...])

    a_scr = (af_s, ab_s)
    b_scr = (bf_s, bb_s)
    for k in range(2):
        nl = -lam_ref[k:k + 1, :]
        sp = jnp.maximum(nl, 0.0) + jnp.log1p(jnp.exp(-jnp.abs(nl)))
        for hh in range(cw // RG_BS):
            sl = slice(hh * RG_BS, (hh + 1) * RG_BS)
            uh = uc[:, sl]
            ub = uh.astype(BF16)
            r = _sigmoid(jnp.dot(ub, wa_ref[k, hh].astype(BF16), preferred_element_type=F32) + ba_ref[k:k + 1, sl])
            i = _sigmoid(jnp.dot(ub, wx_ref[k, hh].astype(BF16), preferred_element_type=F32) + bx_ref[k:k + 1, sl])
            log_a = (-RG_C * r) * sp[:, sl]
            a = jnp.exp(log_a)
            a_scr[k][:, sl] = a
            b_scr[k][:, sl] = jnp.sqrt(jnp.tanh(-log_a) * (a * a + 1.0)) * (i * uh)

    nblk = t // SUBLANES
    srow = lax.broadcasted_iota(jnp.int32, (SUBLANES, cw), 0)

    def body(n, carry):
        cf, cb = carry
        rf = pl.multiple_of(n * SUBLANES, SUBLANES)
        rb = pl.multiple_of((nblk - 1 - n) * SUBLANES, SUBLANES)
        a = af_s[pl.ds(rf, SUBLANES), :]
        b = bf_s[pl.ds(rf, SUBLANES), :]
        a2 = ab_s[pl.ds(rb, SUBLANES), :]
        b2 = bb_s[pl.ds(rb, SUBLANES), :]
        for s in (1, 2, 4):
            m = srow >= s
            b = jnp.where(m, a * pltpu.roll(b, s, axis=0) + b, b)
            a = jnp.where(m, a * pltpu.roll(a, s, axis=0), a)
            m2 = srow < SUBLANES - s
            b2 = jnp.where(m2, a2 * pltpu.roll(b2, SUBLANES - s, axis=0) + b2, b2)
            a2 = jnp.where(m2, a2 * pltpu.roll(a2, SUBLANES - s, axis=0), a2)
        hf = a * cf + b
        hb = a2 * cb + b2
        hf_s[pl.ds(rf, SUBLANES), :] = hf
        hb_s[pl.ds(rb, SUBLANES), :] = hb
        return hf[SUBLANES - 1:SUBLANES, :], hb[0:1, :]

    cf, cb = lax.fori_loop(0, nblk, body, (h0_ref[0:1, :], h0_ref[1:2, :]))
    fin_ref[0:1, :] = cf
    fin_ref[1:2, :] = cb
    y_ref[...] = (hf_s[...] + hb_s[...]) * _gelu(gate_ref[...])


def rglru_scan(gu, row0, n_seq, t, j, conv_w, conv_b, w_a, b_a, w_x, b_x, lam, h0, h0_j, *, cw=RG_TILE):
    r = gu.shape[1] // 2
    nh = cw // RG_BS
    blk0 = row0 // t
    nc = r // cw
    scr = [pltpu.VMEM((t, cw), F32) for _ in range(6)]
    vec2 = pl.BlockSpec((None, 2, cw), lambda b, c: (j, 0, c))
    gatew = pl.BlockSpec((None, 2, nh, RG_BS, RG_BS), lambda b, c: (j, 0, c, 0, 0))
    return pl.pallas_call(
        _rglru_kernel,
        grid=(n_seq, nc),
        in_specs=[pl.BlockSpec((t, cw), lambda b, c: (blk0 + b, c)),
                  pl.BlockSpec((t, cw), lambda b, c: (blk0 + b, nc + c)),
                  pl.BlockSpec((None, 4, cw), lambda b, c: (j, 0, c)),
                  pl.BlockSpec((None, 1, cw), lambda b, c: (j, 0, c)),
                  gatew, vec2, gatew, vec2, vec2,
                  pl.BlockSpec((None, None, 2, cw), lambda b, c: (b, h0_j, 0, c))],
        out_specs=[pl.BlockSpec((t, cw), lambda b, c: (b, c)),
                   pl.BlockSpec((None, 2, cw), lambda b, c: (b, 0, c))],
        out_shape=[jax.ShapeDtypeStruct((n_seq * t, r), F32),
                   jax.ShapeDtypeStruct((n_seq, 2, r), F32)],
        scratch_shapes=scr,
        compiler_params=_cparams("arbitrary", "arbitrary"),
        name="rglru_scan",
    )(gu, gu, conv_w, conv_b.reshape(conv_b.shape[0], 1, r), w_a, b_a, w_x, b_x, lam, h0)


def _s5_kernel(*refs, bp, n_seq, nc, gpb):
    ell, gw = S5_L, S5_H
    (x_ref, tm_ref, win_ref, wre_ref, wim_ref, ar_ref, ai_ref, d_ref, s0re_ref, s0im_ref,
     o_ref, fin_ref, xg_s, ure_s, uim_s, fre_s, fim_s, bre_s, bim_s) = refs
    m, mp = n_seq * nc, bp * nc
    lanes = x_ref.shape[1]
    token = lambda l: pl.ds(l, m, stride=ell)
    per_tile = lanes // gw
    half = ure_s.shape[2] // 2
    lane_grp = lax.shift_right_logical(lax.broadcasted_iota(jnp.int32, (m, lanes), 1), gw.bit_length() - 1)

    def perm(shape, chunk_major_axis):
        i = lax.broadcasted_iota(jnp.int32, shape, chunk_major_axis)
        j = lax.broadcasted_iota(jnp.int32, shape, 1 - chunk_major_axis)
        b = i & (bp - 1)
        c = lax.shift_right_logical(i, bp.bit_length() - 1)
        return jnp.where((j == b * nc + c) & (b < n_seq), 1.0, 0.0).astype(BF16)

    to_chunk_major = perm((mp, m), 0)
    to_batch_major = perm((m, mp), 1)

    def block_transpose(v):
        k = per_tile // 2
        while k >= 1:
            low = (lane_grp & k) == 0
            nxt = list(v)
            for i in range(per_tile):
                if i & k == 0:
                    a, b = v[i], v[i + k]
                    nxt[i] = jnp.where(low, a, pltpu.roll(b, k * gw, axis=1))
                    nxt[i + k] = jnp.where(low, pltpu.roll(a, lanes - k * gw, axis=1), b)
            v = nxt
            k //= 2
        return v

    for tile in range(ell // per_tile):
        by_group = block_transpose([x_ref[token(tile * per_tile + j), :] for j in range(per_tile)])
        for g in range(gpb):
            xg_s[g, :, tile * lanes:(tile + 1) * lanes] = by_group[g]

    for g in range(gpb):
        xg = xg_s[g]
        xp = jnp.dot(to_chunk_major, xg.astype(BF16), preferred_element_type=F32).astype(BF16)
        u = jnp.dot(xp, win_ref[g].astype(BF16), preferred_element_type=F32)
        ure_s[g] = u[:, :2 * half]
        uim_s[g] = u[:, 2 * half:]

    is_fwd = lax.broadcasted_iota(jnp.int32, (bp, 2 * half), 1) < half
    ar = [ar_ref[g] for g in range(gpb)]
    ai = [ai_ref[g] for g in range(gpb)]

    def body(k, carry):
        rf = pl.multiple_of(k * bp, bp)
        rb = pl.multiple_of((nc - 1 - k) * bp, bp)
        out = []
        for g in range(gpb):
            re, im = carry[2 * g], carry[2 * g + 1]
            fre_s[g, pl.ds(rf, bp), :] = re
            fim_s[g, pl.ds(rf, bp), :] = im
            bre_s[g, pl.ds(rb, bp), :] = re
            bim_s[g, pl.ds(rb, bp), :] = im
            ure = jnp.where(is_fwd, ure_s[g, pl.ds(rf, bp), :], ure_s[g, pl.ds(rb, bp), :])
            uim = jnp.where(is_fwd, uim_s[g, pl.ds(rf, bp), :], uim_s[g, pl.ds(rb, bp), :])
            out += [ar[g] * re - ai[g] * im + ure, ar[g] * im + ai[g] * re + uim]
        return tuple(out)

    init = tuple(r[g] for g in range(gpb) for r in (s0re_ref, s0im_ref))
    fin = lax.fori_loop(0, nc, body, init)
    fwd_all = lax.broadcasted_iota(jnp.int32, (mp, 2 * half), 1) < half
    for g in range(gpb):
        fin_ref[g, :, :2 * half] = fin[2 * g]
        fin_ref[g, :, 2 * half:] = fin[2 * g + 1]
        hre = jnp.where(fwd_all, fre_s[g], bre_s[g]).astype(BF16)
        him = jnp.where(fwd_all, fim_s[g], bim_s[g]).astype(BF16)
        hre = jnp.dot(to_batch_major, hre, preferred_element_type=F32).astype(BF16)
        him = jnp.dot(to_batch_major, him, preferred_element_type=F32).astype(BF16)
        xg = xg_s[g]
        y = (jnp.dot(xg.astype(BF16), tm_ref[g].astype(BF16), preferred_element_type=F32)
             + jnp.dot(hre, wre_ref[g].astype(BF16), preferred_element_type=F32)
             + jnp.dot(him, wim_ref[g].astype(BF16), preferred_element_type=F32)
             + d_ref[g] * xg)
        xg_s[g] = _gelu(y)

    for tile in range(ell // per_tile):
        by_token = block_transpose([xg_s[g, :, tile * lanes:(tile + 1) * lanes] for g in range(gpb)])
        for j in range(per_tile):
            o_ref[token(tile * per_tile + j), :] = by_token[j]


def s5_chunked(hn, row_blk, n_seq, nc, mats, s0re, s0im, *, bp):
    tmat, win, wre, wim, ar, ai, dg = mats
    g, w, _ = tmat.shape
    p2 = ar.shape[-1]
    ell = S5_L
    lanes = 128
    gpb = lanes // S5_H
    d = hn.shape[1]
    m, mp = n_seq * nc, bp * nc
    n = m * ell
    assert bp & (bp - 1) == 0 and S5_H & (S5_H - 1) == 0 and g % gpb == 0
    blk = lambda shape: pl.BlockSpec((gpb,) + shape, lambda i: (i, 0, 0))
    return pl.pallas_call(
        functools.partial(_s5_kernel, bp=bp, n_seq=n_seq, nc=nc, gpb=gpb),
        grid=(g // gpb,),
        in_specs=[pl.BlockSpec((n, lanes), lambda i: (row_blk, i)),
                  blk((w, w)), blk((w, 2 * p2)), blk((p2, w)), blk((p2, w)),
                  blk((1, p2)), blk((1, p2)), blk((1, w)), blk((bp, p2)), blk((bp, p2))],
        out_specs=[pl.BlockSpec((n, lanes), lambda i: (0, i)), blk((bp, 2 * p2))],
        out_shape=[jax.ShapeDtypeStruct((n, d), F32), jax.ShapeDtypeStruct((g, bp, 2 * p2), F32)],
        scratch_shapes=[pltpu.VMEM((gpb, m, w), F32)] + [pltpu.VMEM((gpb, mp, p2), F32) for _ in range(6)],
        compiler_params=_cparams("arbitrary"),
        name="s5_chunked",
    )(hn, tmat, win, wre, wim, ar, ai, dg, s0re, s0im)


def _cmul(ar, ai, br, bi):
    return ar * br - ai * bi, ar * bi + ai * br


def _s5_prep_kernel(are_ref, aim_ref, ldt_ref, btr_ref, bti_ref, cr_ref, ci_ref,
                    tm_ref, win_ref, wre_ref, wim_ref, ar_ref, ai_ref):
    for gi in range(cr_ref.shape[0]):
        _s5_prep_group(*(r.at[gi] for r in (are_ref, aim_ref, ldt_ref, btr_ref, bti_ref, cr_ref, ci_ref,
                                           tm_ref, win_ref, wre_ref, wim_ref, ar_ref, ai_ref)))


def _s5_prep_group(are_ref, aim_ref, ldt_ref, btr_ref, bti_ref, cr_ref, ci_ref,
                   tm_ref, win_ref, wre_ref, wim_ref, ar_ref, ai_ref):
    ell = S5_L
    h, p2 = cr_ref.shape
    w = ell * h
    a_re, a_im = are_ref[...], aim_ref[...]
    dt = jnp.exp(ldt_ref[...])
    steps = lax.broadcasted_iota(jnp.int32, (3 * SUBLANES, p2), 0).astype(F32)
    mag = jnp.exp(steps * (a_re * dt))
    ang = steps * (a_im * dt)
    pw_r, pw_i = mag * jnp.cos(ang), mag * jnp.sin(ang)
    nr, ni = pw_r[1:2] - 1.0, pw_i[1:2]
    den = a_re * a_re + a_im * a_im
    qr, qi = (nr * a_re + ni * a_im) / den, (ni * a_re - nr * a_im) / den
    bb_r, bb_i = _cmul(qr, qi, btr_ref[...], bti_ref[...])
    c_r, c_i = cr_ref[...], ci_ref[...]
    fwd = lax.broadcasted_iota(jnp.int32, (1, p2), 1) < p2 // 2

    def power_rows(m_fwd, m_bwd):
        return (jnp.where(fwd, pw_r[m_fwd:m_fwd + 1], pw_r[m_bwd:m_bwd + 1]),
                jnp.where(fwd, pw_i[m_fwd:m_fwd + 1], pw_i[m_bwd:m_bwd + 1]))

    def stack(x_r, x_i, powers):
        parts = [_cmul(x_r, x_i, *power_rows(*powers(l))) for l in range(ell)]
        return (jnp.concatenate([q[0] for q in parts], axis=0), jnp.concatenate([q[1] for q in parts], axis=0))

    win_r, win_i = stack(bb_r, bb_i, lambda l: (ell - 1 - l, l))
    win_ref[...] = jnp.concatenate([win_r, win_i], axis=1).astype(win_ref.dtype)
    z_r, z_i = stack(c_r, c_i, lambda l: (l + 1, ell - l))
    wre_ref[...] = z_r.T.astype(wre_ref.dtype)
    wim_ref[...] = (-z_i).T.astype(wim_ref.dtype)
    k_r, k_i = stack(c_r, c_i, lambda m: (m, ell - 1 - m))
    mask_f = jnp.where(fwd, 1.0, 0.0)

    def lag_rows(mask):
        return (lax.dot_general(bb_r * mask, k_r, NT_DIMS, precision=HIGHEST, preferred_element_type=F32)
                - lax.dot_general(bb_i * mask, k_i, NT_DIMS, precision=HIGHEST, preferred_element_type=F32))

    kf = lag_rows(mask_f)
    kb = lag_rows(1.0 - mask_f)
    lane = lax.broadcasted_iota(jnp.int32, (h, w), 1)
    blocks = []
    for li in range(ell):
        f_part = kf if li == 0 else pltpu.roll(kf, li * h, axis=1)
        s_b = (w - (ell - 1 - li) * h) % w
        b_part = kb if s_b == 0 else pltpu.roll(kb, s_b, axis=1)
        blocks.append(jnp.where(lane >= li * h, f_part, 0.0) + jnp.where(lane < (li + 1) * h, b_part, 0.0))
    tm_ref[...] = jnp.concatenate(blocks, axis=0).astype(tm_ref.dtype)
    ar_ref[...] = pw_r[ell:ell + 1]
    ai_ref[...] = pw_i[ell:ell + 1]


def s5_chunk_operators(a_re, a_im, log_dt, b_re, b_im, c_re, c_im, d):
    _, g, p = a_re.shape
    h = b_re.shape[-1]
    w = S5_L * h
    two_dir = lambda x: jnp.transpose(x, (1, 0, 2)).reshape(g, 1, 2 * p)
    ldt = two_dir(jnp.broadcast_to(log_dt[:, :, None], (2, g, p)))
    bt = lambda x: jnp.transpose(x, (1, 3, 0, 2)).reshape(g, h, 2 * p)
    ct = lambda x: jnp.transpose(x, (1, 2, 0, 3)).reshape(g, h, 2 * p)
    gps = math.gcd(g, S5_PREP_GROUPS)
    per_g = lambda shape: pl.BlockSpec((gps,) + shape, lambda i: (i, 0, 0))
    tmat, win, wre, wim, ar, ai = pl.pallas_call(
        _s5_prep_kernel,
        grid=(g // gps,),
        in_specs=[per_g((1, 2 * p))] * 3 + [per_g((h, 2 * p))] * 4,
        out_specs=[per_g((w, w)), per_g((w, 4 * p)), per_g((2 * p, w)), per_g((2 * p, w)),
                   per_g((1, 2 * p)), per_g((1, 2 * p))],
        out_shape=[jax.ShapeDtypeStruct((g, w, w), BF16), jax.ShapeDtypeStruct((g, w, 4 * p), BF16),
                   jax.ShapeDtypeStruct((g, 2 * p, w), BF16), jax.ShapeDtypeStruct((g, 2 * p, w), BF16),
                   jax.ShapeDtypeStruct((g, 1, 2 * p), F32), jax.ShapeDtypeStruct((g, 1, 2 * p), F32)],
        compiler_params=_cparams("arbitrary"),
        name="s5_chunk_operators",
    )(two_dir(a_re), two_dir(a_im), ldt, bt(b_re), bt(b_im), ct(c_re), ct(c_im))
    dg = jnp.tile(d.reshape(g, 1, h), (1, S5_L, 1)).reshape(g, 1, w)
    return tmat, win, wre, wim, ar, ai, dg


def s5_mixer_group(hn, row0, n_seq, t, mats, s0):
    ar = mats[4]
    g = ar.shape[0]
    p = ar.shape[-1] // 2
    nc = t // S5_L
    bp = -(-n_seq // SUBLANES) * SUBLANES
    n = n_seq * t
    assert row0 % n == 0
    if s0 is None:
        s0re = jnp.zeros((g, bp, 2 * p), F32)
        s0im = s0re
    else:
        st = jnp.transpose(s0, (3, 0, 2, 1, 4)).reshape(g, n_seq, 2, 2 * p)
        st = jnp.pad(st, ((0, 0), (0, bp - n_seq), (0, 0), (0, 0)))
        s0re, s0im = st[:, :, 0], st[:, :, 1]
    u, fin = s5_chunked(hn, row0 // n, n_seq, nc, mats, s0re, s0im, bp=bp)
    fin = fin.reshape(g, bp, 2, 2, p)[:, :n_seq]
    return u, jnp.transpose(fin, (1, 3, 2, 0, 4))


def _softmax_pv(scores, values, sink):
    m = sink
    for s in scores:
        m = jnp.maximum(m, jnp.max(s, axis=-1, keepdims=True))
    den = jnp.exp(sink - m)
    acc = None
    for s, v in zip(scores, values):
        p = jnp.exp(s - m)
        den = den + jnp.sum(p, axis=-1, keepdims=True)
        pv = jnp.dot(p.astype(BF16), v, preferred_element_type=F32)
        acc = pv if acc is None else acc + pv
    return acc / den


def _attn_prompt_kernel(sink_ref, q_ref, k_ref, v_ref, o_ref):
    k = k_ref[...].astype(BF16)
    v = v_ref[...].astype(BF16)
    for h in range(N_KV):
        hs = slice(h * HEAD_DIM, (h + 1) * HEAD_DIM)
        kh, vh = k[:, hs], v[:, hs]
        for g in range(Q_PER_KV):
            c0 = (h * Q_PER_KV + g) * HEAD_DIM
            qg = (q_ref[:, c0:c0 + HEAD_DIM] * ATT_SCALE).astype(BF16)
            s = lax.dot_general(qg, kh, NT_DIMS, preferred_element_type=F32)
            o_ref[:, c0:c0 + HEAD_DIM] = _softmax_pv([s], [vh], sink_ref[h * Q_PER_KV + g])


def attn_prompt(qkv, sink, n_seq, t):
    dq = N_HEADS * HEAD_DIM
    kcol = dq // KV_W
    return pl.pallas_call(
        _attn_prompt_kernel,
        grid=(n_seq,),
        in_specs=[pl.BlockSpec(memory_space=pltpu.SMEM),
                  pl.BlockSpec((t, dq), lambda b: (b, 0)),
                  pl.BlockSpec((t, KV_W), lambda b: (b, kcol)),
                  pl.BlockSpec((t, KV_W), lambda b: (b, kcol + 1))],
        out_specs=pl.BlockSpec((t, dq), lambda b: (b, 0)),
        out_shape=jax.ShapeDtypeStruct((n_seq * t, dq), F32),
        compiler_params=_cparams("arbitrary"),
        name="attn_prompt",
    )(sink, qkv, qkv, qkv)


def _rope(x, cos, sin):
    w = x.shape[1]
    low = (lax.broadcasted_iota(jnp.int32, x.shape, 1) & (HEAD_DIM // 4)) == 0
    partner = jnp.where(low, pltpu.roll(x, w - HEAD_DIM // 4, axis=1), pltpu.roll(x, HEAD_DIM // 4, axis=1))
    return x * cos + partner * sin


def _attn_sample_kernel(sink_ref, q_ref, k_ref, v_ref, kc_ref, vc_ref, cos_ref, sin_ref, o_ref,
                        kw_s, vw_s, kc_s, vc_s, *, t):
    n = pl.program_id(1)
    blk = ATT_BLOCK

    @pl.when(n == 0)
    def _():
        zeros = jnp.zeros((blk, KV_W), BF16)
        kw_s[0:blk, :] = zeros
        vw_s[0:blk, :] = zeros
        kw_s[blk + t:2 * blk + t, :] = zeros
        vw_s[blk + t:2 * blk + t, :] = zeros
        kw_s[blk:blk + t, :] = _rope(k_ref[...], cos_ref[...], sin_ref[...]).astype(BF16)
        vw_s[blk:blk + t, :] = v_ref[...].astype(BF16)
        kc_s[...] = kc_ref[...].astype(BF16)
        vc_s[...] = vc_ref[...].astype(BF16)

    r0 = pl.multiple_of(n * blk, blk)
    cq = cos_ref[pl.ds(r0, blk), :]
    sq = sin_ref[pl.ds(r0, blk), :]
    kw = kw_s[pl.ds(r0, 3 * blk), :]
    vw = vw_s[pl.ds(r0, 3 * blk), :]
    qi = lax.broadcasted_iota(jnp.int32, (blk, 3 * blk), 0)
    kj = lax.broadcasted_iota(jnp.int32, (blk, 3 * blk), 1)
    kpos = n * blk - blk + kj
    valid = (jnp.abs(kj - blk - qi) <= WINDOW) & (kpos >= 0) & (kpos < t)
    valid2 = jnp.concatenate([valid, valid], axis=0)
    for h in range(N_KV):
        hs = slice(h * HEAD_DIM, (h + 1) * HEAD_DIM)
        qh = (_rope(q_ref[:, h * KV_W:(h + 1) * KV_W], cq, sq) * ATT_SCALE).astype(BF16)
        kh, vh, kch, vch = kw[:, hs], vw[:, hs], kc_s[:, hs], vc_s[:, hs]
        for g in range(0, Q_PER_KV, 2):
            qg = jnp.concatenate([qh[:, g * HEAD_DIM:(g + 1) * HEAD_DIM], qh[:, (g + 1) * HEAD_DIM:(g + 2) * HEAD_DIM]], axis=0)
            s_loc = lax.dot_general(qg, kh, NT_DIMS, preferred_element_type=F32)
            s_loc = jnp.where(valid2, s_loc, NEG_INF)
            s_ctx = lax.dot_general(qg, kch, NT_DIMS, preferred_element_type=F32)
            sink = jnp.concatenate([jnp.full((blk, 1), sink_ref[h * Q_PER_KV + g + u], F32) for u in range(2)], axis=0)
            o2 = _softmax_pv([s_loc, s_ctx], [vh, vch], sink)
            for u in range(2):
                c0 = (h * Q_PER_KV + g + u) * HEAD_DIM
                o_ref[:, c0:c0 + HEAD_DIM] = o2[u * blk:(u + 1) * blk, :]


def _rope_tables(t):
    quarter = HEAD_DIM // 4
    freqs = ROPE_BASE ** (-jnp.arange(quarter, dtype=F32) / quarter)
    pos = jnp.arange(t)
    ang_r = (pos // GRID_W).astype(F32)[:, None] * freqs
    ang_c = (pos % GRID_W).astype(F32)[:, None] * freqs
    cos = jnp.concatenate([jnp.cos(ang_r), jnp.cos(ang_r), jnp.cos(ang_c), jnp.cos(ang_c)], axis=-1)
    sin = jnp.concatenate([-jnp.sin(ang_r), jnp.sin(ang_r), -jnp.sin(ang_c), jnp.sin(ang_c)], axis=-1)
    return jnp.tile(cos, (1, N_KV)), jnp.tile(sin, (1, N_KV))


def attn_sample(qkv, row0, sink, n_seq, t, k_ctx, v_ctx):
    dq = N_HEADS * HEAD_DIM
    kcol = dq // KV_W
    nb = t // ATT_BLOCK
    lc = k_ctx.shape[1]
    cos, sin = _rope_tables(t)
    qblk0, sblk0 = row0 // ATT_BLOCK, row0 // t
    return pl.pallas_call(
        functools.partial(_attn_sample_kernel, t=t),
        grid=(n_seq, nb),
        in_specs=[pl.BlockSpec(memory_space=pltpu.SMEM),
                  pl.BlockSpec((ATT_BLOCK, dq), lambda b, n: (qblk0 + b * nb + n, 0)),
                  pl.BlockSpec((t, KV_W), lambda b, n: (sblk0 + b, kcol)),
                  pl.BlockSpec((t, KV_W), lambda b, n: (sblk0 + b, kcol + 1)),
                  pl.BlockSpec((None, lc, KV_W), lambda b, n: (b, 0, 0)),
                  pl.BlockSpec((None, lc, KV_W), lambda b, n: (b, 0, 0)),
                  pl.BlockSpec((t, KV_W), lambda b, n: (0, 0)),
                  pl.BlockSpec((t, KV_W), lambda b, n: (0, 0))],
        out_specs=pl.BlockSpec((ATT_BLOCK, dq), lambda b, n: (b * nb + n, 0)),
        out_shape=jax.ShapeDtypeStruct((n_seq * t, dq), F32),
        scratch_shapes=[pltpu.VMEM((t + 2 * ATT_BLOCK, KV_W), BF16), pltpu.VMEM((t + 2 * ATT_BLOCK, KV_W), BF16),
                        pltpu.VMEM((lc, KV_W), BF16), pltpu.VMEM((lc, KV_W), BF16)],
        compiler_params=_cparams("arbitrary", "arbitrary"),
        name="attn_sample",
    )(sink, qkv, qkv, qkv, k_ctx, v_ctx, cos, sin)


def _moe_kernel(x_ref, wg_ref, wu_ref, wd_ref, gs_ref, g2_ref, o_ref, wg_bf, wu_bf, wd_bf, *, n_seg):
    wg_bf[...] = wg_ref[...].astype(BF16)
    wu_bf[...] = wu_ref[...].astype(BF16)
    wd_bf[...] = wd_ref[...].astype(BF16)
    rsub = min(MOE_ROW_SUB, x_ref.shape[0])
    for r in range(x_ref.shape[0] // rsub):
        rs = slice(r * rsub, (r + 1) * rsub)
        x = x_ref[rs, :]
        hg = jnp.dot(x, wg_bf[...], preferred_element_type=F32)
        hu = jnp.dot(x, wu_bf[...], preferred_element_type=F32)
        he = ((hg * jax.nn.sigmoid(hg)) * hu).astype(BF16)
        y = jnp.dot(he, wd_bf[...], preferred_element_type=F32)
        gs = gs_ref[rs, :]
        seg = gs[:, 1:2]
        g2 = jnp.zeros(y.shape, F32)
        for s in range(n_seg):
            g2 = jnp.where(seg == float(s), g2_ref[s:s + 1, :], g2)
        o_ref[rs, :] = y * (gs[:, 0:1] * g2)


def moe_experts(xe, w_gate, w_up, w_down, layer, gate_seg, mod, n_seg):
    ne, r, d = xe.shape
    dff = w_gate.shape[3]
    return pl.pallas_call(
        functools.partial(_moe_kernel, n_seg=n_seg),
        grid=(ne,),
        in_specs=[pl.BlockSpec((None, r, d), lambda e: (e, 0, 0)),
                  pl.BlockSpec((None, None, d, dff), lambda e: (layer, e, 0, 0)),
                  pl.BlockSpec((None, None, d, dff), lambda e: (layer, e, 0, 0)),
                  pl.BlockSpec((None, None, dff, d), lambda e: (layer, e, 0, 0)),
                  pl.BlockSpec((None, r, 2), lambda e: (e, 0, 0)),
                  pl.BlockSpec((SUBLANES, None, d), lambda e: (layer, 0, 5))],
        out_specs=pl.BlockSpec((None, r, d), lambda e: (e, 0, 0)),
        out_shape=jax.ShapeDtypeStruct((ne, r, d), F32),
        scratch_shapes=[pltpu.VMEM((d, dff), BF16), pltpu.VMEM((d, dff), BF16), pltpu.VMEM((dff, d), BF16)],
        compiler_params=_cparams("arbitrary"),
        name="moe_experts",
    )(xe, w_gate, w_up, w_down, gate_seg, mod)


def _expert_choice(aff, rows):
    ne = aff.shape[1]
    sizes = (rows.n_prompt, rows.n_sample)
    caps = [(EC_FACTOR * n) // ne for n in sizes]
    if sizes[0] == sizes[1]:
        n = sizes[0]
        gt, ix = lax.top_k(jnp.swapaxes(aff.reshape(2, n, ne), 1, 2), caps[0])
        ix = ix + jnp.array([0, n], jnp.int32)[:, None, None]
        return (jnp.concatenate([gt[0], gt[1]], axis=1), jnp.concatenate([ix[0], ix[1]], axis=1))
    gts, ixs, off = [], [], 0
    for n, cap in zip(sizes, caps):
        gt, ix = lax.top_k(aff[off:off + n].T, cap)
        gts.append(gt)
        ixs.append(ix + off)
        off += n
    return jnp.concatenate(gts, axis=1), jnp.concatenate(ixs, axis=1)


def _combine_kernel(idx_ref, x_hbm, ye_ref, o_hbm, acc, sem, *, n_grp, rows_per):
    g = pl.program_id(0)
    e = pl.program_id(1)
    last = pl.num_programs(1) - 1

    def load(grp):
        return pltpu.make_async_copy(x_hbm.at[pl.ds(grp * n_grp, n_grp), :], acc.at[grp], sem.at[grp])

    def store(grp):
        return pltpu.make_async_copy(acc.at[grp], o_hbm.at[pl.ds(grp * n_grp, n_grp), :], sem.at[2 + grp])

    @pl.when((g == 0) & (e == 0))
    def _():
        load(0).start()
        load(1).start()

    @pl.when(e == 0)
    def _():
        load(g).wait()

    def add_rows(grp):
        base = (2 * e + grp) * rows_per

        def body(i, carry):
            toks = [idx_ref[base + i * COMBINE_UNROLL + k] for k in range(COMBINE_UNROLL)]
            old = [acc[grp, pl.ds(t, 1), :] for t in toks]
            add = [ye_ref[pl.ds(i * COMBINE_UNROLL + k, 1), :] for k in range(COMBINE_UNROLL)]
            for t, a, b in zip(toks, old, add):
                acc[grp, pl.ds(t, 1), :] = a + b
            return carry

        lax.fori_loop(0, rows_per // COMBINE_UNROLL, body, 0)

    pl.when(g == 0)(lambda: add_rows(0))
    pl.when(g == 1)(lambda: add_rows(1))

    @pl.when(e == last)
    def _():
        store(g).start()

    @pl.when((g == 1) & (e == last))
    def _():
        store(0).wait()
        store(1).wait()


def moe_combine(x, ye, idx, n_grp):
    m, d = x.shape
    ne, r, _ = ye.shape
    rows_per = r // 2
    assert m == 2 * n_grp and rows_per % COMBINE_UNROLL == 0
    grid_spec = pltpu.PrefetchScalarGridSpec(
        num_scalar_prefetch=1,
        grid=(2, ne),
        in_specs=[pl.BlockSpec(memory_space=pl.ANY),
                  pl.BlockSpec((None, rows_per, d), lambda g, e, ix: (e, g, 0))],
        out_specs=pl.BlockSpec(memory_space=pl.ANY),
        scratch_shapes=[pltpu.VMEM((2, n_grp, d), F32), pltpu.SemaphoreType.DMA((4,))])
    return pl.pallas_call(
        functools.partial(_combine_kernel, n_grp=n_grp, rows_per=rows_per),
        grid_spec=grid_spec,
        out_shape=jax.ShapeDtypeStruct((m, d), F32),
        compiler_params=_cparams("arbitrary", "arbitrary"),
        name="moe_combine",
    )(idx.reshape(-1), x, ye)


def moe_layer(x, h2, aff, mod, rows, layer, w_gate, w_up, w_down, n_seg):
    m, d = x.shape
    gates, idx = _expert_choice(aff, rows)
    seg = jnp.where(idx < rows.n_prompt, 0, 1 + (idx - rows.n_prompt) // rows.t_sample)
    xe = h2[idx]
    gate_seg = jnp.stack([gates, seg.astype(F32)], axis=-1)
    ye = moe_experts(xe, w_gate, w_up, w_down, layer, gate_seg, mod, n_seg)
    if rows.n_prompt == rows.n_sample:
        return moe_combine(x, ye, idx % rows.n_prompt, rows.n_prompt)
    return x.at[idx.reshape(-1)].add(ye.reshape(-1, d))


def kernel(x_prompt, x_sample, state_rglru, state_s5, cache_k, cache_v, c, c_ctx, ada_w, ada_b, norm1_g, norm2_g, rg_w_in, rg_conv_w, rg_conv_b, rg_w_a, rg_b_a, rg_w_x, rg_b_x, rg_lambda, rg_w_out, s5_a_re, s5_a_im, s5_log_dt, s5_b_re, s5_b_im, s5_c_re, s5_c_im, s5_d, s5_w_glu, attn_w_qkv, attn_w_o, attn_sink, router_w, moe_w_gate, moe_w_up, moe_w_down, final_norm_g):
    bp_, tp, d = x_prompt.shape
    bs, ts, _ = x_sample.shape
    n_p, n_s = bp_ * tp, bs * ts
    depth = ada_w.shape[0]
    rows = _Rows(n_p, n_s, ts)
    assert bs + 1 <= SUBLANES and n_p % ts == 0

    xs = (x_prompt.reshape(n_p, d), x_sample.reshape(n_s, d), 0)
    cond = jnp.concatenate([c_ctx[None, :], c, jnp.zeros((SUBLANES - 1 - bs, d), F32)], axis=0)
    mod_all = ada_modulation_all(cond, ada_w, ada_b)
    mod = mod_all.reshape(depth * SUBLANES, 1, 6 * d)
    g1 = norm1_g.reshape(depth, 1, d)
    g2 = norm2_g.reshape(depth, 1, d)

    new_rg, new_s5, new_k, new_v = [], [], [], []
    for l in range(depth):
        kind, j = l % 3, l // 3
        if kind == 0:
            gu = norm_mod_matmul(xs, g1, mod, rows, l, rg_w_in, j)
            args = (j, rg_conv_w, rg_conv_b, rg_w_a, rg_b_a, rg_w_x, rg_b_x, rg_lambda)
            r = gu.shape[1] // 2
            yp, fin = rglru_scan(gu, 0, bp_, tp, *args, jnp.zeros((bp_, 1, 2, r), F32), 0)
            ys, _ = rglru_scan(gu, n_p, bs, ts, *args, state_rglru, j)
            new_rg.append(fin)
            x, h2, aff = matmul_gated_residual(yp, ys, rg_w_out, j, xs, mod, rows, l, g2, router_w)
        elif kind == 1:
            hn = norm_mod(x, g1, mod, rows, l)
            mats = s5_chunk_operators(s5_a_re[j], s5_a_im[j], s5_log_dt[j], s5_b_re[j], s5_b_im[j],
                                      s5_c_re[j], s5_c_im[j], s5_d[j])
            up, st = s5_mixer_group(hn, 0, bp_, tp, mats, None)
            us, _ = s5_mixer_group(hn, n_p, bs, ts, mats, state_s5[:, j])
            new_s5.append(st)
            x, h2, aff = glu_gated_residual(up, us, s5_w_glu, j, x, mod, rows, l, g2, router_w)
        else:
            qkv = norm_mod_matmul(xs, g1, mod, rows, l, attn_w_qkv, j)
            dq = N_HEADS * HEAD_DIM
            new_k.append(qkv[:n_p, dq:dq + KV_W].reshape(bp_, tp, N_KV, HEAD_DIM))
            new_v.append(qkv[:n_p, dq + KV_W:].reshape(bp_, tp, N_KV, HEAD_DIM))
            op = attn_prompt(qkv, attn_sink[j], bp_, tp)
            lc = cache_k.shape[2]
            os_ = attn_sample(qkv, n_p, attn_sink[j], bs, ts,
                              cache_k[:, j].reshape(bs, lc, KV_W), cache_v[:, j].reshape(bs, lc, KV_W))
            x, h2, aff = matmul_gated_residual(op, os_, attn_w_o, j, xs, mod, rows, l, g2, router_w)
        x = moe_layer(x, h2, aff, mod, rows, l, moe_w_gate, moe_w_up, moe_w_down, bs + 1)
        xs = (x, x, rows.prompt_blocks)

    y_p, y_s = final_norm(x, final_norm_g, rows)
    return (y_p.reshape(bp_, tp, d), y_s.reshape(bs, ts, d),
            jnp.stack(new_rg, axis=1), jnp.stack(new_s5, axis=1),
            jnp.stack(new_k, axis=1), jnp.stack(new_v, axis=1))
```
